```python
import jax, jax.numpy as jnp
from jax import lax
import numpy as np

D_MODEL = 4096
BATCH = 4
SEQ = 2048
DEPTH = 2
DEC_BATCH = 32
DEC_SEQ = 8
PAST_LEN = 16384
PAGE_SIZE = 128

N_META = 16
POOL_WINDOWS = (2, 4, 8, 16)
POOL_GROUPS = len(POOL_WINDOWS)
POOL_W = D_MODEL // 4
POOL_GW = POOL_W // POOL_GROUPS
POOL_HIST = max(POOL_WINDOWS) - 1
DN_HEADS = D_MODEL // 512
DN_DK = 128
DN_DV = 128
DN_CONV = 4
DN_CHUNK = 64
DN_QKV = DN_HEADS * (2 * DN_DK + DN_DV)
ATT_HEADS = D_MODEL // 256
ATT_KV_HEADS = 4
ATT_HD = 128
WINDOW = 128
ATT_BLOCK = 128
MIX_W = POOL_W + DN_HEADS * DN_DV + ATT_HEADS * ATT_HD
PEER_HEADS = 8
PEER_NKEYS = 128
PEER_EXPERTS = PEER_NKEYS * PEER_NKEYS
PEER_DQ = 256
PEER_TOPK = 16
PEER_TOKEN_BLOCK = 64
IN_WIDTHS = (POOL_W, DN_QKV, DN_HEADS, DN_HEADS, DN_HEADS * DN_DV,
             ATT_HEADS * ATT_HD, ATT_KV_HEADS * ATT_HD, ATT_KV_HEADS * ATT_HD)
IN_W = sum(IN_WIDTHS)
EPS = 1e-6
NEG_INF = -1e30

kernel_name = 'hymba_pool_delta_swa_peer_step'


def rmsnorm(x, g):
    xf = x.astype(jnp.float32)
    y = xf * lax.rsqrt(jnp.mean(xf * xf, axis=-1, keepdims=True) + EPS)
    return (y * g.astype(jnp.float32)).astype(x.dtype)


def l2norm(x):
    return x * lax.rsqrt(jnp.sum(x * x, axis=-1, keepdims=True) + EPS)


def alibi_slopes():
    return jnp.exp2(-8.0 * (jnp.arange(ATT_HEADS, dtype=jnp.float32) + 1.0) / ATT_HEADS)


def pool_mixer(u, hist, pos0, w_pool, s_pool):
    B, T, _ = u.shape
    h0 = hist.shape[1]
    ext = jnp.concatenate([hist, u], axis=1)
    c = jnp.cumsum(ext.astype(jnp.float32), axis=1)
    c = jnp.concatenate([jnp.zeros_like(c[:, :1]), c], axis=1)
    t = pos0 + jnp.arange(T)
    outs = []
    for gi, w in enumerate(POOL_WINDOWS):
        sl = slice(gi * POOL_GW, (gi + 1) * POOL_GW)
        win = c[:, h0 + 1:h0 + 1 + T, sl] - c[:, h0 + 1 - w:h0 + 1 - w + T, sl]
        cnt = jnp.minimum(w, t + 1).astype(jnp.float32)[None, :, None]
        d = win / cnt - u[:, :, sl].astype(jnp.float32)
        outs.append(jnp.einsum('btc,cd->btd', d, w_pool[gi].astype(jnp.float32)))
    y = jnp.concatenate(outs, axis=-1) * s_pool.astype(jnp.float32)
    return y.astype(u.dtype), ext[:, -POOL_HIST:]


def delta_chunk(S, q, k, v, g, beta):
    L = q.shape[2]
    G = jnp.cumsum(g, axis=-1)
    idx = jnp.arange(L)
    incl = idx[:, None] >= idx[None, :]
    strict = idx[:, None] > idx[None, :]
    diff = G[..., :, None] - G[..., None, :]
    decay = jnp.where(incl, jnp.exp(jnp.where(incl, diff, 0.0)), 0.0)
    a_mat = jnp.where(strict, jnp.einsum('bhid,bhjd->bhij', k, k) * decay * beta[..., :, None], 0.0)
    rhs = jnp.concatenate([(beta * jnp.exp(G))[..., None] * k, beta[..., None] * v], axis=-1)
    wu = lax.linalg.triangular_solve(jnp.eye(L, dtype=a_mat.dtype) + a_mat, rhs,
                                     left_side=True, lower=True, unit_diagonal=True)
    w, u = wu[..., :DN_DK], wu[..., DN_DK:]
    v_new = u - jnp.einsum('bhld,bhde->bhle', w, S)
    o = (jnp.exp(G)[..., None] * jnp.einsum('bhld,bhde->bhle', q, S)
         + jnp.einsum('bhij,bhje->bhie', jnp.einsum('bhid,bhjd->bhij', q, k) * decay, v_new))
    g_last = G[..., -1:]
    S = (jnp.exp(g_last)[..., None] * S
         + jnp.einsum('bhld,bhle->bhde', k * jnp.exp(g_last - G)[..., None], v_new))
    return S, o


def delta_scan(S, q, k, v, g, beta, chunk):
    B, H, T = g.shape
    n = T // chunk
    def split(a):
        return jnp.moveaxis(a.reshape(B, H, n, chunk, *a.shape[3:]), 2, 0)
    def step(s, xs):
        return delta_chunk(s, *xs)
    S, o = lax.scan(step, S, tuple(split(a) for a in (q, k, v, g, beta)))
    return S, jnp.moveaxis(o, 0, 2).reshape(B, H, T, DN_DV)


def delta_mixer(qkv, b_raw, a_raw, gate, conv_hist, S0, w_conv, a_log, dt_bias, norm_g, lead):
    B, T, _ = qkv.shape
    ext = jnp.concatenate([conv_hist, qkv], axis=1)
    conv = ext[:, 0:T] * w_conv[0]
    for i in range(1, DN_CONV):
        conv = conv + ext[:, i:i + T] * w_conv[i]
    conv_new = ext[:, -(DN_CONV - 1):]
    c = jax.nn.silu(conv.astype(jnp.float32))
    q, k, v = jnp.split(c, [DN_HEADS * DN_DK, 2 * DN_HEADS * DN_DK], axis=-1)
    q = l2norm(q.reshape(B, T, DN_HEADS, DN_DK)) * (DN_DK ** -0.5)
    k = l2norm(k.reshape(B, T, DN_HEADS, DN_DK))
    v = v.reshape(B, T, DN_HEADS, DN_DV)
    beta = jax.nn.sigmoid(b_raw.astype(jnp.float32))
    g = -jnp.exp(a_log.astype(jnp.float32)) * jax.nn.softplus(a_raw.astype(jnp.float32) + dt_bias.astype(jnp.float32))
    q, k, v, beta, g = (jnp.swapaxes(a, 1, 2) for a in (q, k, v, beta, g))
    S, o_lead = delta_chunk(S0.astype(jnp.float32), q[:, :, :lead], k[:, :, :lead], v[:, :, :lead],
                            g[:, :, :lead], beta[:, :, :lead])
    outs = [o_lead]
    if T > lead:
        S, o_rest = delta_scan(S, q[:, :, lead:], k[:, :, lead:], v[:, :, lead:],
                               g[:, :, lead:], beta[:, :, lead:], DN_CHUNK)
        outs.append(o_rest)
    o = jnp.swapaxes(jnp.concatenate(outs, axis=2), 1, 2)
    o = (o * lax.rsqrt(jnp.mean(o * o, axis=-1, keepdims=True) + EPS) * norm_g.astype(jnp.float32)
         * jax.nn.silu(gate.astype(jnp.float32).reshape(B, T, DN_HEADS, DN_DV)))
    return o.reshape(B, T, DN_HEADS * DN_DV).astype(qkv.dtype), conv_new, S.astype(S0.dtype)


def sink_probs(s, sink):
    m = jnp.maximum(jnp.max(s, axis=-1, keepdims=True), sink)
    p = jnp.exp(s - m)
    return p / (jnp.sum(p, axis=-1, keepdims=True) + jnp.exp(sink - m))


def swa_banded(q, k, v, sinks):
    B, T, HQ, D = q.shape
    G = HQ // ATT_KV_HEADS
    nb = -(-T // ATT_BLOCK)
    pad = nb * ATT_BLOCK - T
    def padt(a):
        return jnp.pad(a, ((0, 0), (0, pad), (0, 0), (0, 0)))
    qb = padt(q).reshape(B, nb, ATT_BLOCK, ATT_KV_HEADS, G, D)
    kb = padt(k).reshape(B, nb, ATT_BLOCK, ATT_KV_HEADS, D)
    vb = padt(v).reshape(B, nb, ATT_BLOCK, ATT_KV_HEADS, D)
    def shift(a):
        return jnp.concatenate([jnp.zeros_like(a[:, :1]), a[:, :-1]], axis=1)
    kk = jnp.concatenate([shift(kb), kb], axis=2)
    vv = jnp.concatenate([shift(vb), vb], axis=2)
    s = jnp.einsum('bnikgd,bnjkd->bkgnij', qb, kk).astype(jnp.float32) * (D ** -0.5)
    i = jnp.arange(ATT_BLOCK)[:, None]
    j = jnp.arange(2 * ATT_BLOCK)[None, :]
    dist = ATT_BLOCK + i - j
    n = jnp.arange(nb)[:, None, None]
    valid = (dist >= 0) & (dist < WINDOW) & ((n > 0) | (j >= ATT_BLOCK))
    slopes = alibi_slopes().reshape(ATT_KV_HEADS, G)
    bias = -slopes[:, :, None, None, None] * dist.astype(jnp.float32)
    s = jnp.where(valid, s + bias, NEG_INF)
    p = sink_probs(s, sinks.astype(jnp.float32).reshape(ATT_KV_HEADS, G, 1, 1, 1))
    o = jnp.einsum('bkgnij,bnjkd->bnikgd', p.astype(v.dtype), vv)
    return o.reshape(B, nb * ATT_BLOCK, HQ * D)[:, :T]


def swa_cached(q, k, v, k_cache, v_cache, sinks):
    B, S, HQ, D = q.shape
    G = HQ // ATT_KV_HEADS
    WC = k_cache.shape[1]
    kk = jnp.concatenate([k_cache, k], axis=1)
    vv = jnp.concatenate([v_cache, v], axis=1)
    s = jnp.einsum('bikgd,bjkd->bkgij', q.reshape(B, S, ATT_KV_HEADS, G, D), kk).astype(jnp.float32) * (D ** -0.5)
    dist = WC + jnp.arange(S)[:, None] - jnp.arange(WC + S)[None, :]
    valid = (dist >= 0) & (dist < WINDOW)
    slopes = alibi_slopes().reshape(ATT_KV_HEADS, G)
    s = jnp.where(valid, s - slopes[:, :, None, None] * dist.astype(jnp.float32), NEG_INF)
    p = sink_probs(s, sinks.astype(jnp.float32).reshape(ATT_KV_HEADS, G, 1, 1))
    o = jnp.einsum('bkgij,bjkd->bikgd', p.astype(v.dtype), vv).reshape(B, S, HQ * D)
    return o, kk[:, -WC:], vv[:, -WC:]


def mixer_block(hn, pos0, lead, pool_hist, conv_hist, S0, k_cache, v_cache,
                w_in, w_pool, s_pool, w_conv, a_log, dt_bias, dn_norm_g, sinks, w_out):
    B, T, _ = hn.shape
    pts = np.cumsum(IN_WIDTHS)[:-1].tolist()
    u_pool, qkv, b_raw, a_raw, gate, q_att, k_att, v_att = jnp.split(hn @ w_in, pts, axis=-1)
    y_pool, pool_new = pool_mixer(u_pool, pool_hist, pos0, w_pool, s_pool)
    y_dn, conv_new, S_new = delta_mixer(qkv, b_raw, a_raw, gate, conv_hist, S0, w_conv,
                                        a_log, dt_bias, dn_norm_g, lead)
    q = q_att.reshape(B, T, ATT_HEADS, ATT_HD)
    k = k_att.reshape(B, T, ATT_KV_HEADS, ATT_HD)
    v = v_att.reshape(B, T, ATT_KV_HEADS, ATT_HD)
    if k_cache is None:
        y_att = swa_banded(q, k, v, sinks)
        k_new, v_new = k[:, -WINDOW:], v[:, -WINDOW:]
    else:
        y_att, k_new, v_new = swa_cached(q, k, v, k_cache, v_cache, sinks)
    y = jnp.concatenate([y_pool, y_dn, y_att], axis=-1) @ w_out
    return y, (pool_new, conv_new, S_new, k_new, v_new)


def peer(x, w_query, sub_keys, w_down, w_up):
    B, T, D = x.shape
    n = B * T
    xt = x.reshape(n, D)
    q = (xt @ w_query).astype(jnp.float32).reshape(n, PEER_HEADS, 2, PEER_DQ // 2)
    sc = jnp.einsum('nhpc,hpkc->nhpk', q, sub_keys.astype(jnp.float32))
    s_half, i_half = lax.top_k(sc, PEER_TOPK)
    cand_s = (s_half[:, :, 0, :, None] + s_half[:, :, 1, None, :]).reshape(n, PEER_HEADS, PEER_TOPK * PEER_TOPK)
    cand_i = (i_half[:, :, 0, :, None] * PEER_NKEYS + i_half[:, :, 1, None, :]).reshape(n, PEER_HEADS, PEER_TOPK * PEER_TOPK)
    top_s, top_pos = lax.top_k(cand_s, PEER_TOPK)
    eidx = jnp.take_along_axis(cand_i, top_pos, axis=-1).reshape(n, PEER_HEADS * PEER_TOPK)
    gate = jax.nn.softmax(top_s, axis=-1).reshape(n, PEER_HEADS * PEER_TOPK)
    nblk = -(-n // PEER_TOKEN_BLOCK)
    pad = nblk * PEER_TOKEN_BLOCK - n
    xb = jnp.pad(xt, ((0, pad), (0, 0))).reshape(nblk, PEER_TOKEN_BLOCK, D)
    eb = jnp.pad(eidx, ((0, pad), (0, 0))).reshape(nblk, PEER_TOKEN_BLOCK, PEER_HEADS * PEER_TOPK)
    gb = jnp.pad(gate, ((0, pad), (0, 0))).reshape(nblk, PEER_TOKEN_BLOCK, PEER_HEADS * PEER_TOPK)
    def expert_block(args):
        xk, ek, gk = args
        h = jnp.einsum('ted,td->te', w_down[ek], xk).astype(jnp.float32)
        a = (jax.nn.gelu(h, approximate=False) * gk).astype(xk.dtype)
        return jnp.einsum('te,ted->td', a, w_up[ek])
    y = lax.map(expert_block, (xb, eb, gb))
    return y.reshape(nblk * PEER_TOKEN_BLOCK, D)[:n].reshape(B, T, D).astype(x.dtype)


def setup_inputs(seed: int = 0) -> dict:
    key = jax.random.key(seed)
    ks = jax.random.split(key, 26)
    f32 = jnp.float32
    def nrm(k, shape, scale):
        return jax.random.normal(k, shape, f32) * scale
    wc = min(WINDOW, PAST_LEN)
    dt = jnp.exp(jax.random.uniform(ks[11], (DEPTH, DN_HEADS), f32, np.log(1e-3), np.log(0.1)))
    return {
        'x_prompt': nrm(ks[0], (BATCH, SEQ, D_MODEL), 1.0),
        'x_sample': nrm(ks[1], (DEC_BATCH, DEC_SEQ, D_MODEL), 1.0),
        'state_pool': nrm(ks[2], (DEPTH, DEC_BATCH, POOL_HIST, POOL_W), 1.0),
        'state_conv': nrm(ks[3], (DEPTH, DEC_BATCH, DN_CONV - 1, DN_QKV), 1.0),
        'state_delta': nrm(ks[4], (DEPTH, DEC_BATCH, DN_HEADS, DN_DK, DN_DV), 0.3),
        'cache_k': nrm(ks[5], (DEPTH, DEC_BATCH, wc, ATT_KV_HEADS, ATT_HD), 1.0),
        'cache_v': nrm(ks[6], (DEPTH, DEC_BATCH, wc, ATT_KV_HEADS, ATT_HD), 1.0),
        'meta_tokens': nrm(ks[7], (N_META, D_MODEL), 1.0),
        'norm1_g': 1.0 + nrm(ks[8], (DEPTH, D_MODEL), 0.02),
        'w_in': nrm(ks[9], (DEPTH, D_MODEL, IN_W), D_MODEL ** -0.5),
        'w_pool': nrm(ks[10], (DEPTH, POOL_GROUPS, POOL_GW, POOL_GW), POOL_GW ** -0.5),
        's_pool': 1.0 + nrm(ks[12], (DEPTH, POOL_W), 0.02),
        'w_conv': nrm(ks[13], (DEPTH, DN_CONV, DN_QKV), DN_CONV ** -0.5),
        'a_log': jnp.log(jax.random.uniform(ks[14], (DEPTH, DN_HEADS), f32, 1.0, 16.0)),
        'dt_bias': dt + jnp.log(-jnp.expm1(-dt)),
        'dn_norm_g': 1.0 + nrm(ks[15], (DEPTH, DN_DV), 0.02),
        'attn_sinks': nrm(ks[16], (DEPTH, ATT_HEADS), 1.0),
        'w_out': nrm(ks[17], (DEPTH, MIX_W, D_MODEL), MIX_W ** -0.5),
        'norm2_g': 1.0 + nrm(ks[18], (DEPTH, D_MODEL), 0.02),
        'peer_w_query': nrm(ks[19], (DEPTH, D_MODEL, PEER_HEADS * PEER_DQ), D_MODEL ** -0.5),
        'peer_sub_keys': nrm(ks[20], (DEPTH, PEER_HEADS, 2, PEER_NKEYS, PEER_DQ // 2), (PEER_DQ // 2) ** -0.5),
        'peer_w_down': nrm(ks[21], (DEPTH, PEER_EXPERTS, D_MODEL), D_MODEL ** -0.5),
        'peer_w_up': nrm(ks[22], (DEPTH, PEER_EXPERTS, D_MODEL), PEER_HEADS ** -0.5),
        'final_norm_g': 1.0 + nrm(ks[23], (D_MODEL,), 0.02),
    }


def reference(x_prompt, x_sample, state_pool, state_conv, state_delta, cache_k, cache_v,
              meta_tokens, norm1_g, w_in, w_pool, s_pool, w_conv, a_log, dt_bias, dn_norm_g,
              attn_sinks, w_out, norm2_g, peer_w_query, peer_sub_keys, peer_w_down, peer_w_up,
              final_norm_g):
    bp = x_prompt.shape[0]
    meta = jnp.broadcast_to(meta_tokens.astype(x_prompt.dtype)[None], (bp, N_META, D_MODEL))
    hp = jnp.concatenate([meta, x_prompt], axis=1)
    hs = x_sample
    new_p = [[] for _ in range(5)]
    new_s = [[] for _ in range(5)]
    for l in range(DEPTH):
        mix_w = (w_in[l], w_pool[l], s_pool[l], w_conv[l], a_log[l], dt_bias[l],
                 dn_norm_g[l], attn_sinks[l], w_out[l])
        peer_w = (peer_w_query[l], peer_sub_keys[l], peer_w_down[l], peer_w_up[l])
        zero_pool = jnp.zeros((bp, POOL_HIST, POOL_W), hp.dtype)
        zero_conv = jnp.zeros((bp, DN_CONV - 1, DN_QKV), hp.dtype)
        zero_S = jnp.zeros((bp, DN_HEADS, DN_DK, DN_DV), state_delta.dtype)
        y, st = mixer_block(rmsnorm(hp, norm1_g[l]), 0, N_META, zero_pool, zero_conv, zero_S,
                            None, None, *mix_w)
        hp = hp + y
        hp = hp + peer(rmsnorm(hp, norm2_g[l]), *peer_w)
        for lst, a in zip(new_p, st):
            lst.append(a)
        y, st = mixer_block(rmsnorm(hs, norm1_g[l]), PAST_LEN, hs.shape[1], state_pool[l],
                            state_conv[l], state_delta[l], cache_k[l], cache_v[l], *mix_w)
        hs = hs + y
        hs = hs + peer(rmsnorm(hs, norm2_g[l]), *peer_w)
        for lst, a in zip(new_s, st):
            lst.append(a)
    y_prompt = rmsnorm(hp, final_norm_g)[:, N_META:]
    y_sample = rmsnorm(hs, final_norm_g)
    pool_p, conv_p, delta_p, k_p, v_p = (jnp.stack(a) for a in new_p)
    pool_s, conv_s, delta_s, k_s, v_s = (jnp.stack(a) for a in new_s)
    return (y_prompt, y_sample, pool_p, pool_s, conv_p, conv_s, delta_p, delta_s, k_p, k_s, v_p, v_s)
```

```python
import functools

import jax
import jax.numpy as jnp
from jax import lax
from jax.experimental import pallas as pl
from jax.experimental.pallas import tpu as pltpu

F32 = jnp.float32
BF16 = jnp.bfloat16

EPS = 1e-6
NEG_INF = -1e30
LOWEST = -3.0e38

N_META = 16
BLK = 128
FRONT = BLK - N_META
DEC_T = 8
SLOTS = 8
POOL_WINDOWS = (2, 4, 8, 16)
POOL_HIST = 15
DN_CONV = 4
TOPK = 16
NKEYS = 128
VMEM_LIMIT = 56 * 1024 * 1024


def _cparams(*sem):
    return pltpu.CompilerParams(dimension_semantics=sem, vmem_limit_bytes=VMEM_LIMIT)


def _bdot(a, b):
    return jnp.dot(a.astype(BF16), b.astype(BF16), preferred_element_type=F32)


def _bdot_nt(a, b):
    return lax.dot_general(a.astype(BF16), b.astype(BF16), (((1,), (1,)), ((), ())),
                           preferred_element_type=F32)


def _split2(a):
    hi = a.astype(BF16)
    lo = (a - hi.astype(F32)).astype(BF16)
    return hi, lo


def _dot_hi(a, b):
    a0, a1 = _split2(a)
    b0, b1 = _split2(b)
    d = functools.partial(jnp.dot, preferred_element_type=F32)
    return d(a0, b0) + (d(a0, b1) + d(a1, b0))


def _silu(x):
    return x * (1.0 / (1.0 + jnp.exp(-x)))


def _addnorm_body(*refs, add):
    if add:
        h_ref, y_ref, g_ref, hs_ref, xn_ref = refs
        x = h_ref[...] + y_ref[...]
        hs_ref[...] = x
    else:
        h_ref, g_ref, xn_ref = refs
        x = h_ref[...]
    ms = jnp.mean(x * x, axis=-1, keepdims=True)
    xn_ref[...] = (x * lax.rsqrt(ms + EPS) * g_ref[...]).astype(xn_ref.dtype)


def _addnorm(h, y, g, *, tm=256, out_dtype=BF16):
    n, d = h.shape
    row = pl.BlockSpec((tm, d), lambda i: (i, 0))
    gspec = pl.BlockSpec((1, d), lambda i: (0, 0))
    g2 = g.reshape(1, d).astype(F32)
    if y is None:
        xn = pl.pallas_call(
            functools.partial(_addnorm_body, add=False),
            grid=(n // tm,), in_specs=[row, gspec], out_specs=row,
            out_shape=jax.ShapeDtypeStruct((n, d), out_dtype),
            compiler_params=_cparams("parallel"), name="norm")(h, g2)
        return h, xn
    hs, xn = pl.pallas_call(
        functools.partial(_addnorm_body, add=True),
        grid=(n // tm,), in_specs=[row, row, gspec], out_specs=[row, row],
        out_shape=[jax.ShapeDtypeStruct((n, d), F32), jax.ShapeDtypeStruct((n, d), out_dtype)],
        compiler_params=_cparams("parallel"), name="add_norm")(h, y, g2)
    return hs, xn


def _mm_body(x_ref, w_ref, o_ref):
    o_ref[...] = jnp.dot(x_ref[...], w_ref[...], preferred_element_type=F32)


def _mm(x, w, *, tm=512, tn=1024):
    m, k = x.shape
    n = w.shape[1]
    tn = min(tn, n)
    return pl.pallas_call(
        _mm_body, grid=(n // tn, m // tm),
        in_specs=[pl.BlockSpec((tm, k), lambda j, i: (i, 0)),
                  pl.BlockSpec((k, tn), lambda j, i: (0, j))],
        out_specs=pl.BlockSpec((tm, tn), lambda j, i: (i, j)),
        out_shape=jax.ShapeDtypeStruct((m, n), F32),
        compiler_params=_cparams("parallel", "parallel"), name="matmul")(x, w)


def _outproj_body(yp_ref, yd_ref, ya_ref, w_ref, h_ref, o_ref, *, wp, wd):
    acc = h_ref[...]
    acc += jnp.dot(yp_ref[...].astype(BF16), w_ref[0:wp, :], preferred_element_type=F32)
    acc += jnp.dot(yd_ref[...].astype(BF16), w_ref[wp:wp + wd, :], preferred_element_type=F32)
    acc += jnp.dot(ya_ref[...].astype(BF16), w_ref[wp + wd:, :], preferred_element_type=F32)
    o_ref[...] = acc


def _outproj(yp, yd, ya, w, h, *, tm=512, tn=1024):
    m, d = h.shape
    wp, wd, wa = yp.shape[1], yd.shape[1], ya.shape[1]
    k = wp + wd + wa
    return pl.pallas_call(
        functools.partial(_outproj_body, wp=wp, wd=wd), grid=(d // tn, m // tm),
        in_specs=[pl.BlockSpec((tm, wp), lambda j, i: (i, 0)),
                  pl.BlockSpec((tm, wd), lambda j, i: (i, 0)),
                  pl.BlockSpec((tm, wa), lambda j, i: (i, 0)),
                  pl.BlockSpec((k, tn), lambda j, i: (0, j)),
                  pl.BlockSpec((tm, tn), lambda j, i: (i, j))],
        out_specs=pl.BlockSpec((tm, tn), lambda j, i: (i, j)),
        out_shape=jax.ShapeDtypeStruct((m, d), F32),
        compiler_params=_cparams("parallel", "parallel"), name="out_proj")(yp, yd, ya, w, h)


def _pool_windows(ext_ref, u, t, pos, w_ref, s_ref, gw):
    outs = []
    for gi, w in enumerate(POOL_WINDOWS):
        sl = slice(gi * gw, (gi + 1) * gw)
        win = u[:, sl]
        for k in range(1, w):
            win = win + ext_ref[16 - k:16 - k + t, sl]
        cnt = jnp.clip(pos + 1, 1, w).astype(F32)
        d = win / cnt - u[:, sl]
        outs.append(_bdot(d, w_ref[gi]))
    return jnp.concatenate(outs, axis=-1) * s_ref[...]


def _pool_prompt_body(u_ref, w_ref, s_ref, y_ref, ext_ref, *, gw):
    n = pl.program_id(1)

    @pl.when(n == 0)
    def _():
        ext_ref[0:16, :] = jnp.zeros((16, ext_ref.shape[1]), F32)

    u = u_ref[...]
    ext_ref[16:16 + BLK, :] = u
    row = n * BLK + lax.broadcasted_iota(jnp.int32, (BLK, 1), 0)
    pos = row - FRONT
    y = _pool_windows(ext_ref, u, BLK, pos, w_ref, s_ref, gw)
    y_ref[...] = jnp.where(pos >= 0, y, 0.0)
    ext_ref[0:16, :] = u[BLK - 16:, :]


def _pool_prompt(p, w_pool, s_pool, nseq, nblk):
    pw = w_pool.shape[0] * w_pool.shape[1]
    gw = w_pool.shape[1]
    return pl.pallas_call(
        functools.partial(_pool_prompt_body, gw=gw), grid=(nseq, nblk),
        in_specs=[pl.BlockSpec((BLK, pw), lambda b, n: (b * nblk + n, 0)),
                  pl.BlockSpec(w_pool.shape, lambda b, n: (0, 0, 0)),
                  pl.BlockSpec((1, pw), lambda b, n: (0, 0))],
        out_specs=pl.BlockSpec((BLK, pw), lambda b, n: (b * nblk + n, 0)),
        out_shape=jax.ShapeDtypeStruct((p.shape[0], pw), F32),
        scratch_shapes=[pltpu.VMEM((16 + BLK, pw), F32)],
        compiler_params=_cparams("parallel", "arbitrary"), name="pool_prompt")(p, w_pool, s_pool)


def _pool_sample_body(u_ref, hist_ref, w_ref, s_ref, ybuf_ref, y_ref, ext_ref, *, gw, pos0):
    del ybuf_ref
    u = u_ref[...]
    ext_ref[0:1, :] = jnp.zeros((1, ext_ref.shape[1]), F32)
    ext_ref[1:16, :] = hist_ref[0]
    ext_ref[16:16 + DEC_T, :] = u
    pos = pos0 + lax.broadcasted_iota(jnp.int32, (DEC_T, 1), 0)
    y_ref[...] = _pool_windows(ext_ref, u, DEC_T, pos, w_ref, s_ref, gw)


def _sample_row_block(s, rows_per_seq):
    return (s // SLOTS) * (rows_per_seq // DEC_T) + s % SLOTS


def _pool_sample(p, hist, w_pool, s_pool, ybuf, rows_per_seq, pos0):
    nb = hist.shape[0]
    pw = hist.shape[2]
    gw = w_pool.shape[1]
    rowmap = lambda s: (_sample_row_block(s, rows_per_seq), 0)
    return pl.pallas_call(
        functools.partial(_pool_sample_body, gw=gw, pos0=pos0), grid=(nb,),
        in_specs=[pl.BlockSpec((DEC_T, pw), rowmap),
                  pl.BlockSpec((1, POOL_HIST, pw), lambda s: (s, 0, 0)),
                  pl.BlockSpec(w_pool.shape, lambda s: (0, 0, 0)),
                  pl.BlockSpec((1, pw), lambda s: (0, 0)),
                  pl.BlockSpec(memory_space=pl.ANY)],
        out_specs=pl.BlockSpec((DEC_T, pw), rowmap),
        out_shape=jax.ShapeDtypeStruct(ybuf.shape, F32),
        scratch_shapes=[pltpu.VMEM((16 + DEC_T, pw), F32)],
        input_output_aliases={4: 0},
        compiler_params=_cparams("arbitrary"), name="pool_sample")(p, hist, w_pool, s_pool, ybuf)


def _cumsum_lanes(x):
    lane = lax.broadcasted_iota(jnp.int32, x.shape, 1)
    s = 1
    while s < x.shape[1]:
        x = x + jnp.where(lane >= s, pltpu.roll(x, s, axis=1), 0.0)
        s *= 2
    return x


def _unit_lower_inverse(a):
    n = a.shape[0]
    ii = lax.broadcasted_iota(jnp.int32, (n, n), 0)
    jj = lax.broadcasted_iota(jnp.int32, (n, n), 1)
    x = jnp.where(ii == jj, 1.0, 0.0).astype(F32)
    s = 1
    while s < n:
        same_pair = (ii // (2 * s)) == (jj // (2 * s))
        lower_left = ((ii // s) % 2 == 1) & ((jj // s) % 2 == 0)
        off = jnp.where(same_pair & lower_left, a, 0.0)
        x = x - _dot_hi(x, _dot_hi(off, x))
        s *= 2
    return x


def _delta_chunk(xq, xk, xv, ba, valid, alog_ref, dtb_ref, s_ref, nh, dk):
    bat = ba.T
    beta = jnp.where(valid, 1.0 / (1.0 + jnp.exp(-bat[0:nh])), 0.0)
    z = bat[nh:2 * nh] + dtb_ref[...]
    softplus = jnp.maximum(z, 0.0) + jnp.log(1.0 + jnp.exp(-jnp.abs(z)))
    g = jnp.where(valid, -jnp.exp(alog_ref[...]) * softplus, 0.0)
    gc = _cumsum_lanes(g)
    glast = jnp.broadcast_to(gc[:, BLK - 1:BLK], gc.shape)
    eg = jnp.exp(gc)
    rows = jnp.concatenate(
        [gc, eg, beta, beta * eg, jnp.exp(glast - gc), jnp.exp(glast),
         jnp.zeros((BLK - 6 * nh, BLK), F32)], axis=0)
    cols = rows.T
    ii = lax.broadcasted_iota(jnp.int32, (BLK, BLK), 0)
    jj = lax.broadcasted_iota(jnp.int32, (BLK, BLK), 1)
    incl = ii >= jj
    strict = ii > jj
    outs = []
    for h in range(nh):
        sl = slice(h * dk, (h + 1) * dk)
        q = xq[:, sl]
        k = xk[:, sl]
        v = xv[:, sl]
        q = q * lax.rsqrt(jnp.sum(q * q, axis=-1, keepdims=True) + EPS) * (dk ** -0.5)
        k = k * lax.rsqrt(jnp.sum(k * k, axis=-1, keepdims=True) + EPS)
        col = lambda qi: cols[:, qi * nh + h:qi * nh + h + 1]
        diff = col(0) - gc[h:h + 1, :]
        decay = jnp.where(incl, jnp.exp(jnp.where(incl, diff, 0.0)), 0.0)
        a_mat = jnp.where(strict, _bdot_nt(k, k) * decay * col(2), 0.0)
        qk = _bdot_nt(q, k) * decay
        x = _unit_lower_inverse(a_mat)
        rhs = jnp.concatenate([col(3) * k, col(2) * v], axis=-1)
        wu = _dot_hi(x, rhs)
        w = wu[:, :dk]
        u = wu[:, dk:]
        s = s_ref[h]
        v_new = u - _bdot(w, s)
        o = col(1) * _bdot(q, s) + _bdot(qk, v_new)
        kd = (k * col(4)).T
        s_ref[h] = col(5) * s + _bdot(kd, v_new)
        outs.append(o)
    return outs


def _delta_out(o, gate, ng_ref):
    return o * lax.rsqrt(jnp.mean(o * o, axis=-1, keepdims=True) + EPS) * ng_ref[...] * _silu(gate)


def _conv_silu(ext_ref, w_ref, t):
    acc = ext_ref[5:5 + t, :] * w_ref[0:1, :]
    for i in range(1, DN_CONV):
        acc = acc + ext_ref[5 + i:5 + i + t, :] * w_ref[i:i + 1, :]
    return _silu(acc)


def _delta_prompt_body(q_ref, k_ref, v_ref, gt_ref, ba_ref, wq_ref, wk_ref, wv_ref, alog_ref, dtb_ref, ng_ref,
                       y_ref, sout_ref, eq_ref, ek_ref, ev_ref, s_ref, *, nh, dk, nblk):
    c = pl.program_id(1)

    @pl.when(c == 0)
    def _():
        for e in (eq_ref, ek_ref, ev_ref):
            e[0:8, :] = jnp.zeros((8, e.shape[1]), F32)
        s_ref[...] = jnp.zeros(s_ref.shape, F32)

    xs = []
    for x_ref, e_ref, w_ref in ((q_ref, eq_ref, wq_ref), (k_ref, ek_ref, wk_ref), (v_ref, ev_ref, wv_ref)):
        e_ref[8:8 + BLK, :] = x_ref[...]
        xs.append(_conv_silu(e_ref, w_ref, BLK))
        e_ref[0:8, :] = x_ref[BLK - 8:, :]
    lane = lax.broadcasted_iota(jnp.int32, (nh, BLK), 1)
    valid = (c > 0) | (lane >= FRONT)
    outs = _delta_chunk(xs[0], xs[1], xs[2], ba_ref[...], valid, alog_ref, dtb_ref, s_ref, nh, dk)
    row = c * BLK + lax.broadcasted_iota(jnp.int32, (BLK, 1), 0)
    for h in range(nh):
        sl = slice(h * dk, (h + 1) * dk)
        y_ref[:, sl] = jnp.where(row >= FRONT, _delta_out(outs[h], gt_ref[:, sl], ng_ref), 0.0)

    @pl.when(c == nblk - 1)
    def _():
        sout_ref[0] = s_ref[...]


def _delta_prompt(p, ba, w_conv, a_log, dt_bias, norm_g, nseq, nblk, col_q, col_gate, nh, dk):
    hw = nh * dk
    cq, ck, cv, cg = col_q // hw, col_q // hw + 1, col_q // hw + 2, col_gate // hw
    rows = lambda cb: pl.BlockSpec((BLK, hw), lambda b, c: (b * nblk + c, cb))
    wcs = lambda cb: pl.BlockSpec((DN_CONV, hw), lambda b, c: (0, cb))
    small = lambda shape: pl.BlockSpec(shape, lambda b, c: (0, 0))
    return pl.pallas_call(
        functools.partial(_delta_prompt_body, nh=nh, dk=dk, nblk=nblk), grid=(nseq, nblk),
        in_specs=[rows(cq), rows(ck), rows(cv), rows(cg),
                  pl.BlockSpec((BLK, BLK), lambda b, c: (b * nblk + c, 0)),
                  wcs(0), wcs(1), wcs(2), small((nh, 1)), small((nh, 1)), small((1, dk))],
        out_specs=[pl.BlockSpec((BLK, hw), lambda b, c: (b * nblk + c, 0)),
                   pl.BlockSpec((1, nh, dk, dk), lambda b, c: (b, 0, 0, 0))],
        out_shape=[jax.ShapeDtypeStruct((p.shape[0], hw), F32),
                   jax.ShapeDtypeStruct((nseq, nh, dk, dk), F32)],
        scratch_shapes=[pltpu.VMEM((8 + BLK, hw), F32)] * 3 + [pltpu.VMEM((nh, dk, dk), F32)],
        compiler_params=_cparams("parallel", "arbitrary"), name="delta_prompt",
    )(p, p, p, p, ba, w_conv, w_conv, w_conv, a_log.reshape(nh, 1), dt_bias.reshape(nh, 1),
      norm_g.reshape(1, dk))


def _delta_sample_body(q_ref, k_ref, v_ref, gt_ref, ba_ref, hq_ref, hk_ref, hv_ref, s0_ref,
                       wq_ref, wk_ref, wv_ref, alog_ref, dtb_ref, ng_ref, ybuf_ref,
                       y_ref, sout_ref, eq_ref, ek_ref, ev_ref, bap_ref, s_ref, *, nh, dk):
    del ybuf_ref
    xs = []
    for x_ref, h_ref, e_ref, w_ref in ((q_ref, hq_ref, eq_ref, wq_ref), (k_ref, hk_ref, ek_ref, wk_ref),
                                       (v_ref, hv_ref, ev_ref, wv_ref)):
        e_ref[...] = jnp.zeros(e_ref.shape, F32)
        e_ref[5:8, :] = h_ref[0]
        e_ref[8:8 + DEC_T, :] = x_ref[...]
        xs.append(_conv_silu(e_ref, w_ref, BLK))
    bap_ref[...] = jnp.zeros(bap_ref.shape, F32)
    bap_ref[0:DEC_T, :] = ba_ref[...]
    s_ref[...] = s0_ref[0]
    lane = lax.broadcasted_iota(jnp.int32, (nh, BLK), 1)
    outs = _delta_chunk(xs[0], xs[1], xs[2], bap_ref[...], lane < DEC_T, alog_ref, dtb_ref, s_ref, nh, dk)
    for h in range(nh):
        sl = slice(h * dk, (h + 1) * dk)
        y_ref[:, sl] = _delta_out(outs[h][0:DEC_T, :], gt_ref[:, sl], ng_ref)
    sout_ref[0] = s_ref[...]


def _delta_sample(p, ba, conv_hist, s0, w_conv, a_log, dt_bias, norm_g, ybuf, rows_per_seq,
                  col_q, col_gate, nh, dk):
    nb = s0.shape[0]
    hw = nh * dk
    cq, ck, cv, cg = col_q // hw, col_q // hw + 1, col_q // hw + 2, col_gate // hw
    rowmap = lambda cb: (lambda s: (_sample_row_block(s, rows_per_seq), cb))
    rows = lambda cb: pl.BlockSpec((DEC_T, hw), rowmap(cb))
    hist = lambda cb: pl.BlockSpec((1, DN_CONV - 1, hw), lambda s: (s, 0, cb))
    wcs = lambda cb: pl.BlockSpec((DN_CONV, hw), lambda s: (0, cb))
    small = lambda shape: pl.BlockSpec(shape, lambda s: (0, 0))
    return pl.pallas_call(
        functools.partial(_delta_sample_body, nh=nh, dk=dk), grid=(nb,),
        in_specs=[rows(cq), rows(ck), rows(cv), rows(cg), pl.BlockSpec((DEC_T, BLK), rowmap(0)),
                  hist(0), hist(1), hist(2),
                  pl.BlockSpec((1, nh, dk, dk), lambda s: (s, 0, 0, 0)),
                  wcs(0), wcs(1), wcs(2), small((nh, 1)), small((nh, 1)), small((1, dk)),
                  pl.BlockSpec(memory_space=pl.ANY)],
        out_specs=[pl.BlockSpec((DEC_T, hw), rowmap(0)),
                   pl.BlockSpec((1, nh, dk, dk), lambda s: (s, 0, 0, 0))],
        out_shape=[jax.ShapeDtypeStruct(ybuf.shape, F32), jax.ShapeDtypeStruct(s0.shape, F32)],
        scratch_shapes=[pltpu.VMEM((8 + BLK, hw), F32)] * 3
        + [pltpu.VMEM((BLK, BLK), F32), pltpu.VMEM((nh, dk, dk), F32)],
        input_output_aliases={15: 0},
        compiler_params=_cparams("arbitrary"), name="delta_sample",
    )(p, p, p, p, ba, conv_hist, conv_hist, conv_hist, s0, w_conv, w_conv, w_conv,
      a_log.reshape(nh, 1), dt_bias.reshape(nh, 1), norm_g.reshape(1, dk), ybuf)


def _sink_softmax_pv(parts, sink):
    m = sink
    for s, _ in parts:
        m = jnp.maximum(m, jnp.max(s, axis=-1, keepdims=True))
    den = jnp.exp(sink - m)
    acc = None
    for s, v in parts:
        p = jnp.exp(s - m)
        den = den + jnp.sum(p, axis=-1, keepdims=True)
        pv = _bdot(p, v)
        acc = pv if acc is None else acc + pv
    return acc / den


def _attn_prompt_body(q_ref, kp_ref, kc_ref, vp_ref, vc_ref, slope_ref, sink_ref, y_ref, *, grp, hd, window):
    kv = pl.program_id(1)
    n = pl.program_id(2)
    i = lax.broadcasted_iota(jnp.int32, (BLK, 2 * BLK), 0)
    j = lax.broadcasted_iota(jnp.int32, (BLK, 2 * BLK), 1)
    dist = BLK + i - j
    krow = (n - 1) * BLK + j
    valid = (dist >= 0) & (dist < window) & (krow >= FRONT)
    distf = dist.astype(F32)
    kk = jnp.concatenate([kp_ref[...], kc_ref[...]], axis=0)
    vv = jnp.concatenate([vp_ref[...], vc_ref[...]], axis=0)
    for g in range(grp):
        head = kv * grp + g
        q = q_ref[:, g * hd:(g + 1) * hd]
        s = _bdot_nt(q, kk) * (hd ** -0.5)
        s = jnp.where(valid, s - slope_ref[head] * distf, NEG_INF)
        y_ref[:, g * hd:(g + 1) * hd] = _sink_softmax_pv([(s, vv)], sink_ref[head])


def _attn_prompt(p, slopes, sinks, nseq, nblk, col_q, col_k, col_v, nkv, grp, hd, window):
    gw = grp * hd
    cq, ck, cv = col_q // gw, col_k // hd, col_v // hd
    prev = lambda base: (lambda b, kv, n: (b * nblk + jnp.maximum(n - 1, 0), base + kv))
    cur = lambda base: (lambda b, kv, n: (b * nblk + n, base + kv))
    smem = pl.BlockSpec(memory_space=pltpu.SMEM)
    return pl.pallas_call(
        functools.partial(_attn_prompt_body, grp=grp, hd=hd, window=window), grid=(nseq, nkv, nblk),
        in_specs=[pl.BlockSpec((BLK, gw), cur(cq)),
                  pl.BlockSpec((BLK, hd), prev(ck)), pl.BlockSpec((BLK, hd), cur(ck)),
                  pl.BlockSpec((BLK, hd), prev(cv)), pl.BlockSpec((BLK, hd), cur(cv)),
                  smem, smem],
        out_specs=pl.BlockSpec((BLK, gw), cur(0)),
        out_shape=jax.ShapeDtypeStruct((p.shape[0], nkv * gw), F32),
        compiler_params=_cparams("parallel", "parallel", "arbitrary"), name="attn_prompt",
    )(p, p, p, p, p, slopes, sinks)


def _attn_sample_body(q_ref, k_ref, v_ref, ck_ref, cv_ref, slope_ref, sink_ref, ybuf_ref, y_ref,
                      *, grp, hd, window):
    del ybuf_ref
    kv = pl.program_id(1)
    wc = ck_ref.shape[1]
    i = lax.broadcasted_iota(jnp.int32, (DEC_T, wc), 0)
    j = lax.broadcasted_iota(jnp.int32, (DEC_T, wc), 1)
    dist_c = wc + i - j
    valid_c = (dist_c >= 0) & (dist_c < window)
    i2 = lax.broadcasted_iota(jnp.int32, (DEC_T, DEC_T), 0)
    j2 = lax.broadcasted_iota(jnp.int32, (DEC_T, DEC_T), 1)
    dist_n = i2 - j2
    valid_n = (dist_n >= 0) & (dist_n < window)
    kc, vc, kn, vn = ck_ref[0], cv_ref[0], k_ref[...], v_ref[...]
    for g in range(grp):
        head = kv * grp + g
        q = q_ref[:, g * hd:(g + 1) * hd]
        sc = _bdot_nt(q, kc) * (hd ** -0.5)
        sc = jnp.where(valid_c, sc - slope_ref[head] * dist_c.astype(F32), NEG_INF)
        sn = _bdot_nt(q, kn) * (hd ** -0.5)
        sn = jnp.where(valid_n, sn - slope_ref[head] * dist_n.astype(F32), NEG_INF)
        y_ref[:, g * hd:(g + 1) * hd] = _sink_softmax_pv([(sc, vc), (sn, vn)], sink_ref[head])


def _attn_sample(p, cache_k, cache_v, slopes, sinks, ybuf, rows_per_seq, col_q, col_k, col_v,
                 nkv, grp, hd, window):
    nb, wc = cache_k.shape[0], cache_k.shape[1]
    gw = grp * hd
    ck3 = cache_k.reshape(nb, wc, nkv * hd)
    cv3 = cache_v.reshape(nb, wc, nkv * hd)
    rowmap = lambda base: (lambda s, kv: (_sample_row_block(s, rows_per_seq), base + kv))
    smem = pl.BlockSpec(memory_space=pltpu.SMEM)
    return pl.pallas_call(
        functools.partial(_attn_sample_body, grp=grp, hd=hd, window=window), grid=(nb, nkv),
        in_specs=[pl.BlockSpec((DEC_T, gw), rowmap(col_q // gw)),
                  pl.BlockSpec((DEC_T, hd), rowmap(col_k // hd)),
                  pl.BlockSpec((DEC_T, hd), rowmap(col_v // hd)),
                  pl.BlockSpec((1, wc, hd), lambda s, kv: (s, 0, kv)),
                  pl.BlockSpec((1, wc, hd), lambda s, kv: (s, 0, kv)),
                  smem, smem, pl.BlockSpec(memory_space=pl.ANY)],
        out_specs=pl.BlockSpec((DEC_T, gw), rowmap(0)),
        out_shape=jax.ShapeDtypeStruct(ybuf.shape, F32),
        input_output_aliases={7: 0},
        compiler_params=_cparams("arbitrary", "arbitrary"), name="attn_sample",
    )(p, p, p, ck3, cv3, slopes, sinks, ybuf)


def _top_values(s, k):
    riota = lax.broadcasted_iota(jnp.int32, s.shape, 0)
    vals = []
    for _ in range(k):
        m = jnp.max(s, axis=0, keepdims=True)
        vals.append(m)
        first = jnp.min(jnp.where(s == m, riota, s.shape[0]), axis=0, keepdims=True)
        s = jnp.where(riota == first, LOWEST, s)
    return jnp.concatenate(vals, axis=0)


def _peer_topk_body(q_ref, keys_ref, s1_ref, s2_ref, st_ref, *, nh):
    thr, mx, rz = [], [], []
    for h in range(nh):
        halves = []
        for half, out_ref in ((0, s1_ref), (1, s2_ref)):
            hp = 2 * h + half
            sc = _bdot_nt(keys_ref[hp], q_ref[:, hp * NKEYS:(hp + 1) * NKEYS])
            out_ref[h] = sc
            halves.append(_top_values(sc, TOPK))
        a, b = halves
        cand = [a[0:1] + b, a[8:16] + b[0:1]] + [a[i:i + 1] + b[0:8] for i in range(1, 8)]
        top = _top_values(jnp.concatenate(cand, axis=0), TOPK)
        m = top[0:1]
        thr.append(top[TOPK - 1:TOPK])
        mx.append(m)
        rz.append(1.0 / jnp.sum(jnp.exp(top - m), axis=0, keepdims=True))
    st_ref[...] = jnp.concatenate(thr + mx + rz, axis=0)


def _peer_topk(q, sub_keys, *, tm=256):
    n = q.shape[0]
    nh = sub_keys.shape[0]
    keys = sub_keys.reshape(2 * nh, NKEYS, sub_keys.shape[-1])
    sspec = pl.BlockSpec((nh, NKEYS, tm), lambda i: (0, 0, i))
    return pl.pallas_call(
        functools.partial(_peer_topk_body, nh=nh), grid=(n // tm,),
        in_specs=[pl.BlockSpec((tm, q.shape[1]), lambda i: (i, 0)),
                  pl.BlockSpec(keys.shape, lambda i: (0, 0, 0))],
        out_specs=[sspec, sspec, pl.BlockSpec((3 * nh, tm), lambda i: (0, i))],
        out_shape=[jax.ShapeDtypeStruct((nh, NKEYS, n), F32), jax.ShapeDtypeStruct((nh, NKEYS, n), F32),
                   jax.ShapeDtypeStruct((3 * nh, n), F32)],
        compiler_params=_cparams("parallel"), name="peer_topk")(q, keys)


def _gelu(x):
    return 0.5 * x * (1.0 + lax.erf(x * (2.0 ** -0.5)))


def _peer_expert_body(xn_ref, wd_ref, wu_ref, s1_ref, s2_ref, st_ref, y_ref, *, nh, ei):
    c = pl.program_id(1)

    @pl.when(c == 0)
    def _():
        y_ref[...] = jnp.zeros(y_ref.shape, F32)

    ht = lax.dot_general(wd_ref[...], xn_ref[...], (((1,), (1,)), ((), ())),
                         preferred_element_type=F32)
    act = _gelu(ht)
    pieces = []
    for ii in range(ei):
        i1 = c * ei + ii
        gate = jnp.zeros((NKEYS, ht.shape[1]), F32)
        for h in range(nh):
            ssum = s1_ref[h, pl.ds(i1, 1), :] + s2_ref[h]
            w = jnp.exp(ssum - st_ref[nh + h:nh + h + 1, :]) * st_ref[2 * nh + h:2 * nh + h + 1, :]
            gate = gate + jnp.where(ssum >= st_ref[h:h + 1, :], w, 0.0)
        pieces.append(act[ii * NKEYS:(ii + 1) * NKEYS, :] * gate)
    at = jnp.concatenate(pieces, axis=0)
    y_ref[...] += jnp.dot(at.T.astype(BF16), wu_ref[...], preferred_element_type=F32)


def _peer_expert(xn, w_down, w_up, s1t, s2t, stats, *, tm=512, ei=2):
    n, d = xn.shape
    nh = s1t.shape[0]
    e = ei * NKEYS
    sspec = pl.BlockSpec((nh, NKEYS, tm), lambda i, c: (0, 0, i))
    return pl.pallas_call(
        functools.partial(_peer_expert_body, nh=nh, ei=ei), grid=(n // tm, w_down.shape[0] // e),
        in_specs=[pl.BlockSpec((tm, d), lambda i, c: (i, 0)),
                  pl.BlockSpec((e, d), lambda i, c: (c, 0)),
                  pl.BlockSpec((e, d), lambda i, c: (c, 0)),
                  sspec, sspec, pl.BlockSpec((3 * nh, tm), lambda i, c: (0, i))],
        out_specs=pl.BlockSpec((tm, d), lambda i, c: (i, 0)),
        out_shape=jax.ShapeDtypeStruct((n, d), F32),
        compiler_params=_cparams("parallel", "arbitrary"), name="peer_expert",
    )(xn, w_down, w_up, s1t, s2t, stats)


def _final_norm(h, y, g, nseq, nblk, row_blk, first_blk, nblk_out):
    d = h.shape[1]
    row = pl.BlockSpec((row_blk, d), lambda b, i: (b * nblk + first_blk + i, 0))
    return pl.pallas_call(
        _final_norm_body, grid=(nseq, nblk_out),
        in_specs=[row, row, pl.BlockSpec((1, d), lambda b, i: (0, 0))],
        out_specs=pl.BlockSpec((1, row_blk, d), lambda b, i: (b, i, 0)),
        out_shape=jax.ShapeDtypeStruct((nseq, nblk_out * row_blk, d), F32),
        compiler_params=_cparams("parallel", "parallel"), name="final_norm",
    )(h, y, g.reshape(1, d).astype(F32))


def _final_norm_body(h_ref, y_ref, g_ref, o_ref):
    x = h_ref[...] + y_ref[...]
    ms = jnp.mean(x * x, axis=-1, keepdims=True)
    o_ref[0] = x * lax.rsqrt(ms + EPS) * g_ref[...]


def kernel(x_prompt, x_sample, state_pool, state_conv, state_delta, cache_k, cache_v, meta_tokens, norm1_g,
           w_in, w_pool, s_pool, w_conv, a_log, dt_bias, dn_norm_g, attn_sinks, w_out, norm2_g,
           peer_w_query, peer_sub_keys, peer_w_down, peer_w_up, final_norm_g):
    nseq, seq, d = x_prompt.shape
    nsamp, dec_t, _ = x_sample.shape
    depth = w_in.shape[0]
    pool_w = w_pool.shape[1] * w_pool.shape[2]
    dn_qkv = w_conv.shape[2]
    nh, dk = state_delta.shape[2], state_delta.shape[3]
    wc, nkv, hd = cache_k.shape[2], cache_k.shape[3], cache_k.shape[4]
    nq = attn_sinks.shape[1]
    grp = nq // nkv
    window = wc
    assert dec_t == DEC_T and nsamp == nseq * SLOTS and (N_META + seq) % BLK == N_META
    assert SLOTS * DEC_T + POOL_HIST <= FRONT and dn_qkv == 3 * nh * dk and wc == BLK
    rows_per_seq = FRONT + N_META + seq
    nblk = rows_per_seq // BLK
    past_len = 16384
    col_qkv = pool_w
    col_gate = col_qkv + dn_qkv
    col_q = col_gate + nh * dk
    col_k = col_q + nq * hd
    col_v = col_k + nkv * hd
    src_ba = pool_w + dn_qkv

    xs = x_sample.reshape(nseq, SLOTS * DEC_T, d)
    zeros = jnp.zeros((nseq, FRONT - SLOTS * DEC_T, d), F32)
    meta = jnp.broadcast_to(meta_tokens[None], (nseq, N_META, d))
    h = jnp.concatenate([xs, zeros, meta, x_prompt], axis=1).reshape(nseq * rows_per_seq, d)

    slopes = jnp.exp2(-8.0 * (jnp.arange(nq, dtype=F32) + 1.0) / nq)
    new_p = [[] for _ in range(5)]
    new_s = [[] for _ in range(5)]
    y_peer = None
    for l in range(depth):
        w_main = jnp.concatenate([w_in[l][:, :src_ba], w_in[l][:, src_ba + 2 * nh:]], axis=1).astype(BF16)
        w_ba = jnp.pad(w_in[l][:, src_ba:src_ba + 2 * nh], ((0, 0), (0, BLK - 2 * nh))).astype(BF16)
        h, xn = _addnorm(h, y_peer, norm1_g[l])
        p = _mm(xn, w_main)
        ba = _mm(xn, w_ba)

        y_pool = _pool_prompt(p, w_pool[l], s_pool[l].reshape(1, pool_w), nseq, nblk)
        y_pool = _pool_sample(p, state_pool[l], w_pool[l], s_pool[l].reshape(1, pool_w), y_pool,
                              rows_per_seq, past_len)
        y_dn, s_p = _delta_prompt(p, ba, w_conv[l], a_log[l], dt_bias[l], dn_norm_g[l], nseq, nblk,
                                  col_qkv, col_gate, nh, dk)
        y_dn, s_s = _delta_sample(p, ba, state_conv[l], state_delta[l], w_conv[l], a_log[l], dt_bias[l],
                                  dn_norm_g[l], y_dn, rows_per_seq, col_qkv, col_gate, nh, dk)
        y_att = _attn_prompt(p, slopes, attn_sinks[l], nseq, nblk, col_q, col_k, col_v, nkv, grp, hd, window)
        y_att = _attn_sample(p, cache_k[l], cache_v[l], slopes, attn_sinks[l], y_att, rows_per_seq,
                             col_q, col_k, col_v, nkv, grp, hd, window)
        h = _outproj(y_pool, y_dn, y_att, w_out[l].astype(BF16), h)

        _, xn2 = _addnorm(h, None, norm2_g[l])
        q = _mm(xn2, peer_w_query[l].astype(BF16))
        s1t, s2t, stats = _peer_topk(q, peer_sub_keys[l])
        y_peer = _peer_expert(xn2, peer_w_down[l].astype(BF16), peer_w_up[l].astype(BF16), s1t, s2t, stats)

        p3 = p.reshape(nseq, rows_per_seq, p.shape[1])
        ps = p3[:, :SLOTS * DEC_T].reshape(nsamp, DEC_T, p.shape[1])
        new_p[0].append(p3[:, -POOL_HIST:, :pool_w])
        new_s[0].append(jnp.concatenate([state_pool[l], ps[:, :, :pool_w]], axis=1)[:, -POOL_HIST:])
        new_p[1].append(p3[:, -(DN_CONV - 1):, col_qkv:col_gate])
        new_s[1].append(jnp.concatenate([state_conv[l], ps[:, :, col_qkv:col_gate]], axis=1)[:, -(DN_CONV - 1):])
        new_p[2].append(s_p)
        new_s[2].append(s_s)
        new_p[3].append(p3[:, -window:, col_k:col_v].reshape(nseq, window, nkv, hd))
        new_s[3].append(jnp.concatenate([cache_k[l], ps[:, :, col_k:col_v].reshape(nsamp, DEC_T, nkv, hd)],
                                        axis=1)[:, -wc:])
        new_p[4].append(p3[:, -window:, col_v:].reshape(nseq, window, nkv, hd))
        new_s[4].append(jnp.concatenate([cache_v[l], ps[:, :, col_v:].reshape(nsamp, DEC_T, nkv, hd)],
                                        axis=1)[:, -wc:])

    y_prompt = _final_norm(h, y_peer, final_norm_g, nseq, nblk, BLK, 1, nblk - 1)
    y_sample = _final_norm(h, y_peer, final_norm_g, nseq, rows_per_seq // (SLOTS * DEC_T), SLOTS * DEC_T, 0, 1)
    y_sample = y_sample.reshape(nsamp, DEC_T, d)
    pool_p, conv_p, delta_p, k_p, v_p = (jnp.stack(a) for a in new_p)
    pool_s, conv_s, delta_s, k_s, v_s = (jnp.stack(a) for a in new_s)
    return (y_prompt, y_sample, pool_p, pool_s, conv_p, conv_s, delta_p, delta_s, k_p, k_s, v_p, v_s)
```

```python
import functools

import jax
import jax.numpy as jnp
from jax import lax
from jax.experimental import pallas as pl
from jax.experimental.pallas import tpu as pltpu

F32 = jnp.float32
BF16 = jnp.bfloat16

EPS = 1e-6
NEG_INF = -1e30
LOWEST = -3.0e38

N_META = 16
BLK = 128
FRONT = BLK - N_META
DEC_T = 8
SLOTS = 8
POOL_WINDOWS = (2, 4, 8, 16)
POOL_HIST = 15
DN_CONV = 4
TOPK = 16
NKEYS = 128
VMEM_LIMIT = 56 * 1024 * 1024


def _cparams(*sem):
    return pltpu.CompilerParams(dimension_semantics=sem, vmem_limit_bytes=VMEM_LIMIT)


def _bdot(a, b):
    return jnp.dot(a.astype(BF16), b.astype(BF16), preferred_element_type=F32)


def _bdot_nt(a, b):
    return lax.dot_general(a.astype(BF16), b.astype(BF16), (((1,), (1,)), ((), ())),
                           preferred_element_type=F32)


def _silu(x):
    return x * (1.0 / (1.0 + jnp.exp(-x)))


def _addnorm_body(*refs, add):
    if add:
        h_ref, y_ref, g_ref, hs_ref, xn_ref = refs
        x = h_ref[...] + y_ref[...]
        hs_ref[...] = x
    else:
        h_ref, g_ref, xn_ref = refs
        x = h_ref[...]
    ms = jnp.mean(x * x, axis=-1, keepdims=True)
    xn_ref[...] = (x * lax.rsqrt(ms + EPS) * g_ref[...]).astype(xn_ref.dtype)


def _addnorm(h, y, g, *, tm=256, out_dtype=BF16):
    n, d = h.shape
    row = pl.BlockSpec((tm, d), lambda i: (i, 0))
    gspec = pl.BlockSpec((1, d), lambda i: (0, 0))
    g2 = g.reshape(1, d).astype(F32)
    if y is None:
        xn = pl.pallas_call(
            functools.partial(_addnorm_body, add=False),
            grid=(n // tm,), in_specs=[row, gspec], out_specs=row,
            out_shape=jax.ShapeDtypeStruct((n, d), out_dtype),
            compiler_params=_cparams("parallel"), name="norm")(h, g2)
        return h, xn
    hs, xn = pl.pallas_call(
        functools.partial(_addnorm_body, add=True),
        grid=(n // tm,), in_specs=[row, row, gspec], out_specs=[row, row],
        out_shape=[jax.ShapeDtypeStruct((n, d), F32), jax.ShapeDtypeStruct((n, d), out_dtype)],
        compiler_params=_cparams("parallel"), name="add_norm")(h, y, g2)
    return hs, xn


def _mm_body(x_ref, w_ref, o_ref):
    o_ref[...] = jnp.dot(x_ref[...], w_ref[...], preferred_element_type=F32)


def _mm(x, w, *, tm=512, tn=1024):
    m, k = x.shape
    n = w.shape[1]
    tn = min(tn, n)
    return pl.pallas_call(
        _mm_body, grid=(n // tn, m // tm),
        in_specs=[pl.BlockSpec((tm, k), lambda j, i: (i, 0)),
                  pl.BlockSpec((k, tn), lambda j, i: (0, j))],
        out_specs=pl.BlockSpec((tm, tn), lambda j, i: (i, j)),
        out_shape=jax.ShapeDtypeStruct((m, n), F32),
        compiler_params=_cparams("parallel", "parallel"), name="matmul")(x, w)


def _outproj_body(yp_ref, yd_ref, ya_ref, w_ref, h_ref, o_ref, *, wp, wd):
    acc = h_ref[...]
    acc += jnp.dot(yp_ref[...].astype(BF16), w_ref[0:wp, :], preferred_element_type=F32)
    acc += jnp.dot(yd_ref[...].astype(BF16), w_ref[wp:wp + wd, :], preferred_element_type=F32)
    acc += jnp.dot(ya_ref[...].astype(BF16), w_ref[wp + wd:, :], preferred_element_type=F32)
    o_ref[...] = acc


def _outproj(yp, yd, ya, w, h, *, tm=512, tn=1024):
    m, d = h.shape
    wp, wd, wa = yp.shape[1], yd.shape[1], ya.shape[1]
    k = wp + wd + wa
    return pl.pallas_call(
        functools.partial(_outproj_body, wp=wp, wd=wd), grid=(d // tn, m // tm),
        in_specs=[pl.BlockSpec((tm, wp), lambda j, i: (i, 0)),
                  pl.BlockSpec((tm, wd), lambda j, i: (i, 0)),
                  pl.BlockSpec((tm, wa), lambda j, i: (i, 0)),
                  pl.BlockSpec((k, tn), lambda j, i: (0, j)),
                  pl.BlockSpec((tm, tn), lambda j, i: (i, j))],
        out_specs=pl.BlockSpec((tm, tn), lambda j, i: (i, j)),
        out_shape=jax.ShapeDtypeStruct((m, d), F32),
        compiler_params=_cparams("parallel", "parallel"), name="out_proj")(yp, yd, ya, w, h)


def _pool_windows(ext_ref, u, t, pos, w_ref, s_ref, gw):
    outs = []
    for gi, w in enumerate(POOL_WINDOWS):
        sl = slice(gi * gw, (gi + 1) * gw)
        win = u[:, sl]
        for k in range(1, w):
            win = win + ext_ref[16 - k:16 - k + t, sl]
        cnt = jnp.clip(pos + 1, 1, w).astype(F32)
        d = win / cnt - u[:, sl]
        outs.append(_bdot(d, w_ref[gi]))
    return jnp.concatenate(outs, axis=-1) * s_ref[...]


def _pool_prompt_body(u_ref, w_ref, s_ref, y_ref, ext_ref, *, gw):
    n = pl.program_id(1)

    @pl.when(n == 0)
    def _():
        ext_ref[0:16, :] = jnp.zeros((16, ext_ref.shape[1]), F32)

    u = u_ref[...]
    ext_ref[16:16 + BLK, :] = u
    row = n * BLK + lax.broadcasted_iota(jnp.int32, (BLK, 1), 0)
    pos = row - FRONT
    y = _pool_windows(ext_ref, u, BLK, pos, w_ref, s_ref, gw)
    y_ref[...] = jnp.where(pos >= 0, y, 0.0)
    ext_ref[0:16, :] = u[BLK - 16:, :]


def _pool_prompt(p, w_pool, s_pool, nseq, nblk):
    pw = w_pool.shape[0] * w_pool.shape[1]
    gw = w_pool.shape[1]
    return pl.pallas_call(
        functools.partial(_pool_prompt_body, gw=gw), grid=(nseq, nblk),
        in_specs=[pl.BlockSpec((BLK, pw), lambda b, n: (b * nblk + n, 0)),
                  pl.BlockSpec(w_pool.shape, lambda b, n: (0, 0, 0)),
                  pl.BlockSpec((1, pw), lambda b, n: (0, 0))],
        out_specs=pl.BlockSpec((BLK, pw), lambda b, n: (b * nblk + n, 0)),
        out_shape=jax.ShapeDtypeStruct((p.shape[0], pw), F32),
        scratch_shapes=[pltpu.VMEM((16 + BLK, pw), F32)],
        compiler_params=_cparams("parallel", "arbitrary"), name="pool_prompt")(p, w_pool, s_pool)


def _pool_sample_body(u_ref, hist_ref, w_ref, s_ref, ybuf_ref, y_ref, ext_ref, *, gw, pos0):
    del ybuf_ref
    u = u_ref[...]
    ext_ref[0:1, :] = jnp.zeros((1, ext_ref.shape[1]), F32)
    ext_ref[1:16, :] = hist_ref[0]
    ext_ref[16:16 + DEC_T, :] = u
    pos = pos0 + lax.broadcasted_iota(jnp.int32, (DEC_T, 1), 0)
    y_ref[...] = _pool_windows(ext_ref, u, DEC_T, pos, w_ref, s_ref, gw)


def _sample_row_block(s, rows_per_seq):
    return (s // SLOTS) * (rows_per_seq // DEC_T) + s % SLOTS


def _pool_sample(p, hist, w_pool, s_pool, ybuf, rows_per_seq, pos0):
    nb = hist.shape[0]
    pw = hist.shape[2]
    gw = w_pool.shape[1]
    rowmap = lambda s: (_sample_row_block(s, rows_per_seq), 0)
    return pl.pallas_call(
        functools.partial(_pool_sample_body, gw=gw, pos0=pos0), grid=(nb,),
        in_specs=[pl.BlockSpec((DEC_T, pw), rowmap),
                  pl.BlockSpec((1, POOL_HIST, pw), lambda s: (s, 0, 0)),
                  pl.BlockSpec(w_pool.shape, lambda s: (0, 0, 0)),
                  pl.BlockSpec((1, pw), lambda s: (0, 0)),
                  pl.BlockSpec(memory_space=pl.ANY)],
        out_specs=pl.BlockSpec((DEC_T, pw), rowmap),
        out_shape=jax.ShapeDtypeStruct(ybuf.shape, F32),
        scratch_shapes=[pltpu.VMEM((16 + DEC_T, pw), F32)],
        input_output_aliases={4: 0},
        compiler_params=_cparams("arbitrary"), name="pool_sample")(p, hist, w_pool, s_pool, ybuf)


def _cumsum_lanes(x):
    lane = lax.broadcasted_iota(jnp.int32, x.shape, 1)
    s = 1
    while s < x.shape[1]:
        x = x + jnp.where(lane >= s, pltpu.roll(x, s, axis=1), 0.0)
        s *= 2
    return x


HEAD_PACK = 2


def _unit_lower_inverse(a, ii, jj):
    x = jnp.where(ii == jj, 1.0, 0.0).astype(F32)
    s = 1
    while s < BLK:
        same_pair = (ii // (2 * s)) == (jj // (2 * s))
        lower_left = ((ii // s) % 2 == 1) & ((jj // s) % 2 == 0)
        off = jnp.where(same_pair & lower_left, a, 0.0)
        x = x - _bdot(x, _bdot(off, x))
        s *= 2
    return x


def _delta_scalars(ba, valid, alog_ref, dtb_ref, nh):
    bat = ba.T
    beta = jnp.where(valid, 1.0 / (1.0 + jnp.exp(-bat[0:nh])), 0.0)
    z = bat[nh:2 * nh] + dtb_ref[...]
    softplus = jnp.maximum(z, 0.0) + jnp.log(1.0 + jnp.exp(-jnp.abs(z)))
    g = jnp.where(valid, -jnp.exp(alog_ref[...]) * softplus, 0.0)
    gc = _cumsum_lanes(g)
    glast = jnp.broadcast_to(gc[:, BLK - 1:BLK], gc.shape)
    eg = jnp.exp(gc)
    rows = jnp.concatenate(
        [gc, eg, beta, beta * eg, jnp.exp(glast - gc), jnp.exp(glast),
         jnp.zeros((BLK - 6 * nh, BLK), F32)], axis=0)
    return gc, rows.T


def _l2n(x):
    return x * lax.rsqrt(jnp.sum(x * x, axis=-1, keepdims=True) + EPS)


def _delta_chunk_small(xq, xk, xv, ba, alog_ref, dtb_ref, s_ref, nh, dk):
    t = xq.shape[0]
    zrows = jnp.zeros((BLK - t, BLK), F32)
    lane = lax.broadcasted_iota(jnp.int32, (nh, BLK), 1)
    gc, cols = _delta_scalars(jnp.concatenate([ba, zrows], axis=0), lane < t, alog_ref, dtb_ref, nh)
    ii = lax.broadcasted_iota(jnp.int32, (t, BLK), 0)
    jj = lax.broadcasted_iota(jnp.int32, (t, BLK), 1)
    incl = ii >= jj
    strict = ii > jj
    outs = []
    for h in range(nh):
        sl = slice(h * dk, (h + 1) * dk)
        q = _l2n(xq[:, sl]) * (dk ** -0.5)
        k = _l2n(xk[:, sl])
        v = xv[:, sl]
        col = lambda qi: cols[0:t, qi * nh + h:qi * nh + h + 1]
        kpad = jnp.concatenate([k, zrows], axis=0)
        diff = col(0) - gc[h:h + 1, :]
        decay = jnp.where(incl, jnp.exp(jnp.where(incl, diff, 0.0)), 0.0)
        a_mat = jnp.where(strict, _bdot_nt(k, kpad) * decay * col(2), 0.0)
        qk = _bdot_nt(q, kpad) * decay
        wu = jnp.concatenate([col(3) * k, col(2) * v], axis=-1)
        for j in range(t - 1):
            wu = wu - a_mat[:, j:j + 1] * wu[j:j + 1, :]
        s = s_ref[h]
        v_new = wu[:, dk:] - _bdot(wu[:, :dk], s)
        o = col(1) * _bdot(q, s)
        for j in range(t):
            o = o + qk[:, j:j + 1] * v_new[j:j + 1, :]
        kd = jnp.concatenate([k * col(4), zrows], axis=0).T
        s_ref[h] = cols[:, 5 * nh + h:5 * nh + h + 1] * s + _bdot(kd, jnp.concatenate([v_new, zrows], axis=0))
        outs.append(o)
    return outs


def _delta_chunk(xq, xk, xv, ba, valid, alog_ref, dtb_ref, s_ref, nh, dk):
    assert dk == BLK and nh % HEAD_PACK == 0
    gc, cols = _delta_scalars(ba, valid, alog_ref, dtb_ref, nh)
    n = HEAD_PACK * BLK
    ii = lax.broadcasted_iota(jnp.int32, (n, n), 0)
    jj = lax.broadcasted_iota(jnp.int32, (n, n), 1)
    same = (ii // BLK) == (jj // BLK)
    incl = same & (ii >= jj)
    strict = same & (ii > jj)
    outs = []
    for p in range(nh // HEAD_PACK):
        heads = range(p * HEAD_PACK, (p + 1) * HEAD_PACK)
        stack = lambda f: jnp.concatenate([f(h) for h in heads], axis=0)
        q = stack(lambda h: _l2n(xq[:, h * dk:(h + 1) * dk]) * (dk ** -0.5))
        k = stack(lambda h: _l2n(xk[:, h * dk:(h + 1) * dk]))
        v = stack(lambda h: xv[:, h * dk:(h + 1) * dk])
        col = lambda qi: stack(lambda h: cols[:, qi * nh + h:qi * nh + h + 1])
        diag = lambda m: jnp.where(same, jnp.concatenate([m] * HEAD_PACK, axis=1), 0.0)
        diff = col(0) - jnp.concatenate([gc[h:h + 1, :] for h in heads], axis=1)
        decay = jnp.where(incl, jnp.exp(jnp.where(incl, diff, 0.0)), 0.0)
        a_mat = jnp.where(strict, _bdot_nt(k, k) * decay * col(2), 0.0)
        qk = _bdot_nt(q, k) * decay
        x = _unit_lower_inverse(a_mat, ii, jj)
        rhs = jnp.concatenate([col(3) * k, col(2) * v], axis=-1)
        wu = _bdot(x, rhs)
        w = wu[:, :dk]
        u = wu[:, dk:]
        s = stack(lambda h: s_ref[h])
        v_new = u - _bdot(diag(w), s)
        o = col(1) * _bdot(diag(q), s) + _bdot(qk, v_new)
        kdt = (k * col(4)).T
        kd = jnp.where(same, jnp.concatenate([kdt] * HEAD_PACK, axis=0), 0.0)
        s_new = col(5) * s + _bdot(kd, v_new)
        for i, h in enumerate(heads):
            s_ref[h] = s_new[i * BLK:(i + 1) * BLK, :]
            outs.append(o[i * BLK:(i + 1) * BLK, :])
    return outs


def _delta_out(o, gate, ng_ref):
    return o * lax.rsqrt(jnp.mean(o * o, axis=-1, keepdims=True) + EPS) * ng_ref[...] * _silu(gate)


def _conv_silu(ext_ref, w_ref, t):
    acc = ext_ref[5:5 + t, :] * w_ref[0:1, :]
    for i in range(1, DN_CONV):
        acc = acc + ext_ref[5 + i:5 + i + t, :] * w_ref[i:i + 1, :]
    return _silu(acc)


def _delta_prompt_body(q_ref, k_ref, v_ref, gt_ref, ba_ref, wq_ref, wk_ref, wv_ref, alog_ref, dtb_ref, ng_ref,
                       y_ref, sout_ref, eq_ref, ek_ref, ev_ref, s_ref, *, nh, dk, nblk):
    c = pl.program_id(1)

    @pl.when(c == 0)
    def _():
        for e in (eq_ref, ek_ref, ev_ref):
            e[0:8, :] = jnp.zeros((8, e.shape[1]), F32)
        s_ref[...] = jnp.zeros(s_ref.shape, F32)

    xs = []
    for x_ref, e_ref, w_ref in ((q_ref, eq_ref, wq_ref), (k_ref, ek_ref, wk_ref), (v_ref, ev_ref, wv_ref)):
        e_ref[8:8 + BLK, :] = x_ref[...]
        xs.append(_conv_silu(e_ref, w_ref, BLK))
        e_ref[0:8, :] = x_ref[BLK - 8:, :]
    lane = lax.broadcasted_iota(jnp.int32, (nh, BLK), 1)
    valid = (c > 0) | (lane >= FRONT)
    outs = _delta_chunk(xs[0], xs[1], xs[2], ba_ref[...], valid, alog_ref, dtb_ref, s_ref, nh, dk)
    row = c * BLK + lax.broadcasted_iota(jnp.int32, (BLK, 1), 0)
    for h in range(nh):
        sl = slice(h * dk, (h + 1) * dk)
        y_ref[:, sl] = jnp.where(row >= FRONT, _delta_out(outs[h], gt_ref[:, sl], ng_ref), 0.0)

    @pl.when(c == nblk - 1)
    def _():
        sout_ref[0] = s_ref[...]


def _delta_prompt(p, ba, w_conv, a_log, dt_bias, norm_g, nseq, nblk, col_q, col_gate, nh, dk):
    hw = nh * dk
    cq, ck, cv, cg = col_q // hw, col_q // hw + 1, col_q // hw + 2, col_gate // hw
    rows = lambda cb: pl.BlockSpec((BLK, hw), lambda b, c: (b * nblk + c, cb))
    wcs = lambda cb: pl.BlockSpec((DN_CONV, hw), lambda b, c: (0, cb))
    small = lambda shape: pl.BlockSpec(shape, lambda b, c: (0, 0))
    return pl.pallas_call(
        functools.partial(_delta_prompt_body, nh=nh, dk=dk, nblk=nblk), grid=(nseq, nblk),
        in_specs=[rows(cq), rows(ck), rows(cv), rows(cg),
                  pl.BlockSpec((BLK, BLK), lambda b, c: (b * nblk + c, 0)),
                  wcs(0), wcs(1), wcs(2), small((nh, 1)), small((nh, 1)), small((1, dk))],
        out_specs=[pl.BlockSpec((BLK, hw), lambda b, c: (b * nblk + c, 0)),
                   pl.BlockSpec((1, nh, dk, dk), lambda b, c: (b, 0, 0, 0))],
        out_shape=[jax.ShapeDtypeStruct((p.shape[0], hw), F32),
                   jax.ShapeDtypeStruct((nseq, nh, dk, dk), F32)],
        scratch_shapes=[pltpu.VMEM((8 + BLK, hw), F32)] * 3 + [pltpu.VMEM((nh, dk, dk), F32)],
        compiler_params=_cparams("parallel", "arbitrary"), name="delta_prompt",
    )(p, p, p, p, ba, w_conv, w_conv, w_conv, a_log.reshape(nh, 1), dt_bias.reshape(nh, 1),
      norm_g.reshape(1, dk))


def _delta_sample_body(q_ref, k_ref, v_ref, gt_ref, ba_ref, hq_ref, hk_ref, hv_ref, s0_ref,
                       wq_ref, wk_ref, wv_ref, alog_ref, dtb_ref, ng_ref, ybuf_ref,
                       y_ref, sout_ref, eq_ref, ek_ref, ev_ref, s_ref, *, nh, dk):
    del ybuf_ref
    xs = []
    for x_ref, h_ref, e_ref, w_ref in ((q_ref, hq_ref, eq_ref, wq_ref), (k_ref, hk_ref, ek_ref, wk_ref),
                                       (v_ref, hv_ref, ev_ref, wv_ref)):
        e_ref[5:8, :] = h_ref[0]
        e_ref[8:8 + DEC_T, :] = x_ref[...]
        xs.append(_conv_silu(e_ref, w_ref, DEC_T))
    s_ref[...] = s0_ref[0]
    outs = _delta_chunk_small(xs[0], xs[1], xs[2], ba_ref[...], alog_ref, dtb_ref, s_ref, nh, dk)
    for h in range(nh):
        sl = slice(h * dk, (h + 1) * dk)
        y_ref[:, sl] = _delta_out(outs[h], gt_ref[:, sl], ng_ref)
    sout_ref[0] = s_ref[...]


def _delta_sample(p, ba, conv_hist, s0, w_conv, a_log, dt_bias, norm_g, ybuf, rows_per_seq,
                  col_q, col_gate, nh, dk):
    nb = s0.shape[0]
    hw = nh * dk
    cq, ck, cv, cg = col_q // hw, col_q // hw + 1, col_q // hw + 2, col_gate // hw
    rowmap = lambda cb: (lambda s: (_sample_row_block(s, rows_per_seq), cb))
    rows = lambda cb: pl.BlockSpec((DEC_T, hw), rowmap(cb))
    hist = lambda cb: pl.BlockSpec((1, DN_CONV - 1, hw), lambda s: (s, 0, cb))
    wcs = lambda cb: pl.BlockSpec((DN_CONV, hw), lambda s: (0, cb))
    small = lambda shape: pl.BlockSpec(shape, lambda s: (0, 0))
    return pl.pallas_call(
        functools.partial(_delta_sample_body, nh=nh, dk=dk), grid=(nb,),
        in_specs=[rows(cq), rows(ck), rows(cv), rows(cg), pl.BlockSpec((DEC_T, BLK), rowmap(0)),
                  hist(0), hist(1), hist(2),
                  pl.BlockSpec((1, nh, dk, dk), lambda s: (s, 0, 0, 0)),
                  wcs(0), wcs(1), wcs(2), small((nh, 1)), small((nh, 1)), small((1, dk)),
                  pl.BlockSpec(memory_space=pl.ANY)],
        out_specs=[pl.BlockSpec((DEC_T, hw), rowmap(0)),
                   pl.BlockSpec((1, nh, dk, dk), lambda s: (s, 0, 0, 0))],
        out_shape=[jax.ShapeDtypeStruct(ybuf.shape, F32), jax.ShapeDtypeStruct(s0.shape, F32)],
        scratch_shapes=[pltpu.VMEM((8 + DEC_T, hw), F32)] * 3 + [pltpu.VMEM((nh, dk, dk), F32)],
        input_output_aliases={15: 0},
        compiler_params=_cparams("arbitrary"), name="delta_sample",
    )(p, p, p, p, ba, conv_hist, conv_hist, conv_hist, s0, w_conv, w_conv, w_conv,
      a_log.reshape(nh, 1), dt_bias.reshape(nh, 1), norm_g.reshape(1, dk), ybuf)


def _sink_softmax_pv(parts, sink):
    m = sink
    for s, _ in parts:
        m = jnp.maximum(m, jnp.max(s, axis=-1, keepdims=True))
    den = jnp.exp(sink - m)
    acc = None
    for s, v in parts:
        p = jnp.exp(s - m)
        den = den + jnp.sum(p, axis=-1, keepdims=True)
        pv = _bdot(p, v)
        acc = pv if acc is None else acc + pv
    return acc / den


def _attn_prompt_body(q_ref, kp_ref, kc_ref, vp_ref, vc_ref, slope_ref, sink_ref, y_ref, *, grp, hd, window):
    kv = pl.program_id(1)
    n = pl.program_id(2)
    i = lax.broadcasted_iota(jnp.int32, (BLK, 2 * BLK), 0)
    j = lax.broadcasted_iota(jnp.int32, (BLK, 2 * BLK), 1)
    dist = BLK + i - j
    krow = (n - 1) * BLK + j
    valid = (dist >= 0) & (dist < window) & (krow >= FRONT)
    distf = dist.astype(F32)
    kk = jnp.concatenate([kp_ref[...], kc_ref[...]], axis=0)
    vv = jnp.concatenate([vp_ref[...], vc_ref[...]], axis=0)
    for g in range(grp):
        head = kv * grp + g
        q = q_ref[:, g * hd:(g + 1) * hd]
        s = _bdot_nt(q, kk) * (hd ** -0.5)
        s = jnp.where(valid, s - slope_ref[head] * distf, NEG_INF)
        y_ref[:, g * hd:(g + 1) * hd] = _sink_softmax_pv([(s, vv)], sink_ref[head])


def _attn_prompt(p, slopes, sinks, nseq, nblk, col_q, col_k, col_v, nkv, grp, hd, window):
    gw = grp * hd
    cq, ck, cv = col_q // gw, col_k // hd, col_v // hd
    prev = lambda base: (lambda b, kv, n: (b * nblk + jnp.maximum(n - 1, 0), base + kv))
    cur = lambda base: (lambda b, kv, n: (b * nblk + n, base + kv))
    smem = pl.BlockSpec(memory_space=pltpu.SMEM)
    return pl.pallas_call(
        functools.partial(_attn_prompt_body, grp=grp, hd=hd, window=window), grid=(nseq, nkv, nblk),
        in_specs=[pl.BlockSpec((BLK, gw), cur(cq)),
                  pl.BlockSpec((BLK, hd), prev(ck)), pl.BlockSpec((BLK, hd), cur(ck)),
                  pl.BlockSpec((BLK, hd), prev(cv)), pl.BlockSpec((BLK, hd), cur(cv)),
                  smem, smem],
        out_specs=pl.BlockSpec((BLK, gw), cur(0)),
        out_shape=jax.ShapeDtypeStruct((p.shape[0], nkv * gw), F32),
        compiler_params=_cparams("parallel", "parallel", "arbitrary"), name="attn_prompt",
    )(p, p, p, p, p, slopes, sinks)


def _attn_sample_body(q_ref, k_ref, v_ref, ck_ref, cv_ref, slope_ref, sink_ref, ybuf_ref, y_ref,
                      *, grp, hd, window):
    del ybuf_ref
    kv = pl.program_id(1)
    wc = ck_ref.shape[1]
    i = lax.broadcasted_iota(jnp.int32, (DEC_T, wc), 0)
    j = lax.broadcasted_iota(jnp.int32, (DEC_T, wc), 1)
    dist_c = wc + i - j
    valid_c = (dist_c >= 0) & (dist_c < window)
    i2 = lax.broadcasted_iota(jnp.int32, (DEC_T, DEC_T), 0)
    j2 = lax.broadcasted_iota(jnp.int32, (DEC_T, DEC_T), 1)
    dist_n = i2 - j2
    valid_n = (dist_n >= 0) & (dist_n < window)
    kc, vc, kn, vn = ck_ref[0], cv_ref[0], k_ref[...], v_ref[...]
    for g in range(grp):
        head = kv * grp + g
        q = q_ref[:, g * hd:(g + 1) * hd]
        sc = _bdot_nt(q, kc) * (hd ** -0.5)
        sc = jnp.where(valid_c, sc - slope_ref[head] * dist_c.astype(F32), NEG_INF)
        sn = _bdot_nt(q, kn) * (hd ** -0.5)
        sn = jnp.where(valid_n, sn - slope_ref[head] * dist_n.astype(F32), NEG_INF)
        y_ref[:, g * hd:(g + 1) * hd] = _sink_softmax_pv([(sc, vc), (sn, vn)], sink_ref[head])


def _attn_sample(p, cache_k, cache_v, slopes, sinks, ybuf, rows_per_seq, col_q, col_k, col_v,
                 nkv, grp, hd, window):
    nb, wc = cache_k.shape[0], cache_k.shape[1]
    gw = grp * hd
    ck3 = cache_k.reshape(nb, wc, nkv * hd)
    cv3 = cache_v.reshape(nb, wc, nkv * hd)
    rowmap = lambda base: (lambda s, kv: (_sample_row_block(s, rows_per_seq), base + kv))
    smem = pl.BlockSpec(memory_space=pltpu.SMEM)
    return pl.pallas_call(
        functools.partial(_attn_sample_body, grp=grp, hd=hd, window=window), grid=(nb, nkv),
        in_specs=[pl.BlockSpec((DEC_T, gw), rowmap(col_q // gw)),
                  pl.BlockSpec((DEC_T, hd), rowmap(col_k // hd)),
                  pl.BlockSpec((DEC_T, hd), rowmap(col_v // hd)),
                  pl.BlockSpec((1, wc, hd), lambda s, kv: (s, 0, kv)),
                  pl.BlockSpec((1, wc, hd), lambda s, kv: (s, 0, kv)),
                  smem, smem, pl.BlockSpec(memory_space=pl.ANY)],
        out_specs=pl.BlockSpec((DEC_T, gw), rowmap(0)),
        out_shape=jax.ShapeDtypeStruct(ybuf.shape, F32),
        input_output_aliases={7: 0},
        compiler_params=_cparams("arbitrary", "arbitrary"), name="attn_sample",
    )(p, p, p, ck3, cv3, slopes, sinks, ybuf)


def _top_values(s, k):
    riota = lax.broadcasted_iota(jnp.int32, s.shape, 0)
    vals = []
    for _ in range(k):
        m = jnp.max(s, axis=0, keepdims=True)
        vals.append(m)
        first = jnp.min(jnp.where(s == m, riota, s.shape[0]), axis=0, keepdims=True)
        s = jnp.where(riota == first, LOWEST, s)
    return jnp.concatenate(vals, axis=0)


def _peer_topk_body(q_ref, keys_ref, s1_ref, s2_ref, st_ref, *, nh):
    thr, mx1, mx2, rz = [], [], [], []
    for h in range(nh):
        halves = []
        for half, out_ref in ((0, s1_ref), (1, s2_ref)):
            hp = 2 * h + half
            sc = _bdot_nt(keys_ref[hp], q_ref[:, hp * NKEYS:(hp + 1) * NKEYS])
            out_ref[h] = sc
            halves.append(_top_values(sc, TOPK))
        a, b = halves
        cand = [a[0:1] + b, a[8:16] + b[0:1]] + [a[i:i + 1] + b[0:8] for i in range(1, 8)]
        top = _top_values(jnp.concatenate(cand, axis=0), TOPK)
        thr.append(top[TOPK - 1:TOPK])
        mx1.append(a[0:1])
        mx2.append(b[0:1])
        rz.append(1.0 / jnp.sum(jnp.exp(top - top[0:1]), axis=0, keepdims=True))
    st_ref[...] = jnp.concatenate(thr + mx1 + mx2 + rz, axis=0)


def _peer_topk(q, sub_keys, *, tm=256):
    n = q.shape[0]
    nh = sub_keys.shape[0]
    keys = sub_keys.reshape(2 * nh, NKEYS, sub_keys.shape[-1])
    sspec = pl.BlockSpec((nh, NKEYS, tm), lambda i: (0, 0, i))
    return pl.pallas_call(
        functools.partial(_peer_topk_body, nh=nh), grid=(n // tm,),
        in_specs=[pl.BlockSpec((tm, q.shape[1]), lambda i: (i, 0)),
                  pl.BlockSpec(keys.shape, lambda i: (0, 0, 0))],
        out_specs=[sspec, sspec, pl.BlockSpec((4 * nh, tm), lambda i: (0, i))],
        out_shape=[jax.ShapeDtypeStruct((nh, NKEYS, n), F32), jax.ShapeDtypeStruct((nh, NKEYS, n), F32),
                   jax.ShapeDtypeStruct((4 * nh, n), F32)],
        compiler_params=_cparams("parallel"), name="peer_topk")(q, keys)


def _gelu(x):
    return 0.5 * x * (1.0 + lax.erf(x * (2.0 ** -0.5)))


def _peer_expert_body(xn_ref, wd_ref, wu_ref, s1_ref, s2_ref, st_ref, y_ref, e1_ref, e2_ref, xt_ref,
                      *, nh, ei, sub):
    c = pl.program_id(1)

    @pl.when(c == 0)
    def _():
        y_ref[...] = jnp.zeros(y_ref.shape, F32)
        for h in range(nh):
            e1_ref[h] = jnp.exp(s1_ref[h] - st_ref[nh + h:nh + h + 1, :])
            e2_ref[h] = jnp.exp(s2_ref[h] - st_ref[2 * nh + h:2 * nh + h + 1, :]) * st_ref[3 * nh + h:3 * nh + h + 1, :]

        xt_ref[...] = xn_ref[...].astype(F32).T.astype(BF16)

    acc = None
    for k in range(ei // sub):
        rows = slice(k * sub * NKEYS, (k + 1) * sub * NKEYS)
        gates = []
        for ii in range(sub):
            i1 = c * ei + k * sub + ii
            gate = None
            for h in range(nh):
                ssum = s1_ref[h, pl.ds(i1, 1), :] + s2_ref[h]
                w = jnp.where(ssum >= st_ref[h:h + 1, :], e1_ref[h, pl.ds(i1, 1), :] * e2_ref[h], 0.0)
                gate = w if gate is None else gate + w
            gates.append(gate)
        ht = jnp.dot(wd_ref[rows, :], xt_ref[...], preferred_element_type=F32)
        at = _gelu(ht) * jnp.concatenate(gates, axis=0)
        part = jnp.dot(at.T.astype(BF16), wu_ref[rows, :], preferred_element_type=F32)
        acc = part if acc is None else acc + part

    y_ref[...] += acc


PEER_TM, PEER_EI, PEER_SUB = 512, 4, 2


def _peer_expert(xn, w_down, w_up, s1t, s2t, stats, *, tm=PEER_TM, ei=PEER_EI, sub=PEER_SUB):
    n, d = xn.shape
    nh = s1t.shape[0]
    e = ei * NKEYS
    once = dict(pipeline_mode=pl.Buffered(1))
    sspec = pl.BlockSpec((nh, NKEYS, tm), lambda i, c: (0, 0, i), **once)
    return pl.pallas_call(
        functools.partial(_peer_expert_body, nh=nh, ei=ei, sub=sub), grid=(n // tm, w_down.shape[0] // e),
        in_specs=[pl.BlockSpec((tm, d), lambda i, c: (i, 0), **once),
                  pl.BlockSpec((e, d), lambda i, c: (c, 0)),
                  pl.BlockSpec((e, d), lambda i, c: (c, 0)),
                  sspec, sspec, pl.BlockSpec((4 * nh, tm), lambda i, c: (0, i), **once)],
        out_specs=pl.BlockSpec((tm, d), lambda i, c: (i, 0), **once),
        out_shape=jax.ShapeDtypeStruct((n, d), F32),
        scratch_shapes=[pltpu.VMEM((nh, NKEYS, tm), F32)] * 2 + [pltpu.VMEM((d, tm), BF16)],
        compiler_params=_cparams("parallel", "arbitrary"), name="peer_expert",
    )(xn, w_down, w_up, s1t, s2t, stats)


def _final_norm(h, y, g, nseq, nblk, row_blk, first_blk, nblk_out):
    d = h.shape[1]
    row = pl.BlockSpec((row_blk, d), lambda b, i: (b * nblk + first_blk + i, 0))
    return pl.pallas_call(
        _final_norm_body, grid=(nseq, nblk_out),
        in_specs=[row, row, pl.BlockSpec((1, d), lambda b, i: (0, 0))],
        out_specs=pl.BlockSpec((1, row_blk, d), lambda b, i: (b, i, 0)),
        out_shape=jax.ShapeDtypeStruct((nseq, nblk_out * row_blk, d), F32),
        compiler_params=_cparams("parallel", "parallel"), name="final_norm",
    )(h, y, g.reshape(1, d).astype(F32))


def _final_norm_body(h_ref, y_ref, g_ref, o_ref):
    x = h_ref[...] + y_ref[...]
    ms = jnp.mean(x * x, axis=-1, keepdims=True)
    o_ref[0] = x * lax.rsqrt(ms + EPS) * g_ref[...]


def kernel(x_prompt, x_sample, state_pool, state_conv, state_delta, cache_k, cache_v, meta_tokens, norm1_g,
           w_in, w_pool, s_pool, w_conv, a_log, dt_bias, dn_norm_g, attn_sinks, w_out, norm2_g,
           peer_w_query, peer_sub_keys, peer_w_down, peer_w_up, final_norm_g):
    nseq, seq, d = x_prompt.shape
    nsamp, dec_t, _ = x_sample.shape
    depth = w_in.shape[0]
    pool_w = w_pool.shape[1] * w_pool.shape[2]
    dn_qkv = w_conv.shape[2]
    nh, dk = state_delta.shape[2], state_delta.shape[3]
    wc, nkv, hd = cache_k.shape[2], cache_k.shape[3], cache_k.shape[4]
    nq = attn_sinks.shape[1]
    grp = nq // nkv
    window = wc
    assert dec_t == DEC_T and nsamp == nseq * SLOTS and (N_META + seq) % BLK == N_META
    assert SLOTS * DEC_T + POOL_HIST <= FRONT and dn_qkv == 3 * nh * dk and wc == BLK
    rows_per_seq = FRONT + N_META + seq
    nblk = rows_per_seq // BLK
    past_len = 16384
    col_qkv = pool_w
    col_gate = col_qkv + dn_qkv
    col_q = col_gate + nh * dk
    col_k = col_q + nq * hd
    col_v = col_k + nkv * hd
    src_ba = pool_w + dn_qkv

    xs = x_sample.reshape(nseq, SLOTS * DEC_T, d)
    zeros = jnp.zeros((nseq, FRONT - SLOTS * DEC_T, d), F32)
    meta = jnp.broadcast_to(meta_tokens[None], (nseq, N_META, d))
    h = jnp.concatenate([xs, zeros, meta, x_prompt], axis=1).reshape(nseq * rows_per_seq, d)

    slopes = jnp.exp2(-8.0 * (jnp.arange(nq, dtype=F32) + 1.0) / nq)
    new_p = [[] for _ in range(5)]
    new_s = [[] for _ in range(5)]
    y_peer = None
    for l in range(depth):
        w_main = jnp.concatenate([w_in[l][:, :src_ba], w_in[l][:, src_ba + 2 * nh:]], axis=1).astype(BF16)
        w_ba = jnp.pad(w_in[l][:, src_ba:src_ba + 2 * nh], ((0, 0), (0, BLK - 2 * nh))).astype(BF16)
        h, xn = _addnorm(h, y_peer, norm1_g[l])
        p = _mm(xn, w_main)
        ba = _mm(xn, w_ba)

        y_pool = _pool_prompt(p, w_pool[l], s_pool[l].reshape(1, pool_w), nseq, nblk)
        y_pool = _pool_sample(p, state_pool[l], w_pool[l], s_pool[l].reshape(1, pool_w), y_pool,
                              rows_per_seq, past_len)
        y_dn, s_p = _delta_prompt(p, ba, w_conv[l], a_log[l], dt_bias[l], dn_norm_g[l], nseq, nblk,
                                  col_qkv, col_gate, nh, dk)
        y_dn, s_s = _delta_sample(p, ba, state_conv[l], state_delta[l], w_conv[l], a_log[l], dt_bias[l],
                                  dn_norm_g[l], y_dn, rows_per_seq, col_qkv, col_gate, nh, dk)
        y_att = _attn_prompt(p, slopes, attn_sinks[l], nseq, nblk, col_q, col_k, col_v, nkv, grp, hd, window)
        y_att = _attn_sample(p, cache_k[l], cache_v[l], slopes, attn_sinks[l], y_att, rows_per_seq,
                             col_q, col_k, col_v, nkv, grp, hd, window)
        h = _outproj(y_pool, y_dn, y_att, w_out[l].astype(BF16), h)

        _, xn2 = _addnorm(h, None, norm2_g[l])
        q = _mm(xn2, peer_w_query[l].astype(BF16))
        s1t, s2t, stats = _peer_topk(q, peer_sub_keys[l])
        y_peer = _peer_expert(xn2, peer_w_down[l].astype(BF16), peer_w_up[l].astype(BF16), s1t, s2t, stats)

        p3 = p.reshape(nseq, rows_per_seq, p.shape[1])
        ps = p3[:, :SLOTS * DEC_T].reshape(nsamp, DEC_T, p.shape[1])
        new_p[0].append(p3[:, -POOL_HIST:, :pool_w])
        new_s[0].append(jnp.concatenate([state_pool[l], ps[:, :, :pool_w]], axis=1)[:, -POOL_HIST:])
        new_p[1].append(p3[:, -(DN_CONV - 1):, col_qkv:col_gate])
        new_s[1].append(jnp.concatenate([state_conv[l], ps[:, :, col_qkv:col_gate]], axis=1)[:, -(DN_CONV - 1):])
        new_p[2].append(s_p)
        new_s[2].append(s_s)
        new_p[3].append(p3[:, -window:, col_k:col_v].reshape(nseq, window, nkv, hd))
        new_s[3].append(jnp.concatenate([cache_k[l], ps[:, :, col_k:col_v].reshape(nsamp, DEC_T, nkv, hd)],
                                        axis=1)[:, -wc:])
        new_p[4].append(p3[:, -window:, col_v:].reshape(nseq, window, nkv, hd))
        new_s[4].append(jnp.concatenate([cache_v[l], ps[:, :, col_v:].reshape(nsamp, DEC_T, nkv, hd)],
                                        axis=1)[:, -wc:])

    y_prompt = _final_norm(h, y_peer, final_norm_g, nseq, nblk, BLK, 1, nblk - 1)
    y_sample = _final_norm(h, y_peer, final_norm_g, nseq, rows_per_seq // (SLOTS * DEC_T), SLOTS * DEC_T, 0, 1)
    y_sample = y_sample.reshape(nsamp, DEC_T, d)
    pool_p, conv_p, delta_p, k_p, v_p = (jnp.stack(a) for a in new_p)
    pool_s, conv_s, delta_s, k_s, v_s = (jnp.stack(a) for a in new_s)
    return (y_prompt, y_sample, pool_p, pool_s, conv_p, conv_s, delta_p, delta_s, k_p, k_s, v_p, v_s)
```

```python
import functools

import jax
import jax.numpy as jnp
from jax import lax
from jax.experimental import pallas as pl
from jax.experimental.pallas import tpu as pltpu

F32 = jnp.float32
BF16 = jnp.bfloat16

EPS = 1e-6
NEG_INF = -1e30
LOWEST = -3.0e38

N_META = 16
BLK = 128
FRONT = BLK - N_META
DEC_T = 8
SLOTS = 8
POOL_WINDOWS = (2, 4, 8, 16)
POOL_HIST = 15
DN_CONV = 4
TOPK = 16
NKEYS = 128
VMEM_LIMIT = 56 * 1024 * 1024


def _cparams(*sem):
    return pltpu.CompilerParams(dimension_semantics=sem, vmem_limit_bytes=VMEM_LIMIT)


def _bdot(a, b):
    return jnp.dot(a.astype(BF16), b.astype(BF16), preferred_element_type=F32)


def _bdot_nt(a, b):
    return lax.dot_general(a.astype(BF16), b.astype(BF16), (((1,), (1,)), ((), ())),
                           preferred_element_type=F32)


def _silu(x):
    return x * (1.0 / (1.0 + jnp.exp(-x)))


def _addnorm_body(*refs, add):
    if add:
        h_ref, y_ref, g_ref, hs_ref, xn_ref = refs
        x = h_ref[...] + y_ref[...]
        hs_ref[...] = x
    else:
        h_ref, g_ref, xn_ref = refs
        x = h_ref[...]
    ms = jnp.mean(x * x, axis=-1, keepdims=True)
    xn_ref[...] = (x * lax.rsqrt(ms + EPS) * g_ref[...]).astype(xn_ref.dtype)


def _addnorm(h, y, g, *, tm=256, out_dtype=BF16):
    n, d = h.shape
    row = pl.BlockSpec((tm, d), lambda i: (i, 0))
    gspec = pl.BlockSpec((1, d), lambda i: (0, 0))
    g2 = g.reshape(1, d).astype(F32)
    if y is None:
        xn = pl.pallas_call(
            functools.partial(_addnorm_body, add=False),
            grid=(n // tm,), in_specs=[row, gspec], out_specs=row,
            out_shape=jax.ShapeDtypeStruct((n, d), out_dtype),
            compiler_params=_cparams("parallel"), name="norm")(h, g2)
        return h, xn
    hs, xn = pl.pallas_call(
        functools.partial(_addnorm_body, add=True),
        grid=(n // tm,), in_specs=[row, row, gspec], out_specs=[row, row],
        out_shape=[jax.ShapeDtypeStruct((n, d), F32), jax.ShapeDtypeStruct((n, d), out_dtype)],
        compiler_params=_cparams("parallel"), name="add_norm")(h, y, g2)
    return hs, xn


def _mm_body(x_ref, w_ref, o_ref):
    o_ref[...] = jnp.dot(x_ref[...], w_ref[...], preferred_element_type=F32)


def _mm(x, w, *, tm=512, tn=1024):
    m, k = x.shape
    n = w.shape[1]
    tn = min(tn, n)
    return pl.pallas_call(
        _mm_body, grid=(n // tn, m // tm),
        in_specs=[pl.BlockSpec((tm, k), lambda j, i: (i, 0)),
                  pl.BlockSpec((k, tn), lambda j, i: (0, j))],
        out_specs=pl.BlockSpec((tm, tn), lambda j, i: (i, j)),
        out_shape=jax.ShapeDtypeStruct((m, n), F32),
        compiler_params=_cparams("parallel", "parallel"), name="matmul")(x, w)


def _outproj_body(yp_ref, yd_ref, ya_ref, w_ref, h_ref, o_ref, *, wp, wd):
    acc = h_ref[...]
    acc += jnp.dot(yp_ref[...].astype(BF16), w_ref[0:wp, :], preferred_element_type=F32)
    acc += jnp.dot(yd_ref[...].astype(BF16), w_ref[wp:wp + wd, :], preferred_element_type=F32)
    acc += jnp.dot(ya_ref[...].astype(BF16), w_ref[wp + wd:, :], preferred_element_type=F32)
    o_ref[...] = acc


def _outproj(yp, yd, ya, w, h, *, tm=512, tn=1024):
    m, d = h.shape
    wp, wd, wa = yp.shape[1], yd.shape[1], ya.shape[1]
    k = wp + wd + wa
    return pl.pallas_call(
        functools.partial(_outproj_body, wp=wp, wd=wd), grid=(d // tn, m // tm),
        in_specs=[pl.BlockSpec((tm, wp), lambda j, i: (i, 0)),
                  pl.BlockSpec((tm, wd), lambda j, i: (i, 0)),
                  pl.BlockSpec((tm, wa), lambda j, i: (i, 0)),
                  pl.BlockSpec((k, tn), lambda j, i: (0, j)),
                  pl.BlockSpec((tm, tn), lambda j, i: (i, j))],
        out_specs=pl.BlockSpec((tm, tn), lambda j, i: (i, j)),
        out_shape=jax.ShapeDtypeStruct((m, d), F32),
        compiler_params=_cparams("parallel", "parallel"), name="out_proj")(yp, yd, ya, w, h)


def _pool_windows(ext_ref, u, t, pos, w_ref, s_ref, gw):
    outs = []
    for gi, w in enumerate(POOL_WINDOWS):
        sl = slice(gi * gw, (gi + 1) * gw)
        win = u[:, sl]
        for k in range(1, w):
            win = win + ext_ref[16 - k:16 - k + t, sl]
        cnt = jnp.clip(pos + 1, 1, w).astype(F32)
        d = win / cnt - u[:, sl]
        outs.append(_bdot(d, w_ref[gi]))
    return jnp.concatenate(outs, axis=-1) * s_ref[...]


def _pool_prompt_body(u_ref, w_ref, s_ref, y_ref, ext_ref, *, gw):
    n = pl.program_id(1)

    @pl.when(n == 0)
    def _():
        ext_ref[0:16, :] = jnp.zeros((16, ext_ref.shape[1]), F32)

    u = u_ref[...]
    ext_ref[16:16 + BLK, :] = u
    row = n * BLK + lax.broadcasted_iota(jnp.int32, (BLK, 1), 0)
    pos = row - FRONT
    y = _pool_windows(ext_ref, u, BLK, pos, w_ref, s_ref, gw)
    y_ref[...] = jnp.where(pos >= 0, y, 0.0)
    ext_ref[0:16, :] = u[BLK - 16:, :]


def _pool_prompt(p, w_pool, s_pool, nseq, nblk):
    pw = w_pool.shape[0] * w_pool.shape[1]
    gw = w_pool.shape[1]
    return pl.pallas_call(
        functools.partial(_pool_prompt_body, gw=gw), grid=(nseq, nblk),
        in_specs=[pl.BlockSpec((BLK, pw), lambda b, n: (b * nblk + n, 0)),
                  pl.BlockSpec(w_pool.shape, lambda b, n: (0, 0, 0)),
                  pl.BlockSpec((1, pw), lambda b, n: (0, 0))],
        out_specs=pl.BlockSpec((BLK, pw), lambda b, n: (b * nblk + n, 0)),
        out_shape=jax.ShapeDtypeStruct((p.shape[0], pw), F32),
        scratch_shapes=[pltpu.VMEM((16 + BLK, pw), F32)],
        compiler_params=_cparams("parallel", "arbitrary"), name="pool_prompt")(p, w_pool, s_pool)


def _pool_sample_body(u_ref, hist_ref, w_ref, s_ref, ybuf_ref, y_ref, ext_ref, *, gw, pos0):
    del ybuf_ref
    u = u_ref[...]
    ext_ref[0:1, :] = jnp.zeros((1, ext_ref.shape[1]), F32)
    ext_ref[1:16, :] = hist_ref[0]
    ext_ref[16:16 + DEC_T, :] = u
    pos = pos0 + lax.broadcasted_iota(jnp.int32, (DEC_T, 1), 0)
    y_ref[...] = _pool_windows(ext_ref, u, DEC_T, pos, w_ref, s_ref, gw)


def _sample_row_block(s, rows_per_seq):
    return (s // SLOTS) * (rows_per_seq // DEC_T) + s % SLOTS


def _pool_sample(p, hist, w_pool, s_pool, ybuf, rows_per_seq, pos0):
    nb = hist.shape[0]
    pw = hist.shape[2]
    gw = w_pool.shape[1]
    rowmap = lambda s: (_sample_row_block(s, rows_per_seq), 0)
    return pl.pallas_call(
        functools.partial(_pool_sample_body, gw=gw, pos0=pos0), grid=(nb,),
        in_specs=[pl.BlockSpec((DEC_T, pw), rowmap),
                  pl.BlockSpec((1, POOL_HIST, pw), lambda s: (s, 0, 0)),
                  pl.BlockSpec(w_pool.shape, lambda s: (0, 0, 0)),
                  pl.BlockSpec((1, pw), lambda s: (0, 0)),
                  pl.BlockSpec(memory_space=pl.ANY)],
        out_specs=pl.BlockSpec((DEC_T, pw), rowmap),
        out_shape=jax.ShapeDtypeStruct(ybuf.shape, F32),
        scratch_shapes=[pltpu.VMEM((16 + DEC_T, pw), F32)],
        input_output_aliases={4: 0},
        compiler_params=_cparams("arbitrary"), name="pool_sample")(p, hist, w_pool, s_pool, ybuf)


def _cumsum_lanes(x):
    lane = lax.broadcasted_iota(jnp.int32, x.shape, 1)
    s = 1
    while s < x.shape[1]:
        x = x + jnp.where(lane >= s, pltpu.roll(x, s, axis=1), 0.0)
        s *= 2
    return x


HEAD_PACK = 2


def _unit_lower_inverse(a, ii, jj):
    x = jnp.where(ii == jj, 1.0, 0.0).astype(F32)
    s = 1
    while s < BLK:
        same_pair = (ii // (2 * s)) == (jj // (2 * s))
        lower_left = ((ii // s) % 2 == 1) & ((jj // s) % 2 == 0)
        off = jnp.where(same_pair & lower_left, a, 0.0)
        x = x - _bdot(x, _bdot(off, x))
        s *= 2
    return x


def _delta_scalars(ba, valid, alog_ref, dtb_ref, nh):
    bat = ba.T
    beta = jnp.where(valid, 1.0 / (1.0 + jnp.exp(-bat[0:nh])), 0.0)
    z = bat[nh:2 * nh] + dtb_ref[...]
    softplus = jnp.maximum(z, 0.0) + jnp.log(1.0 + jnp.exp(-jnp.abs(z)))
    g = jnp.where(valid, -jnp.exp(alog_ref[...]) * softplus, 0.0)
    gc = _cumsum_lanes(g)
    glast = jnp.broadcast_to(gc[:, BLK - 1:BLK], gc.shape)
    eg = jnp.exp(gc)
    rows = jnp.concatenate(
        [gc, eg, beta, beta * eg, jnp.exp(glast - gc), jnp.exp(glast),
         jnp.zeros((BLK - 6 * nh, BLK), F32)], axis=0)
    return gc, rows.T


def _l2n(x):
    return x * lax.rsqrt(jnp.sum(x * x, axis=-1, keepdims=True) + EPS)


def _delta_chunk_small(xq, xk, xv, ba, alog_ref, dtb_ref, s_ref, nh, dk):
    t = xq.shape[0]
    zrows = jnp.zeros((BLK - t, BLK), F32)
    lane = lax.broadcasted_iota(jnp.int32, (nh, BLK), 1)
    gc, cols = _delta_scalars(jnp.concatenate([ba, zrows], axis=0), lane < t, alog_ref, dtb_ref, nh)
    ii = lax.broadcasted_iota(jnp.int32, (t, BLK), 0)
    jj = lax.broadcasted_iota(jnp.int32, (t, BLK), 1)
    incl = ii >= jj
    strict = ii > jj
    outs = []
    for h in range(nh):
        sl = slice(h * dk, (h + 1) * dk)
        q = _l2n(xq[:, sl]) * (dk ** -0.5)
        k = _l2n(xk[:, sl])
        v = xv[:, sl]
        col = lambda qi: cols[0:t, qi * nh + h:qi * nh + h + 1]
        kpad = jnp.concatenate([k, zrows], axis=0)
        diff = col(0) - gc[h:h + 1, :]
        decay = jnp.where(incl, jnp.exp(jnp.where(incl, diff, 0.0)), 0.0)
        a_mat = jnp.where(strict, _bdot_nt(k, kpad) * decay * col(2), 0.0)
        qk = _bdot_nt(q, kpad) * decay
        wu = jnp.concatenate([col(3) * k, col(2) * v], axis=-1)
        for j in range(t - 1):
            wu = wu - a_mat[:, j:j + 1] * wu[j:j + 1, :]
        s = s_ref[h]
        v_new = wu[:, dk:] - _bdot(wu[:, :dk], s)
        o = col(1) * _bdot(q, s)
        for j in range(t):
            o = o + qk[:, j:j + 1] * v_new[j:j + 1, :]
        kd = jnp.concatenate([k * col(4), zrows], axis=0).T
        s_ref[h] = cols[:, 5 * nh + h:5 * nh + h + 1] * s + _bdot(kd, jnp.concatenate([v_new, zrows], axis=0))
        outs.append(o)
    return outs


def _delta_chunk(xq, xk, xv, ba, valid, alog_ref, dtb_ref, s_ref, nh, dk):
    assert dk == BLK and nh % HEAD_PACK == 0
    gc, cols = _delta_scalars(ba, valid, alog_ref, dtb_ref, nh)
    n = HEAD_PACK * BLK
    ii = lax.broadcasted_iota(jnp.int32, (n, n), 0)
    jj = lax.broadcasted_iota(jnp.int32, (n, n), 1)
    same = (ii // BLK) == (jj // BLK)
    incl = same & (ii >= jj)
    strict = same & (ii > jj)
    outs = []
    for p in range(nh // HEAD_PACK):
        heads = range(p * HEAD_PACK, (p + 1) * HEAD_PACK)
        stack = lambda f: jnp.concatenate([f(h) for h in heads], axis=0)
        q = stack(lambda h: _l2n(xq[:, h * dk:(h + 1) * dk]) * (dk ** -0.5))
        k = stack(lambda h: _l2n(xk[:, h * dk:(h + 1) * dk]))
        v = stack(lambda h: xv[:, h * dk:(h + 1) * dk])
        col = lambda qi: stack(lambda h: cols[:, qi * nh + h:qi * nh + h + 1])
        diag = lambda m: jnp.where(same, jnp.concatenate([m] * HEAD_PACK, axis=1), 0.0)
        diff = col(0) - jnp.concatenate([gc[h:h + 1, :] for h in heads], axis=1)
        decay = jnp.where(incl, jnp.exp(jnp.where(incl, diff, 0.0)), 0.0)
        a_mat = jnp.where(strict, _bdot_nt(k, k) * decay * col(2), 0.0)
        qk = _bdot_nt(q, k) * decay
        x = _unit_lower_inverse(a_mat, ii, jj)
        rhs = jnp.concatenate([col(3) * k, col(2) * v], axis=-1)
        wu = _bdot(x, rhs)
        w = wu[:, :dk]
        u = wu[:, dk:]
        s = stack(lambda h: s_ref[h])
        v_new = u - _bdot(diag(w), s)
        o = col(1) * _bdot(diag(q), s) + _bdot(qk, v_new)
        kdt = (k * col(4)).T
        kd = jnp.where(same, jnp.concatenate([kdt] * HEAD_PACK, axis=0), 0.0)
        s_new = col(5) * s + _bdot(kd, v_new)
        for i, h in enumerate(heads):
            s_ref[h] = s_new[i * BLK:(i + 1) * BLK, :]
            outs.append(o[i * BLK:(i + 1) * BLK, :])
    return outs


def _delta_out(o, gate, ng_ref):
    return o * lax.rsqrt(jnp.mean(o * o, axis=-1, keepdims=True) + EPS) * ng_ref[...] * _silu(gate)


def _conv_silu(ext_ref, w_ref, t):
    acc = ext_ref[5:5 + t, :] * w_ref[0:1, :]
    for i in range(1, DN_CONV):
        acc = acc + ext_ref[5 + i:5 + i + t, :] * w_ref[i:i + 1, :]
    return _silu(acc)


def _delta_prompt_body(q_ref, k_ref, v_ref, gt_ref, ba_ref, wq_ref, wk_ref, wv_ref, alog_ref, dtb_ref, ng_ref,
                       y_ref, sout_ref, eq_ref, ek_ref, ev_ref, s_ref, *, nh, dk, nblk):
    c = pl.program_id(1)

    @pl.when(c == 0)
    def _():
        for e in (eq_ref, ek_ref, ev_ref):
            e[0:8, :] = jnp.zeros((8, e.shape[1]), F32)
        s_ref[...] = jnp.zeros(s_ref.shape, F32)

    xs = []
    for x_ref, e_ref, w_ref in ((q_ref, eq_ref, wq_ref), (k_ref, ek_ref, wk_ref), (v_ref, ev_ref, wv_ref)):
        e_ref[8:8 + BLK, :] = x_ref[...]
        xs.append(_conv_silu(e_ref, w_ref, BLK))
        e_ref[0:8, :] = x_ref[BLK - 8:, :]
    lane = lax.broadcasted_iota(jnp.int32, (nh, BLK), 1)
    valid = (c > 0) | (lane >= FRONT)
    outs = _delta_chunk(xs[0], xs[1], xs[2], ba_ref[...], valid, alog_ref, dtb_ref, s_ref, nh, dk)
    row = c * BLK + lax.broadcasted_iota(jnp.int32, (BLK, 1), 0)
    for h in range(nh):
        sl = slice(h * dk, (h + 1) * dk)
        y_ref[:, sl] = jnp.where(row >= FRONT, _delta_out(outs[h], gt_ref[:, sl], ng_ref), 0.0)

    @pl.when(c == nblk - 1)
    def _():
        sout_ref[0] = s_ref[...]


def _delta_prompt(p, ba, w_conv, a_log, dt_bias, norm_g, nseq, nblk, col_q, col_gate, nh, dk):
    hw = nh * dk
    cq, ck, cv, cg = col_q // hw, col_q // hw + 1, col_q // hw + 2, col_gate // hw
    rows = lambda cb: pl.BlockSpec((BLK, hw), lambda b, c: (b * nblk + c, cb))
    wcs = lambda cb: pl.BlockSpec((DN_CONV, hw), lambda b, c: (0, cb))
    small = lambda shape: pl.BlockSpec(shape, lambda b, c: (0, 0))
    return pl.pallas_call(
        functools.partial(_delta_prompt_body, nh=nh, dk=dk, nblk=nblk), grid=(nseq, nblk),
        in_specs=[rows(cq), rows(ck), rows(cv), rows(cg),
                  pl.BlockSpec((BLK, BLK), lambda b, c: (b * nblk + c, 0)),
                  wcs(0), wcs(1), wcs(2), small((nh, 1)), small((nh, 1)), small((1, dk))],
        out_specs=[pl.BlockSpec((BLK, hw), lambda b, c: (b * nblk + c, 0)),
                   pl.BlockSpec((1, nh, dk, dk), lambda b, c: (b, 0, 0, 0))],
        out_shape=[jax.ShapeDtypeStruct((p.shape[0], hw), F32),
                   jax.ShapeDtypeStruct((nseq, nh, dk, dk), F32)],
        scratch_shapes=[pltpu.VMEM((8 + BLK, hw), F32)] * 3 + [pltpu.VMEM((nh, dk, dk), F32)],
        compiler_params=_cparams("parallel", "arbitrary"), name="delta_prompt",
    )(p, p, p, p, ba, w_conv, w_conv, w_conv, a_log.reshape(nh, 1), dt_bias.reshape(nh, 1),
      norm_g.reshape(1, dk))


def _delta_sample_body(q_ref, k_ref, v_ref, gt_ref, ba_ref, hq_ref, hk_ref, hv_ref, s0_ref,
                       wq_ref, wk_ref, wv_ref, alog_ref, dtb_ref, ng_ref, ybuf_ref,
                       y_ref, sout_ref, eq_ref, ek_ref, ev_ref, s_ref, *, nh, dk):
    del ybuf_ref
    xs = []
    for x_ref, h_ref, e_ref, w_ref in ((q_ref, hq_ref, eq_ref, wq_ref), (k_ref, hk_ref, ek_ref, wk_ref),
                                       (v_ref, hv_ref, ev_ref, wv_ref)):
        e_ref[5:8, :] = h_ref[0]
        e_ref[8:8 + DEC_T, :] = x_ref[...]
        xs.append(_conv_silu(e_ref, w_ref, DEC_T))
    s_ref[...] = s0_ref[0]
    outs = _delta_chunk_small(xs[0], xs[1], xs[2], ba_ref[...], alog_ref, dtb_ref, s_ref, nh, dk)
    for h in range(nh):
        sl = slice(h * dk, (h + 1) * dk)
        y_ref[:, sl] = _delta_out(outs[h], gt_ref[:, sl], ng_ref)
    sout_ref[0] = s_ref[...]


def _delta_sample(p, ba, conv_hist, s0, w_conv, a_log, dt_bias, norm_g, ybuf, rows_per_seq,
                  col_q, col_gate, nh, dk):
    nb = s0.shape[0]
    hw = nh * dk
    cq, ck, cv, cg = col_q // hw, col_q // hw + 1, col_q // hw + 2, col_gate // hw
    rowmap = lambda cb: (lambda s: (_sample_row_block(s, rows_per_seq), cb))
    rows = lambda cb: pl.BlockSpec((DEC_T, hw), rowmap(cb))
    hist = lambda cb: pl.BlockSpec((1, DN_CONV - 1, hw), lambda s: (s, 0, cb))
    wcs = lambda cb: pl.BlockSpec((DN_CONV, hw), lambda s: (0, cb))
    small = lambda shape: pl.BlockSpec(shape, lambda s: (0, 0))
    return pl.pallas_call(
        functools.partial(_delta_sample_body, nh=nh, dk=dk), grid=(nb,),
        in_specs=[rows(cq), rows(ck), rows(cv), rows(cg), pl.BlockSpec((DEC_T, BLK), rowmap(0)),
                  hist(0), hist(1), hist(2),
                  pl.BlockSpec((1, nh, dk, dk), lambda s: (s, 0, 0, 0)),
                  wcs(0), wcs(1), wcs(2), small((nh, 1)), small((nh, 1)), small((1, dk)),
                  pl.BlockSpec(memory_space=pl.ANY)],
        out_specs=[pl.BlockSpec((DEC_T, hw), rowmap(0)),
                   pl.BlockSpec((1, nh, dk, dk), lambda s: (s, 0, 0, 0))],
        out_shape=[jax.ShapeDtypeStruct(ybuf.shape, F32), jax.ShapeDtypeStruct(s0.shape, F32)],
        scratch_shapes=[pltpu.VMEM((8 + DEC_T, hw), F32)] * 3 + [pltpu.VMEM((nh, dk, dk), F32)],
        input_output_aliases={15: 0},
        compiler_params=_cparams("arbitrary"), name="delta_sample",
    )(p, p, p, p, ba, conv_hist, conv_hist, conv_hist, s0, w_conv, w_conv, w_conv,
      a_log.reshape(nh, 1), dt_bias.reshape(nh, 1), norm_g.reshape(1, dk), ybuf)


def _sink_softmax_pv(parts, sink):
    m = sink
    for s, _ in parts:
        m = jnp.maximum(m, jnp.max(s, axis=-1, keepdims=True))
    den = jnp.exp(sink - m)
    acc = None
    for s, v in parts:
        p = jnp.exp(s - m)
        den = den + jnp.sum(p, axis=-1, keepdims=True)
        pv = _bdot(p, v)
        acc = pv if acc is None else acc + pv
    return acc / den


def _attn_prompt_body(q_ref, kp_ref, kc_ref, vp_ref, vc_ref, slope_ref, sink_ref, y_ref, *, grp, hd, window):
    kv = pl.program_id(1)
    n = pl.program_id(2)
    i = lax.broadcasted_iota(jnp.int32, (BLK, 2 * BLK), 0)
    j = lax.broadcasted_iota(jnp.int32, (BLK, 2 * BLK), 1)
    dist = BLK + i - j
    krow = (n - 1) * BLK + j
    valid = (dist >= 0) & (dist < window) & (krow >= FRONT)
    distf = dist.astype(F32)
    kk = jnp.concatenate([kp_ref[...], kc_ref[...]], axis=0)
    vv = jnp.concatenate([vp_ref[...], vc_ref[...]], axis=0)
    for g in range(grp):
        head = kv * grp + g
        q = q_ref[:, g * hd:(g + 1) * hd]
        s = _bdot_nt(q, kk) * (hd ** -0.5)
        s = jnp.where(valid, s - slope_ref[head] * distf, NEG_INF)
        y_ref[:, g * hd:(g + 1) * hd] = _sink_softmax_pv([(s, vv)], sink_ref[head])


def _attn_prompt(p, slopes, sinks, nseq, nblk, col_q, col_k, col_v, nkv, grp, hd, window):
    gw = grp * hd
    cq, ck, cv = col_q // gw, col_k // hd, col_v // hd
    prev = lambda base: (lambda b, kv, n: (b * nblk + jnp.maximum(n - 1, 0), base + kv))
    cur = lambda base: (lambda b, kv, n: (b * nblk + n, base + kv))
    smem = pl.BlockSpec(memory_space=pltpu.SMEM)
    return pl.pallas_call(
        functools.partial(_attn_prompt_body, grp=grp, hd=hd, window=window), grid=(nseq, nkv, nblk),
        in_specs=[pl.BlockSpec((BLK, gw), cur(cq)),
                  pl.BlockSpec((BLK, hd), prev(ck)), pl.BlockSpec((BLK, hd), cur(ck)),
                  pl.BlockSpec((BLK, hd), prev(cv)), pl.BlockSpec((BLK, hd), cur(cv)),
                  smem, smem],
        out_specs=pl.BlockSpec((BLK, gw), cur(0)),
        out_shape=jax.ShapeDtypeStruct((p.shape[0], nkv * gw), F32),
        compiler_params=_cparams("parallel", "parallel", "arbitrary"), name="attn_prompt",
    )(p, p, p, p, p, slopes, sinks)


def _attn_sample_body(*refs, nkv, grp, hd, window):
    q_refs = refs[:nkv]
    k_ref, v_ref, ck_ref, cv_ref, slope_ref, sink_ref, ybuf_ref, y_ref = refs[nkv:]
    del ybuf_ref
    wc = ck_ref.shape[1]
    i = lax.broadcasted_iota(jnp.int32, (DEC_T, wc), 0)
    j = lax.broadcasted_iota(jnp.int32, (DEC_T, wc), 1)
    dist_c = wc + i - j
    valid_c = (dist_c >= 0) & (dist_c < window)
    i2 = lax.broadcasted_iota(jnp.int32, (DEC_T, DEC_T), 0)
    j2 = lax.broadcasted_iota(jnp.int32, (DEC_T, DEC_T), 1)
    dist_n = i2 - j2
    valid_n = (dist_n >= 0) & (dist_n < window)
    for kv in range(nkv):
        kc = ck_ref[0, :, kv * hd:(kv + 1) * hd]
        vc = cv_ref[0, :, kv * hd:(kv + 1) * hd]
        kn = k_ref[:, kv * hd:(kv + 1) * hd]
        vn = v_ref[:, kv * hd:(kv + 1) * hd]
        for g in range(grp):
            head = kv * grp + g
            q = q_refs[kv][:, g * hd:(g + 1) * hd]
            sc = _bdot_nt(q, kc) * (hd ** -0.5)
            sc = jnp.where(valid_c, sc - slope_ref[head] * dist_c.astype(F32), NEG_INF)
            sn = _bdot_nt(q, kn) * (hd ** -0.5)
            sn = jnp.where(valid_n, sn - slope_ref[head] * dist_n.astype(F32), NEG_INF)
            y_ref[:, head * hd:(head + 1) * hd] = _sink_softmax_pv([(sc, vc), (sn, vn)], sink_ref[head])


def _attn_sample(p, cache_k, cache_v, slopes, sinks, ybuf, rows_per_seq, col_q, col_k, col_v,
                 nkv, grp, hd, window):
    nb, wc = cache_k.shape[0], cache_k.shape[1]
    gw, kw = grp * hd, nkv * hd
    ck3 = cache_k.reshape(nb, wc, kw)
    cv3 = cache_v.reshape(nb, wc, kw)
    rowmap = lambda cb: (lambda s: (_sample_row_block(s, rows_per_seq), cb))
    smem = pl.BlockSpec(memory_space=pltpu.SMEM)
    return pl.pallas_call(
        functools.partial(_attn_sample_body, nkv=nkv, grp=grp, hd=hd, window=window), grid=(nb,),
        in_specs=[pl.BlockSpec((DEC_T, gw), rowmap(col_q // gw + kv)) for kv in range(nkv)]
        + [pl.BlockSpec((DEC_T, kw), rowmap(col_k // kw)),
           pl.BlockSpec((DEC_T, kw), rowmap(col_v // kw)),
           pl.BlockSpec((1, wc, kw), lambda s: (s, 0, 0)),
           pl.BlockSpec((1, wc, kw), lambda s: (s, 0, 0)),
           smem, smem, pl.BlockSpec(memory_space=pl.ANY)],
        out_specs=pl.BlockSpec((DEC_T, nkv * gw), rowmap(0)),
        out_shape=jax.ShapeDtypeStruct(ybuf.shape, F32),
        input_output_aliases={nkv + 6: 0},
        compiler_params=_cparams("arbitrary"), name="attn_sample",
    )(*([p] * (nkv + 2)), ck3, cv3, slopes, sinks, ybuf)


def _top_values(s, k, want_rank=False):
    riota = lax.broadcasted_iota(jnp.int32, s.shape, 0)
    rank = jnp.full(s.shape, float(s.shape[0] - 1), F32) if want_rank else None
    vals = []
    for it in range(k):
        m = jnp.max(s, axis=0, keepdims=True)
        vals.append(m)
        hit = riota == jnp.min(jnp.where(s == m, riota, s.shape[0]), axis=0, keepdims=True)
        if want_rank:
            rank = jnp.where(hit, float(it), rank)
        s = jnp.where(hit, LOWEST, s)
    return jnp.concatenate(vals, axis=0), rank


def _peer_topk_body(q_ref, keys_ref, rk_ref, cut_ref, e1_ref, e2_ref, *, nh):
    for h in range(nh):
        sc = [_bdot_nt(keys_ref[2 * h + half], q_ref[:, (2 * h + half) * NKEYS:(2 * h + half + 1) * NKEYS])
              for half in (0, 1)]
        a, rank1 = _top_values(sc[0], TOPK, True)
        b, rank2 = _top_values(sc[1], TOPK, True)
        cand = [a[0:1] + b, a[8:16] + b[0:1]] + [a[i:i + 1] + b[0:8] for i in range(1, 8)]
        top, _ = _top_values(jnp.concatenate(cand, axis=0), TOPK)
        thr = top[TOPK - 1:TOPK]
        rz = 1.0 / jnp.sum(jnp.exp(top - top[0:1]), axis=0, keepdims=True)
        cut = jnp.zeros(rank1.shape, F32)
        for r in range(TOPK):
            height = jnp.sum(jnp.where(a[r:r + 1] + b >= thr, 1.0, 0.0), axis=0, keepdims=True)
            cut = jnp.where(rank1 == float(r), height, cut)
        rk_ref[h] = rank2.astype(BF16)
        cut_ref[h] = cut
        e1_ref[h] = jnp.exp(sc[0] - a[0:1])
        e2_ref[h] = (jnp.exp(sc[1] - b[0:1]) * rz).astype(BF16)


def _peer_topk(q, sub_keys, *, tm=256):
    n = q.shape[0]
    nh = sub_keys.shape[0]
    keys = sub_keys.reshape(2 * nh, NKEYS, sub_keys.shape[-1])
    sspec = pl.BlockSpec((nh, NKEYS, tm), lambda i: (0, 0, i))
    shape = lambda dt: jax.ShapeDtypeStruct((nh, NKEYS, n), dt)
    return pl.pallas_call(
        functools.partial(_peer_topk_body, nh=nh), grid=(n // tm,),
        in_specs=[pl.BlockSpec((tm, q.shape[1]), lambda i: (i, 0)),
                  pl.BlockSpec(keys.shape, lambda i: (0, 0, 0))],
        out_specs=[sspec] * 4,
        out_shape=[shape(BF16), shape(F32), shape(F32), shape(BF16)],
        compiler_params=_cparams("parallel"), name="peer_topk")(q, keys)


def _gelu(x):
    return 0.5 * x * (1.0 + lax.erf(x * (2.0 ** -0.5)))


PACK = 16


def _peer_expert_body(xn_ref, wd_ref, wu_ref, rk_ref, cut_ref, e1_ref, e2_ref, y_ref, xt_ref, *, nh, ei, sub):
    c = pl.program_id(1)
    tm = xt_ref.shape[1]

    @pl.when(c == 0)
    def _():
        y_ref[...] = jnp.zeros(y_ref.shape, F32)
        xt_ref[...] = xn_ref[...].astype(F32).T.astype(BF16)

    def row16(ref, h, i1):
        return jnp.broadcast_to(ref[h, pl.ds(i1, 1), :], (PACK, tm)).astype(BF16)

    acc = None
    for k in range(ei // sub):
        rows = slice(k * sub * NKEYS, (k + 1) * sub * NKEYS)
        ht = jnp.dot(wd_ref[rows, :], xt_ref[...], preferred_element_type=F32)
        act = _gelu(ht)
        pieces = []
        for ii in range(sub):
            i1 = c * ei + k * sub + ii
            cut = [row16(cut_ref, h, i1) for h in range(nh)]
            e1 = [row16(e1_ref, h, i1) for h in range(nh)]
            for r in range(NKEYS // PACK):
                sl = slice(r * PACK, (r + 1) * PACK)
                gate = None
                for h in range(nh):
                    w = jnp.where(rk_ref[h, sl, :] < cut[h], e1[h] * e2_ref[h, sl, :], 0.0)
                    gate = w if gate is None else gate + w
                base = ii * NKEYS + r * PACK
                pieces.append(act[base:base + PACK, :] * gate.astype(F32))
        at = jnp.concatenate(pieces, axis=0)
        part = jnp.dot(at.T.astype(BF16), wu_ref[rows, :], preferred_element_type=F32)
        acc = part if acc is None else acc + part

    y_ref[...] += acc


PEER_TM, PEER_EI, PEER_SUB = 512, 4, 2


def _peer_expert(xn, w_down, w_up, rank2, cut, e1, e2, *, tm=PEER_TM, ei=PEER_EI, sub=PEER_SUB):
    n, d = xn.shape
    nh = rank2.shape[0]
    e = ei * NKEYS
    once = dict(pipeline_mode=pl.Buffered(1))
    sspec = pl.BlockSpec((nh, NKEYS, tm), lambda i, c: (0, 0, i), **once)
    return pl.pallas_call(
        functools.partial(_peer_expert_body, nh=nh, ei=ei, sub=sub), grid=(n // tm, w_down.shape[0] // e),
        in_specs=[pl.BlockSpec((tm, d), lambda i, c: (i, 0), **once),
                  pl.BlockSpec((e, d), lambda i, c: (c, 0)),
                  pl.BlockSpec((e, d), lambda i, c: (c, 0)),
                  sspec, sspec, sspec, sspec],
        out_specs=pl.BlockSpec((tm, d), lambda i, c: (i, 0), **once),
        out_shape=jax.ShapeDtypeStruct((n, d), F32),
        scratch_shapes=[pltpu.VMEM((d, tm), BF16)],
        compiler_params=_cparams("parallel", "arbitrary"), name="peer_expert",
    )(xn, w_down, w_up, rank2, cut, e1, e2)


def _final_norm(h, y, g, nseq, nblk, row_blk, first_blk, nblk_out):
    d = h.shape[1]
    row = pl.BlockSpec((row_blk, d), lambda b, i: (b * nblk + first_blk + i, 0))
    return pl.pallas_call(
        _final_norm_body, grid=(nseq, nblk_out),
        in_specs=[row, row, pl.BlockSpec((1, d), lambda b, i: (0, 0))],
        out_specs=pl.BlockSpec((1, row_blk, d), lambda b, i: (b, i, 0)),
        out_shape=jax.ShapeDtypeStruct((nseq, nblk_out * row_blk, d), F32),
        compiler_params=_cparams("parallel", "parallel"), name="final_norm",
    )(h, y, g.reshape(1, d).astype(F32))


def _final_norm_body(h_ref, y_ref, g_ref, o_ref):
    x = h_ref[...] + y_ref[...]
    ms = jnp.mean(x * x, axis=-1, keepdims=True)
    o_ref[0] = x * lax.rsqrt(ms + EPS) * g_ref[...]


def kernel(x_prompt, x_sample, state_pool, state_conv, state_delta, cache_k, cache_v, meta_tokens, norm1_g,
           w_in, w_pool, s_pool, w_conv, a_log, dt_bias, dn_norm_g, attn_sinks, w_out, norm2_g,
           peer_w_query, peer_sub_keys, peer_w_down, peer_w_up, final_norm_g):
    nseq, seq, d = x_prompt.shape
    nsamp, dec_t, _ = x_sample.shape
    depth = w_in.shape[0]
    pool_w = w_pool.shape[1] * w_pool.shape[2]
    dn_qkv = w_conv.shape[2]
    nh, dk = state_delta.shape[2], state_delta.shape[3]
    wc, nkv, hd = cache_k.shape[2], cache_k.shape[3], cache_k.shape[4]
    nq = attn_sinks.shape[1]
    grp = nq // nkv
    window = wc
    assert dec_t == DEC_T and nsamp == nseq * SLOTS and (N_META + seq) % BLK == N_META
    assert SLOTS * DEC_T + POOL_HIST <= FRONT and dn_qkv == 3 * nh * dk and wc == BLK
    rows_per_seq = FRONT + N_META + seq
    nblk = rows_per_seq // BLK
    past_len = 16384
    col_qkv = pool_w
    col_gate = col_qkv + dn_qkv
    col_q = col_gate + nh * dk
    col_k = col_q + nq * hd
    col_v = col_k + nkv * hd
    src_ba = pool_w + dn_qkv

    xs = x_sample.reshape(nseq, SLOTS * DEC_T, d)
    zeros = jnp.zeros((nseq, FRONT - SLOTS * DEC_T, d), F32)
    meta = jnp.broadcast_to(meta_tokens[None], (nseq, N_META, d))
    h = jnp.concatenate([xs, zeros, meta, x_prompt], axis=1).reshape(nseq * rows_per_seq, d)

    slopes = jnp.exp2(-8.0 * (jnp.arange(nq, dtype=F32) + 1.0) / nq)
    new_p = [[] for _ in range(5)]
    new_s = [[] for _ in range(5)]
    y_peer = None
    for l in range(depth):
        w_main = jnp.concatenate([w_in[l][:, :src_ba], w_in[l][:, src_ba + 2 * nh:]], axis=1).astype(BF16)
        w_ba = jnp.pad(w_in[l][:, src_ba:src_ba + 2 * nh], ((0, 0), (0, BLK - 2 * nh))).astype(BF16)
        h, xn = _addnorm(h, y_peer, norm1_g[l])
        p = _mm(xn, w_main)
        ba = _mm(xn, w_ba)

        y_pool = _pool_prompt(p, w_pool[l], s_pool[l].reshape(1, pool_w), nseq, nblk)
        y_pool = _pool_sample(p, state_pool[l], w_pool[l], s_pool[l].reshape(1, pool_w), y_pool,
                              rows_per_seq, past_len)
        y_dn, s_p = _delta_prompt(p, ba, w_conv[l], a_log[l], dt_bias[l], dn_norm_g[l], nseq, nblk,
                                  col_qkv, col_gate, nh, dk)
        y_dn, s_s = _delta_sample(p, ba, state_conv[l], state_delta[l], w_conv[l], a_log[l], dt_bias[l],
                                  dn_norm_g[l], y_dn, rows_per_seq, col_qkv, col_gate, nh, dk)
        y_att = _attn_prompt(p, slopes, attn_sinks[l], nseq, nblk, col_q, col_k, col_v, nkv, grp, hd, window)
        y_att = _attn_sample(p, cache_k[l], cache_v[l], slopes, attn_sinks[l], y_att, rows_per_seq,
                             col_q, col_k, col_v, nkv, grp, hd, window)
        h = _outproj(y_pool, y_dn, y_att, w_out[l].astype(BF16), h)

        _, xn2 = _addnorm(h, None, norm2_g[l])
        q = _mm(xn2, peer_w_query[l].astype(BF16))
        rank2, cut, e1, e2 = _peer_topk(q, peer_sub_keys[l])
        y_peer = _peer_expert(xn2, peer_w_down[l].astype(BF16), peer_w_up[l].astype(BF16), rank2, cut, e1, e2)

        p3 = p.reshape(nseq, rows_per_seq, p.shape[1])
        ps = p3[:, :SLOTS * DEC_T].reshape(nsamp, DEC_T, p.shape[1])
        new_p[0].append(p3[:, -POOL_HIST:, :pool_w])
        new_s[0].append(jnp.concatenate([state_pool[l], ps[:, :, :pool_w]], axis=1)[:, -POOL_HIST:])
        new_p[1].append(p3[:, -(DN_CONV - 1):, col_qkv:col_gate])
        new_s[1].append(jnp.concatenate([state_conv[l], ps[:, :, col_qkv:col_gate]], axis=1)[:, -(DN_CONV - 1):])
        new_p[2].append(s_p)
        new_s[2].append(s_s)
        new_p[3].append(p3[:, -window:, col_k:col_v].reshape(nseq, window, nkv, hd))
        new_s[3].append(jnp.concatenate([cache_k[l], ps[:, :, col_k:col_v].reshape(nsamp, DEC_T, nkv, hd)],
                                        axis=1)[:, -wc:])
        new_p[4].append(p3[:, -window:, col_v:].reshape(nseq, window, nkv, hd))
        new_s[4].append(jnp.concatenate([cache_v[l], ps[:, :, col_v:].reshape(nsamp, DEC_T, nkv, hd)],
                                        axis=1)[:, -wc:])

    y_prompt = _final_norm(h, y_peer, final_norm_g, nseq, nblk, BLK, 1, nblk - 1)
    y_sample = _final_norm(h, y_peer, final_norm_g, nseq, rows_per_seq // (SLOTS * DEC_T), SLOTS * DEC_T, 0, 1)
    y_sample = y_sample.reshape(nsamp, DEC_T, d)
    pool_p, conv_p, delta_p, k_p, v_p = (jnp.stack(a) for a in new_p)
    pool_s, conv_s, delta_s, k_s, v_s = (jnp.stack(a) for a in new_s)
    return (y_prompt, y_sample, pool_p, pool_s, conv_p, conv_s, delta_p, delta_s, k_p, k_s, v_p, v_s)
```

```python
import functools

import jax
import jax.numpy as jnp
from jax import lax
from jax.experimental import pallas as pl
from jax.experimental.pallas import tpu as pltpu

F32 = jnp.float32
BF16 = jnp.bfloat16

EPS = 1e-6
NEG_INF = -1e30
LOWEST = -3.0e38

N_META = 16
BLK = 128
FRONT = BLK - N_META
DEC_T = 8
SLOTS = 8
POOL_WINDOWS = (2, 4, 8, 16)
POOL_HIST = 15
DN_CONV = 4
TOPK = 16
NKEYS = 128
VMEM_LIMIT = 56 * 1024 * 1024


def _cparams(*sem):
    return pltpu.CompilerParams(dimension_semantics=sem, vmem_limit_bytes=VMEM_LIMIT)


def _bdot(a, b):
    return jnp.dot(a.astype(BF16), b.astype(BF16), preferred_element_type=F32)


def _bdot_nt(a, b):
    return lax.dot_general(a.astype(BF16), b.astype(BF16), (((1,), (1,)), ((), ())),
                           preferred_element_type=F32)


def _silu(x):
    return x * (1.0 / (1.0 + jnp.exp(-x)))


def _addnorm_body(*refs, add):
    if add:
        h_ref, y_ref, g_ref, hs_ref, xn_ref = refs
        x = h_ref[...] + y_ref[...]
        hs_ref[...] = x
    else:
        h_ref, g_ref, xn_ref = refs
        x = h_ref[...]
    ms = jnp.mean(x * x, axis=-1, keepdims=True)
    xn_ref[...] = (x * lax.rsqrt(ms + EPS) * g_ref[...]).astype(xn_ref.dtype)


def _addnorm(h, y, g, *, tm=256, out_dtype=BF16):
    n, d = h.shape
    row = pl.BlockSpec((tm, d), lambda i: (i, 0))
    gspec = pl.BlockSpec((1, d), lambda i: (0, 0))
    g2 = g.reshape(1, d).astype(F32)
    if y is None:
        xn = pl.pallas_call(
            functools.partial(_addnorm_body, add=False),
            grid=(n // tm,), in_specs=[row, gspec], out_specs=row,
            out_shape=jax.ShapeDtypeStruct((n, d), out_dtype),
            compiler_params=_cparams("parallel"), name="norm")(h, g2)
        return h, xn
    hs, xn = pl.pallas_call(
        functools.partial(_addnorm_body, add=True),
        grid=(n // tm,), in_specs=[row, row, gspec], out_specs=[row, row],
        out_shape=[jax.ShapeDtypeStruct((n, d), F32), jax.ShapeDtypeStruct((n, d), out_dtype)],
        compiler_params=_cparams("parallel"), name="add_norm")(h, y, g2)
    return hs, xn


def _cast_body(x_ref, o_ref):
    o_ref[...] = x_ref[0].astype(o_ref.dtype)


def _cast(x, l, dtype=BF16, *, tr=512):
    _, r, c = x.shape
    return pl.pallas_call(
        _cast_body, grid=(r // tr,),
        in_specs=[pl.BlockSpec((1, tr, c), lambda i: (l, i, 0))],
        out_specs=pl.BlockSpec((tr, c), lambda i: (i, 0)),
        out_shape=jax.ShapeDtypeStruct((r, c), dtype),
        compiler_params=_cparams("parallel"), name="cast")(x)


def _cast_w_in_body(x_ref, main_ref, ba_ref, *, lo, nba):
    x = x_ref[0]
    main_ref[:, :lo] = x[:, :lo].astype(BF16)
    main_ref[:, lo:] = x[:, lo + nba:].astype(BF16)
    ba_ref[...] = jnp.concatenate(
        [x[:, lo:lo + nba], jnp.zeros((x.shape[0], BLK - nba), F32)], axis=1).astype(BF16)


def _cast_w_in(w_in, l, lo, nba, *, tr=256):
    _, r, c = w_in.shape
    return pl.pallas_call(
        functools.partial(_cast_w_in_body, lo=lo, nba=nba), grid=(r // tr,),
        in_specs=[pl.BlockSpec((1, tr, c), lambda i: (l, i, 0))],
        out_specs=[pl.BlockSpec((tr, c - nba), lambda i: (i, 0)), pl.BlockSpec((tr, BLK), lambda i: (i, 0))],
        out_shape=[jax.ShapeDtypeStruct((r, c - nba), BF16), jax.ShapeDtypeStruct((r, BLK), BF16)],
        compiler_params=_cparams("parallel"), name="cast_w_in")(w_in)


def _mm_body(x_ref, w_ref, o_ref):
    o_ref[...] = jnp.dot(x_ref[...], w_ref[...], preferred_element_type=F32)


def _mm(x, w, *, tm=512, tn=1024):
    m, k = x.shape
    n = w.shape[1]
    tn = min(tn, n)
    return pl.pallas_call(
        _mm_body, grid=(n // tn, m // tm),
        in_specs=[pl.BlockSpec((tm, k), lambda j, i: (i, 0)),
                  pl.BlockSpec((k, tn), lambda j, i: (0, j))],
        out_specs=pl.BlockSpec((tm, tn), lambda j, i: (i, j)),
        out_shape=jax.ShapeDtypeStruct((m, n), F32),
        compiler_params=_cparams("parallel", "parallel"), name="matmul")(x, w)


def _outproj_body(yp_ref, yd_ref, ya_ref, w_ref, h_ref, o_ref, *, wp, wd):
    acc = h_ref[...]
    acc += jnp.dot(yp_ref[...].astype(BF16), w_ref[0:wp, :], preferred_element_type=F32)
    acc += jnp.dot(yd_ref[...].astype(BF16), w_ref[wp:wp + wd, :], preferred_element_type=F32)
    acc += jnp.dot(ya_ref[...].astype(BF16), w_ref[wp + wd:, :], preferred_element_type=F32)
    o_ref[...] = acc


def _outproj(yp, yd, ya, w, h, *, tm=512, tn=1024):
    m, d = h.shape
    wp, wd, wa = yp.shape[1], yd.shape[1], ya.shape[1]
    k = wp + wd + wa
    return pl.pallas_call(
        functools.partial(_outproj_body, wp=wp, wd=wd), grid=(d // tn, m // tm),
        in_specs=[pl.BlockSpec((tm, wp), lambda j, i: (i, 0)),
                  pl.BlockSpec((tm, wd), lambda j, i: (i, 0)),
                  pl.BlockSpec((tm, wa), lambda j, i: (i, 0)),
                  pl.BlockSpec((k, tn), lambda j, i: (0, j)),
                  pl.BlockSpec((tm, tn), lambda j, i: (i, j))],
        out_specs=pl.BlockSpec((tm, tn), lambda j, i: (i, j)),
        out_shape=jax.ShapeDtypeStruct((m, d), F32),
        compiler_params=_cparams("parallel", "parallel"), name="out_proj")(yp, yd, ya, w, h)


def _pool_windows(ext_ref, u, t, pos, w_ref, s_ref, gw):
    outs = []
    for gi, w in enumerate(POOL_WINDOWS):
        sl = slice(gi * gw, (gi + 1) * gw)
        win = u[:, sl]
        for k in range(1, w):
            win = win + ext_ref[16 - k:16 - k + t, sl]
        cnt = jnp.clip(pos + 1, 1, w).astype(F32)
        d = win / cnt - u[:, sl]
        outs.append(_bdot(d, w_ref[gi]))
    return jnp.concatenate(outs, axis=-1) * s_ref[...]


def _pool_prompt_body(u_ref, w_ref, s_ref, y_ref, ext_ref, *, gw):
    n = pl.program_id(1)

    @pl.when(n == 0)
    def _():
        ext_ref[0:16, :] = jnp.zeros((16, ext_ref.shape[1]), F32)

    u = u_ref[...]
    ext_ref[16:16 + BLK, :] = u
    row = n * BLK + lax.broadcasted_iota(jnp.int32, (BLK, 1), 0)
    pos = row - FRONT
    y = _pool_windows(ext_ref, u, BLK, pos, w_ref, s_ref, gw)
    y_ref[...] = jnp.where(pos >= 0, y, 0.0)
    ext_ref[0:16, :] = u[BLK - 16:, :]


def _pool_prompt(p, w_pool, s_pool, nseq, nblk):
    pw = w_pool.shape[0] * w_pool.shape[1]
    gw = w_pool.shape[1]
    return pl.pallas_call(
        functools.partial(_pool_prompt_body, gw=gw), grid=(nseq, nblk),
        in_specs=[pl.BlockSpec((BLK, pw), lambda b, n: (b * nblk + n, 0)),
                  pl.BlockSpec(w_pool.shape, lambda b, n: (0, 0, 0)),
                  pl.BlockSpec((1, pw), lambda b, n: (0, 0))],
        out_specs=pl.BlockSpec((BLK, pw), lambda b, n: (b * nblk + n, 0)),
        out_shape=jax.ShapeDtypeStruct((p.shape[0], pw), F32),
        scratch_shapes=[pltpu.VMEM((16 + BLK, pw), F32)],
        compiler_params=_cparams("parallel", "arbitrary"), name="pool_prompt")(p, w_pool, s_pool)


def _pool_sample_body(u_ref, hist_ref, w_ref, s_ref, ybuf_ref, y_ref, ext_ref, *, gw, pos0):
    del ybuf_ref
    u = u_ref[...]
    ext_ref[0:1, :] = jnp.zeros((1, ext_ref.shape[1]), F32)
    ext_ref[1:16, :] = hist_ref[0]
    ext_ref[16:16 + DEC_T, :] = u
    pos = pos0 + lax.broadcasted_iota(jnp.int32, (DEC_T, 1), 0)
    y_ref[...] = _pool_windows(ext_ref, u, DEC_T, pos, w_ref, s_ref, gw)


def _sample_row_block(s, rows_per_seq):
    return (s // SLOTS) * (rows_per_seq // DEC_T) + s % SLOTS


def _pool_sample(p, hist, w_pool, s_pool, ybuf, rows_per_seq, pos0):
    nb = hist.shape[0]
    pw = hist.shape[2]
    gw = w_pool.shape[1]
    rowmap = lambda s: (_sample_row_block(s, rows_per_seq), 0)
    return pl.pallas_call(
        functools.partial(_pool_sample_body, gw=gw, pos0=pos0), grid=(nb,),
        in_specs=[pl.BlockSpec((DEC_T, pw), rowmap),
                  pl.BlockSpec((1, POOL_HIST, pw), lambda s: (s, 0, 0)),
                  pl.BlockSpec(w_pool.shape, lambda s: (0, 0, 0)),
                  pl.BlockSpec((1, pw), lambda s: (0, 0)),
                  pl.BlockSpec(memory_space=pl.ANY)],
        out_specs=pl.BlockSpec((DEC_T, pw), rowmap),
        out_shape=jax.ShapeDtypeStruct(ybuf.shape, F32),
        scratch_shapes=[pltpu.VMEM((16 + DEC_T, pw), F32)],
        input_output_aliases={4: 0},
        compiler_params=_cparams("arbitrary"), name="pool_sample")(p, hist, w_pool, s_pool, ybuf)


def _cumsum_lanes(x):
    lane = lax.broadcasted_iota(jnp.int32, x.shape, 1)
    s = 1
    while s < x.shape[1]:
        x = x + jnp.where(lane >= s, pltpu.roll(x, s, axis=1), 0.0)
        s *= 2
    return x


HEAD_PACK = 2


def _unit_lower_inverse(a, ii, jj):
    x = jnp.where(ii == jj, 1.0, 0.0).astype(F32)
    s = 1
    while s < BLK:
        same_pair = (ii // (2 * s)) == (jj // (2 * s))
        lower_left = ((ii // s) % 2 == 1) & ((jj // s) % 2 == 0)
        off = jnp.where(same_pair & lower_left, a, 0.0)
        x = x - _bdot(x, _bdot(off, x))
        s *= 2
    return x


def _delta_scalars(ba, valid, alog_ref, dtb_ref, nh):
    bat = ba.T
    beta = jnp.where(valid, 1.0 / (1.0 + jnp.exp(-bat[0:nh])), 0.0)
    z = bat[nh:2 * nh] + dtb_ref[...]
    softplus = jnp.maximum(z, 0.0) + jnp.log(1.0 + jnp.exp(-jnp.abs(z)))
    g = jnp.where(valid, -jnp.exp(alog_ref[...]) * softplus, 0.0)
    gc = _cumsum_lanes(g)
    glast = jnp.broadcast_to(gc[:, BLK - 1:BLK], gc.shape)
    eg = jnp.exp(gc)
    rows = jnp.concatenate(
        [gc, eg, beta, beta * eg, jnp.exp(glast - gc), jnp.exp(glast),
         jnp.zeros((BLK - 6 * nh, BLK), F32)], axis=0)
    return gc, rows.T


def _l2n(x):
    return x * lax.rsqrt(jnp.sum(x * x, axis=-1, keepdims=True) + EPS)


def _delta_chunk_small(xq, xk, xv, ba, alog_ref, dtb_ref, s_ref, nh, dk):
    t = xq.shape[0]
    zrows = jnp.zeros((BLK - t, BLK), F32)
    lane = lax.broadcasted_iota(jnp.int32, (nh, BLK), 1)
    gc, cols = _delta_scalars(jnp.concatenate([ba, zrows], axis=0), lane < t, alog_ref, dtb_ref, nh)
    ii = lax.broadcasted_iota(jnp.int32, (t, BLK), 0)
    jj = lax.broadcasted_iota(jnp.int32, (t, BLK), 1)
    incl = ii >= jj
    strict = ii > jj
    outs = []
    for h in range(nh):
        sl = slice(h * dk, (h + 1) * dk)
        q = _l2n(xq[:, sl]) * (dk ** -0.5)
        k = _l2n(xk[:, sl])
        v = xv[:, sl]
        col = lambda qi: cols[0:t, qi * nh + h:qi * nh + h + 1]
        kpad = jnp.concatenate([k, zrows], axis=0)
        diff = col(0) - gc[h:h + 1, :]
        decay = jnp.where(incl, jnp.exp(jnp.where(incl, diff, 0.0)), 0.0)
        a_mat = jnp.where(strict, _bdot_nt(k, kpad) * decay * col(2), 0.0)
        qk = _bdot_nt(q, kpad) * decay
        wu = jnp.concatenate([col(3) * k, col(2) * v], axis=-1)
        for j in range(t - 1):
            wu = wu - a_mat[:, j:j + 1] * wu[j:j + 1, :]
        s = s_ref[h]
        v_new = wu[:, dk:] - _bdot(wu[:, :dk], s)
        o = col(1) * _bdot(q, s)
        for j in range(t):
            o = o + qk[:, j:j + 1] * v_new[j:j + 1, :]
        kd = jnp.concatenate([k * col(4), zrows], axis=0).T
        s_ref[h] = cols[:, 5 * nh + h:5 * nh + h + 1] * s + _bdot(kd, jnp.concatenate([v_new, zrows], axis=0))
        outs.append(o)
    return outs


def _delta_chunk(xq, xk, xv, ba, valid, alog_ref, dtb_ref, s_ref, nh, dk):
    assert dk == BLK and nh % HEAD_PACK == 0
    gc, cols = _delta_scalars(ba, valid, alog_ref, dtb_ref, nh)
    n = HEAD_PACK * BLK
    ii = lax.broadcasted_iota(jnp.int32, (n, n), 0)
    jj = lax.broadcasted_iota(jnp.int32, (n, n), 1)
    same = (ii // BLK) == (jj // BLK)
    incl = same & (ii >= jj)
    strict = same & (ii > jj)
    outs = []
    s_old = [s_ref[h] for h in range(nh)]
    s_out = []
    for p in range(nh // HEAD_PACK):
        heads = range(p * HEAD_PACK, (p + 1) * HEAD_PACK)
        stack = lambda f: jnp.concatenate([f(h) for h in heads], axis=0)
        q = stack(lambda h: _l2n(xq[:, h * dk:(h + 1) * dk]) * (dk ** -0.5))
        k = stack(lambda h: _l2n(xk[:, h * dk:(h + 1) * dk]))
        v = stack(lambda h: xv[:, h * dk:(h + 1) * dk])
        col = lambda qi: stack(lambda h: cols[:, qi * nh + h:qi * nh + h + 1])
        diag = lambda m: jnp.where(same, jnp.concatenate([m] * HEAD_PACK, axis=1), 0.0)
        diff = col(0) - jnp.concatenate([gc[h:h + 1, :] for h in heads], axis=1)
        decay = jnp.where(incl, jnp.exp(jnp.where(incl, diff, 0.0)), 0.0)
        a_mat = jnp.where(strict, _bdot_nt(k, k) * decay * col(2), 0.0)
        qk = _bdot_nt(q, k) * decay
        x = _unit_lower_inverse(a_mat, ii, jj)
        rhs = jnp.concatenate([col(3) * k, col(2) * v], axis=-1)
        wu = _bdot(x, rhs)
        w = wu[:, :dk]
        u = wu[:, dk:]
        s = stack(lambda h: s_old[h])
        v_new = u - _bdot(diag(w), s)
        o = col(1) * _bdot(diag(q), s) + _bdot(qk, v_new)
        kdt = (k * col(4)).T
        kd = jnp.where(same, jnp.concatenate([kdt] * HEAD_PACK, axis=0), 0.0)
        s_new = col(5) * s + _bdot(kd, v_new)
        for i in range(HEAD_PACK):
            s_out.append(s_new[i * BLK:(i + 1) * BLK, :])
            outs.append(o[i * BLK:(i + 1) * BLK, :])
    for h in range(nh):
        s_ref[h] = s_out[h]
    return outs


def _delta_out(o, gate, ng_ref):
    return o * lax.rsqrt(jnp.mean(o * o, axis=-1, keepdims=True) + EPS) * ng_ref[...] * _silu(gate)


def _conv_silu(ext_ref, w_ref, t):
    acc = ext_ref[5:5 + t, :] * w_ref[0:1, :]
    for i in range(1, DN_CONV):
        acc = acc + ext_ref[5 + i:5 + i + t, :] * w_ref[i:i + 1, :]
    return _silu(acc)


def _delta_prompt_body(q_ref, k_ref, v_ref, gt_ref, ba_ref, wq_ref, wk_ref, wv_ref, alog_ref, dtb_ref, ng_ref,
                       y_ref, sout_ref, eq_ref, ek_ref, ev_ref, s_ref, *, nh, dk, nblk):
    c = pl.program_id(1)

    @pl.when(c == 0)
    def _():
        for e in (eq_ref, ek_ref, ev_ref):
            e[0:8, :] = jnp.zeros((8, e.shape[1]), F32)
        s_ref[...] = jnp.zeros(s_ref.shape, F32)

    xs = []
    for x_ref, e_ref, w_ref in ((q_ref, eq_ref, wq_ref), (k_ref, ek_ref, wk_ref), (v_ref, ev_ref, wv_ref)):
        e_ref[8:8 + BLK, :] = x_ref[...]
        xs.append(_conv_silu(e_ref, w_ref, BLK))
        e_ref[0:8, :] = x_ref[BLK - 8:, :]
    lane = lax.broadcasted_iota(jnp.int32, (nh, BLK), 1)
    valid = (c > 0) | (lane >= FRONT)
    outs = _delta_chunk(xs[0], xs[1], xs[2], ba_ref[...], valid, alog_ref, dtb_ref, s_ref, nh, dk)
    row = c * BLK + lax.broadcasted_iota(jnp.int32, (BLK, 1), 0)
    for h in range(nh):
        sl = slice(h * dk, (h + 1) * dk)
        y_ref[:, sl] = jnp.where(row >= FRONT, _delta_out(outs[h], gt_ref[:, sl], ng_ref), 0.0)

    @pl.when(c == nblk - 1)
    def _():
        sout_ref[0] = s_ref[...]


def _delta_prompt(p, ba, w_conv, a_log, dt_bias, norm_g, nseq, nblk, col_q, col_gate, nh, dk):
    hw = nh * dk
    cq, ck, cv, cg = col_q // hw, col_q // hw + 1, col_q // hw + 2, col_gate // hw
    rows = lambda cb: pl.BlockSpec((BLK, hw), lambda b, c: (b * nblk + c, cb))
    wcs = lambda cb: pl.BlockSpec((DN_CONV, hw), lambda b, c: (0, cb))
    small = lambda shape: pl.BlockSpec(shape, lambda b, c: (0, 0))
    return pl.pallas_call(
        functools.partial(_delta_prompt_body, nh=nh, dk=dk, nblk=nblk), grid=(nseq, nblk),
        in_specs=[rows(cq), rows(ck), rows(cv), rows(cg),
                  pl.BlockSpec((BLK, BLK), lambda b, c: (b * nblk + c, 0)),
                  wcs(0), wcs(1), wcs(2), small((nh, 1)), small((nh, 1)), small((1, dk))],
        out_specs=[pl.BlockSpec((BLK, hw), lambda b, c: (b * nblk + c, 0)),
                   pl.BlockSpec((1, nh, dk, dk), lambda b, c: (b, 0, 0, 0))],
        out_shape=[jax.ShapeDtypeStruct((p.shape[0], hw), F32),
                   jax.ShapeDtypeStruct((nseq, nh, dk, dk), F32)],
        scratch_shapes=[pltpu.VMEM((8 + BLK, hw), F32)] * 3 + [pltpu.VMEM((nh, dk, dk), F32)],
        compiler_params=_cparams("parallel", "arbitrary"), name="delta_prompt",
    )(p, p, p, p, ba, w_conv, w_conv, w_conv, a_log.reshape(nh, 1), dt_bias.reshape(nh, 1),
      norm_g.reshape(1, dk))


def _delta_sample_body(q_ref, k_ref, v_ref, gt_ref, ba_ref, hq_ref, hk_ref, hv_ref, s0_ref,
                       wq_ref, wk_ref, wv_ref, alog_ref, dtb_ref, ng_ref, ybuf_ref,
                       y_ref, sout_ref, eq_ref, ek_ref, ev_ref, s_ref, *, nh, dk):
    del ybuf_ref
    xs = []
    for x_ref, h_ref, e_ref, w_ref in ((q_ref, hq_ref, eq_ref, wq_ref), (k_ref, hk_ref, ek_ref, wk_ref),
                                       (v_ref, hv_ref, ev_ref, wv_ref)):
        e_ref[5:8, :] = h_ref[0]
        e_ref[8:8 + DEC_T, :] = x_ref[...]
        xs.append(_conv_silu(e_ref, w_ref, DEC_T))
    s_ref[...] = s0_ref[0]
    outs = _delta_chunk_small(xs[0], xs[1], xs[2], ba_ref[...], alog_ref, dtb_ref, s_ref, nh, dk)
    for h in range(nh):
        sl = slice(h * dk, (h + 1) * dk)
        y_ref[:, sl] = _delta_out(outs[h], gt_ref[:, sl], ng_ref)
    sout_ref[0] = s_ref[...]


def _delta_sample(p, ba, conv_hist, s0, w_conv, a_log, dt_bias, norm_g, ybuf, rows_per_seq,
                  col_q, col_gate, nh, dk):
    nb = s0.shape[0]
    hw = nh * dk
    cq, ck, cv, cg = col_q // hw, col_q // hw + 1, col_q // hw + 2, col_gate // hw
    rowmap = lambda cb: (lambda s: (_sample_row_block(s, rows_per_seq), cb))
    rows = lambda cb: pl.BlockSpec((DEC_T, hw), rowmap(cb))
    hist = lambda cb: pl.BlockSpec((1, DN_CONV - 1, hw), lambda s: (s, 0, cb))
    wcs = lambda cb: pl.BlockSpec((DN_CONV, hw), lambda s: (0, cb))
    small = lambda shape: pl.BlockSpec(shape, lambda s: (0, 0))
    return pl.pallas_call(
        functools.partial(_delta_sample_body, nh=nh, dk=dk), grid=(nb,),
        in_specs=[rows(cq), rows(ck), rows(cv), rows(cg), pl.BlockSpec((DEC_T, BLK), rowmap(0)),
                  hist(0), hist(1), hist(2),
                  pl.BlockSpec((1, nh, dk, dk), lambda s: (s, 0, 0, 0)),
                  wcs(0), wcs(1), wcs(2), small((nh, 1)), small((nh, 1)), small((1, dk)),
                  pl.BlockSpec(memory_space=pl.ANY)],
        out_specs=[pl.BlockSpec((DEC_T, hw), rowmap(0)),
                   pl.BlockSpec((1, nh, dk, dk), lambda s: (s, 0, 0, 0))],
        out_shape=[jax.ShapeDtypeStruct(ybuf.shape, F32), jax.ShapeDtypeStruct(s0.shape, F32)],
        scratch_shapes=[pltpu.VMEM((8 + DEC_T, hw), F32)] * 3 + [pltpu.VMEM((nh, dk, dk), F32)],
        input_output_aliases={15: 0},
        compiler_params=_cparams("arbitrary"), name="delta_sample",
    )(p, p, p, p, ba, conv_hist, conv_hist, conv_hist, s0, w_conv, w_conv, w_conv,
      a_log.reshape(nh, 1), dt_bias.reshape(nh, 1), norm_g.reshape(1, dk), ybuf)


def _sink_softmax_pv(parts, sink):
    m = sink
    for s, _ in parts:
        m = jnp.maximum(m, jnp.max(s, axis=-1, keepdims=True))
    den = jnp.exp(sink - m)
    acc = None
    for s, v in parts:
        p = jnp.exp(s - m)
        den = den + jnp.sum(p, axis=-1, keepdims=True)
        pv = _bdot(p, v)
        acc = pv if acc is None else acc + pv
    return acc / den


def _attn_prompt_body(q_ref, kp_ref, kc_ref, vp_ref, vc_ref, slope_ref, sink_ref, y_ref, *, grp, hd, window):
    kv = pl.program_id(1)
    n = pl.program_id(2)
    i = lax.broadcasted_iota(jnp.int32, (BLK, 2 * BLK), 0)
    j = lax.broadcasted_iota(jnp.int32, (BLK, 2 * BLK), 1)
    dist = BLK + i - j
    krow = (n - 1) * BLK + j
    valid = (dist >= 0) & (dist < window) & (krow >= FRONT)
    distf = dist.astype(F32)
    kk = jnp.concatenate([kp_ref[...], kc_ref[...]], axis=0)
    vv = jnp.concatenate([vp_ref[...], vc_ref[...]], axis=0)
    for g in range(grp):
        head = kv * grp + g
        q = q_ref[:, g * hd:(g + 1) * hd]
        s = _bdot_nt(q, kk) * (hd ** -0.5)
        s = jnp.where(valid, s - slope_ref[head] * distf, NEG_INF)
        y_ref[:, g * hd:(g + 1) * hd] = _sink_softmax_pv([(s, vv)], sink_ref[head])


def _attn_prompt(p, slopes, sinks, nseq, nblk, col_q, col_k, col_v, nkv, grp, hd, window):
    gw = grp * hd
    cq, ck, cv = col_q // gw, col_k // hd, col_v // hd
    prev = lambda base: (lambda b, kv, n: (b * nblk + jnp.maximum(n - 1, 0), base + kv))
    cur = lambda base: (lambda b, kv, n: (b * nblk + n, base + kv))
    smem = pl.BlockSpec(memory_space=pltpu.SMEM)
    return pl.pallas_call(
        functools.partial(_attn_prompt_body, grp=grp, hd=hd, window=window), grid=(nseq, nkv, nblk),
        in_specs=[pl.BlockSpec((BLK, gw), cur(cq)),
                  pl.BlockSpec((BLK, hd), prev(ck)), pl.BlockSpec((BLK, hd), cur(ck)),
                  pl.BlockSpec((BLK, hd), prev(cv)), pl.BlockSpec((BLK, hd), cur(cv)),
                  smem, smem],
        out_specs=pl.BlockSpec((BLK, gw), cur(0)),
        out_shape=jax.ShapeDtypeStruct((p.shape[0], nkv * gw), F32),
        compiler_params=_cparams("parallel", "parallel", "arbitrary"), name="attn_prompt",
    )(p, p, p, p, p, slopes, sinks)


def _attn_sample_body(*refs, nkv, grp, hd, window):
    q_refs = refs[:nkv]
    k_ref, v_ref, ck_ref, cv_ref, slope_ref, sink_ref, ybuf_ref, y_ref = refs[nkv:]
    del ybuf_ref
    wc = ck_ref.shape[1]
    i = lax.broadcasted_iota(jnp.int32, (DEC_T, wc), 0)
    j = lax.broadcasted_iota(jnp.int32, (DEC_T, wc), 1)
    dist_c = wc + i - j
    valid_c = (dist_c >= 0) & (dist_c < window)
    i2 = lax.broadcasted_iota(jnp.int32, (DEC_T, DEC_T), 0)
    j2 = lax.broadcasted_iota(jnp.int32, (DEC_T, DEC_T), 1)
    dist_n = i2 - j2
    valid_n = (dist_n >= 0) & (dist_n < window)
    for kv in range(nkv):
        kc = ck_ref[0, :, kv * hd:(kv + 1) * hd]
        vc = cv_ref[0, :, kv * hd:(kv + 1) * hd]
        kn = k_ref[:, kv * hd:(kv + 1) * hd]
        vn = v_ref[:, kv * hd:(kv + 1) * hd]
        for g in range(grp):
            head = kv * grp + g
            q = q_refs[kv][:, g * hd:(g + 1) * hd]
            sc = _bdot_nt(q, kc) * (hd ** -0.5)
            sc = jnp.where(valid_c, sc - slope_ref[head] * dist_c.astype(F32), NEG_INF)
            sn = _bdot_nt(q, kn) * (hd ** -0.5)
            sn = jnp.where(valid_n, sn - slope_ref[head] * dist_n.astype(F32), NEG_INF)
            y_ref[:, head * hd:(head + 1) * hd] = _sink_softmax_pv([(sc, vc), (sn, vn)], sink_ref[head])


def _attn_sample(p, cache_k, cache_v, slopes, sinks, ybuf, rows_per_seq, col_q, col_k, col_v,
                 nkv, grp, hd, window):
    nb, wc = cache_k.shape[0], cache_k.shape[1]
    gw, kw = grp * hd, nkv * hd
    ck3 = cache_k.reshape(nb, wc, kw)
    cv3 = cache_v.reshape(nb, wc, kw)
    rowmap = lambda cb: (lambda s: (_sample_row_block(s, rows_per_seq), cb))
    smem = pl.BlockSpec(memory_space=pltpu.SMEM)
    return pl.pallas_call(
        functools.partial(_attn_sample_body, nkv=nkv, grp=grp, hd=hd, window=window), grid=(nb,),
        in_specs=[pl.BlockSpec((DEC_T, gw), rowmap(col_q // gw + kv)) for kv in range(nkv)]
        + [pl.BlockSpec((DEC_T, kw), rowmap(col_k // kw)),
           pl.BlockSpec((DEC_T, kw), rowmap(col_v // kw)),
           pl.BlockSpec((1, wc, kw), lambda s: (s, 0, 0)),
           pl.BlockSpec((1, wc, kw), lambda s: (s, 0, 0)),
           smem, smem, pl.BlockSpec(memory_space=pl.ANY)],
        out_specs=pl.BlockSpec((DEC_T, nkv * gw), rowmap(0)),
        out_shape=jax.ShapeDtypeStruct(ybuf.shape, F32),
        input_output_aliases={nkv + 6: 0},
        compiler_params=_cparams("arbitrary"), name="attn_sample",
    )(*([p] * (nkv + 2)), ck3, cv3, slopes, sinks, ybuf)


def _top_values(s, k, want_rank=False):
    riota = lax.broadcasted_iota(jnp.int32, s.shape, 0)
    rank = jnp.full(s.shape, float(s.shape[0] - 1), F32) if want_rank else None
    vals = []
    for it in range(k):
        m = jnp.max(s, axis=0, keepdims=True)
        vals.append(m)
        hit = riota == jnp.min(jnp.where(s == m, riota, s.shape[0]), axis=0, keepdims=True)
        if want_rank:
            rank = jnp.where(hit, float(it), rank)
        s = jnp.where(hit, LOWEST, s)
    return jnp.concatenate(vals, axis=0), rank


def _peer_topk_body(q_ref, keys_ref, rk_ref, cut_ref, e1_ref, e2_ref, *, nh):
    for h in range(nh):
        sc = [_bdot_nt(keys_ref[2 * h + half], q_ref[:, (2 * h + half) * NKEYS:(2 * h + half + 1) * NKEYS])
              for half in (0, 1)]
        a, rank1 = _top_values(sc[0], TOPK, True)
        b, rank2 = _top_values(sc[1], TOPK, True)
        cand = [a[0:1] + b, a[8:16] + b[0:1]] + [a[i:i + 1] + b[0:8] for i in range(1, 8)]
        top, _ = _top_values(jnp.concatenate(cand, axis=0), TOPK)
        thr = top[TOPK - 1:TOPK]
        rz = 1.0 / jnp.sum(jnp.exp(top - top[0:1]), axis=0, keepdims=True)
        cut = jnp.zeros(rank1.shape, F32)
        for r in range(TOPK):
            height = jnp.sum(jnp.where(a[r:r + 1] + b >= thr, 1.0, 0.0), axis=0, keepdims=True)
            cut = jnp.where(rank1 == float(r), height, cut)
        rk_ref[h] = rank2.astype(BF16)
        cut_ref[h] = cut
        e1_ref[h] = jnp.exp(sc[0] - a[0:1])
        e2_ref[h] = (jnp.exp(sc[1] - b[0:1]) * rz).astype(BF16)


def _peer_topk(q, sub_keys, *, tm=256):
    n = q.shape[0]
    nh = sub_keys.shape[0]
    keys = sub_keys.reshape(2 * nh, NKEYS, sub_keys.shape[-1])
    sspec = pl.BlockSpec((nh, NKEYS, tm), lambda i: (0, 0, i))
    shape = lambda dt: jax.ShapeDtypeStruct((nh, NKEYS, n), dt)
    return pl.pallas_call(
        functools.partial(_peer_topk_body, nh=nh), grid=(n // tm,),
        in_specs=[pl.BlockSpec((tm, q.shape[1]), lambda i: (i, 0)),
                  pl.BlockSpec(keys.shape, lambda i: (0, 0, 0))],
        out_specs=[sspec] * 4,
        out_shape=[shape(BF16), shape(F32), shape(F32), shape(BF16)],
        compiler_params=_cparams("parallel"), name="peer_topk")(q, keys)


def _gelu(x):
    return 0.5 * x * (1.0 + lax.erf(x * (2.0 ** -0.5)))


PACK = 16


def _peer_expert_body(xn_ref, wd_ref, wu_ref, rk_ref, cut_ref, e1_ref, e2_ref, y_ref, xt_ref, *, nh, ei, sub):
    c = pl.program_id(1)
    tm = xt_ref.shape[1]

    @pl.when(c == 0)
    def _():
        y_ref[...] = jnp.zeros(y_ref.shape, F32)
        xt_ref[...] = xn_ref[...].astype(F32).T.astype(BF16)

    def row16(ref, h, i1):
        return jnp.broadcast_to(ref[h, pl.ds(i1, 1), :], (PACK, tm)).astype(BF16)

    acc = None
    for k in range(ei // sub):
        rows = slice(k * sub * NKEYS, (k + 1) * sub * NKEYS)
        ht = jnp.dot(wd_ref[rows, :], xt_ref[...], preferred_element_type=F32)
        act = _gelu(ht)
        pieces = []
        for ii in range(sub):
            i1 = c * ei + k * sub + ii
            cut = [row16(cut_ref, h, i1) for h in range(nh)]
            e1 = [row16(e1_ref, h, i1) for h in range(nh)]
            for r in range(NKEYS // PACK):
                sl = slice(r * PACK, (r + 1) * PACK)
                gate = None
                for h in range(nh):
                    w = jnp.where(rk_ref[h, sl, :] < cut[h], e1[h] * e2_ref[h, sl, :], 0.0)
                    gate = w if gate is None else gate + w
                base = ii * NKEYS + r * PACK
                pieces.append(act[base:base + PACK, :] * gate.astype(F32))
        at = jnp.concatenate(pieces, axis=0)
        part = jnp.dot(at.T.astype(BF16), wu_ref[rows, :], preferred_element_type=F32)
        acc = part if acc is None else acc + part

    y_ref[...] += acc


PEER_TM, PEER_EI, PEER_SUB = 512, 4, 2


def _peer_expert(xn, w_down, w_up, rank2, cut, e1, e2, *, tm=PEER_TM, ei=PEER_EI, sub=PEER_SUB):
    n, d = xn.shape
    nh = rank2.shape[0]
    e = ei * NKEYS
    once = dict(pipeline_mode=pl.Buffered(1))
    sspec = pl.BlockSpec((nh, NKEYS, tm), lambda i, c: (0, 0, i), **once)
    return pl.pallas_call(
        functools.partial(_peer_expert_body, nh=nh, ei=ei, sub=sub), grid=(n // tm, w_down.shape[0] // e),
        in_specs=[pl.BlockSpec((tm, d), lambda i, c: (i, 0), **once),
                  pl.BlockSpec((e, d), lambda i, c: (c, 0)),
                  pl.BlockSpec((e, d), lambda i, c: (c, 0)),
                  sspec, sspec, sspec, sspec],
        out_specs=pl.BlockSpec((tm, d), lambda i, c: (i, 0), **once),
        out_shape=jax.ShapeDtypeStruct((n, d), F32),
        scratch_shapes=[pltpu.VMEM((d, tm), BF16)],
        compiler_params=_cparams("parallel", "arbitrary"), name="peer_expert",
    )(xn, w_down, w_up, rank2, cut, e1, e2)


def _final_norm(h, y, g, nseq, nblk, row_blk, first_blk, nblk_out):
    d = h.shape[1]
    row = pl.BlockSpec((row_blk, d), lambda b, i: (b * nblk + first_blk + i, 0))
    return pl.pallas_call(
        _final_norm_body, grid=(nseq, nblk_out),
        in_specs=[row, row, pl.BlockSpec((1, d), lambda b, i: (0, 0))],
        out_specs=pl.BlockSpec((1, row_blk, d), lambda b, i: (b, i, 0)),
        out_shape=jax.ShapeDtypeStruct((nseq, nblk_out * row_blk, d), F32),
        compiler_params=_cparams("parallel", "parallel"), name="final_norm",
    )(h, y, g.reshape(1, d).astype(F32))


def _final_norm_body(h_ref, y_ref, g_ref, o_ref):
    x = h_ref[...] + y_ref[...]
    ms = jnp.mean(x * x, axis=-1, keepdims=True)
    o_ref[0] = x * lax.rsqrt(ms + EPS) * g_ref[...]


def kernel(x_prompt, x_sample, state_pool, state_conv, state_delta, cache_k, cache_v, meta_tokens, norm1_g,
           w_in, w_pool, s_pool, w_conv, a_log, dt_bias, dn_norm_g, attn_sinks, w_out, norm2_g,
           peer_w_query, peer_sub_keys, peer_w_down, peer_w_up, final_norm_g):
    nseq, seq, d = x_prompt.shape
    nsamp, dec_t, _ = x_sample.shape
    depth = w_in.shape[0]
    pool_w = w_pool.shape[1] * w_pool.shape[2]
    dn_qkv = w_conv.shape[2]
    nh, dk = state_delta.shape[2], state_delta.shape[3]
    wc, nkv, hd = cache_k.shape[2], cache_k.shape[3], cache_k.shape[4]
    nq = attn_sinks.shape[1]
    grp = nq // nkv
    window = wc
    assert dec_t == DEC_T and nsamp == nseq * SLOTS and (N_META + seq) % BLK == N_META
    assert SLOTS * DEC_T + POOL_HIST <= FRONT and dn_qkv == 3 * nh * dk and wc == BLK
    rows_per_seq = FRONT + N_META + seq
    nblk = rows_per_seq // BLK
    past_len = 16384
    col_qkv = pool_w
    col_gate = col_qkv + dn_qkv
    col_q = col_gate + nh * dk
    col_k = col_q + nq * hd
    col_v = col_k + nkv * hd
    src_ba = pool_w + dn_qkv

    xs = x_sample.reshape(nseq, SLOTS * DEC_T, d)
    zeros = jnp.zeros((nseq, FRONT - SLOTS * DEC_T, d), F32)
    meta = jnp.broadcast_to(meta_tokens[None], (nseq, N_META, d))
    h = jnp.concatenate([xs, zeros, meta, x_prompt], axis=1).reshape(nseq * rows_per_seq, d)

    slopes = jnp.exp2(-8.0 * (jnp.arange(nq, dtype=F32) + 1.0) / nq)
    new_p = [[] for _ in range(5)]
    new_s = [[] for _ in range(5)]
    y_peer = None
    for l in range(depth):
        w_main, w_ba = _cast_w_in(w_in, l, src_ba, 2 * nh)
        h, xn = _addnorm(h, y_peer, norm1_g[l])
        p = _mm(xn, w_main)
        ba = _mm(xn, w_ba)

        y_pool = _pool_prompt(p, w_pool[l], s_pool[l].reshape(1, pool_w), nseq, nblk)
        y_pool = _pool_sample(p, state_pool[l], w_pool[l], s_pool[l].reshape(1, pool_w), y_pool,
                              rows_per_seq, past_len)
        y_dn, s_p = _delta_prompt(p, ba, w_conv[l], a_log[l], dt_bias[l], dn_norm_g[l], nseq, nblk,
                                  col_qkv, col_gate, nh, dk)
        y_dn, s_s = _delta_sample(p, ba, state_conv[l], state_delta[l], w_conv[l], a_log[l], dt_bias[l],
                                  dn_norm_g[l], y_dn, rows_per_seq, col_qkv, col_gate, nh, dk)
        y_att = _attn_prompt(p, slopes, attn_sinks[l], nseq, nblk, col_q, col_k, col_v, nkv, grp, hd, window)
        y_att = _attn_sample(p, cache_k[l], cache_v[l], slopes, attn_sinks[l], y_att, rows_per_seq,
                             col_q, col_k, col_v, nkv, grp, hd, window)
        h = _outproj(y_pool, y_dn, y_att, _cast(w_out, l), h)

        _, xn2 = _addnorm(h, None, norm2_g[l])
        q = _mm(xn2, _cast(peer_w_query, l))
        rank2, cut, e1, e2 = _peer_topk(q, peer_sub_keys[l])
        y_peer = _peer_expert(xn2, _cast(peer_w_down, l), _cast(peer_w_up, l), rank2, cut, e1, e2)

        p3 = p.reshape(nseq, rows_per_seq, p.shape[1])
        ps = p3[:, :SLOTS * DEC_T].reshape(nsamp, DEC_T, p.shape[1])
        new_p[0].append(p3[:, -POOL_HIST:, :pool_w])
        new_s[0].append(jnp.concatenate([state_pool[l], ps[:, :, :pool_w]], axis=1)[:, -POOL_HIST:])
        new_p[1].append(p3[:, -(DN_CONV - 1):, col_qkv:col_gate])
        new_s[1].append(jnp.concatenate([state_conv[l], ps[:, :, col_qkv:col_gate]], axis=1)[:, -(DN_CONV - 1):])
        new_p[2].append(s_p)
        new_s[2].append(s_s)
        new_p[3].append(p3[:, -window:, col_k:col_v].reshape(nseq, window, nkv, hd))
        new_s[3].append(jnp.concatenate([cache_k[l], ps[:, :, col_k:col_v].reshape(nsamp, DEC_T, nkv, hd)],
                                        axis=1)[:, -wc:])
        new_p[4].append(p3[:, -window:, col_v:].reshape(nseq, window, nkv, hd))
        new_s[4].append(jnp.concatenate([cache_v[l], ps[:, :, col_v:].reshape(nsamp, DEC_T, nkv, hd)],
                                        axis=1)[:, -wc:])

    y_prompt = _final_norm(h, y_peer, final_norm_g, nseq, nblk, BLK, 1, nblk - 1)
    y_sample = _final_norm(h, y_peer, final_norm_g, nseq, rows_per_seq // (SLOTS * DEC_T), SLOTS * DEC_T, 0, 1)
    y_sample = y_sample.reshape(nsamp, DEC_T, d)
    pool_p, conv_p, delta_p, k_p, v_p = (jnp.stack(a) for a in new_p)
    pool_s, conv_s, delta_s, k_s, v_s = (jnp.stack(a) for a in new_s)
    return (y_prompt, y_sample, pool_p, pool_s, conv_p, conv_s, delta_p, delta_s, k_p, k_s, v_p, v_s)
```

```python
import functools

import jax
import jax.numpy as jnp
from jax import lax
from jax.experimental import pallas as pl
from jax.experimental.pallas import tpu as pltpu

F32 = jnp.float32
BF16 = jnp.bfloat16

EPS = 1e-6
NEG_INF = -1e30
LOWEST = -3.0e38

N_META = 16
BLK = 128
FRONT = BLK - N_META
DEC_T = 8
SLOTS = 8
POOL_WINDOWS = (2, 4, 8, 16)
POOL_HIST = 15
DN_CONV = 4
TOPK = 16
NKEYS = 128
VMEM_LIMIT = 56 * 1024 * 1024


def _cparams(*sem):
    return pltpu.CompilerParams(dimension_semantics=sem, vmem_limit_bytes=VMEM_LIMIT)


def _bdot(a, b):
    return jnp.dot(a.astype(BF16), b.astype(BF16), preferred_element_type=F32)


def _bdot_nt(a, b):
    return lax.dot_general(a.astype(BF16), b.astype(BF16), (((1,), (1,)), ((), ())),
                           preferred_element_type=F32)


def _silu(x):
    return x * (1.0 / (1.0 + jnp.exp(-x)))


def _addnorm_body(*refs, add):
    if add:
        h_ref, y_ref, g_ref, hs_ref, xn_ref = refs
        x = h_ref[...] + y_ref[...]
        hs_ref[...] = x
    else:
        h_ref, g_ref, xn_ref = refs
        x = h_ref[...]
    ms = jnp.mean(x * x, axis=-1, keepdims=True)
    xn_ref[...] = (x * lax.rsqrt(ms + EPS) * g_ref[...]).astype(xn_ref.dtype)


def _addnorm(h, y, g, *, tm=256, out_dtype=BF16):
    n, d = h.shape
    row = pl.BlockSpec((tm, d), lambda i: (i, 0))
    gspec = pl.BlockSpec((1, d), lambda i: (0, 0))
    g2 = g.reshape(1, d).astype(F32)
    if y is None:
        xn = pl.pallas_call(
            functools.partial(_addnorm_body, add=False),
            grid=(n // tm,), in_specs=[row, gspec], out_specs=row,
            out_shape=jax.ShapeDtypeStruct((n, d), out_dtype),
            compiler_params=_cparams("parallel"), name="norm")(h, g2)
        return h, xn
    hs, xn = pl.pallas_call(
        functools.partial(_addnorm_body, add=True),
        grid=(n // tm,), in_specs=[row, row, gspec], out_specs=[row, row],
        out_shape=[jax.ShapeDtypeStruct((n, d), F32), jax.ShapeDtypeStruct((n, d), out_dtype)],
        compiler_params=_cparams("parallel"), name="add_norm")(h, y, g2)
    return hs, xn


def _cast_body(x_ref, o_ref):
    o_ref[...] = x_ref[0].astype(o_ref.dtype)


def _cast(x, l, dtype=BF16, *, tr=512):
    _, r, c = x.shape
    return pl.pallas_call(
        _cast_body, grid=(r // tr,),
        in_specs=[pl.BlockSpec((1, tr, c), lambda i: (l, i, 0))],
        out_specs=pl.BlockSpec((tr, c), lambda i: (i, 0)),
        out_shape=jax.ShapeDtypeStruct((r, c), dtype),
        compiler_params=_cparams("parallel"), name="cast")(x)


FP8 = jnp.float8_e4m3fn
FP8_TARGET = 224.0


def _amax_scale(x):
    a = jnp.max(jnp.max(jnp.abs(x), axis=1, keepdims=True), axis=0, keepdims=True)
    return jnp.where(a > 0.0, FP8_TARGET / a, 1.0)


def _quant_body(x_ref, o_ref, inv_ref):
    x = x_ref[0]
    s = _amax_scale(x)
    o_ref[...] = (x * s).astype(FP8)
    inv_ref[0] = jnp.broadcast_to(1.0 / s, inv_ref.shape[1:])


def _quant_fp8(x, l, *, tr):
    _, r, c = x.shape
    return pl.pallas_call(
        _quant_body, grid=(r // tr,),
        in_specs=[pl.BlockSpec((1, tr, c), lambda i: (l, i, 0))],
        out_specs=[pl.BlockSpec((tr, c), lambda i: (i, 0)), pl.BlockSpec((1, 8, BLK), lambda i: (i, 0, 0))],
        out_shape=[jax.ShapeDtypeStruct((r, c), FP8), jax.ShapeDtypeStruct((r // tr, 8, BLK), F32)],
        compiler_params=_cparams("parallel"), name="quant_fp8")(x)


def _cast_w_in_body(x_ref, main_ref, ba_ref, *, lo, nba):
    x = x_ref[0]
    main_ref[:, :lo] = x[:, :lo].astype(BF16)
    main_ref[:, lo:] = x[:, lo + nba:].astype(BF16)
    ba_ref[...] = jnp.concatenate(
        [x[:, lo:lo + nba], jnp.zeros((x.shape[0], BLK - nba), F32)], axis=1).astype(BF16)


def _cast_w_in(w_in, l, lo, nba, *, tr=256):
    _, r, c = w_in.shape
    return pl.pallas_call(
        functools.partial(_cast_w_in_body, lo=lo, nba=nba), grid=(r // tr,),
        in_specs=[pl.BlockSpec((1, tr, c), lambda i: (l, i, 0))],
        out_specs=[pl.BlockSpec((tr, c - nba), lambda i: (i, 0)), pl.BlockSpec((tr, BLK), lambda i: (i, 0))],
        out_shape=[jax.ShapeDtypeStruct((r, c - nba), BF16), jax.ShapeDtypeStruct((r, BLK), BF16)],
        compiler_params=_cparams("parallel"), name="cast_w_in")(w_in)


def _mm_body(x_ref, w_ref, o_ref):
    o_ref[...] = jnp.dot(x_ref[...], w_ref[...], preferred_element_type=F32)


def _mm(x, w, *, tm=512, tn=1024):
    m, k = x.shape
    n = w.shape[1]
    tn = min(tn, n)
    return pl.pallas_call(
        _mm_body, grid=(n // tn, m // tm),
        in_specs=[pl.BlockSpec((tm, k), lambda j, i: (i, 0)),
                  pl.BlockSpec((k, tn), lambda j, i: (0, j))],
        out_specs=pl.BlockSpec((tm, tn), lambda j, i: (i, j)),
        out_shape=jax.ShapeDtypeStruct((m, n), F32),
        compiler_params=_cparams("parallel", "parallel"), name="matmul")(x, w)


def _outproj_body(yp_ref, yd_ref, ya_ref, w_ref, h_ref, o_ref, *, wp, wd):
    acc = h_ref[...]
    acc += jnp.dot(yp_ref[...].astype(BF16), w_ref[0:wp, :], preferred_element_type=F32)
    acc += jnp.dot(yd_ref[...].astype(BF16), w_ref[wp:wp + wd, :], preferred_element_type=F32)
    acc += jnp.dot(ya_ref[...].astype(BF16), w_ref[wp + wd:, :], preferred_element_type=F32)
    o_ref[...] = acc


def _outproj(yp, yd, ya, w, h, *, tm=512, tn=1024):
    m, d = h.shape
    wp, wd, wa = yp.shape[1], yd.shape[1], ya.shape[1]
    k = wp + wd + wa
    return pl.pallas_call(
        functools.partial(_outproj_body, wp=wp, wd=wd), grid=(d // tn, m // tm),
        in_specs=[pl.BlockSpec((tm, wp), lambda j, i: (i, 0)),
                  pl.BlockSpec((tm, wd), lambda j, i: (i, 0)),
                  pl.BlockSpec((tm, wa), lambda j, i: (i, 0)),
                  pl.BlockSpec((k, tn), lambda j, i: (0, j)),
                  pl.BlockSpec((tm, tn), lambda j, i: (i, j))],
        out_specs=pl.BlockSpec((tm, tn), lambda j, i: (i, j)),
        out_shape=jax.ShapeDtypeStruct((m, d), F32),
        compiler_params=_cparams("parallel", "parallel"), name="out_proj")(yp, yd, ya, w, h)


def _pool_windows(ext_ref, u, t, pos, w_ref, s_ref, gw):
    outs = []
    for gi, w in enumerate(POOL_WINDOWS):
        sl = slice(gi * gw, (gi + 1) * gw)
        win = u[:, sl]
        for k in range(1, w):
            win = win + ext_ref[16 - k:16 - k + t, sl]
        cnt = jnp.clip(pos + 1, 1, w).astype(F32)
        d = win / cnt - u[:, sl]
        outs.append(_bdot(d, w_ref[gi]))
    return jnp.concatenate(outs, axis=-1) * s_ref[...]


def _pool_prompt_body(u_ref, w_ref, s_ref, y_ref, ext_ref, *, gw):
    n = pl.program_id(1)

    @pl.when(n == 0)
    def _():
        ext_ref[0:16, :] = jnp.zeros((16, ext_ref.shape[1]), F32)

    u = u_ref[...]
    ext_ref[16:16 + BLK, :] = u
    row = n * BLK + lax.broadcasted_iota(jnp.int32, (BLK, 1), 0)
    pos = row - FRONT
    y = _pool_windows(ext_ref, u, BLK, pos, w_ref, s_ref, gw)
    y_ref[...] = jnp.where(pos >= 0, y, 0.0)
    ext_ref[0:16, :] = u[BLK - 16:, :]


def _pool_prompt(p, w_pool, s_pool, nseq, nblk):
    pw = w_pool.shape[0] * w_pool.shape[1]
    gw = w_pool.shape[1]
    return pl.pallas_call(
        functools.partial(_pool_prompt_body, gw=gw), grid=(nseq, nblk),
        in_specs=[pl.BlockSpec((BLK, pw), lambda b, n: (b * nblk + n, 0)),
                  pl.BlockSpec(w_pool.shape, lambda b, n: (0, 0, 0)),
                  pl.BlockSpec((1, pw), lambda b, n: (0, 0))],
        out_specs=pl.BlockSpec((BLK, pw), lambda b, n: (b * nblk + n, 0)),
        out_shape=jax.ShapeDtypeStruct((p.shape[0], pw), F32),
        scratch_shapes=[pltpu.VMEM((16 + BLK, pw), F32)],
        compiler_params=_cparams("parallel", "arbitrary"), name="pool_prompt")(p, w_pool, s_pool)


def _pool_sample_body(u_ref, hist_ref, w_ref, s_ref, ybuf_ref, y_ref, ext_ref, *, gw, pos0):
    del ybuf_ref
    u = u_ref[...]
    ext_ref[0:1, :] = jnp.zeros((1, ext_ref.shape[1]), F32)
    ext_ref[1:16, :] = hist_ref[0]
    ext_ref[16:16 + DEC_T, :] = u
    pos = pos0 + lax.broadcasted_iota(jnp.int32, (DEC_T, 1), 0)
    y_ref[...] = _pool_windows(ext_ref, u, DEC_T, pos, w_ref, s_ref, gw)


def _sample_row_block(s, rows_per_seq):
    return (s // SLOTS) * (rows_per_seq // DEC_T) + s % SLOTS


def _pool_sample(p, hist, w_pool, s_pool, ybuf, rows_per_seq, pos0):
    nb = hist.shape[0]
    pw = hist.shape[2]
    gw = w_pool.shape[1]
    rowmap = lambda s: (_sample_row_block(s, rows_per_seq), 0)
    return pl.pallas_call(
        functools.partial(_pool_sample_body, gw=gw, pos0=pos0), grid=(nb,),
        in_specs=[pl.BlockSpec((DEC_T, pw), rowmap),
                  pl.BlockSpec((1, POOL_HIST, pw), lambda s: (s, 0, 0)),
                  pl.BlockSpec(w_pool.shape, lambda s: (0, 0, 0)),
                  pl.BlockSpec((1, pw), lambda s: (0, 0)),
                  pl.BlockSpec(memory_space=pl.ANY)],
        out_specs=pl.BlockSpec((DEC_T, pw), rowmap),
        out_shape=jax.ShapeDtypeStruct(ybuf.shape, F32),
        scratch_shapes=[pltpu.VMEM((16 + DEC_T, pw), F32)],
        input_output_aliases={4: 0},
        compiler_params=_cparams("arbitrary"), name="pool_sample")(p, hist, w_pool, s_pool, ybuf)


def _cumsum_lanes(x):
    lane = lax.broadcasted_iota(jnp.int32, x.shape, 1)
    s = 1
    while s < x.shape[1]:
        x = x + jnp.where(lane >= s, pltpu.roll(x, s, axis=1), 0.0)
        s *= 2
    return x


HEAD_PACK = 2


def _unit_lower_inverse(a, ii, jj):
    x = jnp.where(ii == jj, 1.0, 0.0).astype(F32)
    s = 1
    while s < BLK:
        same_pair = (ii // (2 * s)) == (jj // (2 * s))
        lower_left = ((ii // s) % 2 == 1) & ((jj // s) % 2 == 0)
        off = jnp.where(same_pair & lower_left, a, 0.0)
        x = x - _bdot(x, _bdot(off, x))
        s *= 2
    return x


def _delta_scalars(ba, valid, alog_ref, dtb_ref, nh):
    bat = ba.T
    beta = jnp.where(valid, 1.0 / (1.0 + jnp.exp(-bat[0:nh])), 0.0)
    z = bat[nh:2 * nh] + dtb_ref[...]
    softplus = jnp.maximum(z, 0.0) + jnp.log(1.0 + jnp.exp(-jnp.abs(z)))
    g = jnp.where(valid, -jnp.exp(alog_ref[...]) * softplus, 0.0)
    gc = _cumsum_lanes(g)
    glast = jnp.broadcast_to(gc[:, BLK - 1:BLK], gc.shape)
    eg = jnp.exp(gc)
    rows = jnp.concatenate(
        [gc, eg, beta, beta * eg, jnp.exp(glast - gc), jnp.exp(glast),
         jnp.zeros((BLK - 6 * nh, BLK), F32)], axis=0)
    return gc, rows.T


def _l2n(x):
    return x * lax.rsqrt(jnp.sum(x * x, axis=-1, keepdims=True) + EPS)


def _delta_chunk_small(xq, xk, xv, ba, alog_ref, dtb_ref, s_ref, nh, dk):
    t = xq.shape[0]
    zrows = jnp.zeros((BLK - t, BLK), F32)
    lane = lax.broadcasted_iota(jnp.int32, (nh, BLK), 1)
    gc, cols = _delta_scalars(jnp.concatenate([ba, zrows], axis=0), lane < t, alog_ref, dtb_ref, nh)
    ii = lax.broadcasted_iota(jnp.int32, (t, BLK), 0)
    jj = lax.broadcasted_iota(jnp.int32, (t, BLK), 1)
    incl = ii >= jj
    strict = ii > jj
    outs = []
    for h in range(nh):
        sl = slice(h * dk, (h + 1) * dk)
        q = _l2n(xq[:, sl]) * (dk ** -0.5)
        k = _l2n(xk[:, sl])
        v = xv[:, sl]
        col = lambda qi: cols[0:t, qi * nh + h:qi * nh + h + 1]
        kpad = jnp.concatenate([k, zrows], axis=0)
        diff = col(0) - gc[h:h + 1, :]
        decay = jnp.where(incl, jnp.exp(jnp.where(incl, diff, 0.0)), 0.0)
        a_mat = jnp.where(strict, _bdot_nt(k, kpad) * decay * col(2), 0.0)
        qk = _bdot_nt(q, kpad) * decay
        wu = jnp.concatenate([col(3) * k, col(2) * v], axis=-1)
        for j in range(t - 1):
            wu = wu - a_mat[:, j:j + 1] * wu[j:j + 1, :]
        s = s_ref[h]
        v_new = wu[:, dk:] - _bdot(wu[:, :dk], s)
        o = col(1) * _bdot(q, s)
        for j in range(t):
            o = o + qk[:, j:j + 1] * v_new[j:j + 1, :]
        kd = jnp.concatenate([k * col(4), zrows], axis=0).T
        s_ref[h] = cols[:, 5 * nh + h:5 * nh + h + 1] * s + _bdot(kd, jnp.concatenate([v_new, zrows], axis=0))
        outs.append(o)
    return outs


def _delta_chunk(xq, xk, xv, ba, valid, alog_ref, dtb_ref, s_ref, nh, dk):
    assert dk == BLK and nh % HEAD_PACK == 0
    gc, cols = _delta_scalars(ba, valid, alog_ref, dtb_ref, nh)
    n = HEAD_PACK * BLK
    ii = lax.broadcasted_iota(jnp.int32, (n, n), 0)
    jj = lax.broadcasted_iota(jnp.int32, (n, n), 1)
    same = (ii // BLK) == (jj // BLK)
    incl = same & (ii >= jj)
    strict = same & (ii > jj)
    outs = []
    s_old = [s_ref[h] for h in range(nh)]
    s_out = []
    for p in range(nh // HEAD_PACK):
        heads = range(p * HEAD_PACK, (p + 1) * HEAD_PACK)
        stack = lambda f: jnp.concatenate([f(h) for h in heads], axis=0)
        q = stack(lambda h: _l2n(xq[:, h * dk:(h + 1) * dk]) * (dk ** -0.5))
        k = stack(lambda h: _l2n(xk[:, h * dk:(h + 1) * dk]))
        v = stack(lambda h: xv[:, h * dk:(h + 1) * dk])
        col = lambda qi: stack(lambda h: cols[:, qi * nh + h:qi * nh + h + 1])
        diag = lambda m: jnp.where(same, jnp.concatenate([m] * HEAD_PACK, axis=1), 0.0)
        diff = col(0) - jnp.concatenate([gc[h:h + 1, :] for h in heads], axis=1)
        decay = jnp.where(incl, jnp.exp(jnp.where(incl, diff, 0.0)), 0.0)
        a_mat = jnp.where(strict, _bdot_nt(k, k) * decay * col(2), 0.0)
        qk = _bdot_nt(q, k) * decay
        x = _unit_lower_inverse(a_mat, ii, jj)
        rhs = jnp.concatenate([col(3) * k, col(2) * v], axis=-1)
        wu = _bdot(x, rhs)
        w = wu[:, :dk]
        u = wu[:, dk:]
        s = stack(lambda h: s_old[h])
        v_new = u - _bdot(diag(w), s)
        o = col(1) * _bdot(diag(q), s) + _bdot(qk, v_new)
        kdt = (k * col(4)).T
        kd = jnp.where(same, jnp.concatenate([kdt] * HEAD_PACK, axis=0), 0.0)
        s_new = col(5) * s + _bdot(kd, v_new)
        for i in range(HEAD_PACK):
            s_out.append(s_new[i * BLK:(i + 1) * BLK, :])
            outs.append(o[i * BLK:(i + 1) * BLK, :])
    for h in range(nh):
        s_ref[h] = s_out[h]
    return outs


def _delta_out(o, gate, ng_ref):
    return o * lax.rsqrt(jnp.mean(o * o, axis=-1, keepdims=True) + EPS) * ng_ref[...] * _silu(gate)


def _conv_silu(ext_ref, w_ref, t):
    acc = ext_ref[5:5 + t, :] * w_ref[0:1, :]
    for i in range(1, DN_CONV):
        acc = acc + ext_ref[5 + i:5 + i + t, :] * w_ref[i:i + 1, :]
    return _silu(acc)


def _delta_prompt_body(q_ref, k_ref, v_ref, gt_ref, ba_ref, wq_ref, wk_ref, wv_ref, alog_ref, dtb_ref, ng_ref,
                       y_ref, sout_ref, eq_ref, ek_ref, ev_ref, s_ref, *, nh, dk, nblk):
    c = pl.program_id(1)

    @pl.when(c == 0)
    def _():
        for e in (eq_ref, ek_ref, ev_ref):
            e[0:8, :] = jnp.zeros((8, e.shape[1]), F32)
        s_ref[...] = jnp.zeros(s_ref.shape, F32)

    xs = []
    for x_ref, e_ref, w_ref in ((q_ref, eq_ref, wq_ref), (k_ref, ek_ref, wk_ref), (v_ref, ev_ref, wv_ref)):
        e_ref[8:8 + BLK, :] = x_ref[...]
        xs.append(_conv_silu(e_ref, w_ref, BLK))
        e_ref[0:8, :] = x_ref[BLK - 8:, :]
    lane = lax.broadcasted_iota(jnp.int32, (nh, BLK), 1)
    valid = (c > 0) | (lane >= FRONT)
    outs = _delta_chunk(xs[0], xs[1], xs[2], ba_ref[...], valid, alog_ref, dtb_ref, s_ref, nh, dk)
    row = c * BLK + lax.broadcasted_iota(jnp.int32, (BLK, 1), 0)
    for h in range(nh):
        sl = slice(h * dk, (h + 1) * dk)
        y_ref[:, sl] = jnp.where(row >= FRONT, _delta_out(outs[h], gt_ref[:, sl], ng_ref), 0.0)

    @pl.when(c == nblk - 1)
    def _():
        sout_ref[0] = s_ref[...]


def _delta_prompt(p, ba, w_conv, a_log, dt_bias, norm_g, nseq, nblk, col_q, col_gate, nh, dk):
    hw = nh * dk
    cq, ck, cv, cg = col_q // hw, col_q // hw + 1, col_q // hw + 2, col_gate // hw
    rows = lambda cb: pl.BlockSpec((BLK, hw), lambda b, c: (b * nblk + c, cb))
    wcs = lambda cb: pl.BlockSpec((DN_CONV, hw), lambda b, c: (0, cb))
    small = lambda shape: pl.BlockSpec(shape, lambda b, c: (0, 0))
    return pl.pallas_call(
        functools.partial(_delta_prompt_body, nh=nh, dk=dk, nblk=nblk), grid=(nseq, nblk),
        in_specs=[rows(cq), rows(ck), rows(cv), rows(cg),
                  pl.BlockSpec((BLK, BLK), lambda b, c: (b * nblk + c, 0)),
                  wcs(0), wcs(1), wcs(2), small((nh, 1)), small((nh, 1)), small((1, dk))],
        out_specs=[pl.BlockSpec((BLK, hw), lambda b, c: (b * nblk + c, 0)),
                   pl.BlockSpec((1, nh, dk, dk), lambda b, c: (b, 0, 0, 0))],
        out_shape=[jax.ShapeDtypeStruct((p.shape[0], hw), F32),
                   jax.ShapeDtypeStruct((nseq, nh, dk, dk), F32)],
        scratch_shapes=[pltpu.VMEM((8 + BLK, hw), F32)] * 3 + [pltpu.VMEM((nh, dk, dk), F32)],
        compiler_params=_cparams("parallel", "arbitrary"), name="delta_prompt",
    )(p, p, p, p, ba, w_conv, w_conv, w_conv, a_log.reshape(nh, 1), dt_bias.reshape(nh, 1),
      norm_g.reshape(1, dk))


def _delta_sample_body(q_ref, k_ref, v_ref, gt_ref, ba_ref, hq_ref, hk_ref, hv_ref, s0_ref,
                       wq_ref, wk_ref, wv_ref, alog_ref, dtb_ref, ng_ref, ybuf_ref,
                       y_ref, sout_ref, eq_ref, ek_ref, ev_ref, s_ref, *, nh, dk):
    del ybuf_ref
    xs = []
    for x_ref, h_ref, e_ref, w_ref in ((q_ref, hq_ref, eq_ref, wq_ref), (k_ref, hk_ref, ek_ref, wk_ref),
                                       (v_ref, hv_ref, ev_ref, wv_ref)):
        e_ref[5:8, :] = h_ref[0]
        e_ref[8:8 + DEC_T, :] = x_ref[...]
        xs.append(_conv_silu(e_ref, w_ref, DEC_T))
    s_ref[...] = s0_ref[0]
    outs = _delta_chunk_small(xs[0], xs[1], xs[2], ba_ref[...], alog_ref, dtb_ref, s_ref, nh, dk)
    for h in range(nh):
        sl = slice(h * dk, (h + 1) * dk)
        y_ref[:, sl] = _delta_out(outs[h], gt_ref[:, sl], ng_ref)
    sout_ref[0] = s_ref[...]


def _delta_sample(p, ba, conv_hist, s0, w_conv, a_log, dt_bias, norm_g, ybuf, rows_per_seq,
                  col_q, col_gate, nh, dk):
    nb = s0.shape[0]
    hw = nh * dk
    cq, ck, cv, cg = col_q // hw, col_q // hw + 1, col_q // hw + 2, col_gate // hw
    rowmap = lambda cb: (lambda s: (_sample_row_block(s, rows_per_seq), cb))
    rows = lambda cb: pl.BlockSpec((DEC_T, hw), rowmap(cb))
    hist = lambda cb: pl.BlockSpec((1, DN_CONV - 1, hw), lambda s: (s, 0, cb))
    wcs = lambda cb: pl.BlockSpec((DN_CONV, hw), lambda s: (0, cb))
    small = lambda shape: pl.BlockSpec(shape, lambda s: (0, 0))
    return pl.pallas_call(
        functools.partial(_delta_sample_body, nh=nh, dk=dk), grid=(nb,),
        in_specs=[rows(cq), rows(ck), rows(cv), rows(cg), pl.BlockSpec((DEC_T, BLK), rowmap(0)),
                  hist(0), hist(1), hist(2),
                  pl.BlockSpec((1, nh, dk, dk), lambda s: (s, 0, 0, 0)),
                  wcs(0), wcs(1), wcs(2), small((nh, 1)), small((nh, 1)), small((1, dk)),
                  pl.BlockSpec(memory_space=pl.ANY)],
        out_specs=[pl.BlockSpec((DEC_T, hw), rowmap(0)),
                   pl.BlockSpec((1, nh, dk, dk), lambda s: (s, 0, 0, 0))],
        out_shape=[jax.ShapeDtypeStruct(ybuf.shape, F32), jax.ShapeDtypeStruct(s0.shape, F32)],
        scratch_shapes=[pltpu.VMEM((8 + DEC_T, hw), F32)] * 3 + [pltpu.VMEM((nh, dk, dk), F32)],
        input_output_aliases={15: 0},
        compiler_params=_cparams("arbitrary"), name="delta_sample",
    )(p, p, p, p, ba, conv_hist, conv_hist, conv_hist, s0, w_conv, w_conv, w_conv,
      a_log.reshape(nh, 1), dt_bias.reshape(nh, 1), norm_g.reshape(1, dk), ybuf)


def _sink_softmax_pv(parts, sink):
    m = sink
    for s, _ in parts:
        m = jnp.maximum(m, jnp.max(s, axis=-1, keepdims=True))
    den = jnp.exp(sink - m)
    acc = None
    for s, v in parts:
        p = jnp.exp(s - m)
        den = den + jnp.sum(p, axis=-1, keepdims=True)
        pv = _bdot(p, v)
        acc = pv if acc is None else acc + pv
    return acc / den


def _attn_prompt_body(q_ref, kp_ref, kc_ref, vp_ref, vc_ref, slope_ref, sink_ref, y_ref, *, grp, hd, window):
    kv = pl.program_id(1)
    n = pl.program_id(2)
    i = lax.broadcasted_iota(jnp.int32, (BLK, 2 * BLK), 0)
    j = lax.broadcasted_iota(jnp.int32, (BLK, 2 * BLK), 1)
    dist = BLK + i - j
    krow = (n - 1) * BLK + j
    valid = (dist >= 0) & (dist < window) & (krow >= FRONT)
    distf = dist.astype(F32)
    kk = jnp.concatenate([kp_ref[...], kc_ref[...]], axis=0)
    vv = jnp.concatenate([vp_ref[...], vc_ref[...]], axis=0)
    for g in range(grp):
        head = kv * grp + g
        q = q_ref[:, g * hd:(g + 1) * hd]
        s = _bdot_nt(q, kk) * (hd ** -0.5)
        s = jnp.where(valid, s - slope_ref[head] * distf, NEG_INF)
        y_ref[:, g * hd:(g + 1) * hd] = _sink_softmax_pv([(s, vv)], sink_ref[head])


def _attn_prompt(p, slopes, sinks, nseq, nblk, col_q, col_k, col_v, nkv, grp, hd, window):
    gw = grp * hd
    cq, ck, cv = col_q // gw, col_k // hd, col_v // hd
    prev = lambda base: (lambda b, kv, n: (b * nblk + jnp.maximum(n - 1, 0), base + kv))
    cur = lambda base: (lambda b, kv, n: (b * nblk + n, base + kv))
    smem = pl.BlockSpec(memory_space=pltpu.SMEM)
    return pl.pallas_call(
        functools.partial(_attn_prompt_body, grp=grp, hd=hd, window=window), grid=(nseq, nkv, nblk),
        in_specs=[pl.BlockSpec((BLK, gw), cur(cq)),
                  pl.BlockSpec((BLK, hd), prev(ck)), pl.BlockSpec((BLK, hd), cur(ck)),
                  pl.BlockSpec((BLK, hd), prev(cv)), pl.BlockSpec((BLK, hd), cur(cv)),
                  smem, smem],
        out_specs=pl.BlockSpec((BLK, gw), cur(0)),
        out_shape=jax.ShapeDtypeStruct((p.shape[0], nkv * gw), F32),
        compiler_params=_cparams("parallel", "parallel", "arbitrary"), name="attn_prompt",
    )(p, p, p, p, p, slopes, sinks)


def _attn_sample_body(*refs, nkv, grp, hd, window):
    q_refs = refs[:nkv]
    k_ref, v_ref, ck_ref, cv_ref, slope_ref, sink_ref, ybuf_ref, y_ref = refs[nkv:]
    del ybuf_ref
    wc = ck_ref.shape[1]
    i = lax.broadcasted_iota(jnp.int32, (DEC_T, wc), 0)
    j = lax.broadcasted_iota(jnp.int32, (DEC_T, wc), 1)
    dist_c = wc + i - j
    valid_c = (dist_c >= 0) & (dist_c < window)
    i2 = lax.broadcasted_iota(jnp.int32, (DEC_T, DEC_T), 0)
    j2 = lax.broadcasted_iota(jnp.int32, (DEC_T, DEC_T), 1)
    dist_n = i2 - j2
    valid_n = (dist_n >= 0) & (dist_n < window)
    for kv in range(nkv):
        kc = ck_ref[0, :, kv * hd:(kv + 1) * hd]
        vc = cv_ref[0, :, kv * hd:(kv + 1) * hd]
        kn = k_ref[:, kv * hd:(kv + 1) * hd]
        vn = v_ref[:, kv * hd:(kv + 1) * hd]
        for g in range(grp):
            head = kv * grp + g
            q = q_refs[kv][:, g * hd:(g + 1) * hd]
            sc = _bdot_nt(q, kc) * (hd ** -0.5)
            sc = jnp.where(valid_c, sc - slope_ref[head] * dist_c.astype(F32), NEG_INF)
            sn = _bdot_nt(q, kn) * (hd ** -0.5)
            sn = jnp.where(valid_n, sn - slope_ref[head] * dist_n.astype(F32), NEG_INF)
            y_ref[:, head * hd:(head + 1) * hd] = _sink_softmax_pv([(sc, vc), (sn, vn)], sink_ref[head])


def _attn_sample(p, cache_k, cache_v, slopes, sinks, ybuf, rows_per_seq, col_q, col_k, col_v,
                 nkv, grp, hd, window):
    nb, wc = cache_k.shape[0], cache_k.shape[1]
    gw, kw = grp * hd, nkv * hd
    ck3 = cache_k.reshape(nb, wc, kw)
    cv3 = cache_v.reshape(nb, wc, kw)
    rowmap = lambda cb: (lambda s: (_sample_row_block(s, rows_per_seq), cb))
    smem = pl.BlockSpec(memory_space=pltpu.SMEM)
    return pl.pallas_call(
        functools.partial(_attn_sample_body, nkv=nkv, grp=grp, hd=hd, window=window), grid=(nb,),
        in_specs=[pl.BlockSpec((DEC_T, gw), rowmap(col_q // gw + kv)) for kv in range(nkv)]
        + [pl.BlockSpec((DEC_T, kw), rowmap(col_k // kw)),
           pl.BlockSpec((DEC_T, kw), rowmap(col_v // kw)),
           pl.BlockSpec((1, wc, kw), lambda s: (s, 0, 0)),
           pl.BlockSpec((1, wc, kw), lambda s: (s, 0, 0)),
           smem, smem, pl.BlockSpec(memory_space=pl.ANY)],
        out_specs=pl.BlockSpec((DEC_T, nkv * gw), rowmap(0)),
        out_shape=jax.ShapeDtypeStruct(ybuf.shape, F32),
        input_output_aliases={nkv + 6: 0},
        compiler_params=_cparams("arbitrary"), name="attn_sample",
    )(*([p] * (nkv + 2)), ck3, cv3, slopes, sinks, ybuf)


def _top_values(s, k, want_rank=False):
    riota = lax.broadcasted_iota(jnp.int32, s.shape, 0)
    rank = jnp.full(s.shape, float(s.shape[0] - 1), F32) if want_rank else None
    vals = []
    for it in range(k):
        m = jnp.max(s, axis=0, keepdims=True)
        vals.append(m)
        hit = riota == jnp.min(jnp.where(s == m, riota, s.shape[0]), axis=0, keepdims=True)
        if want_rank:
            rank = jnp.where(hit, float(it), rank)
        s = jnp.where(hit, LOWEST, s)
    return jnp.concatenate(vals, axis=0), rank


def _peer_topk_body(q_ref, keys_ref, rk_ref, cut_ref, e1_ref, e2_ref, *, nh):
    for h in range(nh):
        sc = [_bdot_nt(keys_ref[2 * h + half], q_ref[:, (2 * h + half) * NKEYS:(2 * h + half + 1) * NKEYS])
              for half in (0, 1)]
        a, rank1 = _top_values(sc[0], TOPK, True)
        b, rank2 = _top_values(sc[1], TOPK, True)
        cand = [a[0:1] + b, a[8:16] + b[0:1]] + [a[i:i + 1] + b[0:8] for i in range(1, 8)]
        top, _ = _top_values(jnp.concatenate(cand, axis=0), TOPK)
        thr = top[TOPK - 1:TOPK]
        rz = 1.0 / jnp.sum(jnp.exp(top - top[0:1]), axis=0, keepdims=True)
        cut = jnp.zeros(rank1.shape, F32)
        for r in range(TOPK):
            height = jnp.sum(jnp.where(a[r:r + 1] + b >= thr, 1.0, 0.0), axis=0, keepdims=True)
            cut = jnp.where(rank1 == float(r), height, cut)
        rk_ref[h] = rank2.astype(BF16)
        cut_ref[h] = cut
        e1_ref[h] = jnp.exp(sc[0] - a[0:1])
        e2_ref[h] = (jnp.exp(sc[1] - b[0:1]) * rz).astype(BF16)


def _peer_topk(q, sub_keys, *, tm=256):
    n = q.shape[0]
    nh = sub_keys.shape[0]
    keys = sub_keys.reshape(2 * nh, NKEYS, sub_keys.shape[-1])
    sspec = pl.BlockSpec((nh, NKEYS, tm), lambda i: (0, 0, i))
    shape = lambda dt: jax.ShapeDtypeStruct((nh, NKEYS, n), dt)
    return pl.pallas_call(
        functools.partial(_peer_topk_body, nh=nh), grid=(n // tm,),
        in_specs=[pl.BlockSpec((tm, q.shape[1]), lambda i: (i, 0)),
                  pl.BlockSpec(keys.shape, lambda i: (0, 0, 0))],
        out_specs=[sspec] * 4,
        out_shape=[shape(BF16), shape(F32), shape(F32), shape(BF16)],
        compiler_params=_cparams("parallel"), name="peer_topk")(q, keys)


def _gelu(x):
    return 0.5 * x * (1.0 + lax.erf(x * (2.0 ** -0.5)))


PACK = 16


def _peer_expert_body(xn_ref, wd_ref, wu_ref, id_ref, iu_ref, rk_ref, cut_ref, e1_ref, e2_ref, y_ref,
                      xt_ref, ix_ref, *, nh, ei, sub):
    c = pl.program_id(1)
    tm = xt_ref.shape[1]

    @pl.when(c == 0)
    def _():
        y_ref[...] = jnp.zeros(y_ref.shape, F32)
        x = xn_ref[...].astype(F32)
        sx = _amax_scale(x)
        xt_ref[...] = (x * sx).T.astype(FP8)
        ix_ref[...] = jnp.broadcast_to(1.0 / sx, ix_ref.shape)

    def row16(ref, h, i1):
        return jnp.broadcast_to(ref[h, pl.ds(i1, 1), :], (PACK, tm)).astype(BF16)

    inv_h = id_ref[0, 0:1, 0:1] * ix_ref[0:1, 0:1]
    inv_u = iu_ref[0, 0:1, 0:1]
    for k in range(ei // sub):
        rows = slice(k * sub * NKEYS, (k + 1) * sub * NKEYS)
        ht = jnp.dot(wd_ref[rows, :], xt_ref[...], preferred_element_type=F32) * inv_h
        act = _gelu(ht)
        pieces = []
        for ii in range(sub):
            i1 = c * ei + k * sub + ii
            cut = [row16(cut_ref, h, i1) for h in range(nh)]
            e1 = [row16(e1_ref, h, i1) for h in range(nh)]
            for r in range(NKEYS // PACK):
                sl = slice(r * PACK, (r + 1) * PACK)
                gate = None
                for h in range(nh):
                    w = jnp.where(rk_ref[h, sl, :] < cut[h], e1[h] * e2_ref[h, sl, :], 0.0)
                    gate = w if gate is None else gate + w
                base = ii * NKEYS + r * PACK
                pieces.append(act[base:base + PACK, :] * gate.astype(F32))
        at = jnp.concatenate(pieces, axis=0)
        sa = _amax_scale(at)
        part = jnp.dot((at * sa).T.astype(FP8), wu_ref[rows, :], preferred_element_type=F32)
        y_ref[...] += part * (inv_u / sa)


PEER_TM, PEER_EI, PEER_SUB = 512, 4, 4


def _peer_expert(xn, w_down, w_up, inv_down, inv_up, rank2, cut, e1, e2, *, tm=PEER_TM, ei=PEER_EI, sub=PEER_SUB):
    n, d = xn.shape
    nh = rank2.shape[0]
    e = ei * NKEYS
    once = dict(pipeline_mode=pl.Buffered(1))
    sspec = pl.BlockSpec((nh, NKEYS, tm), lambda i, c: (0, 0, i), **once)
    inv = pl.BlockSpec((1, 8, BLK), lambda i, c: (c, 0, 0))
    return pl.pallas_call(
        functools.partial(_peer_expert_body, nh=nh, ei=ei, sub=sub), grid=(n // tm, w_down.shape[0] // e),
        in_specs=[pl.BlockSpec((tm, d), lambda i, c: (i, 0), **once),
                  pl.BlockSpec((e, d), lambda i, c: (c, 0)),
                  pl.BlockSpec((e, d), lambda i, c: (c, 0)),
                  inv, inv, sspec, sspec, sspec, sspec],
        out_specs=pl.BlockSpec((tm, d), lambda i, c: (i, 0), **once),
        out_shape=jax.ShapeDtypeStruct((n, d), F32),
        scratch_shapes=[pltpu.VMEM((d, tm), FP8), pltpu.VMEM((8, BLK), F32)],
        compiler_params=_cparams("parallel", "arbitrary"), name="peer_expert",
    )(xn, w_down, w_up, inv_down, inv_up, rank2, cut, e1, e2)


def _final_norm(h, y, g, nseq, nblk, row_blk, first_blk, nblk_out):
    d = h.shape[1]
    row = pl.BlockSpec((row_blk, d), lambda b, i: (b * nblk + first_blk + i, 0))
    return pl.pallas_call(
        _final_norm_body, grid=(nseq, nblk_out),
        in_specs=[row, row, pl.BlockSpec((1, d), lambda b, i: (0, 0))],
        out_specs=pl.BlockSpec((1, row_blk, d), lambda b, i: (b, i, 0)),
        out_shape=jax.ShapeDtypeStruct((nseq, nblk_out * row_blk, d), F32),
        compiler_params=_cparams("parallel", "parallel"), name="final_norm",
    )(h, y, g.reshape(1, d).astype(F32))


def _final_norm_body(h_ref, y_ref, g_ref, o_ref):
    x = h_ref[...] + y_ref[...]
    ms = jnp.mean(x * x, axis=-1, keepdims=True)
    o_ref[0] = x * lax.rsqrt(ms + EPS) * g_ref[...]


def kernel(x_prompt, x_sample, state_pool, state_conv, state_delta, cache_k, cache_v, meta_tokens, norm1_g,
           w_in, w_pool, s_pool, w_conv, a_log, dt_bias, dn_norm_g, attn_sinks, w_out, norm2_g,
           peer_w_query, peer_sub_keys, peer_w_down, peer_w_up, final_norm_g):
    nseq, seq, d = x_prompt.shape
    nsamp, dec_t, _ = x_sample.shape
    depth = w_in.shape[0]
    pool_w = w_pool.shape[1] * w_pool.shape[2]
    dn_qkv = w_conv.shape[2]
    nh, dk = state_delta.shape[2], state_delta.shape[3]
    wc, nkv, hd = cache_k.shape[2], cache_k.shape[3], cache_k.shape[4]
    nq = attn_sinks.shape[1]
    grp = nq // nkv
    window = wc
    assert dec_t == DEC_T and nsamp == nseq * SLOTS and (N_META + seq) % BLK == N_META
    assert SLOTS * DEC_T + POOL_HIST <= FRONT and dn_qkv == 3 * nh * dk and wc == BLK
    rows_per_seq = FRONT + N_META + seq
    nblk = rows_per_seq // BLK
    past_len = 16384
    col_qkv = pool_w
    col_gate = col_qkv + dn_qkv
    col_q = col_gate + nh * dk
    col_k = col_q + nq * hd
    col_v = col_k + nkv * hd
    src_ba = pool_w + dn_qkv

    xs = x_sample.reshape(nseq, SLOTS * DEC_T, d)
    zeros = jnp.zeros((nseq, FRONT - SLOTS * DEC_T, d), F32)
    meta = jnp.broadcast_to(meta_tokens[None], (nseq, N_META, d))
    h = jnp.concatenate([xs, zeros, meta, x_prompt], axis=1).reshape(nseq * rows_per_seq, d)

    slopes = jnp.exp2(-8.0 * (jnp.arange(nq, dtype=F32) + 1.0) / nq)
    new_p = [[] for _ in range(5)]
    new_s = [[] for _ in range(5)]
    y_peer = None
    for l in range(depth):
        w_main, w_ba = _cast_w_in(w_in, l, src_ba, 2 * nh)
        h, xn = _addnorm(h, y_peer, norm1_g[l])
        p = _mm(xn, w_main)
        ba = _mm(xn, w_ba)

        y_pool = _pool_prompt(p, w_pool[l], s_pool[l].reshape(1, pool_w), nseq, nblk)
        y_pool = _pool_sample(p, state_pool[l], w_pool[l], s_pool[l].reshape(1, pool_w), y_pool,
                              rows_per_seq, past_len)
        y_dn, s_p = _delta_prompt(p, ba, w_conv[l], a_log[l], dt_bias[l], dn_norm_g[l], nseq, nblk,
                                  col_qkv, col_gate, nh, dk)
        y_dn, s_s = _delta_sample(p, ba, state_conv[l], state_delta[l], w_conv[l], a_log[l], dt_bias[l],
                                  dn_norm_g[l], y_dn, rows_per_seq, col_qkv, col_gate, nh, dk)
        y_att = _attn_prompt(p, slopes, attn_sinks[l], nseq, nblk, col_q, col_k, col_v, nkv, grp, hd, window)
        y_att = _attn_sample(p, cache_k[l], cache_v[l], slopes, attn_sinks[l], y_att, rows_per_seq,
                             col_q, col_k, col_v, nkv, grp, hd, window)
        h = _outproj(y_pool, y_dn, y_att, _cast(w_out, l), h)

        _, xn2 = _addnorm(h, None, norm2_g[l])
        q = _mm(xn2, _cast(peer_w_query, l))
        rank2, cut, e1, e2 = _peer_topk(q, peer_sub_keys[l])
        wd8, inv_d = _quant_fp8(peer_w_down, l, tr=PEER_EI * NKEYS)
        wu8, inv_u = _quant_fp8(peer_w_up, l, tr=PEER_EI * NKEYS)
        y_peer = _peer_expert(xn2, wd8, wu8, inv_d, inv_u, rank2, cut, e1, e2)

        p3 = p.reshape(nseq, rows_per_seq, p.shape[1])
        ps = p3[:, :SLOTS * DEC_T].reshape(nsamp, DEC_T, p.shape[1])
        new_p[0].append(p3[:, -POOL_HIST:, :pool_w])
        new_s[0].append(jnp.concatenate([state_pool[l], ps[:, :, :pool_w]], axis=1)[:, -POOL_HIST:])
        new_p[1].append(p3[:, -(DN_CONV - 1):, col_qkv:col_gate])
        new_s[1].append(jnp.concatenate([state_conv[l], ps[:, :, col_qkv:col_gate]], axis=1)[:, -(DN_CONV - 1):])
        new_p[2].append(s_p)
        new_s[2].append(s_s)
        new_p[3].append(p3[:, -window:, col_k:col_v].reshape(nseq, window, nkv, hd))
        new_s[3].append(jnp.concatenate([cache_k[l], ps[:, :, col_k:col_v].reshape(nsamp, DEC_T, nkv, hd)],
                                        axis=1)[:, -wc:])
        new_p[4].append(p3[:, -window:, col_v:].reshape(nseq, window, nkv, hd))
        new_s[4].append(jnp.concatenate([cache_v[l], ps[:, :, col_v:].reshape(nsamp, DEC_T, nkv, hd)],
                                        axis=1)[:, -wc:])

    y_prompt = _final_norm(h, y_peer, final_norm_g, nseq, nblk, BLK, 1, nblk - 1)
    y_sample = _final_norm(h, y_peer, final_norm_g, nseq, rows_per_seq // (SLOTS * DEC_T), SLOTS * DEC_T, 0, 1)
    y_sample = y_sample.reshape(nsamp, DEC_T, d)
    pool_p, conv_p, delta_p, k_p, v_p = (jnp.stack(a) for a in new_p)
    pool_s, conv_s, delta_s, k_s, v_s = (jnp.stack(a) for a in new_s)
    return (y_prompt, y_sample, pool_p, pool_s, conv_p, conv_s, delta_p, delta_s, k_p, k_s, v_p, v_s)
```

```python
import functools

import jax
import jax.numpy as jnp
from jax import lax
from jax.experimental import pallas as pl
from jax.experimental.pallas import tpu as pltpu

F32 = jnp.float32
BF16 = jnp.bfloat16

EPS = 1e-6
NEG_INF = -1e30
LOWEST = -3.0e38

N_META = 16
BLK = 128
FRONT = BLK - N_META
DEC_T = 8
SLOTS = 8
POOL_WINDOWS = (2, 4, 8, 16)
POOL_HIST = 15
DN_CONV = 4
TOPK = 16
NKEYS = 128
VMEM_LIMIT = 56 * 1024 * 1024


def _cparams(*sem):
    return pltpu.CompilerParams(dimension_semantics=sem, vmem_limit_bytes=VMEM_LIMIT)


def _bdot(a, b):
    return jnp.dot(a.astype(BF16), b.astype(BF16), preferred_element_type=F32)


def _bdot_nt(a, b):
    return lax.dot_general(a.astype(BF16), b.astype(BF16), (((1,), (1,)), ((), ())),
                           preferred_element_type=F32)


def _silu(x):
    return x * (1.0 / (1.0 + jnp.exp(-x)))


def _addnorm_body(*refs, add):
    if add:
        h_ref, y_ref, g_ref, hs_ref, xn_ref = refs
        x = h_ref[...] + y_ref[...]
        hs_ref[...] = x
    else:
        h_ref, g_ref, xn_ref = refs
        x = h_ref[...]
    ms = jnp.mean(x * x, axis=-1, keepdims=True)
    xn_ref[...] = (x * lax.rsqrt(ms + EPS) * g_ref[...]).astype(xn_ref.dtype)


def _addnorm(h, y, g, *, tm=256, out_dtype=BF16):
    n, d = h.shape
    row = pl.BlockSpec((tm, d), lambda i: (i, 0))
    gspec = pl.BlockSpec((1, d), lambda i: (0, 0))
    g2 = g.reshape(1, d).astype(F32)
    if y is None:
        xn = pl.pallas_call(
            functools.partial(_addnorm_body, add=False),
            grid=(n // tm,), in_specs=[row, gspec], out_specs=row,
            out_shape=jax.ShapeDtypeStruct((n, d), out_dtype),
            compiler_params=_cparams("parallel"), name="norm")(h, g2)
        return h, xn
    hs, xn = pl.pallas_call(
        functools.partial(_addnorm_body, add=True),
        grid=(n // tm,), in_specs=[row, row, gspec], out_specs=[row, row],
        out_shape=[jax.ShapeDtypeStruct((n, d), F32), jax.ShapeDtypeStruct((n, d), out_dtype)],
        compiler_params=_cparams("parallel"), name="add_norm")(h, y, g2)
    return hs, xn


def _cast_body(x_ref, o_ref):
    o_ref[...] = x_ref[0].astype(o_ref.dtype)


def _cast(x, l, dtype=BF16, *, tr=512):
    _, r, c = x.shape
    return pl.pallas_call(
        _cast_body, grid=(r // tr,),
        in_specs=[pl.BlockSpec((1, tr, c), lambda i: (l, i, 0))],
        out_specs=pl.BlockSpec((tr, c), lambda i: (i, 0)),
        out_shape=jax.ShapeDtypeStruct((r, c), dtype),
        compiler_params=_cparams("parallel"), name="cast")(x)


FP8 = jnp.float8_e4m3fn
FP8_TARGET = 224.0


def _amax_scale(x):
    a = jnp.max(jnp.max(jnp.abs(x), axis=1, keepdims=True), axis=0, keepdims=True)
    return jnp.where(a > 0.0, FP8_TARGET / a, 1.0)


def _quant_body(x_ref, o_ref, inv_ref):
    x = x_ref[0]
    s = _amax_scale(x)
    o_ref[...] = (x * s).astype(FP8)
    inv_ref[0] = jnp.broadcast_to(1.0 / s, inv_ref.shape[1:])


def _quant_fp8(x, l, *, tr):
    _, r, c = x.shape
    return pl.pallas_call(
        _quant_body, grid=(r // tr,),
        in_specs=[pl.BlockSpec((1, tr, c), lambda i: (l, i, 0))],
        out_specs=[pl.BlockSpec((tr, c), lambda i: (i, 0)), pl.BlockSpec((1, 8, BLK), lambda i: (i, 0, 0))],
        out_shape=[jax.ShapeDtypeStruct((r, c), FP8), jax.ShapeDtypeStruct((r // tr, 8, BLK), F32)],
        compiler_params=_cparams("parallel"), name="quant_fp8")(x)


def _cast_w_in_body(x_ref, main_ref, ba_ref, *, lo, nba):
    x = x_ref[0]
    main_ref[:, :lo] = x[:, :lo].astype(BF16)
    main_ref[:, lo:] = x[:, lo + nba:].astype(BF16)
    ba_ref[...] = jnp.concatenate(
        [x[:, lo:lo + nba], jnp.zeros((x.shape[0], BLK - nba), F32)], axis=1).astype(BF16)


def _cast_w_in(w_in, l, lo, nba, *, tr=256):
    _, r, c = w_in.shape
    return pl.pallas_call(
        functools.partial(_cast_w_in_body, lo=lo, nba=nba), grid=(r // tr,),
        in_specs=[pl.BlockSpec((1, tr, c), lambda i: (l, i, 0))],
        out_specs=[pl.BlockSpec((tr, c - nba), lambda i: (i, 0)), pl.BlockSpec((tr, BLK), lambda i: (i, 0))],
        out_shape=[jax.ShapeDtypeStruct((r, c - nba), BF16), jax.ShapeDtypeStruct((r, BLK), BF16)],
        compiler_params=_cparams("parallel"), name="cast_w_in")(w_in)


def _mm_body(x_ref, w_ref, o_ref):
    o_ref[...] = jnp.dot(x_ref[...], w_ref[...], preferred_element_type=F32)


def _mm(x, w, *, tm=512, tn=1024):
    m, k = x.shape
    n = w.shape[1]
    tn = min(tn, n)
    return pl.pallas_call(
        _mm_body, grid=(n // tn, m // tm),
        in_specs=[pl.BlockSpec((tm, k), lambda j, i: (i, 0)),
                  pl.BlockSpec((k, tn), lambda j, i: (0, j))],
        out_specs=pl.BlockSpec((tm, tn), lambda j, i: (i, j)),
        out_shape=jax.ShapeDtypeStruct((m, n), F32),
        compiler_params=_cparams("parallel", "parallel"), name="matmul")(x, w)


def _outproj_body(yp_ref, yd_ref, ya_ref, w_ref, h_ref, o_ref, *, wp, wd):
    acc = h_ref[...]
    acc += jnp.dot(yp_ref[...].astype(BF16), w_ref[0:wp, :], preferred_element_type=F32)
    acc += jnp.dot(yd_ref[...].astype(BF16), w_ref[wp:wp + wd, :], preferred_element_type=F32)
    acc += jnp.dot(ya_ref[...].astype(BF16), w_ref[wp + wd:, :], preferred_element_type=F32)
    o_ref[...] = acc


def _outproj(yp, yd, ya, w, h, *, tm=512, tn=1024):
    m, d = h.shape
    wp, wd, wa = yp.shape[1], yd.shape[1], ya.shape[1]
    k = wp + wd + wa
    return pl.pallas_call(
        functools.partial(_outproj_body, wp=wp, wd=wd), grid=(d // tn, m // tm),
        in_specs=[pl.BlockSpec((tm, wp), lambda j, i: (i, 0)),
                  pl.BlockSpec((tm, wd), lambda j, i: (i, 0)),
                  pl.BlockSpec((tm, wa), lambda j, i: (i, 0)),
                  pl.BlockSpec((k, tn), lambda j, i: (0, j)),
                  pl.BlockSpec((tm, tn), lambda j, i: (i, j))],
        out_specs=pl.BlockSpec((tm, tn), lambda j, i: (i, j)),
        out_shape=jax.ShapeDtypeStruct((m, d), F32),
        compiler_params=_cparams("parallel", "parallel"), name="out_proj")(yp, yd, ya, w, h)


def _pool_windows(ext_ref, u, t, pos, w_ref, s_ref, gw):
    outs = []
    for gi, w in enumerate(POOL_WINDOWS):
        sl = slice(gi * gw, (gi + 1) * gw)
        win = u[:, sl]
        for k in range(1, w):
            win = win + ext_ref[16 - k:16 - k + t, sl]
        cnt = jnp.clip(pos + 1, 1, w).astype(F32)
        d = win / cnt - u[:, sl]
        outs.append(_bdot(d, w_ref[gi]))
    return jnp.concatenate(outs, axis=-1) * s_ref[...]


def _pool_prompt_body(u_ref, w_ref, s_ref, y_ref, ext_ref, *, gw):
    n = pl.program_id(1)

    @pl.when(n == 0)
    def _():
        ext_ref[0:16, :] = jnp.zeros((16, ext_ref.shape[1]), F32)

    u = u_ref[...]
    ext_ref[16:16 + BLK, :] = u
    row = n * BLK + lax.broadcasted_iota(jnp.int32, (BLK, 1), 0)
    pos = row - FRONT
    y = _pool_windows(ext_ref, u, BLK, pos, w_ref, s_ref, gw)
    y_ref[...] = jnp.where(pos >= 0, y, 0.0)
    ext_ref[0:16, :] = u[BLK - 16:, :]


def _pool_prompt(p, w_pool, s_pool, nseq, nblk):
    pw = w_pool.shape[0] * w_pool.shape[1]
    gw = w_pool.shape[1]
    return pl.pallas_call(
        functools.partial(_pool_prompt_body, gw=gw), grid=(nseq, nblk),
        in_specs=[pl.BlockSpec((BLK, pw), lambda b, n: (b * nblk + n, 0)),
                  pl.BlockSpec(w_pool.shape, lambda b, n: (0, 0, 0)),
                  pl.BlockSpec((1, pw), lambda b, n: (0, 0))],
        out_specs=pl.BlockSpec((BLK, pw), lambda b, n: (b * nblk + n, 0)),
        out_shape=jax.ShapeDtypeStruct((p.shape[0], pw), F32),
        scratch_shapes=[pltpu.VMEM((16 + BLK, pw), F32)],
        compiler_params=_cparams("parallel", "arbitrary"), name="pool_prompt")(p, w_pool, s_pool)


def _pool_sample_body(u_ref, hist_ref, w_ref, s_ref, ybuf_ref, y_ref, ext_ref, *, gw, pos0):
    del ybuf_ref
    u = u_ref[...]
    ext_ref[0:1, :] = jnp.zeros((1, ext_ref.shape[1]), F32)
    ext_ref[1:16, :] = hist_ref[0]
    ext_ref[16:16 + DEC_T, :] = u
    pos = pos0 + lax.broadcasted_iota(jnp.int32, (DEC_T, 1), 0)
    y_ref[...] = _pool_windows(ext_ref, u, DEC_T, pos, w_ref, s_ref, gw)


def _sample_row_block(s, rows_per_seq):
    return (s // SLOTS) * (rows_per_seq // DEC_T) + s % SLOTS


def _pool_sample(p, hist, w_pool, s_pool, ybuf, rows_per_seq, pos0):
    nb = hist.shape[0]
    pw = hist.shape[2]
    gw = w_pool.shape[1]
    rowmap = lambda s: (_sample_row_block(s, rows_per_seq), 0)
    return pl.pallas_call(
        functools.partial(_pool_sample_body, gw=gw, pos0=pos0), grid=(nb,),
        in_specs=[pl.BlockSpec((DEC_T, pw), rowmap),
                  pl.BlockSpec((1, POOL_HIST, pw), lambda s: (s, 0, 0)),
                  pl.BlockSpec(w_pool.shape, lambda s: (0, 0, 0)),
                  pl.BlockSpec((1, pw), lambda s: (0, 0)),
                  pl.BlockSpec(memory_space=pl.ANY)],
        out_specs=pl.BlockSpec((DEC_T, pw), rowmap),
        out_shape=jax.ShapeDtypeStruct(ybuf.shape, F32),
        scratch_shapes=[pltpu.VMEM((16 + DEC_T, pw), F32)],
        input_output_aliases={4: 0},
        compiler_params=_cparams("arbitrary"), name="pool_sample")(p, hist, w_pool, s_pool, ybuf)


def _cumsum_lanes(x):
    lane = lax.broadcasted_iota(jnp.int32, x.shape, 1)
    s = 1
    while s < x.shape[1]:
        x = x + jnp.where(lane >= s, pltpu.roll(x, s, axis=1), 0.0)
        s *= 2
    return x


HEAD_PACK = 2


def _unit_lower_inverse(mats, ii, jj):
    eye = jnp.where(ii == jj, 1.0, 0.0).astype(F32)
    pair = ((ii // 2) == (jj // 2)) & (ii % 2 == 1) & (jj % 2 == 0)
    xs = [eye - jnp.where(pair, a, 0.0) for a in mats]
    s = 2
    while s < BLK:
        mask = ((ii // (2 * s)) == (jj // (2 * s))) & ((ii // s) % 2 == 1) & ((jj // s) % 2 == 0)
        ts = [_bdot(jnp.where(mask, a, 0.0), x) for a, x in zip(mats, xs)]
        xs = [x - _bdot(x, t) for x, t in zip(xs, ts)]
        s *= 2
    return xs


def _delta_scalars(ba, valid, alog_ref, dtb_ref, nh):
    bat = ba.T
    beta = jnp.where(valid, 1.0 / (1.0 + jnp.exp(-bat[0:nh])), 0.0)
    z = bat[nh:2 * nh] + dtb_ref[...]
    softplus = jnp.maximum(z, 0.0) + jnp.log(1.0 + jnp.exp(-jnp.abs(z)))
    g = jnp.where(valid, -jnp.exp(alog_ref[...]) * softplus, 0.0)
    gc = _cumsum_lanes(g)
    glast = jnp.broadcast_to(gc[:, BLK - 1:BLK], gc.shape)
    eg = jnp.exp(gc)
    rows = jnp.concatenate(
        [gc, eg, beta, beta * eg, jnp.exp(glast - gc), jnp.exp(glast),
         jnp.zeros((BLK - 6 * nh, BLK), F32)], axis=0)
    return gc, rows.T


def _l2n(x):
    return x * lax.rsqrt(jnp.sum(x * x, axis=-1, keepdims=True) + EPS)


def _delta_chunk_small(xq, xk, xv, ba, alog_ref, dtb_ref, s_ref, nh, dk):
    t = xq.shape[0]
    zrows = jnp.zeros((BLK - t, BLK), F32)
    lane = lax.broadcasted_iota(jnp.int32, (nh, BLK), 1)
    gc, cols = _delta_scalars(jnp.concatenate([ba, zrows], axis=0), lane < t, alog_ref, dtb_ref, nh)
    ii = lax.broadcasted_iota(jnp.int32, (t, BLK), 0)
    jj = lax.broadcasted_iota(jnp.int32, (t, BLK), 1)
    incl = ii >= jj
    strict = ii > jj
    outs = []
    for h in range(nh):
        sl = slice(h * dk, (h + 1) * dk)
        q = _l2n(xq[:, sl]) * (dk ** -0.5)
        k = _l2n(xk[:, sl])
        v = xv[:, sl]
        col = lambda qi: cols[0:t, qi * nh + h:qi * nh + h + 1]
        kpad = jnp.concatenate([k, zrows], axis=0)
        diff = col(0) - gc[h:h + 1, :]
        decay = jnp.where(incl, jnp.exp(jnp.where(incl, diff, 0.0)), 0.0)
        a_mat = jnp.where(strict, _bdot_nt(k, kpad) * decay * col(2), 0.0)
        qk = _bdot_nt(q, kpad) * decay
        wu = jnp.concatenate([col(3) * k, col(2) * v], axis=-1)
        for j in range(t - 1):
            wu = wu - a_mat[:, j:j + 1] * wu[j:j + 1, :]
        s = s_ref[h]
        v_new = wu[:, dk:] - _bdot(wu[:, :dk], s)
        o = col(1) * _bdot(q, s)
        for j in range(t):
            o = o + qk[:, j:j + 1] * v_new[j:j + 1, :]
        kd = jnp.concatenate([k * col(4), zrows], axis=0).T
        s_ref[h] = cols[:, 5 * nh + h:5 * nh + h + 1] * s + _bdot(kd, jnp.concatenate([v_new, zrows], axis=0))
        outs.append(o)
    return outs


def _delta_chunk(xq, xk, xv, ba, valid, alog_ref, dtb_ref, s_ref, nh, dk):
    assert dk == BLK and nh % HEAD_PACK == 0
    gc, cols = _delta_scalars(ba, valid, alog_ref, dtb_ref, nh)
    n = HEAD_PACK * BLK
    ii = lax.broadcasted_iota(jnp.int32, (n, n), 0)
    jj = lax.broadcasted_iota(jnp.int32, (n, n), 1)
    same = (ii // BLK) == (jj // BLK)
    incl = same & (ii >= jj)
    strict = same & (ii > jj)
    packs = range(nh // HEAD_PACK)
    each = lambda f: [f(p) for p in packs]
    stack = lambda p, f: jnp.concatenate([f(h) for h in range(p * HEAD_PACK, (p + 1) * HEAD_PACK)], axis=0)
    col = lambda p, qi: stack(p, lambda h: cols[:, qi * nh + h:qi * nh + h + 1])
    diag = lambda m: jnp.where(same, jnp.concatenate([m] * HEAD_PACK, axis=1), 0.0)
    q = each(lambda p: stack(p, lambda h: _l2n(xq[:, h * dk:(h + 1) * dk]) * (dk ** -0.5)))
    k = each(lambda p: stack(p, lambda h: _l2n(xk[:, h * dk:(h + 1) * dk])))
    v = each(lambda p: stack(p, lambda h: xv[:, h * dk:(h + 1) * dk]))
    s = each(lambda p: stack(p, lambda h: s_ref[h]))
    diff = each(lambda p: col(p, 0) - jnp.concatenate(
        [gc[h:h + 1, :] for h in range(p * HEAD_PACK, (p + 1) * HEAD_PACK)], axis=1))
    decay = each(lambda p: jnp.where(incl, jnp.exp(jnp.where(incl, diff[p], 0.0)), 0.0))
    a_mat = each(lambda p: jnp.where(strict, _bdot_nt(k[p], k[p]) * decay[p] * col(p, 2), 0.0))
    qk = each(lambda p: _bdot_nt(q[p], k[p]) * decay[p])
    x = _unit_lower_inverse(a_mat, ii, jj)
    rhs = each(lambda p: jnp.concatenate([col(p, 3) * k[p], col(p, 2) * v[p]], axis=-1))
    wu = each(lambda p: _bdot(x[p], rhs[p]))
    v_new = each(lambda p: wu[p][:, dk:] - _bdot(diag(wu[p][:, :dk]), s[p]))
    qs = each(lambda p: _bdot(diag(q[p]), s[p]))
    o = each(lambda p: col(p, 1) * qs[p] + _bdot(qk[p], v_new[p]))
    kd = each(lambda p: jnp.where(same, jnp.concatenate([(k[p] * col(p, 4)).T] * HEAD_PACK, axis=0), 0.0))
    s_new = each(lambda p: col(p, 5) * s[p] + _bdot(kd[p], v_new[p]))
    outs = []
    for p in packs:
        for i in range(HEAD_PACK):
            s_ref[p * HEAD_PACK + i] = s_new[p][i * BLK:(i + 1) * BLK, :]
            outs.append(o[p][i * BLK:(i + 1) * BLK, :])
    return outs


def _delta_out(o, gate, ng_ref):
    return o * lax.rsqrt(jnp.mean(o * o, axis=-1, keepdims=True) + EPS) * ng_ref[...] * _silu(gate)


def _conv_silu(ext_ref, w_ref, t):
    acc = ext_ref[5:5 + t, :] * w_ref[0:1, :]
    for i in range(1, DN_CONV):
        acc = acc + ext_ref[5 + i:5 + i + t, :] * w_ref[i:i + 1, :]
    return _silu(acc)


def _delta_prompt_body(q_ref, k_ref, v_ref, gt_ref, ba_ref, wq_ref, wk_ref, wv_ref, alog_ref, dtb_ref, ng_ref,
                       y_ref, sout_ref, eq_ref, ek_ref, ev_ref, s_ref, *, nh, dk, nblk):
    c = pl.program_id(1)

    @pl.when(c == 0)
    def _():
        for e in (eq_ref, ek_ref, ev_ref):
            e[0:8, :] = jnp.zeros((8, e.shape[1]), F32)
        s_ref[...] = jnp.zeros(s_ref.shape, F32)

    xs = []
    for x_ref, e_ref, w_ref in ((q_ref, eq_ref, wq_ref), (k_ref, ek_ref, wk_ref), (v_ref, ev_ref, wv_ref)):
        e_ref[8:8 + BLK, :] = x_ref[...]
        xs.append(_conv_silu(e_ref, w_ref, BLK))
        e_ref[0:8, :] = x_ref[BLK - 8:, :]
    lane = lax.broadcasted_iota(jnp.int32, (nh, BLK), 1)
    valid = (c > 0) | (lane >= FRONT)
    outs = _delta_chunk(xs[0], xs[1], xs[2], ba_ref[...], valid, alog_ref, dtb_ref, s_ref, nh, dk)
    row = c * BLK + lax.broadcasted_iota(jnp.int32, (BLK, 1), 0)
    for h in range(nh):
        sl = slice(h * dk, (h + 1) * dk)
        y_ref[:, sl] = jnp.where(row >= FRONT, _delta_out(outs[h], gt_ref[:, sl], ng_ref), 0.0)

    @pl.when(c == nblk - 1)
    def _():
        sout_ref[0] = s_ref[...]


def _delta_prompt(p, ba, w_conv, a_log, dt_bias, norm_g, nseq, nblk, col_q, col_gate, nh, dk):
    hw = nh * dk
    cq, ck, cv, cg = col_q // hw, col_q // hw + 1, col_q // hw + 2, col_gate // hw
    rows = lambda cb: pl.BlockSpec((BLK, hw), lambda b, c: (b * nblk + c, cb))
    wcs = lambda cb: pl.BlockSpec((DN_CONV, hw), lambda b, c: (0, cb))
    small = lambda shape: pl.BlockSpec(shape, lambda b, c: (0, 0))
    return pl.pallas_call(
        functools.partial(_delta_prompt_body, nh=nh, dk=dk, nblk=nblk), grid=(nseq, nblk),
        in_specs=[rows(cq), rows(ck), rows(cv), rows(cg),
                  pl.BlockSpec((BLK, BLK), lambda b, c: (b * nblk + c, 0)),
                  wcs(0), wcs(1), wcs(2), small((nh, 1)), small((nh, 1)), small((1, dk))],
        out_specs=[pl.BlockSpec((BLK, hw), lambda b, c: (b * nblk + c, 0)),
                   pl.BlockSpec((1, nh, dk, dk), lambda b, c: (b, 0, 0, 0))],
        out_shape=[jax.ShapeDtypeStruct((p.shape[0], hw), F32),
                   jax.ShapeDtypeStruct((nseq, nh, dk, dk), F32)],
        scratch_shapes=[pltpu.VMEM((8 + BLK, hw), F32)] * 3 + [pltpu.VMEM((nh, dk, dk), F32)],
        compiler_params=_cparams("parallel", "arbitrary"), name="delta_prompt",
    )(p, p, p, p, ba, w_conv, w_conv, w_conv, a_log.reshape(nh, 1), dt_bias.reshape(nh, 1),
      norm_g.reshape(1, dk))


def _delta_sample_body(q_ref, k_ref, v_ref, gt_ref, ba_ref, hq_ref, hk_ref, hv_ref, s0_ref,
                       wq_ref, wk_ref, wv_ref, alog_ref, dtb_ref, ng_ref, ybuf_ref,
                       y_ref, sout_ref, eq_ref, ek_ref, ev_ref, s_ref, *, nh, dk):
    del ybuf_ref
    xs = []
    for x_ref, h_ref, e_ref, w_ref in ((q_ref, hq_ref, eq_ref, wq_ref), (k_ref, hk_ref, ek_ref, wk_ref),
                                       (v_ref, hv_ref, ev_ref, wv_ref)):
        e_ref[5:8, :] = h_ref[0]
        e_ref[8:8 + DEC_T, :] = x_ref[...]
        xs.append(_conv_silu(e_ref, w_ref, DEC_T))
    s_ref[...] = s0_ref[0]
    outs = _delta_chunk_small(xs[0], xs[1], xs[2], ba_ref[...], alog_ref, dtb_ref, s_ref, nh, dk)
    for h in range(nh):
        sl = slice(h * dk, (h + 1) * dk)
        y_ref[:, sl] = _delta_out(outs[h], gt_ref[:, sl], ng_ref)
    sout_ref[0] = s_ref[...]


def _delta_sample(p, ba, conv_hist, s0, w_conv, a_log, dt_bias, norm_g, ybuf, rows_per_seq,
                  col_q, col_gate, nh, dk):
    nb = s0.shape[0]
    hw = nh * dk
    cq, ck, cv, cg = col_q // hw, col_q // hw + 1, col_q // hw + 2, col_gate // hw
    rowmap = lambda cb: (lambda s: (_sample_row_block(s, rows_per_seq), cb))
    rows = lambda cb: pl.BlockSpec((DEC_T, hw), rowmap(cb))
    hist = lambda cb: pl.BlockSpec((1, DN_CONV - 1, hw), lambda s: (s, 0, cb))
    wcs = lambda cb: pl.BlockSpec((DN_CONV, hw), lambda s: (0, cb))
    small = lambda shape: pl.BlockSpec(shape, lambda s: (0, 0))
    return pl.pallas_call(
        functools.partial(_delta_sample_body, nh=nh, dk=dk), grid=(nb,),
        in_specs=[rows(cq), rows(ck), rows(cv), rows(cg), pl.BlockSpec((DEC_T, BLK), rowmap(0)),
                  hist(0), hist(1), hist(2),
                  pl.BlockSpec((1, nh, dk, dk), lambda s: (s, 0, 0, 0)),
                  wcs(0), wcs(1), wcs(2), small((nh, 1)), small((nh, 1)), small((1, dk)),
                  pl.BlockSpec(memory_space=pl.ANY)],
        out_specs=[pl.BlockSpec((DEC_T, hw), rowmap(0)),
                   pl.BlockSpec((1, nh, dk, dk), lambda s: (s, 0, 0, 0))],
        out_shape=[jax.ShapeDtypeStruct(ybuf.shape, F32), jax.ShapeDtypeStruct(s0.shape, F32)],
        scratch_shapes=[pltpu.VMEM((8 + DEC_T, hw), F32)] * 3 + [pltpu.VMEM((nh, dk, dk), F32)],
        input_output_aliases={15: 0},
        compiler_params=_cparams("arbitrary"), name="delta_sample",
    )(p, p, p, p, ba, conv_hist, conv_hist, conv_hist, s0, w_conv, w_conv, w_conv,
      a_log.reshape(nh, 1), dt_bias.reshape(nh, 1), norm_g.reshape(1, dk), ybuf)


def _sink_softmax_pv(parts, sink):
    m = sink
    for s, _ in parts:
        m = jnp.maximum(m, jnp.max(s, axis=-1, keepdims=True))
    den = jnp.exp(sink - m)
    acc = None
    for s, v in parts:
        p = jnp.exp(s - m)
        den = den + jnp.sum(p, axis=-1, keepdims=True)
        pv = _bdot(p, v)
        acc = pv if acc is None else acc + pv
    return acc / den


def _attn_prompt_body(q_ref, kp_ref, kc_ref, vp_ref, vc_ref, slope_ref, sink_ref, y_ref, *, grp, hd, window):
    kv = pl.program_id(1)
    n = pl.program_id(2)
    i = lax.broadcasted_iota(jnp.int32, (BLK, 2 * BLK), 0)
    j = lax.broadcasted_iota(jnp.int32, (BLK, 2 * BLK), 1)
    dist = BLK + i - j
    krow = (n - 1) * BLK + j
    valid = (dist >= 0) & (dist < window) & (krow >= FRONT)
    distf = dist.astype(F32)
    kk = jnp.concatenate([kp_ref[...], kc_ref[...]], axis=0)
    vv = jnp.concatenate([vp_ref[...], vc_ref[...]], axis=0)
    for g in range(grp):
        head = kv * grp + g
        q = q_ref[:, g * hd:(g + 1) * hd]
        s = _bdot_nt(q, kk) * (hd ** -0.5)
        s = jnp.where(valid, s - slope_ref[head] * distf, NEG_INF)
        y_ref[:, g * hd:(g + 1) * hd] = _sink_softmax_pv([(s, vv)], sink_ref[head])


def _attn_prompt(p, slopes, sinks, nseq, nblk, col_q, col_k, col_v, nkv, grp, hd, window):
    gw = grp * hd
    cq, ck, cv = col_q // gw, col_k // hd, col_v // hd
    prev = lambda base: (lambda b, kv, n: (b * nblk + jnp.maximum(n - 1, 0), base + kv))
    cur = lambda base: (lambda b, kv, n: (b * nblk + n, base + kv))
    smem = pl.BlockSpec(memory_space=pltpu.SMEM)
    return pl.pallas_call(
        functools.partial(_attn_prompt_body, grp=grp, hd=hd, window=window), grid=(nseq, nkv, nblk),
        in_specs=[pl.BlockSpec((BLK, gw), cur(cq)),
                  pl.BlockSpec((BLK, hd), prev(ck)), pl.BlockSpec((BLK, hd), cur(ck)),
                  pl.BlockSpec((BLK, hd), prev(cv)), pl.BlockSpec((BLK, hd), cur(cv)),
                  smem, smem],
        out_specs=pl.BlockSpec((BLK, gw), cur(0)),
        out_shape=jax.ShapeDtypeStruct((p.shape[0], nkv * gw), F32),
        compiler_params=_cparams("parallel", "parallel", "arbitrary"), name="attn_prompt",
    )(p, p, p, p, p, slopes, sinks)


def _attn_sample_body(*refs, nkv, grp, hd, window):
    q_refs = refs[:nkv]
    k_ref, v_ref, ck_ref, cv_ref, slope_ref, sink_ref, ybuf_ref, y_ref = refs[nkv:]
    del ybuf_ref
    wc = ck_ref.shape[1]
    i = lax.broadcasted_iota(jnp.int32, (DEC_T, wc), 0)
    j = lax.broadcasted_iota(jnp.int32, (DEC_T, wc), 1)
    dist_c = wc + i - j
    valid_c = (dist_c >= 0) & (dist_c < window)
    i2 = lax.broadcasted_iota(jnp.int32, (DEC_T, DEC_T), 0)
    j2 = lax.broadcasted_iota(jnp.int32, (DEC_T, DEC_T), 1)
    dist_n = i2 - j2
    valid_n = (dist_n >= 0) & (dist_n < window)
    for kv in range(nkv):
        kc = ck_ref[0, :, kv * hd:(kv + 1) * hd]
        vc = cv_ref[0, :, kv * hd:(kv + 1) * hd]
        kn = k_ref[:, kv * hd:(kv + 1) * hd]
        vn = v_ref[:, kv * hd:(kv + 1) * hd]
        for g in range(grp):
            head = kv * grp + g
            q = q_refs[kv][:, g * hd:(g + 1) * hd]
            sc = _bdot_nt(q, kc) * (hd ** -0.5)
            sc = jnp.where(valid_c, sc - slope_ref[head] * dist_c.astype(F32), NEG_INF)
            sn = _bdot_nt(q, kn) * (hd ** -0.5)
            sn = jnp.where(valid_n, sn - slope_ref[head] * dist_n.astype(F32), NEG_INF)
            y_ref[:, head * hd:(head + 1) * hd] = _sink_softmax_pv([(sc, vc), (sn, vn)], sink_ref[head])


def _attn_sample(p, cache_k, cache_v, slopes, sinks, ybuf, rows_per_seq, col_q, col_k, col_v,
                 nkv, grp, hd, window):
    nb, wc = cache_k.shape[0], cache_k.shape[1]
    gw, kw = grp * hd, nkv * hd
    ck3 = cache_k.reshape(nb, wc, kw)
    cv3 = cache_v.reshape(nb, wc, kw)
    rowmap = lambda cb: (lambda s: (_sample_row_block(s, rows_per_seq), cb))
    smem = pl.BlockSpec(memory_space=pltpu.SMEM)
    return pl.pallas_call(
        functools.partial(_attn_sample_body, nkv=nkv, grp=grp, hd=hd, window=window), grid=(nb,),
        in_specs=[pl.BlockSpec((DEC_T, gw), rowmap(col_q // gw + kv)) for kv in range(nkv)]
        + [pl.BlockSpec((DEC_T, kw), rowmap(col_k // kw)),
           pl.BlockSpec((DEC_T, kw), rowmap(col_v // kw)),
           pl.BlockSpec((1, wc, kw), lambda s: (s, 0, 0)),
           pl.BlockSpec((1, wc, kw), lambda s: (s, 0, 0)),
           smem, smem, pl.BlockSpec(memory_space=pl.ANY)],
        out_specs=pl.BlockSpec((DEC_T, nkv * gw), rowmap(0)),
        out_shape=jax.ShapeDtypeStruct(ybuf.shape, F32),
        input_output_aliases={nkv + 6: 0},
        compiler_params=_cparams("arbitrary"), name="attn_sample",
    )(*([p] * (nkv + 2)), ck3, cv3, slopes, sinks, ybuf)


def _top_values(s, k, want_rank=False):
    riota = lax.broadcasted_iota(jnp.int32, s.shape, 0)
    rank = jnp.full(s.shape, float(s.shape[0] - 1), F32) if want_rank else None
    vals = []
    for it in range(k):
        m = jnp.max(s, axis=0, keepdims=True)
        vals.append(m)
        hit = riota == jnp.min(jnp.where(s == m, riota, s.shape[0]), axis=0, keepdims=True)
        if want_rank:
            rank = jnp.where(hit, float(it), rank)
        s = jnp.where(hit, LOWEST, s)
    return jnp.concatenate(vals, axis=0), rank


def _peer_topk_body(q_ref, keys_ref, rk_ref, cut_ref, e1_ref, e2_ref, *, nh):
    for h in range(nh):
        sc = [_bdot_nt(keys_ref[2 * h + half], q_ref[:, (2 * h + half) * NKEYS:(2 * h + half + 1) * NKEYS])
              for half in (0, 1)]
        a, rank1 = _top_values(sc[0], TOPK, True)
        b, rank2 = _top_values(sc[1], TOPK, True)
        cand = [a[0:1] + b, a[8:16] + b[0:1]] + [a[i:i + 1] + b[0:8] for i in range(1, 8)]
        top, _ = _top_values(jnp.concatenate(cand, axis=0), TOPK)
        thr = top[TOPK - 1:TOPK]
        rz = 1.0 / jnp.sum(jnp.exp(top - top[0:1]), axis=0, keepdims=True)
        cut = jnp.zeros(rank1.shape, F32)
        for r in range(TOPK):
            height = jnp.sum(jnp.where(a[r:r + 1] + b >= thr, 1.0, 0.0), axis=0, keepdims=True)
            cut = jnp.where(rank1 == float(r), height, cut)
        rk_ref[h] = rank2.astype(BF16)
        cut_ref[h] = cut
        e1_ref[h] = jnp.exp(sc[0] - a[0:1])
        e2_ref[h] = (jnp.exp(sc[1] - b[0:1]) * rz).astype(BF16)


def _peer_topk(q, sub_keys, *, tm=256):
    n = q.shape[0]
    nh = sub_keys.shape[0]
    keys = sub_keys.reshape(2 * nh, NKEYS, sub_keys.shape[-1])
    sspec = pl.BlockSpec((nh, NKEYS, tm), lambda i: (0, 0, i))
    shape = lambda dt: jax.ShapeDtypeStruct((nh, NKEYS, n), dt)
    return pl.pallas_call(
        functools.partial(_peer_topk_body, nh=nh), grid=(n // tm,),
        in_specs=[pl.BlockSpec((tm, q.shape[1]), lambda i: (i, 0)),
                  pl.BlockSpec(keys.shape, lambda i: (0, 0, 0))],
        out_specs=[sspec] * 4,
        out_shape=[shape(BF16), shape(F32), shape(F32), shape(BF16)],
        compiler_params=_cparams("parallel"), name="peer_topk")(q, keys)


def _gelu(x):
    return 0.5 * x * (1.0 + lax.erf(x * (2.0 ** -0.5)))


PACK = 16


def _peer_expert_body(xn_ref, wd_ref, wu_ref, id_ref, iu_ref, rk_ref, cut_ref, e1_ref, e2_ref, y_ref,
                      xt_ref, ix_ref, *, nh, ei, sub):
    c = pl.program_id(1)
    tm = xt_ref.shape[1]

    @pl.when(c == 0)
    def _():
        y_ref[...] = jnp.zeros(y_ref.shape, F32)
        x = xn_ref[...].astype(F32)
        sx = _amax_scale(x)
        xt_ref[...] = (x * sx).T.astype(FP8)
        ix_ref[...] = jnp.broadcast_to(1.0 / sx, ix_ref.shape)

    def row16(ref, h, i1):
        return jnp.broadcast_to(ref[h, pl.ds(i1, 1), :], (PACK, tm)).astype(BF16)

    inv_h = id_ref[0, 0:1, 0:1] * ix_ref[0:1, 0:1]
    inv_u = iu_ref[0, 0:1, 0:1]
    for k in range(ei // sub):
        rows = slice(k * sub * NKEYS, (k + 1) * sub * NKEYS)
        ht = jnp.dot(wd_ref[rows, :], xt_ref[...], preferred_element_type=F32) * inv_h
        act = _gelu(ht)
        pieces = []
        for ii in range(sub):
            i1 = c * ei + k * sub + ii
            cut = [row16(cut_ref, h, i1) for h in range(nh)]
            e1 = [row16(e1_ref, h, i1) for h in range(nh)]
            for r in range(NKEYS // PACK):
                sl = slice(r * PACK, (r + 1) * PACK)
                gate = None
                for h in range(nh):
                    w = jnp.where(rk_ref[h, sl, :] < cut[h], e1[h] * e2_ref[h, sl, :], 0.0)
                    gate = w if gate is None else gate + w
                base = ii * NKEYS + r * PACK
                pieces.append(act[base:base + PACK, :] * gate.astype(F32))
        at = jnp.concatenate(pieces, axis=0)
        sa = _amax_scale(at)
        part = jnp.dot((at * sa).T.astype(FP8), wu_ref[rows, :], preferred_element_type=F32)
        y_ref[...] += part * (inv_u / sa)


PEER_TM, PEER_EI, PEER_SUB = 512, 4, 4


def _peer_expert(xn, w_down, w_up, inv_down, inv_up, rank2, cut, e1, e2, *, tm=PEER_TM, ei=PEER_EI, sub=PEER_SUB):
    n, d = xn.shape
    nh = rank2.shape[0]
    e = ei * NKEYS
    once = dict(pipeline_mode=pl.Buffered(1))
    sspec = pl.BlockSpec((nh, NKEYS, tm), lambda i, c: (0, 0, i), **once)
    inv = pl.BlockSpec((1, 8, BLK), lambda i, c: (c, 0, 0))
    return pl.pallas_call(
        functools.partial(_peer_expert_body, nh=nh, ei=ei, sub=sub), grid=(n // tm, w_down.shape[0] // e),
        in_specs=[pl.BlockSpec((tm, d), lambda i, c: (i, 0), **once),
                  pl.BlockSpec((e, d), lambda i, c: (c, 0)),
                  pl.BlockSpec((e, d), lambda i, c: (c, 0)),
                  inv, inv, sspec, sspec, sspec, sspec],
        out_specs=pl.BlockSpec((tm, d), lambda i, c: (i, 0), **once),
        out_shape=jax.ShapeDtypeStruct((n, d), F32),
        scratch_shapes=[pltpu.VMEM((d, tm), FP8), pltpu.VMEM((8, BLK), F32)],
        compiler_params=_cparams("parallel", "arbitrary"), name="peer_expert",
    )(xn, w_down, w_up, inv_down, inv_up, rank2, cut, e1, e2)


def _final_norm(h, y, g, nseq, nblk, row_blk, first_blk, nblk_out):
    d = h.shape[1]
    row = pl.BlockSpec((row_blk, d), lambda b, i: (b * nblk + first_blk + i, 0))
    return pl.pallas_call(
        _final_norm_body, grid=(nseq, nblk_out),
        in_specs=[row, row, pl.BlockSpec((1, d), lambda b, i: (0, 0))],
        out_specs=pl.BlockSpec((1, row_blk, d), lambda b, i: (b, i, 0)),
        out_shape=jax.ShapeDtypeStruct((nseq, nblk_out * row_blk, d), F32),
        compiler_params=_cparams("parallel", "parallel"), name="final_norm",
    )(h, y, g.reshape(1, d).astype(F32))


def _final_norm_body(h_ref, y_ref, g_ref, o_ref):
    x = h_ref[...] + y_ref[...]
    ms = jnp.mean(x * x, axis=-1, keepdims=True)
    o_ref[0] = x * lax.rsqrt(ms + EPS) * g_ref[...]


def kernel(x_prompt, x_sample, state_pool, state_conv, state_delta, cache_k, cache_v, meta_tokens, norm1_g,
           w_in, w_pool, s_pool, w_conv, a_log, dt_bias, dn_norm_g, attn_sinks, w_out, norm2_g,
           peer_w_query, peer_sub_keys, peer_w_down, peer_w_up, final_norm_g):
    nseq, seq, d = x_prompt.shape
    nsamp, dec_t, _ = x_sample.shape
    depth = w_in.shape[0]
    pool_w = w_pool.shape[1] * w_pool.shape[2]
    dn_qkv = w_conv.shape[2]
    nh, dk = state_delta.shape[2], state_delta.shape[3]
    wc, nkv, hd = cache_k.shape[2], cache_k.shape[3], cache_k.shape[4]
    nq = attn_sinks.shape[1]
    grp = nq // nkv
    window = wc
    assert dec_t == DEC_T and nsamp == nseq * SLOTS and (N_META + seq) % BLK == N_META
    assert SLOTS * DEC_T + POOL_HIST <= FRONT and dn_qkv == 3 * nh * dk and wc == BLK
    rows_per_seq = FRONT + N_META + seq
    nblk = rows_per_seq // BLK
    past_len = 16384
    col_qkv = pool_w
    col_gate = col_qkv + dn_qkv
    col_q = col_gate + nh * dk
    col_k = col_q + nq * hd
    col_v = col_k + nkv * hd
    src_ba = pool_w + dn_qkv

    xs = x_sample.reshape(nseq, SLOTS * DEC_T, d)
    zeros = jnp.zeros((nseq, FRONT - SLOTS * DEC_T, d), F32)
    meta = jnp.broadcast_to(meta_tokens[None], (nseq, N_META, d))
    h = jnp.concatenate([xs, zeros, meta, x_prompt], axis=1).reshape(nseq * rows_per_seq, d)

    slopes = jnp.exp2(-8.0 * (jnp.arange(nq, dtype=F32) + 1.0) / nq)
    new_p = [[] for _ in range(5)]
    new_s = [[] for _ in range(5)]
    y_peer = None
    for l in range(depth):
        w_main, w_ba = _cast_w_in(w_in, l, src_ba, 2 * nh)
        h, xn = _addnorm(h, y_peer, norm1_g[l])
        p = _mm(xn, w_main)
        ba = _mm(xn, w_ba)

        y_pool = _pool_prompt(p, w_pool[l], s_pool[l].reshape(1, pool_w), nseq, nblk)
        y_pool = _pool_sample(p, state_pool[l], w_pool[l], s_pool[l].reshape(1, pool_w), y_pool,
                              rows_per_seq, past_len)
        y_dn, s_p = _delta_prompt(p, ba, w_conv[l], a_log[l], dt_bias[l], dn_norm_g[l], nseq, nblk,
                                  col_qkv, col_gate, nh, dk)
        y_dn, s_s = _delta_sample(p, ba, state_conv[l], state_delta[l], w_conv[l], a_log[l], dt_bias[l],
                                  dn_norm_g[l], y_dn, rows_per_seq, col_qkv, col_gate, nh, dk)
        y_att = _attn_prompt(p, slopes, attn_sinks[l], nseq, nblk, col_q, col_k, col_v, nkv, grp, hd, window)
        y_att = _attn_sample(p, cache_k[l], cache_v[l], slopes, attn_sinks[l], y_att, rows_per_seq,
                             col_q, col_k, col_v, nkv, grp, hd, window)
        h = _outproj(y_pool, y_dn, y_att, _cast(w_out, l), h)

        _, xn2 = _addnorm(h, None, norm2_g[l])
        q = _mm(xn2, _cast(peer_w_query, l))
        rank2, cut, e1, e2 = _peer_topk(q, peer_sub_keys[l])
        wd8, inv_d = _quant_fp8(peer_w_down, l, tr=PEER_EI * NKEYS)
        wu8, inv_u = _quant_fp8(peer_w_up, l, tr=PEER_EI * NKEYS)
        y_peer = _peer_expert(xn2, wd8, wu8, inv_d, inv_u, rank2, cut, e1, e2)

        p3 = p.reshape(nseq, rows_per_seq, p.shape[1])
        ps = p3[:, :SLOTS * DEC_T].reshape(nsamp, DEC_T, p.shape[1])
        new_p[0].append(p3[:, -POOL_HIST:, :pool_w])
        new_s[0].append(jnp.concatenate([state_pool[l], ps[:, :, :pool_w]], axis=1)[:, -POOL_HIST:])
        new_p[1].append(p3[:, -(DN_CONV - 1):, col_qkv:col_gate])
        new_s[1].append(jnp.concatenate([state_conv[l], ps[:, :, col_qkv:col_gate]], axis=1)[:, -(DN_CONV - 1):])
        new_p[2].append(s_p)
        new_s[2].append(s_s)
        new_p[3].append(p3[:, -window:, col_k:col_v].reshape(nseq, window, nkv, hd))
        new_s[3].append(jnp.concatenate([cache_k[l], ps[:, :, col_k:col_v].reshape(nsamp, DEC_T, nkv, hd)],
                                        axis=1)[:, -wc:])
        new_p[4].append(p3[:, -window:, col_v:].reshape(nseq, window, nkv, hd))
        new_s[4].append(jnp.concatenate([cache_v[l], ps[:, :, col_v:].reshape(nsamp, DEC_T, nkv, hd)],
                                        axis=1)[:, -wc:])

    y_prompt = _final_norm(h, y_peer, final_norm_g, nseq, nblk, BLK, 1, nblk - 1)
    y_sample = _final_norm(h, y_peer, final_norm_g, nseq, rows_per_seq // (SLOTS * DEC_T), SLOTS * DEC_T, 0, 1)
    y_sample = y_sample.reshape(nsamp, DEC_T, d)
    pool_p, conv_p, delta_p, k_p, v_p = (jnp.stack(a) for a in new_p)
    pool_s, conv_s, delta_s, k_s, v_s = (jnp.stack(a) for a in new_s)
    return (y_prompt, y_sample, pool_p, pool_s, conv_p, conv_s, delta_p, delta_s, k_p, k_s, v_p, v_s)
```

```python
import functools

import jax
import jax.numpy as jnp
from jax import lax
from jax.experimental import pallas as pl
from jax.experimental.pallas import tpu as pltpu

F32 = jnp.float32
BF16 = jnp.bfloat16

EPS = 1e-6
NEG_INF = -1e30
LOWEST = -3.0e38

N_META = 16
BLK = 128
FRONT = BLK - N_META
DEC_T = 8
SLOTS = 8
POOL_WINDOWS = (2, 4, 8, 16)
POOL_HIST = 15
DN_CONV = 4
TOPK = 16
NKEYS = 128
VMEM_LIMIT = 56 * 1024 * 1024


def _cparams(*sem):
    return pltpu.CompilerParams(dimension_semantics=sem, vmem_limit_bytes=VMEM_LIMIT)


def _bdot(a, b):
    return jnp.dot(a.astype(BF16), b.astype(BF16), preferred_element_type=F32)


def _bdot_nt(a, b):
    return lax.dot_general(a.astype(BF16), b.astype(BF16), (((1,), (1,)), ((), ())),
                           preferred_element_type=F32)


def _silu(x):
    return x * (1.0 / (1.0 + jnp.exp(-x)))


def _addnorm_body(*refs, add):
    if add:
        h_ref, y_ref, g_ref, hs_ref, xn_ref = refs
        x = h_ref[...] + y_ref[...]
        hs_ref[...] = x
    else:
        h_ref, g_ref, xn_ref = refs
        x = h_ref[...]
    ms = jnp.mean(x * x, axis=-1, keepdims=True)
    xn_ref[...] = (x * lax.rsqrt(ms + EPS) * g_ref[...]).astype(xn_ref.dtype)


def _addnorm(h, y, g, *, tm=256, out_dtype=BF16):
    n, d = h.shape
    row = pl.BlockSpec((tm, d), lambda i: (i, 0))
    gspec = pl.BlockSpec((1, d), lambda i: (0, 0))
    g2 = g.reshape(1, d).astype(F32)
    if y is None:
        xn = pl.pallas_call(
            functools.partial(_addnorm_body, add=False),
            grid=(n // tm,), in_specs=[row, gspec], out_specs=row,
            out_shape=jax.ShapeDtypeStruct((n, d), out_dtype),
            compiler_params=_cparams("parallel"), name="norm")(h, g2)
        return h, xn
    hs, xn = pl.pallas_call(
        functools.partial(_addnorm_body, add=True),
        grid=(n // tm,), in_specs=[row, row, gspec], out_specs=[row, row],
        out_shape=[jax.ShapeDtypeStruct((n, d), F32), jax.ShapeDtypeStruct((n, d), out_dtype)],
        compiler_params=_cparams("parallel"), name="add_norm")(h, y, g2)
    return hs, xn


def _cast_body(x_ref, o_ref):
    o_ref[...] = x_ref[0].astype(o_ref.dtype)


def _cast(x, l, dtype=BF16, *, tr=512):
    _, r, c = x.shape
    return pl.pallas_call(
        _cast_body, grid=(r // tr,),
        in_specs=[pl.BlockSpec((1, tr, c), lambda i: (l, i, 0))],
        out_specs=pl.BlockSpec((tr, c), lambda i: (i, 0)),
        out_shape=jax.ShapeDtypeStruct((r, c), dtype),
        compiler_params=_cparams("parallel"), name="cast")(x)


FP8 = jnp.float8_e4m3fn
FP8_TARGET = 224.0


def _amax_scale(x):
    a = jnp.max(jnp.max(jnp.abs(x), axis=1, keepdims=True), axis=0, keepdims=True)
    return jnp.where(a > 0.0, FP8_TARGET / a, 1.0)


def _quant_body(x_ref, o_ref, inv_ref):
    x = x_ref[0]
    s = _amax_scale(x)
    o_ref[...] = (x * s).astype(FP8)
    inv_ref[0] = jnp.broadcast_to(1.0 / s, inv_ref.shape[1:])


def _quant_fp8(x, l, *, tr):
    _, r, c = x.shape
    return pl.pallas_call(
        _quant_body, grid=(r // tr,),
        in_specs=[pl.BlockSpec((1, tr, c), lambda i: (l, i, 0))],
        out_specs=[pl.BlockSpec((tr, c), lambda i: (i, 0)), pl.BlockSpec((1, 8, BLK), lambda i: (i, 0, 0))],
        out_shape=[jax.ShapeDtypeStruct((r, c), FP8), jax.ShapeDtypeStruct((r // tr, 8, BLK), F32)],
        compiler_params=_cparams("parallel"), name="quant_fp8")(x)


def _cast_w_in_body(x_ref, main_ref, ba_ref, *, lo, nba):
    x = x_ref[0]
    main_ref[:, :lo] = x[:, :lo].astype(BF16)
    main_ref[:, lo:] = x[:, lo + nba:].astype(BF16)
    ba_ref[...] = jnp.concatenate(
        [x[:, lo:lo + nba], jnp.zeros((x.shape[0], BLK - nba), F32)], axis=1).astype(BF16)


def _cast_w_in(w_in, l, lo, nba, *, tr=256):
    _, r, c = w_in.shape
    return pl.pallas_call(
        functools.partial(_cast_w_in_body, lo=lo, nba=nba), grid=(r // tr,),
        in_specs=[pl.BlockSpec((1, tr, c), lambda i: (l, i, 0))],
        out_specs=[pl.BlockSpec((tr, c - nba), lambda i: (i, 0)), pl.BlockSpec((tr, BLK), lambda i: (i, 0))],
        out_shape=[jax.ShapeDtypeStruct((r, c - nba), BF16), jax.ShapeDtypeStruct((r, BLK), BF16)],
        compiler_params=_cparams("parallel"), name="cast_w_in")(w_in)


def _mm_body(x_ref, w_ref, o_ref):
    o_ref[...] = jnp.dot(x_ref[...], w_ref[...], preferred_element_type=F32)


def _mm(x, w, *, tm=512, tn=1024):
    m, k = x.shape
    n = w.shape[1]
    tn = min(tn, n)
    return pl.pallas_call(
        _mm_body, grid=(n // tn, m // tm),
        in_specs=[pl.BlockSpec((tm, k), lambda j, i: (i, 0)),
                  pl.BlockSpec((k, tn), lambda j, i: (0, j))],
        out_specs=pl.BlockSpec((tm, tn), lambda j, i: (i, j)),
        out_shape=jax.ShapeDtypeStruct((m, n), F32),
        compiler_params=_cparams("parallel", "parallel"), name="matmul")(x, w)


def _outproj_body(yp_ref, yd_ref, ya_ref, w_ref, h_ref, o_ref, *, wp, wd):
    acc = h_ref[...]
    acc += jnp.dot(yp_ref[...].astype(BF16), w_ref[0:wp, :], preferred_element_type=F32)
    acc += jnp.dot(yd_ref[...].astype(BF16), w_ref[wp:wp + wd, :], preferred_element_type=F32)
    acc += jnp.dot(ya_ref[...].astype(BF16), w_ref[wp + wd:, :], preferred_element_type=F32)
    o_ref[...] = acc


def _outproj(yp, yd, ya, w, h, *, tm=512, tn=1024):
    m, d = h.shape
    wp, wd, wa = yp.shape[1], yd.shape[1], ya.shape[1]
    k = wp + wd + wa
    return pl.pallas_call(
        functools.partial(_outproj_body, wp=wp, wd=wd), grid=(d // tn, m // tm),
        in_specs=[pl.BlockSpec((tm, wp), lambda j, i: (i, 0)),
                  pl.BlockSpec((tm, wd), lambda j, i: (i, 0)),
                  pl.BlockSpec((tm, wa), lambda j, i: (i, 0)),
                  pl.BlockSpec((k, tn), lambda j, i: (0, j)),
                  pl.BlockSpec((tm, tn), lambda j, i: (i, j))],
        out_specs=pl.BlockSpec((tm, tn), lambda j, i: (i, j)),
        out_shape=jax.ShapeDtypeStruct((m, d), F32),
        compiler_params=_cparams("parallel", "parallel"), name="out_proj")(yp, yd, ya, w, h)


def _pool_windows(ext_ref, u, t, pos, w_ref, s_ref, gw):
    outs = []
    for gi, w in enumerate(POOL_WINDOWS):
        sl = slice(gi * gw, (gi + 1) * gw)
        win = u[:, sl]
        for k in range(1, w):
            win = win + ext_ref[16 - k:16 - k + t, sl]
        cnt = jnp.clip(pos + 1, 1, w).astype(F32)
        d = win / cnt - u[:, sl]
        outs.append(_bdot(d, w_ref[gi]))
    return jnp.concatenate(outs, axis=-1) * s_ref[...]


def _pool_prompt_body(u_ref, w_ref, s_ref, y_ref, ext_ref, *, gw):
    n = pl.program_id(1)

    @pl.when(n == 0)
    def _():
        ext_ref[0:16, :] = jnp.zeros((16, ext_ref.shape[1]), F32)

    u = u_ref[...]
    ext_ref[16:16 + BLK, :] = u
    row = n * BLK + lax.broadcasted_iota(jnp.int32, (BLK, 1), 0)
    pos = row - FRONT
    y = _pool_windows(ext_ref, u, BLK, pos, w_ref, s_ref, gw)
    y_ref[...] = jnp.where(pos >= 0, y, 0.0)
    ext_ref[0:16, :] = u[BLK - 16:, :]


def _pool_prompt(p, w_pool, s_pool, nseq, nblk):
    pw = w_pool.shape[0] * w_pool.shape[1]
    gw = w_pool.shape[1]
    return pl.pallas_call(
        functools.partial(_pool_prompt_body, gw=gw), grid=(nseq, nblk),
        in_specs=[pl.BlockSpec((BLK, pw), lambda b, n: (b * nblk + n, 0)),
                  pl.BlockSpec(w_pool.shape, lambda b, n: (0, 0, 0)),
                  pl.BlockSpec((1, pw), lambda b, n: (0, 0))],
        out_specs=pl.BlockSpec((BLK, pw), lambda b, n: (b * nblk + n, 0)),
        out_shape=jax.ShapeDtypeStruct((p.shape[0], pw), F32),
        scratch_shapes=[pltpu.VMEM((16 + BLK, pw), F32)],
        compiler_params=_cparams("parallel", "arbitrary"), name="pool_prompt")(p, w_pool, s_pool)


def _pool_sample_body(u_ref, hist_ref, w_ref, s_ref, ybuf_ref, y_ref, ext_ref, *, gw, pos0):
    del ybuf_ref
    u = u_ref[...]
    ext_ref[0:1, :] = jnp.zeros((1, ext_ref.shape[1]), F32)
    ext_ref[1:16, :] = hist_ref[0]
    ext_ref[16:16 + DEC_T, :] = u
    pos = pos0 + lax.broadcasted_iota(jnp.int32, (DEC_T, 1), 0)
    y_ref[...] = _pool_windows(ext_ref, u, DEC_T, pos, w_ref, s_ref, gw)


def _sample_row_block(s, rows_per_seq):
    return (s // SLOTS) * (rows_per_seq // DEC_T) + s % SLOTS


def _pool_sample(p, hist, w_pool, s_pool, ybuf, rows_per_seq, pos0):
    nb = hist.shape[0]
    pw = hist.shape[2]
    gw = w_pool.shape[1]
    rowmap = lambda s: (_sample_row_block(s, rows_per_seq), 0)
    return pl.pallas_call(
        functools.partial(_pool_sample_body, gw=gw, pos0=pos0), grid=(nb,),
        in_specs=[pl.BlockSpec((DEC_T, pw), rowmap),
                  pl.BlockSpec((1, POOL_HIST, pw), lambda s: (s, 0, 0)),
                  pl.BlockSpec(w_pool.shape, lambda s: (0, 0, 0)),
                  pl.BlockSpec((1, pw), lambda s: (0, 0)),
                  pl.BlockSpec(memory_space=pl.ANY)],
        out_specs=pl.BlockSpec((DEC_T, pw), rowmap),
        out_shape=jax.ShapeDtypeStruct(ybuf.shape, F32),
        scratch_shapes=[pltpu.VMEM((16 + DEC_T, pw), F32)],
        input_output_aliases={4: 0},
        compiler_params=_cparams("arbitrary"), name="pool_sample")(p, hist, w_pool, s_pool, ybuf)


def _cumsum_lanes(x):
    lane = lax.broadcasted_iota(jnp.int32, x.shape, 1)
    s = 1
    while s < x.shape[1]:
        x = x + jnp.where(lane >= s, pltpu.roll(x, s, axis=1), 0.0)
        s *= 2
    return x


HEAD_PACK = 2


def _unit_lower_inverse(mats, ii, jj):
    eye = jnp.where(ii == jj, 1.0, 0.0).astype(F32)
    pair = ((ii // 2) == (jj // 2)) & (ii % 2 == 1) & (jj % 2 == 0)
    xs = [eye - jnp.where(pair, a, 0.0) for a in mats]
    s = 2
    while s < BLK:
        mask = ((ii // (2 * s)) == (jj // (2 * s))) & ((ii // s) % 2 == 1) & ((jj // s) % 2 == 0)
        ts = [_bdot(jnp.where(mask, a, 0.0), x) for a, x in zip(mats, xs)]
        xs = [x - _bdot(x, t) for x, t in zip(xs, ts)]
        s *= 2
    return xs


def _delta_scalars(ba, valid, alog_ref, dtb_ref, nh):
    bat = ba.T
    beta = jnp.where(valid, 1.0 / (1.0 + jnp.exp(-bat[0:nh])), 0.0)
    z = bat[nh:2 * nh] + dtb_ref[...]
    softplus = jnp.maximum(z, 0.0) + jnp.log(1.0 + jnp.exp(-jnp.abs(z)))
    g = jnp.where(valid, -jnp.exp(alog_ref[...]) * softplus, 0.0)
    gc = _cumsum_lanes(g)
    glast = jnp.broadcast_to(gc[:, BLK - 1:BLK], gc.shape)
    eg = jnp.exp(gc)
    rows = jnp.concatenate(
        [gc, eg, beta, beta * eg, jnp.exp(glast - gc), jnp.exp(glast),
         jnp.zeros((BLK - 6 * nh, BLK), F32)], axis=0)
    return gc, rows.T


def _l2n(x):
    return x * lax.rsqrt(jnp.sum(x * x, axis=-1, keepdims=True) + EPS)


def _delta_chunk_small(xq, xk, xv, ba, alog_ref, dtb_ref, s_ref, nh, dk):
    t = xq.shape[0]
    zrows = jnp.zeros((BLK - t, BLK), F32)
    lane = lax.broadcasted_iota(jnp.int32, (nh, BLK), 1)
    gc, cols = _delta_scalars(jnp.concatenate([ba, zrows], axis=0), lane < t, alog_ref, dtb_ref, nh)
    ii = lax.broadcasted_iota(jnp.int32, (t, BLK), 0)
    jj = lax.broadcasted_iota(jnp.int32, (t, BLK), 1)
    incl = ii >= jj
    strict = ii > jj
    outs = []
    for h in range(nh):
        sl = slice(h * dk, (h + 1) * dk)
        q = _l2n(xq[:, sl]) * (dk ** -0.5)
        k = _l2n(xk[:, sl])
        v = xv[:, sl]
        col = lambda qi: cols[0:t, qi * nh + h:qi * nh + h + 1]
        kpad = jnp.concatenate([k, zrows], axis=0)
        diff = col(0) - gc[h:h + 1, :]
        decay = jnp.where(incl, jnp.exp(jnp.where(incl, diff, 0.0)), 0.0)
        a_mat = jnp.where(strict, _bdot_nt(k, kpad) * decay * col(2), 0.0)
        qk = _bdot_nt(q, kpad) * decay
        wu = jnp.concatenate([col(3) * k, col(2) * v], axis=-1)
        for j in range(t - 1):
            wu = wu - a_mat[:, j:j + 1] * wu[j:j + 1, :]
        s = s_ref[h]
        v_new = wu[:, dk:] - _bdot(wu[:, :dk], s)
        o = col(1) * _bdot(q, s)
        for j in range(t):
            o = o + qk[:, j:j + 1] * v_new[j:j + 1, :]
        kd = jnp.concatenate([k * col(4), zrows], axis=0).T
        s_ref[h] = cols[:, 5 * nh + h:5 * nh + h + 1] * s + _bdot(kd, jnp.concatenate([v_new, zrows], axis=0))
        outs.append(o)
    return outs


def _delta_chunk(xq, xk, xv, ba, valid, alog_ref, dtb_ref, s_ref, nh, dk):
    assert dk == BLK and nh % HEAD_PACK == 0
    gc, cols = _delta_scalars(ba, valid, alog_ref, dtb_ref, nh)
    n = HEAD_PACK * BLK
    ii = lax.broadcasted_iota(jnp.int32, (n, n), 0)
    jj = lax.broadcasted_iota(jnp.int32, (n, n), 1)
    same = (ii // BLK) == (jj // BLK)
    incl = same & (ii >= jj)
    strict = same & (ii > jj)
    packs = range(nh // HEAD_PACK)
    each = lambda f: [f(p) for p in packs]
    stack = lambda p, f: jnp.concatenate([f(h) for h in range(p * HEAD_PACK, (p + 1) * HEAD_PACK)], axis=0)
    col = lambda p, qi: stack(p, lambda h: cols[:, qi * nh + h:qi * nh + h + 1])
    diag = lambda m: jnp.where(same, jnp.concatenate([m] * HEAD_PACK, axis=1), 0.0)
    q = each(lambda p: stack(p, lambda h: _l2n(xq[:, h * dk:(h + 1) * dk]) * (dk ** -0.5)))
    k = each(lambda p: stack(p, lambda h: _l2n(xk[:, h * dk:(h + 1) * dk])))
    v = each(lambda p: stack(p, lambda h: xv[:, h * dk:(h + 1) * dk]))
    s = each(lambda p: stack(p, lambda h: s_ref[h]))
    diff = each(lambda p: col(p, 0) - jnp.concatenate(
        [gc[h:h + 1, :] for h in range(p * HEAD_PACK, (p + 1) * HEAD_PACK)], axis=1))
    decay = each(lambda p: jnp.where(incl, jnp.exp(jnp.where(incl, diff[p], 0.0)), 0.0))
    a_mat = each(lambda p: jnp.where(strict, _bdot_nt(k[p], k[p]) * decay[p] * col(p, 2), 0.0))
    qk = each(lambda p: _bdot_nt(q[p], k[p]) * decay[p])
    x = _unit_lower_inverse(a_mat, ii, jj)
    rhs = each(lambda p: jnp.concatenate([col(p, 3) * k[p], col(p, 2) * v[p]], axis=-1))
    wu = each(lambda p: _bdot(x[p], rhs[p]))
    v_new = each(lambda p: wu[p][:, dk:] - _bdot(diag(wu[p][:, :dk]), s[p]))
    qs = each(lambda p: _bdot(diag(q[p]), s[p]))
    o = each(lambda p: col(p, 1) * qs[p] + _bdot(qk[p], v_new[p]))
    kd = each(lambda p: jnp.where(same, jnp.concatenate([(k[p] * col(p, 4)).T] * HEAD_PACK, axis=0), 0.0))
    s_new = each(lambda p: col(p, 5) * s[p] + _bdot(kd[p], v_new[p]))
    outs = []
    for p in packs:
        for i in range(HEAD_PACK):
            s_ref[p * HEAD_PACK + i] = s_new[p][i * BLK:(i + 1) * BLK, :]
            outs.append(o[p][i * BLK:(i + 1) * BLK, :])
    return outs


def _delta_out(o, gate, ng_ref):
    return o * lax.rsqrt(jnp.mean(o * o, axis=-1, keepdims=True) + EPS) * ng_ref[...] * _silu(gate)


def _conv_silu(ext_ref, w_ref, t):
    acc = ext_ref[5:5 + t, :] * w_ref[0:1, :]
    for i in range(1, DN_CONV):
        acc = acc + ext_ref[5 + i:5 + i + t, :] * w_ref[i:i + 1, :]
    return _silu(acc)


def _delta_prompt_body(q_ref, k_ref, v_ref, gt_ref, ba_ref, wq_ref, wk_ref, wv_ref, alog_ref, dtb_ref, ng_ref,
                       y_ref, sout_ref, eq_ref, ek_ref, ev_ref, s_ref, *, nh, dk, nblk):
    c = pl.program_id(1)

    @pl.when(c == 0)
    def _():
        for e in (eq_ref, ek_ref, ev_ref):
            e[0:8, :] = jnp.zeros((8, e.shape[1]), F32)
        s_ref[...] = jnp.zeros(s_ref.shape, F32)

    xs = []
    for x_ref, e_ref, w_ref in ((q_ref, eq_ref, wq_ref), (k_ref, ek_ref, wk_ref), (v_ref, ev_ref, wv_ref)):
        e_ref[8:8 + BLK, :] = x_ref[...]
        xs.append(_conv_silu(e_ref, w_ref, BLK))
        e_ref[0:8, :] = x_ref[BLK - 8:, :]
    lane = lax.broadcasted_iota(jnp.int32, (nh, BLK), 1)
    valid = (c > 0) | (lane >= FRONT)
    outs = _delta_chunk(xs[0], xs[1], xs[2], ba_ref[...], valid, alog_ref, dtb_ref, s_ref, nh, dk)
    row = c * BLK + lax.broadcasted_iota(jnp.int32, (BLK, 1), 0)
    for h in range(nh):
        sl = slice(h * dk, (h + 1) * dk)
        y_ref[:, sl] = jnp.where(row >= FRONT, _delta_out(outs[h], gt_ref[:, sl], ng_ref), 0.0)

    @pl.when(c == nblk - 1)
    def _():
        sout_ref[0] = s_ref[...]


def _delta_prompt(p, ba, w_conv, a_log, dt_bias, norm_g, nseq, nblk, col_q, col_gate, nh, dk):
    hw = nh * dk
    cq, ck, cv, cg = col_q // hw, col_q // hw + 1, col_q // hw + 2, col_gate // hw
    rows = lambda cb: pl.BlockSpec((BLK, hw), lambda b, c: (b * nblk + c, cb))
    wcs = lambda cb: pl.BlockSpec((DN_CONV, hw), lambda b, c: (0, cb))
    small = lambda shape: pl.BlockSpec(shape, lambda b, c: (0, 0))
    return pl.pallas_call(
        functools.partial(_delta_prompt_body, nh=nh, dk=dk, nblk=nblk), grid=(nseq, nblk),
        in_specs=[rows(cq), rows(ck), rows(cv), rows(cg),
                  pl.BlockSpec((BLK, BLK), lambda b, c: (b * nblk + c, 0)),
                  wcs(0), wcs(1), wcs(2), small((nh, 1)), small((nh, 1)), small((1, dk))],
        out_specs=[pl.BlockSpec((BLK, hw), lambda b, c: (b * nblk + c, 0)),
                   pl.BlockSpec((1, nh, dk, dk), lambda b, c: (b, 0, 0, 0))],
        out_shape=[jax.ShapeDtypeStruct((p.shape[0], hw), F32),
                   jax.ShapeDtypeStruct((nseq, nh, dk, dk), F32)],
        scratch_shapes=[pltpu.VMEM((8 + BLK, hw), F32)] * 3 + [pltpu.VMEM((nh, dk, dk), F32)],
        compiler_params=_cparams("parallel", "arbitrary"), name="delta_prompt",
    )(p, p, p, p, ba, w_conv, w_conv, w_conv, a_log.reshape(nh, 1), dt_bias.reshape(nh, 1),
      norm_g.reshape(1, dk))


def _delta_sample_body(q_ref, k_ref, v_ref, gt_ref, ba_ref, hq_ref, hk_ref, hv_ref, s0_ref,
                       wq_ref, wk_ref, wv_ref, alog_ref, dtb_ref, ng_ref, ybuf_ref,
                       y_ref, sout_ref, eq_ref, ek_ref, ev_ref, s_ref, *, nh, dk):
    del ybuf_ref
    xs = []
    for x_ref, h_ref, e_ref, w_ref in ((q_ref, hq_ref, eq_ref, wq_ref), (k_ref, hk_ref, ek_ref, wk_ref),
                                       (v_ref, hv_ref, ev_ref, wv_ref)):
        e_ref[5:8, :] = h_ref[0]
        e_ref[8:8 + DEC_T, :] = x_ref[...]
        xs.append(_conv_silu(e_ref, w_ref, DEC_T))
    s_ref[...] = s0_ref[0]
    outs = _delta_chunk_small(xs[0], xs[1], xs[2], ba_ref[...], alog_ref, dtb_ref, s_ref, nh, dk)
    for h in range(nh):
        sl = slice(h * dk, (h + 1) * dk)
        y_ref[:, sl] = _delta_out(outs[h], gt_ref[:, sl], ng_ref)
    sout_ref[0] = s_ref[...]


def _delta_sample(p, ba, conv_hist, s0, w_conv, a_log, dt_bias, norm_g, ybuf, rows_per_seq,
                  col_q, col_gate, nh, dk):
    nb = s0.shape[0]
    hw = nh * dk
    cq, ck, cv, cg = col_q // hw, col_q // hw + 1, col_q // hw + 2, col_gate // hw
    rowmap = lambda cb: (lambda s: (_sample_row_block(s, rows_per_seq), cb))
    rows = lambda cb: pl.BlockSpec((DEC_T, hw), rowmap(cb))
    hist = lambda cb: pl.BlockSpec((1, DN_CONV - 1, hw), lambda s: (s, 0, cb))
    wcs = lambda cb: pl.BlockSpec((DN_CONV, hw), lambda s: (0, cb))
    small = lambda shape: pl.BlockSpec(shape, lambda s: (0, 0))
    return pl.pallas_call(
        functools.partial(_delta_sample_body, nh=nh, dk=dk), grid=(nb,),
        in_specs=[rows(cq), rows(ck), rows(cv), rows(cg), pl.BlockSpec((DEC_T, BLK), rowmap(0)),
                  hist(0), hist(1), hist(2),
                  pl.BlockSpec((1, nh, dk, dk), lambda s: (s, 0, 0, 0)),
                  wcs(0), wcs(1), wcs(2), small((nh, 1)), small((nh, 1)), small((1, dk)),
                  pl.BlockSpec(memory_space=pl.ANY)],
        out_specs=[pl.BlockSpec((DEC_T, hw), rowmap(0)),
                   pl.BlockSpec((1, nh, dk, dk), lambda s: (s, 0, 0, 0))],
        out_shape=[jax.ShapeDtypeStruct(ybuf.shape, F32), jax.ShapeDtypeStruct(s0.shape, F32)],
        scratch_shapes=[pltpu.VMEM((8 + DEC_T, hw), F32)] * 3 + [pltpu.VMEM((nh, dk, dk), F32)],
        input_output_aliases={15: 0},
        compiler_params=_cparams("arbitrary"), name="delta_sample",
    )(p, p, p, p, ba, conv_hist, conv_hist, conv_hist, s0, w_conv, w_conv, w_conv,
      a_log.reshape(nh, 1), dt_bias.reshape(nh, 1), norm_g.reshape(1, dk), ybuf)


def _sink_softmax_pv(parts, sink):
    m = sink
    for s, _ in parts:
        m = jnp.maximum(m, jnp.max(s, axis=-1, keepdims=True))
    den = jnp.exp(sink - m)
    acc = None
    for s, v in parts:
        p = jnp.exp(s - m)
        den = den + jnp.sum(p, axis=-1, keepdims=True)
        pv = _bdot(p, v)
        acc = pv if acc is None else acc + pv
    return acc / den


def _attn_prompt_body(*refs, nkv, grp, hd, window):
    q_refs = refs[:nkv]
    kp_ref, kc_ref, vp_ref, vc_ref, slope_ref, sink_ref, y_ref = refs[nkv:]
    n = pl.program_id(1)
    i = lax.broadcasted_iota(jnp.int32, (BLK, 2 * BLK), 0)
    j = lax.broadcasted_iota(jnp.int32, (BLK, 2 * BLK), 1)
    dist = BLK + i - j
    krow = (n - 1) * BLK + j
    valid = (dist >= 0) & (dist < window) & (krow >= FRONT)
    distf = dist.astype(F32)
    kk = jnp.concatenate([kp_ref[...], kc_ref[...]], axis=0).astype(BF16)
    vv = jnp.concatenate([vp_ref[...], vc_ref[...]], axis=0).astype(BF16)
    heads = [(kv, g) for kv in range(nkv) for g in range(grp)]
    each = lambda f: [f(t, kv, g) for t, (kv, g) in enumerate(heads)]
    s = each(lambda t, kv, g: _bdot_nt(q_refs[kv][:, g * hd:(g + 1) * hd], kk[:, kv * hd:(kv + 1) * hd]))
    s = each(lambda t, kv, g: jnp.where(valid, s[t] * (hd ** -0.5) - slope_ref[t] * distf, NEG_INF))
    m = each(lambda t, kv, g: jnp.maximum(jnp.max(s[t], axis=-1, keepdims=True), sink_ref[t]))
    pr = each(lambda t, kv, g: jnp.exp(s[t] - m[t]))
    den = each(lambda t, kv, g: jnp.sum(pr[t], axis=-1, keepdims=True) + jnp.exp(sink_ref[t] - m[t]))
    pv = each(lambda t, kv, g: _bdot(pr[t], vv[:, kv * hd:(kv + 1) * hd]))
    for t in range(len(heads)):
        y_ref[:, t * hd:(t + 1) * hd] = pv[t] / den[t]


def _attn_prompt(p, slopes, sinks, nseq, nblk, col_q, col_k, col_v, nkv, grp, hd, window):
    gw, kw = grp * hd, nkv * hd
    prev = lambda cb: (lambda b, n: (b * nblk + jnp.maximum(n - 1, 0), cb))
    cur = lambda cb: (lambda b, n: (b * nblk + n, cb))
    smem = pl.BlockSpec(memory_space=pltpu.SMEM)
    return pl.pallas_call(
        functools.partial(_attn_prompt_body, nkv=nkv, grp=grp, hd=hd, window=window), grid=(nseq, nblk),
        in_specs=[pl.BlockSpec((BLK, gw), cur(col_q // gw + kv)) for kv in range(nkv)]
        + [pl.BlockSpec((BLK, kw), prev(col_k // kw)), pl.BlockSpec((BLK, kw), cur(col_k // kw)),
           pl.BlockSpec((BLK, kw), prev(col_v // kw)), pl.BlockSpec((BLK, kw), cur(col_v // kw)),
           smem, smem],
        out_specs=pl.BlockSpec((BLK, nkv * gw), cur(0)),
        out_shape=jax.ShapeDtypeStruct((p.shape[0], nkv * gw), F32),
        compiler_params=_cparams("parallel", "arbitrary"), name="attn_prompt",
    )(*([p] * (nkv + 4)), slopes, sinks)


def _attn_sample_body(*refs, nkv, grp, hd, window):
    q_refs = refs[:nkv]
    k_ref, v_ref, ck_ref, cv_ref, slope_ref, sink_ref, ybuf_ref, y_ref = refs[nkv:]
    del ybuf_ref
    wc = ck_ref.shape[1]
    i = lax.broadcasted_iota(jnp.int32, (DEC_T, wc), 0)
    j = lax.broadcasted_iota(jnp.int32, (DEC_T, wc), 1)
    dist_c = wc + i - j
    valid_c = (dist_c >= 0) & (dist_c < window)
    i2 = lax.broadcasted_iota(jnp.int32, (DEC_T, DEC_T), 0)
    j2 = lax.broadcasted_iota(jnp.int32, (DEC_T, DEC_T), 1)
    dist_n = i2 - j2
    valid_n = (dist_n >= 0) & (dist_n < window)
    tile = lambda a: jnp.concatenate([a] * grp, axis=0)
    valid_c, valid_n = tile(valid_c), tile(valid_n)
    dist_c, dist_n = tile(dist_c).astype(F32), tile(dist_n).astype(F32)
    gi = lax.broadcasted_iota(jnp.int32, (grp * DEC_T, 1), 0) // DEC_T

    def per_row(ref, kv):
        out = jnp.full((grp * DEC_T, 1), ref[kv * grp], F32)
        for g in range(1, grp):
            out = jnp.where(gi == g, ref[kv * grp + g], out)
        return out

    each = lambda f: [f(kv) for kv in range(nkv)]
    cols = lambda ref, kv: ref[:, kv * hd:(kv + 1) * hd]
    q = each(lambda kv: jnp.concatenate([q_refs[kv][:, g * hd:(g + 1) * hd] for g in range(grp)], axis=0))
    slope = each(lambda kv: per_row(slope_ref, kv))
    sink = each(lambda kv: per_row(sink_ref, kv))
    sc = each(lambda kv: _bdot_nt(q[kv], ck_ref[0, :, kv * hd:(kv + 1) * hd]) * (hd ** -0.5))
    sn = each(lambda kv: _bdot_nt(q[kv], cols(k_ref, kv)) * (hd ** -0.5))
    sc = each(lambda kv: jnp.where(valid_c, sc[kv] - slope[kv] * dist_c, NEG_INF))
    sn = each(lambda kv: jnp.where(valid_n, sn[kv] - slope[kv] * dist_n, NEG_INF))
    m = each(lambda kv: jnp.maximum(jnp.maximum(jnp.max(sc[kv], axis=-1, keepdims=True),
                                                jnp.max(sn[kv], axis=-1, keepdims=True)), sink[kv]))
    pc = each(lambda kv: jnp.exp(sc[kv] - m[kv]))
    pn = each(lambda kv: jnp.exp(sn[kv] - m[kv]))
    den = each(lambda kv: jnp.sum(pc[kv], axis=-1, keepdims=True) + jnp.sum(pn[kv], axis=-1, keepdims=True)
               + jnp.exp(sink[kv] - m[kv]))
    o = each(lambda kv: (_bdot(pc[kv], cv_ref[0, :, kv * hd:(kv + 1) * hd]) + _bdot(pn[kv], cols(v_ref, kv)))
             / den[kv])
    for kv in range(nkv):
        for g in range(grp):
            head = kv * grp + g
            y_ref[:, head * hd:(head + 1) * hd] = o[kv][g * DEC_T:(g + 1) * DEC_T, :]


def _attn_sample(p, cache_k, cache_v, slopes, sinks, ybuf, rows_per_seq, col_q, col_k, col_v,
                 nkv, grp, hd, window):
    nb, wc = cache_k.shape[0], cache_k.shape[1]
    gw, kw = grp * hd, nkv * hd
    ck3 = cache_k.reshape(nb, wc, kw)
    cv3 = cache_v.reshape(nb, wc, kw)
    rowmap = lambda cb: (lambda s: (_sample_row_block(s, rows_per_seq), cb))
    smem = pl.BlockSpec(memory_space=pltpu.SMEM)
    return pl.pallas_call(
        functools.partial(_attn_sample_body, nkv=nkv, grp=grp, hd=hd, window=window), grid=(nb,),
        in_specs=[pl.BlockSpec((DEC_T, gw), rowmap(col_q // gw + kv)) for kv in range(nkv)]
        + [pl.BlockSpec((DEC_T, kw), rowmap(col_k // kw)),
           pl.BlockSpec((DEC_T, kw), rowmap(col_v // kw)),
           pl.BlockSpec((1, wc, kw), lambda s: (s, 0, 0)),
           pl.BlockSpec((1, wc, kw), lambda s: (s, 0, 0)),
           smem, smem, pl.BlockSpec(memory_space=pl.ANY)],
        out_specs=pl.BlockSpec((DEC_T, nkv * gw), rowmap(0)),
        out_shape=jax.ShapeDtypeStruct(ybuf.shape, F32),
        input_output_aliases={nkv + 6: 0},
        compiler_params=_cparams("arbitrary"), name="attn_sample",
    )(*([p] * (nkv + 2)), ck3, cv3, slopes, sinks, ybuf)


def _top_values(s, k, want_rank=False):
    riota = lax.broadcasted_iota(jnp.int32, s.shape, 0).astype(F32)
    rank = jnp.full(s.shape, float(s.shape[0] - 1), F32) if want_rank else None
    vals = []
    for it in range(k):
        m = jnp.max(s, axis=0, keepdims=True)
        vals.append(m)
        hit = riota == jnp.min(jnp.where(s == m, riota, float(s.shape[0])), axis=0, keepdims=True)
        if want_rank:
            rank = jnp.where(hit, float(it), rank)
        s = jnp.where(hit, LOWEST, s)
    return jnp.concatenate(vals, axis=0), rank


def _peer_topk_body(q_ref, keys_ref, rk_ref, cut_ref, e1_ref, e2_ref, *, nh):
    for h in range(nh):
        sc = [_bdot_nt(keys_ref[2 * h + half], q_ref[:, (2 * h + half) * NKEYS:(2 * h + half + 1) * NKEYS])
              for half in (0, 1)]
        a, rank1 = _top_values(sc[0], TOPK, True)
        b, rank2 = _top_values(sc[1], TOPK, True)
        cand = [a[0:1] + b, a[8:16] + b[0:1]] + [a[i:i + 1] + b[0:8] for i in range(1, 8)]
        top, _ = _top_values(jnp.concatenate(cand, axis=0), TOPK)
        thr = top[TOPK - 1:TOPK]
        rz = 1.0 / jnp.sum(jnp.exp(top - top[0:1]), axis=0, keepdims=True)
        cut = jnp.zeros(rank1.shape, F32)
        for r in range(TOPK):
            height = jnp.sum(jnp.where(a[r:r + 1] + b >= thr, 1.0, 0.0), axis=0, keepdims=True)
            cut = jnp.where(rank1 == float(r), height, cut)
        rk_ref[h] = rank2.astype(BF16)
        cut_ref[h] = cut
        e1_ref[h] = jnp.exp(sc[0] - a[0:1])
        e2_ref[h] = (jnp.exp(sc[1] - b[0:1]) * rz).astype(BF16)


def _peer_topk(q, sub_keys, *, tm=128):
    n = q.shape[0]
    nh = sub_keys.shape[0]
    keys = sub_keys.reshape(2 * nh, NKEYS, sub_keys.shape[-1])
    sspec = pl.BlockSpec((nh, NKEYS, tm), lambda i: (0, 0, i))
    shape = lambda dt: jax.ShapeDtypeStruct((nh, NKEYS, n), dt)
    return pl.pallas_call(
        functools.partial(_peer_topk_body, nh=nh), grid=(n // tm,),
        in_specs=[pl.BlockSpec((tm, q.shape[1]), lambda i: (i, 0)),
                  pl.BlockSpec(keys.shape, lambda i: (0, 0, 0))],
        out_specs=[sspec] * 4,
        out_shape=[shape(BF16), shape(F32), shape(F32), shape(BF16)],
        compiler_params=_cparams("parallel"), name="peer_topk")(q, keys)


def _gelu(x):
    return 0.5 * x * (1.0 + lax.erf(x * (2.0 ** -0.5)))


PACK = 16


def _peer_expert_body(xn_ref, wd_ref, wu_ref, id_ref, iu_ref, rk_ref, cut_ref, e1_ref, e2_ref, y_ref,
                      xt_ref, ix_ref, *, nh, ei, sub):
    c = pl.program_id(1)
    tm = xt_ref.shape[1]

    @pl.when(c == 0)
    def _():
        y_ref[...] = jnp.zeros(y_ref.shape, F32)
        x = xn_ref[...].astype(F32)
        sx = _amax_scale(x)
        xt_ref[...] = (x * sx).T.astype(FP8)
        ix_ref[...] = jnp.broadcast_to(1.0 / sx, ix_ref.shape)

    def row16(ref, h, i1):
        return jnp.broadcast_to(ref[h, pl.ds(i1, 1), :], (PACK, tm)).astype(BF16)

    inv_h = id_ref[0, 0:1, 0:1] * ix_ref[0:1, 0:1]
    inv_u = iu_ref[0, 0:1, 0:1]
    for k in range(ei // sub):
        rows = slice(k * sub * NKEYS, (k + 1) * sub * NKEYS)
        ht = jnp.dot(wd_ref[rows, :], xt_ref[...], preferred_element_type=F32) * inv_h
        act = _gelu(ht)
        pieces = []
        for ii in range(sub):
            i1 = c * ei + k * sub + ii
            cut = [row16(cut_ref, h, i1) for h in range(nh)]
            e1 = [row16(e1_ref, h, i1) for h in range(nh)]
            for r in range(NKEYS // PACK):
                sl = slice(r * PACK, (r + 1) * PACK)
                gate = None
                for h in range(nh):
                    w = jnp.where(rk_ref[h, sl, :] < cut[h], e1[h] * e2_ref[h, sl, :], 0.0)
                    gate = w if gate is None else gate + w
                base = ii * NKEYS + r * PACK
                pieces.append(act[base:base + PACK, :] * gate.astype(F32))
        at = jnp.concatenate(pieces, axis=0)
        sa = _amax_scale(at)
        part = jnp.dot((at * sa).T.astype(FP8), wu_ref[rows, :], preferred_element_type=F32)
        y_ref[...] += part * (inv_u / sa)


PEER_TM, PEER_EI, PEER_SUB = 512, 4, 4


def _peer_expert(xn, w_down, w_up, inv_down, inv_up, rank2, cut, e1, e2, *, tm=PEER_TM, ei=PEER_EI, sub=PEER_SUB):
    n, d = xn.shape
    nh = rank2.shape[0]
    e = ei * NKEYS
    once = dict(pipeline_mode=pl.Buffered(1))
    sspec = pl.BlockSpec((nh, NKEYS, tm), lambda i, c: (0, 0, i), **once)
    inv = pl.BlockSpec((1, 8, BLK), lambda i, c: (c, 0, 0))
    return pl.pallas_call(
        functools.partial(_peer_expert_body, nh=nh, ei=ei, sub=sub), grid=(n // tm, w_down.shape[0] // e),
        in_specs=[pl.BlockSpec((tm, d), lambda i, c: (i, 0), **once),
                  pl.BlockSpec((e, d), lambda i, c: (c, 0)),
                  pl.BlockSpec((e, d), lambda i, c: (c, 0)),
                  inv, inv, sspec, sspec, sspec, sspec],
        out_specs=pl.BlockSpec((tm, d), lambda i, c: (i, 0), **once),
        out_shape=jax.ShapeDtypeStruct((n, d), F32),
        scratch_shapes=[pltpu.VMEM((d, tm), FP8), pltpu.VMEM((8, BLK), F32)],
        compiler_params=_cparams("parallel", "arbitrary"), name="peer_expert",
    )(xn, w_down, w_up, inv_down, inv_up, rank2, cut, e1, e2)


def _final_norm(h, y, g, nseq, nblk, row_blk, first_blk, nblk_out):
    d = h.shape[1]
    row = pl.BlockSpec((row_blk, d), lambda b, i: (b * nblk + first_blk + i, 0))
    return pl.pallas_call(
        _final_norm_body, grid=(nseq, nblk_out),
        in_specs=[row, row, pl.BlockSpec((1, d), lambda b, i: (0, 0))],
        out_specs=pl.BlockSpec((1, row_blk, d), lambda b, i: (b, i, 0)),
        out_shape=jax.ShapeDtypeStruct((nseq, nblk_out * row_blk, d), F32),
        compiler_params=_cparams("parallel", "parallel"), name="final_norm",
    )(h, y, g.reshape(1, d).astype(F32))


def _final_norm_body(h_ref, y_ref, g_ref, o_ref):
    x = h_ref[...] + y_ref[...]
    ms = jnp.mean(x * x, axis=-1, keepdims=True)
    o_ref[0] = x * lax.rsqrt(ms + EPS) * g_ref[...]


def kernel(x_prompt, x_sample, state_pool, state_conv, state_delta, cache_k, cache_v, meta_tokens, norm1_g,
           w_in, w_pool, s_pool, w_conv, a_log, dt_bias, dn_norm_g, attn_sinks, w_out, norm2_g,
           peer_w_query, peer_sub_keys, peer_w_down, peer_w_up, final_norm_g):
    nseq, seq, d = x_prompt.shape
    nsamp, dec_t, _ = x_sample.shape
    depth = w_in.shape[0]
    pool_w = w_pool.shape[1] * w_pool.shape[2]
    dn_qkv = w_conv.shape[2]
    nh, dk = state_delta.shape[2], state_delta.shape[3]
    wc, nkv, hd = cache_k.shape[2], cache_k.shape[3], cache_k.shape[4]
    nq = attn_sinks.shape[1]
    grp = nq // nkv
    window = wc
    assert dec_t == DEC_T and nsamp == nseq * SLOTS and (N_META + seq) % BLK == N_META
    assert SLOTS * DEC_T + POOL_HIST <= FRONT and dn_qkv == 3 * nh * dk and wc == BLK
    rows_per_seq = FRONT + N_META + seq
    nblk = rows_per_seq // BLK
    past_len = 16384
    col_qkv = pool_w
    col_gate = col_qkv + dn_qkv
    col_q = col_gate + nh * dk
    col_k = col_q + nq * hd
    col_v = col_k + nkv * hd
    src_ba = pool_w + dn_qkv

    xs = x_sample.reshape(nseq, SLOTS * DEC_T, d)
    zeros = jnp.zeros((nseq, FRONT - SLOTS * DEC_T, d), F32)
    meta = jnp.broadcast_to(meta_tokens[None], (nseq, N_META, d))
    h = jnp.concatenate([xs, zeros, meta, x_prompt], axis=1).reshape(nseq * rows_per_seq, d)

    slopes = jnp.exp2(-8.0 * (jnp.arange(nq, dtype=F32) + 1.0) / nq)
    new_p = [[] for _ in range(5)]
    new_s = [[] for _ in range(5)]
    y_peer = None
    for l in range(depth):
        w_main, w_ba = _cast_w_in(w_in, l, src_ba, 2 * nh)
        h, xn = _addnorm(h, y_peer, norm1_g[l])
        p = _mm(xn, w_main)
        ba = _mm(xn, w_ba)

        y_pool = _pool_prompt(p, w_pool[l], s_pool[l].reshape(1, pool_w), nseq, nblk)
        y_pool = _pool_sample(p, state_pool[l], w_pool[l], s_pool[l].reshape(1, pool_w), y_pool,
                              rows_per_seq, past_len)
        y_dn, s_p = _delta_prompt(p, ba, w_conv[l], a_log[l], dt_bias[l], dn_norm_g[l], nseq, nblk,
                                  col_qkv, col_gate, nh, dk)
        y_dn, s_s = _delta_sample(p, ba, state_conv[l], state_delta[l], w_conv[l], a_log[l], dt_bias[l],
                                  dn_norm_g[l], y_dn, rows_per_seq, col_qkv, col_gate, nh, dk)
        y_att = _attn_prompt(p, slopes, attn_sinks[l], nseq, nblk, col_q, col_k, col_v, nkv, grp, hd, window)
        y_att = _attn_sample(p, cache_k[l], cache_v[l], slopes, attn_sinks[l], y_att, rows_per_seq,
                             col_q, col_k, col_v, nkv, grp, hd, window)
        h = _outproj(y_pool, y_dn, y_att, _cast(w_out, l), h)

        _, xn2 = _addnorm(h, None, norm2_g[l])
        q = _mm(xn2, _cast(peer_w_query, l))
        rank2, cut, e1, e2 = _peer_topk(q, peer_sub_keys[l])
        wd8, inv_d = _quant_fp8(peer_w_down, l, tr=PEER_EI * NKEYS)
        wu8, inv_u = _quant_fp8(peer_w_up, l, tr=PEER_EI * NKEYS)
        y_peer = _peer_expert(xn2, wd8, wu8, inv_d, inv_u, rank2, cut, e1, e2)

        p3 = p.reshape(nseq, rows_per_seq, p.shape[1])
        ps = p3[:, :SLOTS * DEC_T].reshape(nsamp, DEC_T, p.shape[1])
        new_p[0].append(p3[:, -POOL_HIST:, :pool_w])
        new_s[0].append(jnp.concatenate([state_pool[l], ps[:, :, :pool_w]], axis=1)[:, -POOL_HIST:])
        new_p[1].append(p3[:, -(DN_CONV - 1):, col_qkv:col_gate])
        new_s[1].append(jnp.concatenate([state_conv[l], ps[:, :, col_qkv:col_gate]], axis=1)[:, -(DN_CONV - 1):])
        new_p[2].append(s_p)
        new_s[2].append(s_s)
        new_p[3].append(p3[:, -window:, col_k:col_v].reshape(nseq, window, nkv, hd))
        new_s[3].append(jnp.concatenate([cache_k[l], ps[:, :, col_k:col_v].reshape(nsamp, DEC_T, nkv, hd)],
                                        axis=1)[:, -wc:])
        new_p[4].append(p3[:, -window:, col_v:].reshape(nseq, window, nkv, hd))
        new_s[4].append(jnp.concatenate([cache_v[l], ps[:, :, col_v:].reshape(nsamp, DEC_T, nkv, hd)],
                                        axis=1)[:, -wc:])

    y_prompt = _final_norm(h, y_peer, final_norm_g, nseq, nblk, BLK, 1, nblk - 1)
    y_sample = _final_norm(h, y_peer, final_norm_g, nseq, rows_per_seq // (SLOTS * DEC_T), SLOTS * DEC_T, 0, 1)
    y_sample = y_sample.reshape(nsamp, DEC_T, d)
    pool_p, conv_p, delta_p, k_p, v_p = (jnp.stack(a) for a in new_p)
    pool_s, conv_s, delta_s, k_s, v_s = (jnp.stack(a) for a in new_s)
    return (y_prompt, y_sample, pool_p, pool_s, conv_p, conv_s, delta_p, delta_s, k_p, k_s, v_p, v_s)
```

```python
import functools

import jax
import jax.numpy as jnp
from jax import lax
from jax.experimental import pallas as pl
from jax.experimental.pallas import tpu as pltpu

F32 = jnp.float32
BF16 = jnp.bfloat16

EPS = 1e-6
NEG_INF = -1e30
LOWEST = -3.0e38

N_META = 16
BLK = 128
FRONT = BLK - N_META
DEC_T = 8
SLOTS = 8
POOL_WINDOWS = (2, 4, 8, 16)
POOL_HIST = 15
DN_CONV = 4
TOPK = 16
NKEYS = 128
VMEM_LIMIT = 56 * 1024 * 1024


def _cparams(*sem):
    return pltpu.CompilerParams(dimension_semantics=sem, vmem_limit_bytes=VMEM_LIMIT)


def _bdot(a, b):
    return jnp.dot(a.astype(BF16), b.astype(BF16), preferred_element_type=F32)


def _bdot_nt(a, b):
    return lax.dot_general(a.astype(BF16), b.astype(BF16), (((1,), (1,)), ((), ())),
                           preferred_element_type=F32)


def _silu(x):
    return x * (1.0 / (1.0 + jnp.exp(-x)))


def _addnorm_body(*refs, add):
    if add:
        h_ref, y_ref, g_ref, hs_ref, xn_ref = refs
        x = h_ref[...] + y_ref[...]
        hs_ref[...] = x
    else:
        h_ref, g_ref, xn_ref = refs
        x = h_ref[...]
    ms = jnp.mean(x * x, axis=-1, keepdims=True)
    xn_ref[...] = (x * lax.rsqrt(ms + EPS) * g_ref[...]).astype(xn_ref.dtype)


def _addnorm(h, y, g, *, tm=256, out_dtype=BF16):
    n, d = h.shape
    row = pl.BlockSpec((tm, d), lambda i: (i, 0))
    gspec = pl.BlockSpec((1, d), lambda i: (0, 0))
    g2 = g.reshape(1, d).astype(F32)
    if y is None:
        xn = pl.pallas_call(
            functools.partial(_addnorm_body, add=False),
            grid=(n // tm,), in_specs=[row, gspec], out_specs=row,
            out_shape=jax.ShapeDtypeStruct((n, d), out_dtype),
            compiler_params=_cparams("parallel"), name="norm")(h, g2)
        return h, xn
    hs, xn = pl.pallas_call(
        functools.partial(_addnorm_body, add=True),
        grid=(n // tm,), in_specs=[row, row, gspec], out_specs=[row, row],
        out_shape=[jax.ShapeDtypeStruct((n, d), F32), jax.ShapeDtypeStruct((n, d), out_dtype)],
        compiler_params=_cparams("parallel"), name="add_norm")(h, y, g2)
    return hs, xn


def _cast_body(x_ref, o_ref):
    o_ref[...] = x_ref[0].astype(o_ref.dtype)


def _cast(x, l, dtype=BF16, *, tr=512):
    _, r, c = x.shape
    return pl.pallas_call(
        _cast_body, grid=(r // tr,),
        in_specs=[pl.BlockSpec((1, tr, c), lambda i: (l, i, 0))],
        out_specs=pl.BlockSpec((tr, c), lambda i: (i, 0)),
        out_shape=jax.ShapeDtypeStruct((r, c), dtype),
        compiler_params=_cparams("parallel"), name="cast")(x)


FP8 = jnp.float8_e4m3fn
FP8_TARGET = 224.0


def _amax_scale(x):
    a = jnp.max(jnp.max(jnp.abs(x), axis=1, keepdims=True), axis=0, keepdims=True)
    return jnp.where(a > 0.0, FP8_TARGET / a, 1.0)


def _quant_body(x_ref, o_ref, inv_ref):
    x = x_ref[0]
    s = _amax_scale(x)
    o_ref[...] = (x * s).astype(FP8)
    inv_ref[0] = jnp.broadcast_to(1.0 / s, inv_ref.shape[1:])


def _quant_fp8(x, l, *, tr):
    _, r, c = x.shape
    return pl.pallas_call(
        _quant_body, grid=(r // tr,),
        in_specs=[pl.BlockSpec((1, tr, c), lambda i: (l, i, 0))],
        out_specs=[pl.BlockSpec((tr, c), lambda i: (i, 0)), pl.BlockSpec((1, 8, BLK), lambda i: (i, 0, 0))],
        out_shape=[jax.ShapeDtypeStruct((r, c), FP8), jax.ShapeDtypeStruct((r // tr, 8, BLK), F32)],
        compiler_params=_cparams("parallel"), name="quant_fp8")(x)


def _cast_w_in_body(x_ref, main_ref, ba_ref, *, lo, nba):
    x = x_ref[0]
    main_ref[:, :lo] = x[:, :lo].astype(BF16)
    main_ref[:, lo:] = x[:, lo + nba:].astype(BF16)
    ba_ref[...] = jnp.concatenate(
        [x[:, lo:lo + nba], jnp.zeros((x.shape[0], BLK - nba), F32)], axis=1).astype(BF16)


def _cast_w_in(w_in, l, lo, nba, *, tr=256):
    _, r, c = w_in.shape
    return pl.pallas_call(
        functools.partial(_cast_w_in_body, lo=lo, nba=nba), grid=(r // tr,),
        in_specs=[pl.BlockSpec((1, tr, c), lambda i: (l, i, 0))],
        out_specs=[pl.BlockSpec((tr, c - nba), lambda i: (i, 0)), pl.BlockSpec((tr, BLK), lambda i: (i, 0))],
        out_shape=[jax.ShapeDtypeStruct((r, c - nba), BF16), jax.ShapeDtypeStruct((r, BLK), BF16)],
        compiler_params=_cparams("parallel"), name="cast_w_in")(w_in)


def _mm_body(x_ref, w_ref, o_ref):
    o_ref[...] = jnp.dot(x_ref[...], w_ref[...], preferred_element_type=F32)


def _mm(x, w, *, tm=512, tn=1024):
    m, k = x.shape
    n = w.shape[1]
    tn = min(tn, n)
    return pl.pallas_call(
        _mm_body, grid=(n // tn, m // tm),
        in_specs=[pl.BlockSpec((tm, k), lambda j, i: (i, 0)),
                  pl.BlockSpec((k, tn), lambda j, i: (0, j))],
        out_specs=pl.BlockSpec((tm, tn), lambda j, i: (i, j)),
        out_shape=jax.ShapeDtypeStruct((m, n), F32),
        compiler_params=_cparams("parallel", "parallel"), name="matmul")(x, w)


def _outproj_body(yp_ref, yd_ref, ya_ref, w_ref, h_ref, o_ref, *, wp, wd):
    acc = h_ref[...]
    acc += jnp.dot(yp_ref[...].astype(BF16), w_ref[0:wp, :], preferred_element_type=F32)
    acc += jnp.dot(yd_ref[...].astype(BF16), w_ref[wp:wp + wd, :], preferred_element_type=F32)
    acc += jnp.dot(ya_ref[...].astype(BF16), w_ref[wp + wd:, :], preferred_element_type=F32)
    o_ref[...] = acc


def _outproj(yp, yd, ya, w, h, *, tm=512, tn=1024):
    m, d = h.shape
    wp, wd, wa = yp.shape[1], yd.shape[1], ya.shape[1]
    k = wp + wd + wa
    return pl.pallas_call(
        functools.partial(_outproj_body, wp=wp, wd=wd), grid=(d // tn, m // tm),
        in_specs=[pl.BlockSpec((tm, wp), lambda j, i: (i, 0)),
                  pl.BlockSpec((tm, wd), lambda j, i: (i, 0)),
                  pl.BlockSpec((tm, wa), lambda j, i: (i, 0)),
                  pl.BlockSpec((k, tn), lambda j, i: (0, j)),
                  pl.BlockSpec((tm, tn), lambda j, i: (i, j))],
        out_specs=pl.BlockSpec((tm, tn), lambda j, i: (i, j)),
        out_shape=jax.ShapeDtypeStruct((m, d), F32),
        compiler_params=_cparams("parallel", "parallel"), name="out_proj")(yp, yd, ya, w, h)


def _pool_windows(ext_ref, u, t, pos, w_ref, s_ref, gw):
    outs = []
    for gi, w in enumerate(POOL_WINDOWS):
        sl = slice(gi * gw, (gi + 1) * gw)
        win = u[:, sl]
        for k in range(1, w):
            win = win + ext_ref[16 - k:16 - k + t, sl]
        cnt = jnp.clip(pos + 1, 1, w).astype(F32)
        d = win / cnt - u[:, sl]
        outs.append(_bdot(d, w_ref[gi]))
    return jnp.concatenate(outs, axis=-1) * s_ref[...]


def _pool_prompt_body(u_ref, w_ref, s_ref, y_ref, ext_ref, *, gw):
    n = pl.program_id(1)

    @pl.when(n == 0)
    def _():
        ext_ref[0:16, :] = jnp.zeros((16, ext_ref.shape[1]), F32)

    u = u_ref[...]
    ext_ref[16:16 + BLK, :] = u
    row = n * BLK + lax.broadcasted_iota(jnp.int32, (BLK, 1), 0)
    pos = row - FRONT
    y = _pool_windows(ext_ref, u, BLK, pos, w_ref, s_ref, gw)
    y_ref[...] = jnp.where(pos >= 0, y, 0.0)
    ext_ref[0:16, :] = u[BLK - 16:, :]


def _pool_prompt(p, w_pool, s_pool, nseq, nblk):
    pw = w_pool.shape[0] * w_pool.shape[1]
    gw = w_pool.shape[1]
    return pl.pallas_call(
        functools.partial(_pool_prompt_body, gw=gw), grid=(nseq, nblk),
        in_specs=[pl.BlockSpec((BLK, pw), lambda b, n: (b * nblk + n, 0)),
                  pl.BlockSpec(w_pool.shape, lambda b, n: (0, 0, 0)),
                  pl.BlockSpec((1, pw), lambda b, n: (0, 0))],
        out_specs=pl.BlockSpec((BLK, pw), lambda b, n: (b * nblk + n, 0)),
        out_shape=jax.ShapeDtypeStruct((p.shape[0], pw), F32),
        scratch_shapes=[pltpu.VMEM((16 + BLK, pw), F32)],
        compiler_params=_cparams("parallel", "arbitrary"), name="pool_prompt")(p, w_pool, s_pool)


def _pool_sample_body(u_ref, hist_ref, w_ref, s_ref, ybuf_ref, y_ref, ext_ref, *, gw, pos0):
    del ybuf_ref
    u = u_ref[...]
    ext_ref[0:1, :] = jnp.zeros((1, ext_ref.shape[1]), F32)
    ext_ref[1:16, :] = hist_ref[0]
    ext_ref[16:16 + DEC_T, :] = u
    pos = pos0 + lax.broadcasted_iota(jnp.int32, (DEC_T, 1), 0)
    y_ref[...] = _pool_windows(ext_ref, u, DEC_T, pos, w_ref, s_ref, gw)


def _sample_row_block(s, rows_per_seq):
    return (s // SLOTS) * (rows_per_seq // DEC_T) + s % SLOTS


def _pool_sample(p, hist, w_pool, s_pool, ybuf, rows_per_seq, pos0):
    nb = hist.shape[0]
    pw = hist.shape[2]
    gw = w_pool.shape[1]
    rowmap = lambda s: (_sample_row_block(s, rows_per_seq), 0)
    return pl.pallas_call(
        functools.partial(_pool_sample_body, gw=gw, pos0=pos0), grid=(nb,),
        in_specs=[pl.BlockSpec((DEC_T, pw), rowmap),
                  pl.BlockSpec((1, POOL_HIST, pw), lambda s: (s, 0, 0)),
                  pl.BlockSpec(w_pool.shape, lambda s: (0, 0, 0)),
                  pl.BlockSpec((1, pw), lambda s: (0, 0)),
                  pl.BlockSpec(memory_space=pl.ANY)],
        out_specs=pl.BlockSpec((DEC_T, pw), rowmap),
        out_shape=jax.ShapeDtypeStruct(ybuf.shape, F32),
        scratch_shapes=[pltpu.VMEM((16 + DEC_T, pw), F32)],
        input_output_aliases={4: 0},
        compiler_params=_cparams("arbitrary"), name="pool_sample")(p, hist, w_pool, s_pool, ybuf)


def _cumsum_lanes(x):
    lane = lax.broadcasted_iota(jnp.int32, x.shape, 1)
    s = 1
    while s < x.shape[1]:
        x = x + jnp.where(lane >= s, pltpu.roll(x, s, axis=1), 0.0)
        s *= 2
    return x


HEAD_PACK = 2


def _unit_lower_inverse(mats, ii, jj):
    eye = jnp.where(ii == jj, 1.0, 0.0).astype(F32)
    pair = ((ii // 2) == (jj // 2)) & (ii % 2 == 1) & (jj % 2 == 0)
    xs = [eye - jnp.where(pair, a, 0.0) for a in mats]
    s = 2
    while s < BLK:
        mask = ((ii // (2 * s)) == (jj // (2 * s))) & ((ii // s) % 2 == 1) & ((jj // s) % 2 == 0)
        ts = [_bdot(jnp.where(mask, a, 0.0), x) for a, x in zip(mats, xs)]
        xs = [x - _bdot(x, t) for x, t in zip(xs, ts)]
        s *= 2
    return xs


def _delta_scalars(ba, valid, alog_ref, dtb_ref, nh):
    bat = ba.T
    beta = jnp.where(valid, 1.0 / (1.0 + jnp.exp(-bat[0:nh])), 0.0)
    z = bat[nh:2 * nh] + dtb_ref[...]
    softplus = jnp.maximum(z, 0.0) + jnp.log(1.0 + jnp.exp(-jnp.abs(z)))
    g = jnp.where(valid, -jnp.exp(alog_ref[...]) * softplus, 0.0)
    gc = _cumsum_lanes(g)
    glast = jnp.broadcast_to(gc[:, BLK - 1:BLK], gc.shape)
    eg = jnp.exp(gc)
    rows = jnp.concatenate(
        [gc, eg, beta, beta * eg, jnp.exp(glast - gc), jnp.exp(glast),
         jnp.zeros((BLK - 6 * nh, BLK), F32)], axis=0)
    return gc, rows.T


def _l2n(x):
    return x * lax.rsqrt(jnp.sum(x * x, axis=-1, keepdims=True) + EPS)


def _delta_chunk_small(xq, xk, xv, ba, alog_ref, dtb_ref, s_ref, nh, dk):
    t = xq.shape[0]
    zrows = jnp.zeros((BLK - t, BLK), F32)
    lane = lax.broadcasted_iota(jnp.int32, (nh, BLK), 1)
    gc, cols = _delta_scalars(jnp.concatenate([ba, zrows], axis=0), lane < t, alog_ref, dtb_ref, nh)
    ii = lax.broadcasted_iota(jnp.int32, (t, BLK), 0)
    jj = lax.broadcasted_iota(jnp.int32, (t, BLK), 1)
    incl = ii >= jj
    strict = ii > jj
    each = lambda f: [f(h) for h in range(nh)]
    col = lambda h, qi: cols[0:t, qi * nh + h:qi * nh + h + 1]
    pad = lambda x: jnp.concatenate([x, zrows], axis=0)
    q = each(lambda h: _l2n(xq[:, h * dk:(h + 1) * dk]) * (dk ** -0.5))
    k = each(lambda h: _l2n(xk[:, h * dk:(h + 1) * dk]))
    s = each(lambda h: s_ref[h])
    kpad = each(lambda h: pad(k[h]))
    decay = each(lambda h: jnp.where(incl, jnp.exp(jnp.where(incl, col(h, 0) - gc[h:h + 1, :], 0.0)), 0.0))
    a_mat = each(lambda h: jnp.where(strict, _bdot_nt(k[h], kpad[h]) * decay[h] * col(h, 2), 0.0))
    qk = each(lambda h: _bdot_nt(q[h], kpad[h]) * decay[h])
    wu = each(lambda h: jnp.concatenate([col(h, 3) * k[h], col(h, 2) * xv[:, h * dk:(h + 1) * dk]], axis=-1))
    for j in range(t - 1):
        wu = each(lambda h: wu[h] - a_mat[h][:, j:j + 1] * wu[h][j:j + 1, :])
    v_new = each(lambda h: wu[h][:, dk:] - _bdot(wu[h][:, :dk], s[h]))
    o = each(lambda h: col(h, 1) * _bdot(q[h], s[h]))
    for j in range(t):
        o = each(lambda h: o[h] + qk[h][:, j:j + 1] * v_new[h][j:j + 1, :])
    kd = each(lambda h: pad(k[h] * col(h, 4)).T)
    s_new = each(lambda h: cols[:, 5 * nh + h:5 * nh + h + 1] * s[h] + _bdot(kd[h], pad(v_new[h])))
    for h in range(nh):
        s_ref[h] = s_new[h]
    return o


def _delta_chunk(xq, xk, xv, ba, valid, alog_ref, dtb_ref, s_ref, nh, dk):
    assert dk == BLK and nh % HEAD_PACK == 0
    gc, cols = _delta_scalars(ba, valid, alog_ref, dtb_ref, nh)
    n = HEAD_PACK * BLK
    ii = lax.broadcasted_iota(jnp.int32, (n, n), 0)
    jj = lax.broadcasted_iota(jnp.int32, (n, n), 1)
    same = (ii // BLK) == (jj // BLK)
    incl = same & (ii >= jj)
    strict = same & (ii > jj)
    packs = range(nh // HEAD_PACK)
    each = lambda f: [f(p) for p in packs]
    stack = lambda p, f: jnp.concatenate([f(h) for h in range(p * HEAD_PACK, (p + 1) * HEAD_PACK)], axis=0)
    col = lambda p, qi: stack(p, lambda h: cols[:, qi * nh + h:qi * nh + h + 1])
    diag = lambda m: jnp.where(same, jnp.concatenate([m] * HEAD_PACK, axis=1), 0.0)
    q = each(lambda p: stack(p, lambda h: _l2n(xq[:, h * dk:(h + 1) * dk]) * (dk ** -0.5)))
    k = each(lambda p: stack(p, lambda h: _l2n(xk[:, h * dk:(h + 1) * dk])))
    v = each(lambda p: stack(p, lambda h: xv[:, h * dk:(h + 1) * dk]))
    s = each(lambda p: stack(p, lambda h: s_ref[h]))
    diff = each(lambda p: col(p, 0) - jnp.concatenate(
        [gc[h:h + 1, :] for h in range(p * HEAD_PACK, (p + 1) * HEAD_PACK)], axis=1))
    decay = each(lambda p: jnp.where(incl, jnp.exp(jnp.where(incl, diff[p], 0.0)), 0.0))
    a_mat = each(lambda p: jnp.where(strict, _bdot_nt(k[p], k[p]) * decay[p] * col(p, 2), 0.0))
    qk = each(lambda p: _bdot_nt(q[p], k[p]) * decay[p])
    x = _unit_lower_inverse(a_mat, ii, jj)
    rhs = each(lambda p: jnp.concatenate([col(p, 3) * k[p], col(p, 2) * v[p]], axis=-1))
    wu = each(lambda p: _bdot(x[p], rhs[p]))
    v_new = each(lambda p: wu[p][:, dk:] - _bdot(diag(wu[p][:, :dk]), s[p]))
    qs = each(lambda p: _bdot(diag(q[p]), s[p]))
    o = each(lambda p: col(p, 1) * qs[p] + _bdot(qk[p], v_new[p]))
    kd = each(lambda p: jnp.where(same, jnp.concatenate([(k[p] * col(p, 4)).T] * HEAD_PACK, axis=0), 0.0))
    s_new = each(lambda p: col(p, 5) * s[p] + _bdot(kd[p], v_new[p]))
    outs = []
    for p in packs:
        for i in range(HEAD_PACK):
            s_ref[p * HEAD_PACK + i] = s_new[p][i * BLK:(i + 1) * BLK, :]
            outs.append(o[p][i * BLK:(i + 1) * BLK, :])
    return outs


def _delta_out(o, gate, ng_ref):
    return o * lax.rsqrt(jnp.mean(o * o, axis=-1, keepdims=True) + EPS) * ng_ref[...] * _silu(gate)


def _conv_silu(ext_ref, w_ref, t):
    acc = ext_ref[5:5 + t, :] * w_ref[0:1, :]
    for i in range(1, DN_CONV):
        acc = acc + ext_ref[5 + i:5 + i + t, :] * w_ref[i:i + 1, :]
    return _silu(acc)


def _delta_prompt_body(q_ref, k_ref, v_ref, gt_ref, ba_ref, wq_ref, wk_ref, wv_ref, alog_ref, dtb_ref, ng_ref,
                       y_ref, sout_ref, eq_ref, ek_ref, ev_ref, s_ref, *, nh, dk, nblk):
    c = pl.program_id(1)

    @pl.when(c == 0)
    def _():
        for e in (eq_ref, ek_ref, ev_ref):
            e[0:8, :] = jnp.zeros((8, e.shape[1]), F32)
        s_ref[...] = jnp.zeros(s_ref.shape, F32)

    xs = []
    for x_ref, e_ref, w_ref in ((q_ref, eq_ref, wq_ref), (k_ref, ek_ref, wk_ref), (v_ref, ev_ref, wv_ref)):
        e_ref[8:8 + BLK, :] = x_ref[...]
        xs.append(_conv_silu(e_ref, w_ref, BLK))
        e_ref[0:8, :] = x_ref[BLK - 8:, :]
    lane = lax.broadcasted_iota(jnp.int32, (nh, BLK), 1)
    valid = (c > 0) | (lane >= FRONT)
    outs = _delta_chunk(xs[0], xs[1], xs[2], ba_ref[...], valid, alog_ref, dtb_ref, s_ref, nh, dk)
    row = c * BLK + lax.broadcasted_iota(jnp.int32, (BLK, 1), 0)
    for h in range(nh):
        sl = slice(h * dk, (h + 1) * dk)
        y_ref[:, sl] = jnp.where(row >= FRONT, _delta_out(outs[h], gt_ref[:, sl], ng_ref), 0.0)

    @pl.when(c == nblk - 1)
    def _():
        sout_ref[0] = s_ref[...]


def _delta_prompt(p, ba, w_conv, a_log, dt_bias, norm_g, nseq, nblk, col_q, col_gate, nh, dk):
    hw = nh * dk
    cq, ck, cv, cg = col_q // hw, col_q // hw + 1, col_q // hw + 2, col_gate // hw
    rows = lambda cb: pl.BlockSpec((BLK, hw), lambda b, c: (b * nblk + c, cb))
    wcs = lambda cb: pl.BlockSpec((DN_CONV, hw), lambda b, c: (0, cb))
    small = lambda shape: pl.BlockSpec(shape, lambda b, c: (0, 0))
    return pl.pallas_call(
        functools.partial(_delta_prompt_body, nh=nh, dk=dk, nblk=nblk), grid=(nseq, nblk),
        in_specs=[rows(cq), rows(ck), rows(cv), rows(cg),
                  pl.BlockSpec((BLK, BLK), lambda b, c: (b * nblk + c, 0)),
                  wcs(0), wcs(1), wcs(2), small((nh, 1)), small((nh, 1)), small((1, dk))],
        out_specs=[pl.BlockSpec((BLK, hw), lambda b, c: (b * nblk + c, 0)),
                   pl.BlockSpec((1, nh, dk, dk), lambda b, c: (b, 0, 0, 0))],
        out_shape=[jax.ShapeDtypeStruct((p.shape[0], hw), F32),
                   jax.ShapeDtypeStruct((nseq, nh, dk, dk), F32)],
        scratch_shapes=[pltpu.VMEM((8 + BLK, hw), F32)] * 3 + [pltpu.VMEM((nh, dk, dk), F32)],
        compiler_params=_cparams("parallel", "arbitrary"), name="delta_prompt",
    )(p, p, p, p, ba, w_conv, w_conv, w_conv, a_log.reshape(nh, 1), dt_bias.reshape(nh, 1),
      norm_g.reshape(1, dk))


def _delta_sample_body(q_ref, k_ref, v_ref, gt_ref, ba_ref, hq_ref, hk_ref, hv_ref, s0_ref,
                       wq_ref, wk_ref, wv_ref, alog_ref, dtb_ref, ng_ref, ybuf_ref,
                       y_ref, sout_ref, eq_ref, ek_ref, ev_ref, s_ref, *, nh, dk):
    del ybuf_ref
    xs = []
    for x_ref, h_ref, e_ref, w_ref in ((q_ref, hq_ref, eq_ref, wq_ref), (k_ref, hk_ref, ek_ref, wk_ref),
                                       (v_ref, hv_ref, ev_ref, wv_ref)):
        e_ref[5:8, :] = h_ref[0]
        e_ref[8:8 + DEC_T, :] = x_ref[...]
        xs.append(_conv_silu(e_ref, w_ref, DEC_T))
    s_ref[...] = s0_ref[0]
    outs = _delta_chunk_small(xs[0], xs[1], xs[2], ba_ref[...], alog_ref, dtb_ref, s_ref, nh, dk)
    for h in range(nh):
        sl = slice(h * dk, (h + 1) * dk)
        y_ref[:, sl] = _delta_out(outs[h], gt_ref[:, sl], ng_ref)
    sout_ref[0] = s_ref[...]


def _delta_sample(p, ba, conv_hist, s0, w_conv, a_log, dt_bias, norm_g, ybuf, rows_per_seq,
                  col_q, col_gate, nh, dk):
    nb = s0.shape[0]
    hw = nh * dk
    cq, ck, cv, cg = col_q // hw, col_q // hw + 1, col_q // hw + 2, col_gate // hw
    rowmap = lambda cb: (lambda s: (_sample_row_block(s, rows_per_seq), cb))
    rows = lambda cb: pl.BlockSpec((DEC_T, hw), rowmap(cb))
    hist = lambda cb: pl.BlockSpec((1, DN_CONV - 1, hw), lambda s: (s, 0, cb))
    wcs = lambda cb: pl.BlockSpec((DN_CONV, hw), lambda s: (0, cb))
    small = lambda shape: pl.BlockSpec(shape, lambda s: (0, 0))
    return pl.pallas_call(
        functools.partial(_delta_sample_body, nh=nh, dk=dk), grid=(nb,),
        in_specs=[rows(cq), rows(ck), rows(cv), rows(cg), pl.BlockSpec((DEC_T, BLK), rowmap(0)),
                  hist(0), hist(1), hist(2),
                  pl.BlockSpec((1, nh, dk, dk), lambda s: (s, 0, 0, 0)),
                  wcs(0), wcs(1), wcs(2), small((nh, 1)), small((nh, 1)), small((1, dk)),
                  pl.BlockSpec(memory_space=pl.ANY)],
        out_specs=[pl.BlockSpec((DEC_T, hw), rowmap(0)),
                   pl.BlockSpec((1, nh, dk, dk), lambda s: (s, 0, 0, 0))],
        out_shape=[jax.ShapeDtypeStruct(ybuf.shape, F32), jax.ShapeDtypeStruct(s0.shape, F32)],
        scratch_shapes=[pltpu.VMEM((8 + DEC_T, hw), F32)] * 3 + [pltpu.VMEM((nh, dk, dk), F32)],
        input_output_aliases={15: 0},
        compiler_params=_cparams("arbitrary"), name="delta_sample",
    )(p, p, p, p, ba, conv_hist, conv_hist, conv_hist, s0, w_conv, w_conv, w_conv,
      a_log.reshape(nh, 1), dt_bias.reshape(nh, 1), norm_g.reshape(1, dk), ybuf)


def _attn_prompt_body(*refs, nkv, grp, hd, window):
    q_refs = refs[:nkv]
    kp_ref, kc_ref, vp_ref, vc_ref, slope_ref, sink_ref, y_ref = refs[nkv:]
    n = pl.program_id(1)
    i = lax.broadcasted_iota(jnp.int32, (BLK, 2 * BLK), 0)
    j = lax.broadcasted_iota(jnp.int32, (BLK, 2 * BLK), 1)
    dist = BLK + i - j
    krow = (n - 1) * BLK + j
    valid = (dist >= 0) & (dist < window) & (krow >= FRONT)
    distf = dist.astype(F32)
    kk = jnp.concatenate([kp_ref[...], kc_ref[...]], axis=0).astype(BF16)
    vv = jnp.concatenate([vp_ref[...], vc_ref[...]], axis=0).astype(BF16)
    heads = [(kv, g) for kv in range(nkv) for g in range(grp)]
    each = lambda f: [f(t, kv, g) for t, (kv, g) in enumerate(heads)]
    s = each(lambda t, kv, g: _bdot_nt(q_refs[kv][:, g * hd:(g + 1) * hd], kk[:, kv * hd:(kv + 1) * hd]))
    s = each(lambda t, kv, g: jnp.where(valid, s[t] * (hd ** -0.5) - slope_ref[t] * distf, NEG_INF))
    m = each(lambda t, kv, g: jnp.maximum(jnp.max(s[t], axis=-1, keepdims=True), sink_ref[t]))
    pr = each(lambda t, kv, g: jnp.exp(s[t] - m[t]))
    den = each(lambda t, kv, g: jnp.sum(pr[t], axis=-1, keepdims=True) + jnp.exp(sink_ref[t] - m[t]))
    pv = each(lambda t, kv, g: _bdot(pr[t], vv[:, kv * hd:(kv + 1) * hd]))
    for t in range(len(heads)):
        y_ref[:, t * hd:(t + 1) * hd] = pv[t] / den[t]


def _attn_prompt(p, slopes, sinks, nseq, nblk, col_q, col_k, col_v, nkv, grp, hd, window):
    gw, kw = grp * hd, nkv * hd
    prev = lambda cb: (lambda b, n: (b * nblk + jnp.maximum(n - 1, 0), cb))
    cur = lambda cb: (lambda b, n: (b * nblk + n, cb))
    smem = pl.BlockSpec(memory_space=pltpu.SMEM)
    return pl.pallas_call(
        functools.partial(_attn_prompt_body, nkv=nkv, grp=grp, hd=hd, window=window), grid=(nseq, nblk),
        in_specs=[pl.BlockSpec((BLK, gw), cur(col_q // gw + kv)) for kv in range(nkv)]
        + [pl.BlockSpec((BLK, kw), prev(col_k // kw)), pl.BlockSpec((BLK, kw), cur(col_k // kw)),
           pl.BlockSpec((BLK, kw), prev(col_v // kw)), pl.BlockSpec((BLK, kw), cur(col_v // kw)),
           smem, smem],
        out_specs=pl.BlockSpec((BLK, nkv * gw), cur(0)),
        out_shape=jax.ShapeDtypeStruct((p.shape[0], nkv * gw), F32),
        compiler_params=_cparams("parallel", "arbitrary"), name="attn_prompt",
    )(*([p] * (nkv + 4)), slopes, sinks)


def _attn_sample_body(*refs, nkv, grp, hd, window):
    q_refs = refs[:nkv]
    k_ref, v_ref, ck_ref, cv_ref, slope_ref, sink_ref, ybuf_ref, y_ref = refs[nkv:]
    del ybuf_ref
    wc = ck_ref.shape[1]
    i = lax.broadcasted_iota(jnp.int32, (DEC_T, wc), 0)
    j = lax.broadcasted_iota(jnp.int32, (DEC_T, wc), 1)
    dist_c = wc + i - j
    valid_c = (dist_c >= 0) & (dist_c < window)
    i2 = lax.broadcasted_iota(jnp.int32, (DEC_T, DEC_T), 0)
    j2 = lax.broadcasted_iota(jnp.int32, (DEC_T, DEC_T), 1)
    dist_n = i2 - j2
    valid_n = (dist_n >= 0) & (dist_n < window)
    tile = lambda a: jnp.concatenate([a] * grp, axis=0)
    valid_c, valid_n = tile(valid_c), tile(valid_n)
    dist_c, dist_n = tile(dist_c).astype(F32), tile(dist_n).astype(F32)
    gi = lax.broadcasted_iota(jnp.int32, (grp * DEC_T, 1), 0) // DEC_T

    def per_row(ref, kv):
        out = jnp.full((grp * DEC_T, 1), ref[kv * grp], F32)
        for g in range(1, grp):
            out = jnp.where(gi == g, ref[kv * grp + g], out)
        return out

    each = lambda f: [f(kv) for kv in range(nkv)]
    cols = lambda ref, kv: ref[:, kv * hd:(kv + 1) * hd]
    q = each(lambda kv: jnp.concatenate([q_refs[kv][:, g * hd:(g + 1) * hd] for g in range(grp)], axis=0))
    slope = each(lambda kv: per_row(slope_ref, kv))
    sink = each(lambda kv: per_row(sink_ref, kv))
    sc = each(lambda kv: _bdot_nt(q[kv], ck_ref[0, :, kv * hd:(kv + 1) * hd]) * (hd ** -0.5))
    sn = each(lambda kv: _bdot_nt(q[kv], cols(k_ref, kv)) * (hd ** -0.5))
    sc = each(lambda kv: jnp.where(valid_c, sc[kv] - slope[kv] * dist_c, NEG_INF))
    sn = each(lambda kv: jnp.where(valid_n, sn[kv] - slope[kv] * dist_n, NEG_INF))
    m = each(lambda kv: jnp.maximum(jnp.maximum(jnp.max(sc[kv], axis=-1, keepdims=True),
                                                jnp.max(sn[kv], axis=-1, keepdims=True)), sink[kv]))
    pc = each(lambda kv: jnp.exp(sc[kv] - m[kv]))
    pn = each(lambda kv: jnp.exp(sn[kv] - m[kv]))
    den = each(lambda kv: jnp.sum(pc[kv], axis=-1, keepdims=True) + jnp.sum(pn[kv], axis=-1, keepdims=True)
               + jnp.exp(sink[kv] - m[kv]))
    o = each(lambda kv: (_bdot(pc[kv], cv_ref[0, :, kv * hd:(kv + 1) * hd]) + _bdot(pn[kv], cols(v_ref, kv)))
             / den[kv])
    for kv in range(nkv):
        for g in range(grp):
            head = kv * grp + g
            y_ref[:, head * hd:(head + 1) * hd] = o[kv][g * DEC_T:(g + 1) * DEC_T, :]


def _attn_sample(p, cache_k, cache_v, slopes, sinks, ybuf, rows_per_seq, col_q, col_k, col_v,
                 nkv, grp, hd, window):
    nb, wc = cache_k.shape[0], cache_k.shape[1]
    gw, kw = grp * hd, nkv * hd
    ck3 = cache_k.reshape(nb, wc, kw)
    cv3 = cache_v.reshape(nb, wc, kw)
    rowmap = lambda cb: (lambda s: (_sample_row_block(s, rows_per_seq), cb))
    smem = pl.BlockSpec(memory_space=pltpu.SMEM)
    return pl.pallas_call(
        functools.partial(_attn_sample_body, nkv=nkv, grp=grp, hd=hd, window=window), grid=(nb,),
        in_specs=[pl.BlockSpec((DEC_T, gw), rowmap(col_q // gw + kv)) for kv in range(nkv)]
        + [pl.BlockSpec((DEC_T, kw), rowmap(col_k // kw)),
           pl.BlockSpec((DEC_T, kw), rowmap(col_v // kw)),
           pl.BlockSpec((1, wc, kw), lambda s: (s, 0, 0)),
           pl.BlockSpec((1, wc, kw), lambda s: (s, 0, 0)),
           smem, smem, pl.BlockSpec(memory_space=pl.ANY)],
        out_specs=pl.BlockSpec((DEC_T, nkv * gw), rowmap(0)),
        out_shape=jax.ShapeDtypeStruct(ybuf.shape, F32),
        input_output_aliases={nkv + 6: 0},
        compiler_params=_cparams("arbitrary"), name="attn_sample",
    )(*([p] * (nkv + 2)), ck3, cv3, slopes, sinks, ybuf)


SUB = 8


def _bitonic_pairs(n, merge_only=False):
    out = []
    k = n if merge_only else 2
    while k <= n:
        j = k // 2
        while j >= 1:
            out += [(i, i ^ j, (i & k) == 0) for i in range(n) if (i ^ j) > i]
            j //= 2
        k *= 2
    return out


def _compare_exchange(v, pairs):
    v = list(v)
    for i, l, desc in pairs:
        hi, lo = jnp.maximum(v[i], v[l]), jnp.minimum(v[i], v[l])
        v[i], v[l] = (hi, lo) if desc else (lo, hi)
    return v


def _top16(rows):
    v = _compare_exchange(rows, _bitonic_pairs(TOPK))
    shift = SUB // 2
    while shift >= 1:
        w = [pltpu.roll(v[TOPK - 1 - r], shift, axis=0) for r in range(TOPK)]
        v = _compare_exchange([jnp.maximum(a, b) for a, b in zip(v, w)], _bitonic_pairs(TOPK, merge_only=True))
        shift //= 2
    return v


def _sublane_sum(x):
    shift = SUB // 2
    while shift >= 1:
        x = x + pltpu.roll(x, shift, axis=0)
        shift //= 2
    return x


def _on_sublanes(vs):
    sub = lax.broadcasted_iota(jnp.int32, vs[0].shape, 0)
    out = vs[SUB - 1]
    for j in range(SUB - 2, -1, -1):
        out = jnp.where(sub == j, vs[j], out)
    return out


def _peer_topk_body(q_ref, keys_ref, rk_ref, cut_ref, e1_ref, e2_ref, *, nh):
    nv = NKEYS // SUB
    for h in range(nh):
        sc = [_bdot_nt(keys_ref[2 * h + half], q_ref[:, (2 * h + half) * NKEYS:(2 * h + half + 1) * NKEYS])
              for half in (0, 1)]
        s1 = [sc[0][SUB * i:SUB * (i + 1), :] for i in range(nv)]
        s2 = [sc[1][SUB * i:SUB * (i + 1), :] for i in range(nv)]
        a = _top16(s1)
        b = _top16(s2)
        b_lo, b_hi, a_hi = _on_sublanes(b[:SUB]), _on_sublanes(b[SUB:]), _on_sublanes(a[SUB:])
        cand = [a[0] + b_lo, a[0] + b_hi, a_hi + b[0]] + [a[i] + b_lo for i in range(1, SUB)]
        cand += [jnp.full(cand[0].shape, LOWEST, F32)] * (TOPK - len(cand))
        top = _top16(cand)
        thr = top[TOPK - 1]
        zsum = jnp.exp(top[0] - top[0])
        for r in range(1, TOPK):
            zsum = zsum + jnp.exp(top[r] - top[0])
        rz = 1.0 / zsum
        height = [_sublane_sum(jnp.where(a[r] + b_lo >= thr, 1.0, 0.0) + jnp.where(a[r] + b_hi >= thr, 1.0, 0.0))
                  for r in range(TOPK)]
        cut, rank2 = [], []
        for i in range(nv):
            c = jnp.zeros(s1[i].shape, F32)
            for r in range(TOPK - 1, -1, -1):
                c = jnp.where(s1[i] == a[r], height[r], c)
            cut.append(c)
            k = jnp.where(b[0] > s2[i], 1.0, 0.0)
            for r in range(1, TOPK):
                k = k + jnp.where(b[r] > s2[i], 1.0, 0.0)
            rank2.append(k)
        rk_ref[h] = jnp.concatenate(rank2, axis=0).astype(BF16)
        cut_ref[h] = jnp.concatenate(cut, axis=0)
        e1_ref[h] = jnp.exp(sc[0] - a[0][0:1, :])
        e2_ref[h] = (jnp.exp(sc[1] - b[0][0:1, :]) * rz[0:1, :]).astype(BF16)


def _peer_topk(q, sub_keys, *, tm=128):
    n = q.shape[0]
    nh = sub_keys.shape[0]
    keys = sub_keys.reshape(2 * nh, NKEYS, sub_keys.shape[-1])
    sspec = pl.BlockSpec((nh, NKEYS, tm), lambda i: (0, 0, i))
    shape = lambda dt: jax.ShapeDtypeStruct((nh, NKEYS, n), dt)
    return pl.pallas_call(
        functools.partial(_peer_topk_body, nh=nh), grid=(n // tm,),
        in_specs=[pl.BlockSpec((tm, q.shape[1]), lambda i: (i, 0)),
                  pl.BlockSpec(keys.shape, lambda i: (0, 0, 0))],
        out_specs=[sspec] * 4,
        out_shape=[shape(BF16), shape(F32), shape(F32), shape(BF16)],
        compiler_params=_cparams("parallel"), name="peer_topk")(q, keys)


def _gelu(x):
    return 0.5 * x * (1.0 + lax.erf(x * (2.0 ** -0.5)))


PACK = 16


def _peer_expert_body(xn_ref, wd_ref, wu_ref, id_ref, iu_ref, rk_ref, cut_ref, e1_ref, e2_ref, y_ref,
                      xt_ref, ix_ref, *, nh, ei, sub):
    c = pl.program_id(1)
    tm = xt_ref.shape[1]

    @pl.when(c == 0)
    def _():
        y_ref[...] = jnp.zeros(y_ref.shape, F32)
        x = xn_ref[...].astype(F32)
        sx = _amax_scale(x)
        xt_ref[...] = (x * sx).T.astype(FP8)
        ix_ref[...] = jnp.broadcast_to(1.0 / sx, ix_ref.shape)

    def row16(ref, h, i1):
        return jnp.broadcast_to(ref[h, pl.ds(i1, 1), :], (PACK, tm)).astype(BF16)

    inv_h = id_ref[0, 0:1, 0:1] * ix_ref[0:1, 0:1]
    inv_u = iu_ref[0, 0:1, 0:1]
    for k in range(ei // sub):
        rows = slice(k * sub * NKEYS, (k + 1) * sub * NKEYS)
        ht = jnp.dot(wd_ref[rows, :], xt_ref[...], preferred_element_type=F32) * inv_h
        act = _gelu(ht)
        pieces = []
        for ii in range(sub):
            i1 = c * ei + k * sub + ii
            cut = [row16(cut_ref, h, i1) for h in range(nh)]
            e1 = [row16(e1_ref, h, i1) for h in range(nh)]
            for r in range(NKEYS // PACK):
                sl = slice(r * PACK, (r + 1) * PACK)
                gate = None
                for h in range(nh):
                    w = jnp.where(rk_ref[h, sl, :] < cut[h], e1[h] * e2_ref[h, sl, :], 0.0)
                    gate = w if gate is None else gate + w
                base = ii * NKEYS + r * PACK
                pieces.append(act[base:base + PACK, :] * gate.astype(F32))
        at = jnp.concatenate(pieces, axis=0)
        sa = _amax_scale(at)
        part = jnp.dot((at * sa).T.astype(FP8), wu_ref[rows, :], preferred_element_type=F32)
        y_ref[...] += part * (inv_u / sa)


PEER_TM, PEER_EI, PEER_SUB = 512, 4, 4


def _peer_expert(xn, w_down, w_up, inv_down, inv_up, rank2, cut, e1, e2, *, tm=PEER_TM, ei=PEER_EI, sub=PEER_SUB):
    n, d = xn.shape
    nh = rank2.shape[0]
    e = ei * NKEYS
    once = dict(pipeline_mode=pl.Buffered(1))
    sspec = pl.BlockSpec((nh, NKEYS, tm), lambda i, c: (0, 0, i), **once)
    inv = pl.BlockSpec((1, 8, BLK), lambda i, c: (c, 0, 0))
    return pl.pallas_call(
        functools.partial(_peer_expert_body, nh=nh, ei=ei, sub=sub), grid=(n // tm, w_down.shape[0] // e),
        in_specs=[pl.BlockSpec((tm, d), lambda i, c: (i, 0), **once),
                  pl.BlockSpec((e, d), lambda i, c: (c, 0)),
                  pl.BlockSpec((e, d), lambda i, c: (c, 0)),
                  inv, inv, sspec, sspec, sspec, sspec],
        out_specs=pl.BlockSpec((tm, d), lambda i, c: (i, 0), **once),
        out_shape=jax.ShapeDtypeStruct((n, d), F32),
        scratch_shapes=[pltpu.VMEM((d, tm), FP8), pltpu.VMEM((8, BLK), F32)],
        compiler_params=_cparams("parallel", "arbitrary"), name="peer_expert",
    )(xn, w_down, w_up, inv_down, inv_up, rank2, cut, e1, e2)


def _final_norm(h, y, g, nseq, nblk, row_blk, first_blk, nblk_out):
    d = h.shape[1]
    row = pl.BlockSpec((row_blk, d), lambda b, i: (b * nblk + first_blk + i, 0))
    return pl.pallas_call(
        _final_norm_body, grid=(nseq, nblk_out),
        in_specs=[row, row, pl.BlockSpec((1, d), lambda b, i: (0, 0))],
        out_specs=pl.BlockSpec((1, row_blk, d), lambda b, i: (b, i, 0)),
        out_shape=jax.ShapeDtypeStruct((nseq, nblk_out * row_blk, d), F32),
        compiler_params=_cparams("parallel", "parallel"), name="final_norm",
    )(h, y, g.reshape(1, d).astype(F32))


def _final_norm_body(h_ref, y_ref, g_ref, o_ref):
    x = h_ref[...] + y_ref[...]
    ms = jnp.mean(x * x, axis=-1, keepdims=True)
    o_ref[0] = x * lax.rsqrt(ms + EPS) * g_ref[...]


def kernel(x_prompt, x_sample, state_pool, state_conv, state_delta, cache_k, cache_v, meta_tokens, norm1_g,
           w_in, w_pool, s_pool, w_conv, a_log, dt_bias, dn_norm_g, attn_sinks, w_out, norm2_g,
           peer_w_query, peer_sub_keys, peer_w_down, peer_w_up, final_norm_g):
    nseq, seq, d = x_prompt.shape
    nsamp, dec_t, _ = x_sample.shape
    depth = w_in.shape[0]
    pool_w = w_pool.shape[1] * w_pool.shape[2]
    dn_qkv = w_conv.shape[2]
    nh, dk = state_delta.shape[2], state_delta.shape[3]
    wc, nkv, hd = cache_k.shape[2], cache_k.shape[3], cache_k.shape[4]
    nq = attn_sinks.shape[1]
    grp = nq // nkv
    window = wc
    assert dec_t == DEC_T and nsamp == nseq * SLOTS and (N_META + seq) % BLK == N_META
    assert SLOTS * DEC_T + POOL_HIST <= FRONT and dn_qkv == 3 * nh * dk and wc == BLK
    rows_per_seq = FRONT + N_META + seq
    nblk = rows_per_seq // BLK
    past_len = 16384
    col_qkv = pool_w
    col_gate = col_qkv + dn_qkv
    col_q = col_gate + nh * dk
    col_k = col_q + nq * hd
    col_v = col_k + nkv * hd
    src_ba = pool_w + dn_qkv

    xs = x_sample.reshape(nseq, SLOTS * DEC_T, d)
    zeros = jnp.zeros((nseq, FRONT - SLOTS * DEC_T, d), F32)
    meta = jnp.broadcast_to(meta_tokens[None], (nseq, N_META, d))
    h = jnp.concatenate([xs, zeros, meta, x_prompt], axis=1).reshape(nseq * rows_per_seq, d)

    slopes = jnp.exp2(-8.0 * (jnp.arange(nq, dtype=F32) + 1.0) / nq)
    new_p = [[] for _ in range(5)]
    new_s = [[] for _ in range(5)]
    y_peer = None
    for l in range(depth):
        w_main, w_ba = _cast_w_in(w_in, l, src_ba, 2 * nh)
        h, xn = _addnorm(h, y_peer, norm1_g[l])
        p = _mm(xn, w_main)
        ba = _mm(xn, w_ba)

        y_pool = _pool_prompt(p, w_pool[l], s_pool[l].reshape(1, pool_w), nseq, nblk)
        y_pool = _pool_sample(p, state_pool[l], w_pool[l], s_pool[l].reshape(1, pool_w), y_pool,
                              rows_per_seq, past_len)
        y_dn, s_p = _delta_prompt(p, ba, w_conv[l], a_log[l], dt_bias[l], dn_norm_g[l], nseq, nblk,
                                  col_qkv, col_gate, nh, dk)
        y_dn, s_s = _delta_sample(p, ba, state_conv[l], state_delta[l], w_conv[l], a_log[l], dt_bias[l],
                                  dn_norm_g[l], y_dn, rows_per_seq, col_qkv, col_gate, nh, dk)
        y_att = _attn_prompt(p, slopes, attn_sinks[l], nseq, nblk, col_q, col_k, col_v, nkv, grp, hd, window)
        y_att = _attn_sample(p, cache_k[l], cache_v[l], slopes, attn_sinks[l], y_att, rows_per_seq,
                             col_q, col_k, col_v, nkv, grp, hd, window)
        h = _outproj(y_pool, y_dn, y_att, _cast(w_out, l), h)

        _, xn2 = _addnorm(h, None, norm2_g[l])
        q = _mm(xn2, _cast(peer_w_query, l))
        rank2, cut, e1, e2 = _peer_topk(q, peer_sub_keys[l])
        wd8, inv_d = _quant_fp8(peer_w_down, l, tr=PEER_EI * NKEYS)
        wu8, inv_u = _quant_fp8(peer_w_up, l, tr=PEER_EI * NKEYS)
        y_peer = _peer_expert(xn2, wd8, wu8, inv_d, inv_u, rank2, cut, e1, e2)

        p3 = p.reshape(nseq, rows_per_seq, p.shape[1])
        ps = p3[:, :SLOTS * DEC_T].reshape(nsamp, DEC_T, p.shape[1])
        new_p[0].append(p3[:, -POOL_HIST:, :pool_w])
        new_s[0].append(jnp.concatenate([state_pool[l], ps[:, :, :pool_w]], axis=1)[:, -POOL_HIST:])
        new_p[1].append(p3[:, -(DN_CONV - 1):, col_qkv:col_gate])
        new_s[1].append(jnp.concatenate([state_conv[l], ps[:, :, col_qkv:col_gate]], axis=1)[:, -(DN_CONV - 1):])
        new_p[2].append(s_p)
        new_s[2].append(s_s)
        new_p[3].append(p3[:, -window:, col_k:col_v].reshape(nseq, window, nkv, hd))
        new_s[3].append(jnp.concatenate([cache_k[l], ps[:, :, col_k:col_v].reshape(nsamp, DEC_T, nkv, hd)],
                                        axis=1)[:, -wc:])
        new_p[4].append(p3[:, -window:, col_v:].reshape(nseq, window, nkv, hd))
        new_s[4].append(jnp.concatenate([cache_v[l], ps[:, :, col_v:].reshape(nsamp, DEC_T, nkv, hd)],
                                        axis=1)[:, -wc:])

    y_prompt = _final_norm(h, y_peer, final_norm_g, nseq, nblk, BLK, 1, nblk - 1)
    y_sample = _final_norm(h, y_peer, final_norm_g, nseq, rows_per_seq // (SLOTS * DEC_T), SLOTS * DEC_T, 0, 1)
    y_sample = y_sample.reshape(nsamp, DEC_T, d)
    pool_p, conv_p, delta_p, k_p, v_p = (jnp.stack(a) for a in new_p)
    pool_s, conv_s, delta_s, k_s, v_s = (jnp.stack(a) for a in new_s)
    return (y_prompt, y_sample, pool_p, pool_s, conv_p, conv_s, delta_p, delta_s, k_p, k_s, v_p, v_s)
```

```python
import functools

import jax
import jax.numpy as jnp
from jax import lax
from jax.experimental import pallas as pl
from jax.experimental.pallas import tpu as pltpu

F32 = jnp.float32
BF16 = jnp.bfloat16

EPS = 1e-6
NEG_INF = -1e30
LOWEST = -3.0e38

N_META = 16
BLK = 128
FRONT = BLK - N_META
DEC_T = 8
SLOTS = 8
POOL_WINDOWS = (2, 4, 8, 16)
POOL_HIST = 15
DN_CONV = 4
TOPK = 16
NKEYS = 128
VMEM_LIMIT = 56 * 1024 * 1024


def _cparams(*sem):
    return pltpu.CompilerParams(dimension_semantics=sem, vmem_limit_bytes=VMEM_LIMIT)


def _bdot(a, b):
    return jnp.dot(a.astype(BF16), b.astype(BF16), preferred_element_type=F32)


def _bdot_nt(a, b):
    return lax.dot_general(a.astype(BF16), b.astype(BF16), (((1,), (1,)), ((), ())),
                           preferred_element_type=F32)


def _silu(x):
    return x * (1.0 / (1.0 + jnp.exp(-x)))


def _addnorm_body(*refs, add):
    if add:
        h_ref, y_ref, g_ref, hs_ref, xn_ref = refs
        x = h_ref[...] + y_ref[...]
        hs_ref[...] = x
    else:
        h_ref, g_ref, xn_ref = refs
        x = h_ref[...]
    ms = jnp.mean(x * x, axis=-1, keepdims=True)
    xn_ref[...] = (x * lax.rsqrt(ms + EPS) * g_ref[...]).astype(xn_ref.dtype)


def _addnorm(h, y, g, *, tm=256, out_dtype=BF16):
    n, d = h.shape
    row = pl.BlockSpec((tm, d), lambda i: (i, 0))
    gspec = pl.BlockSpec((1, d), lambda i: (0, 0))
    g2 = g.reshape(1, d).astype(F32)
    if y is None:
        xn = pl.pallas_call(
            functools.partial(_addnorm_body, add=False),
            grid=(n // tm,), in_specs=[row, gspec], out_specs=row,
            out_shape=jax.ShapeDtypeStruct((n, d), out_dtype),
            compiler_params=_cparams("parallel"), name="norm")(h, g2)
        return h, xn
    hs, xn = pl.pallas_call(
        functools.partial(_addnorm_body, add=True),
        grid=(n // tm,), in_specs=[row, row, gspec], out_specs=[row, row],
        out_shape=[jax.ShapeDtypeStruct((n, d), F32), jax.ShapeDtypeStruct((n, d), out_dtype)],
        compiler_params=_cparams("parallel"), name="add_norm")(h, y, g2)
    return hs, xn


def _cast_body(x_ref, o_ref):
    o_ref[...] = x_ref[0].astype(o_ref.dtype)


def _cast(x, l, dtype=BF16, *, tr=512):
    _, r, c = x.shape
    return pl.pallas_call(
        _cast_body, grid=(r // tr,),
        in_specs=[pl.BlockSpec((1, tr, c), lambda i: (l, i, 0))],
        out_specs=pl.BlockSpec((tr, c), lambda i: (i, 0)),
        out_shape=jax.ShapeDtypeStruct((r, c), dtype),
        compiler_params=_cparams("parallel"), name="cast")(x)


FP8 = jnp.float8_e4m3fn
FP8_TARGET = 224.0


def _amax_scale(x):
    a = jnp.max(jnp.max(jnp.abs(x), axis=1, keepdims=True), axis=0, keepdims=True)
    return jnp.where(a > 0.0, FP8_TARGET / a, 1.0)


def _quant_body(x_ref, o_ref, inv_ref):
    x = x_ref[0]
    s = _amax_scale(x)
    o_ref[...] = (x * s).astype(FP8)
    inv_ref[0] = jnp.broadcast_to(1.0 / s, inv_ref.shape[1:])


def _quant_fp8(x, l, *, tr):
    _, r, c = x.shape
    return pl.pallas_call(
        _quant_body, grid=(r // tr,),
        in_specs=[pl.BlockSpec((1, tr, c), lambda i: (l, i, 0))],
        out_specs=[pl.BlockSpec((tr, c), lambda i: (i, 0)), pl.BlockSpec((1, 8, BLK), lambda i: (i, 0, 0))],
        out_shape=[jax.ShapeDtypeStruct((r, c), FP8), jax.ShapeDtypeStruct((r // tr, 8, BLK), F32)],
        compiler_params=_cparams("parallel"), name="quant_fp8")(x)


def _cast_w_in_body(x_ref, main_ref, ba_ref, *, lo, nba):
    x = x_ref[0]
    main_ref[:, :lo] = x[:, :lo].astype(BF16)
    main_ref[:, lo:] = x[:, lo + nba:].astype(BF16)
    ba_ref[...] = jnp.concatenate(
        [x[:, lo:lo + nba], jnp.zeros((x.shape[0], BLK - nba), F32)], axis=1).astype(BF16)


def _cast_w_in(w_in, l, lo, nba, *, tr=256):
    _, r, c = w_in.shape
    return pl.pallas_call(
        functools.partial(_cast_w_in_body, lo=lo, nba=nba), grid=(r // tr,),
        in_specs=[pl.BlockSpec((1, tr, c), lambda i: (l, i, 0))],
        out_specs=[pl.BlockSpec((tr, c - nba), lambda i: (i, 0)), pl.BlockSpec((tr, BLK), lambda i: (i, 0))],
        out_shape=[jax.ShapeDtypeStruct((r, c - nba), BF16), jax.ShapeDtypeStruct((r, BLK), BF16)],
        compiler_params=_cparams("parallel"), name="cast_w_in")(w_in)


def _mm_body(x_ref, w_ref, o_ref):
    o_ref[...] = jnp.dot(x_ref[...], w_ref[...], preferred_element_type=F32)


def _mm(x, w, *, tm=512, tn=1024):
    m, k = x.shape
    n = w.shape[1]
    tn = min(tn, n)
    return pl.pallas_call(
        _mm_body, grid=(n // tn, m // tm),
        in_specs=[pl.BlockSpec((tm, k), lambda j, i: (i, 0)),
                  pl.BlockSpec((k, tn), lambda j, i: (0, j))],
        out_specs=pl.BlockSpec((tm, tn), lambda j, i: (i, j)),
        out_shape=jax.ShapeDtypeStruct((m, n), F32),
        compiler_params=_cparams("parallel", "parallel"), name="matmul")(x, w)


def _outproj_body(yp_ref, yd_ref, ya_ref, w_ref, h_ref, o_ref, *, wp, wd):
    acc = h_ref[...]
    acc += jnp.dot(yp_ref[...].astype(BF16), w_ref[0:wp, :], preferred_element_type=F32)
    acc += jnp.dot(yd_ref[...].astype(BF16), w_ref[wp:wp + wd, :], preferred_element_type=F32)
    acc += jnp.dot(ya_ref[...].astype(BF16), w_ref[wp + wd:, :], preferred_element_type=F32)
    o_ref[...] = acc


def _outproj(yp, yd, ya, w, h, *, tm=512, tn=1024):
    m, d = h.shape
    wp, wd, wa = yp.shape[1], yd.shape[1], ya.shape[1]
    k = wp + wd + wa
    return pl.pallas_call(
        functools.partial(_outproj_body, wp=wp, wd=wd), grid=(d // tn, m // tm),
        in_specs=[pl.BlockSpec((tm, wp), lambda j, i: (i, 0)),
                  pl.BlockSpec((tm, wd), lambda j, i: (i, 0)),
                  pl.BlockSpec((tm, wa), lambda j, i: (i, 0)),
                  pl.BlockSpec((k, tn), lambda j, i: (0, j)),
                  pl.BlockSpec((tm, tn), lambda j, i: (i, j))],
        out_specs=pl.BlockSpec((tm, tn), lambda j, i: (i, j)),
        out_shape=jax.ShapeDtypeStruct((m, d), F32),
        compiler_params=_cparams("parallel", "parallel"), name="out_proj")(yp, yd, ya, w, h)


def _pool_windows(ext_ref, u, t, pos, w_ref, s_ref, gw):
    outs = []
    for gi, w in enumerate(POOL_WINDOWS):
        sl = slice(gi * gw, (gi + 1) * gw)
        win = u[:, sl]
        for k in range(1, w):
            win = win + ext_ref[16 - k:16 - k + t, sl]
        cnt = jnp.clip(pos + 1, 1, w).astype(F32)
        d = win / cnt - u[:, sl]
        outs.append(_bdot(d, w_ref[gi]))
    return jnp.concatenate(outs, axis=-1) * s_ref[...]


def _pool_prompt_body(u_ref, w_ref, s_ref, y_ref, ext_ref, *, gw):
    n = pl.program_id(1)

    @pl.when(n == 0)
    def _():
        ext_ref[0:16, :] = jnp.zeros((16, ext_ref.shape[1]), F32)

    u = u_ref[...]
    ext_ref[16:16 + BLK, :] = u
    row = n * BLK + lax.broadcasted_iota(jnp.int32, (BLK, 1), 0)
    pos = row - FRONT
    y = _pool_windows(ext_ref, u, BLK, pos, w_ref, s_ref, gw)
    y_ref[...] = jnp.where(pos >= 0, y, 0.0)
    ext_ref[0:16, :] = u[BLK - 16:, :]


def _pool_prompt(p, w_pool, s_pool, nseq, nblk):
    pw = w_pool.shape[0] * w_pool.shape[1]
    gw = w_pool.shape[1]
    return pl.pallas_call(
        functools.partial(_pool_prompt_body, gw=gw), grid=(nseq, nblk),
        in_specs=[pl.BlockSpec((BLK, pw), lambda b, n: (b * nblk + n, 0)),
                  pl.BlockSpec(w_pool.shape, lambda b, n: (0, 0, 0)),
                  pl.BlockSpec((1, pw), lambda b, n: (0, 0))],
        out_specs=pl.BlockSpec((BLK, pw), lambda b, n: (b * nblk + n, 0)),
        out_shape=jax.ShapeDtypeStruct((p.shape[0], pw), F32),
        scratch_shapes=[pltpu.VMEM((16 + BLK, pw), F32)],
        compiler_params=_cparams("parallel", "arbitrary"), name="pool_prompt")(p, w_pool, s_pool)


def _pool_sample_body(u_ref, hist_ref, w_ref, s_ref, ybuf_ref, y_ref, ext_ref, *, gw, pos0):
    del ybuf_ref
    u = u_ref[...]
    ext_ref[0:1, :] = jnp.zeros((1, ext_ref.shape[1]), F32)
    ext_ref[1:16, :] = hist_ref[0]
    ext_ref[16:16 + DEC_T, :] = u
    pos = pos0 + lax.broadcasted_iota(jnp.int32, (DEC_T, 1), 0)
    y_ref[...] = _pool_windows(ext_ref, u, DEC_T, pos, w_ref, s_ref, gw)


def _sample_row_block(s, rows_per_seq):
    return (s // SLOTS) * (rows_per_seq // DEC_T) + s % SLOTS


def _pool_sample(p, hist, w_pool, s_pool, ybuf, rows_per_seq, pos0):
    nb = hist.shape[0]
    pw = hist.shape[2]
    gw = w_pool.shape[1]
    rowmap = lambda s: (_sample_row_block(s, rows_per_seq), 0)
    return pl.pallas_call(
        functools.partial(_pool_sample_body, gw=gw, pos0=pos0), grid=(nb,),
        in_specs=[pl.BlockSpec((DEC_T, pw), rowmap),
                  pl.BlockSpec((1, POOL_HIST, pw), lambda s: (s, 0, 0)),
                  pl.BlockSpec(w_pool.shape, lambda s: (0, 0, 0)),
                  pl.BlockSpec((1, pw), lambda s: (0, 0)),
                  pl.BlockSpec(memory_space=pl.ANY)],
        out_specs=pl.BlockSpec((DEC_T, pw), rowmap),
        out_shape=jax.ShapeDtypeStruct(ybuf.shape, F32),
        scratch_shapes=[pltpu.VMEM((16 + DEC_T, pw), F32)],
        input_output_aliases={4: 0},
        compiler_params=_cparams("arbitrary"), name="pool_sample")(p, hist, w_pool, s_pool, ybuf)


def _cumsum_lanes(x):
    lane = lax.broadcasted_iota(jnp.int32, x.shape, 1)
    s = 1
    while s < x.shape[1]:
        x = x + jnp.where(lane >= s, pltpu.roll(x, s, axis=1), 0.0)
        s *= 2
    return x


HEAD_PACK = 2


def _unit_lower_inverse(mats, ii, jj):
    eye = jnp.where(ii == jj, 1.0, 0.0).astype(F32)
    pair = ((ii // 2) == (jj // 2)) & (ii % 2 == 1) & (jj % 2 == 0)
    xs = [eye - jnp.where(pair, a, 0.0) for a in mats]
    s = 2
    while s < BLK:
        mask = ((ii // (2 * s)) == (jj // (2 * s))) & ((ii // s) % 2 == 1) & ((jj // s) % 2 == 0)
        ts = [_bdot(jnp.where(mask, a, 0.0), x) for a, x in zip(mats, xs)]
        xs = [x - _bdot(x, t) for x, t in zip(xs, ts)]
        s *= 2
    return xs


def _delta_scalars(ba, valid, alog_ref, dtb_ref, nh):
    bat = ba.T
    beta = jnp.where(valid, 1.0 / (1.0 + jnp.exp(-bat[0:nh])), 0.0)
    z = bat[nh:2 * nh] + dtb_ref[...]
    softplus = jnp.maximum(z, 0.0) + jnp.log(1.0 + jnp.exp(-jnp.abs(z)))
    g = jnp.where(valid, -jnp.exp(alog_ref[...]) * softplus, 0.0)
    gc = _cumsum_lanes(g)
    glast = jnp.broadcast_to(gc[:, BLK - 1:BLK], gc.shape)
    eg = jnp.exp(gc)
    rows = jnp.concatenate(
        [gc, eg, beta, beta * eg, jnp.exp(glast - gc), jnp.exp(glast),
         jnp.zeros((BLK - 6 * nh, BLK), F32)], axis=0)
    return gc, rows.T


def _l2n(x):
    return x * lax.rsqrt(jnp.sum(x * x, axis=-1, keepdims=True) + EPS)


def _delta_chunk_small(xq, xk, xv, ba, alog_ref, dtb_ref, s_ref, nh, dk):
    t = xq.shape[0]
    zrows = jnp.zeros((BLK - t, BLK), F32)
    lane = lax.broadcasted_iota(jnp.int32, (nh, BLK), 1)
    gc, cols = _delta_scalars(jnp.concatenate([ba, zrows], axis=0), lane < t, alog_ref, dtb_ref, nh)
    ii = lax.broadcasted_iota(jnp.int32, (t, BLK), 0)
    jj = lax.broadcasted_iota(jnp.int32, (t, BLK), 1)
    incl = ii >= jj
    strict = ii > jj
    each = lambda f: [f(h) for h in range(nh)]
    col = lambda h, qi: cols[0:t, qi * nh + h:qi * nh + h + 1]
    pad = lambda x: jnp.concatenate([x, zrows], axis=0)
    q = each(lambda h: _l2n(xq[:, h * dk:(h + 1) * dk]) * (dk ** -0.5))
    k = each(lambda h: _l2n(xk[:, h * dk:(h + 1) * dk]))
    s = each(lambda h: s_ref[h])
    kpad = each(lambda h: pad(k[h]))
    decay = each(lambda h: jnp.where(incl, jnp.exp(jnp.where(incl, col(h, 0) - gc[h:h + 1, :], 0.0)), 0.0))
    a_mat = each(lambda h: jnp.where(strict, _bdot_nt(k[h], kpad[h]) * decay[h] * col(h, 2), 0.0))
    qk = each(lambda h: _bdot_nt(q[h], kpad[h]) * decay[h])
    wu = each(lambda h: jnp.concatenate([col(h, 3) * k[h], col(h, 2) * xv[:, h * dk:(h + 1) * dk]], axis=-1))
    for j in range(t - 1):
        wu = each(lambda h: wu[h] - a_mat[h][:, j:j + 1] * wu[h][j:j + 1, :])
    v_new = each(lambda h: wu[h][:, dk:] - _bdot(wu[h][:, :dk], s[h]))
    o = each(lambda h: col(h, 1) * _bdot(q[h], s[h]))
    for j in range(t):
        o = each(lambda h: o[h] + qk[h][:, j:j + 1] * v_new[h][j:j + 1, :])
    kd = each(lambda h: pad(k[h] * col(h, 4)).T)
    s_new = each(lambda h: cols[:, 5 * nh + h:5 * nh + h + 1] * s[h] + _bdot(kd[h], pad(v_new[h])))
    for h in range(nh):
        s_ref[h] = s_new[h]
    return o


def _delta_chunk(xq, xk, xv, ba, valid, alog_ref, dtb_ref, s_ref, nh, dk):
    assert dk == BLK and nh % HEAD_PACK == 0
    gc, cols = _delta_scalars(ba, valid, alog_ref, dtb_ref, nh)
    n = HEAD_PACK * BLK
    ii = lax.broadcasted_iota(jnp.int32, (n, n), 0)
    jj = lax.broadcasted_iota(jnp.int32, (n, n), 1)
    same = (ii // BLK) == (jj // BLK)
    incl = same & (ii >= jj)
    strict = same & (ii > jj)
    packs = range(nh // HEAD_PACK)
    each = lambda f: [f(p) for p in packs]
    stack = lambda p, f: jnp.concatenate([f(h) for h in range(p * HEAD_PACK, (p + 1) * HEAD_PACK)], axis=0)
    col = lambda p, qi: stack(p, lambda h: cols[:, qi * nh + h:qi * nh + h + 1])
    diag = lambda m: jnp.where(same, jnp.concatenate([m] * HEAD_PACK, axis=1), 0.0)
    q = each(lambda p: stack(p, lambda h: _l2n(xq[:, h * dk:(h + 1) * dk]) * (dk ** -0.5)))
    k = each(lambda p: stack(p, lambda h: _l2n(xk[:, h * dk:(h + 1) * dk])))
    v = each(lambda p: stack(p, lambda h: xv[:, h * dk:(h + 1) * dk]))
    s = each(lambda p: stack(p, lambda h: s_ref[h]))
    diff = each(lambda p: col(p, 0) - jnp.concatenate(
        [gc[h:h + 1, :] for h in range(p * HEAD_PACK, (p + 1) * HEAD_PACK)], axis=1))
    decay = each(lambda p: jnp.where(incl, jnp.exp(jnp.where(incl, diff[p], 0.0)), 0.0))
    a_mat = each(lambda p: jnp.where(strict, _bdot_nt(k[p], k[p]) * decay[p] * col(p, 2), 0.0))
    qk = each(lambda p: _bdot_nt(q[p], k[p]) * decay[p])
    x = _unit_lower_inverse(a_mat, ii, jj)
    rhs = each(lambda p: jnp.concatenate([col(p, 3) * k[p], col(p, 2) * v[p]], axis=-1))
    wu = each(lambda p: _bdot(x[p], rhs[p]))
    v_new = each(lambda p: wu[p][:, dk:] - _bdot(diag(wu[p][:, :dk]), s[p]))
    qs = each(lambda p: _bdot(diag(q[p]), s[p]))
    o = each(lambda p: col(p, 1) * qs[p] + _bdot(qk[p], v_new[p]))
    kd = each(lambda p: jnp.where(same, jnp.concatenate([(k[p] * col(p, 4)).T] * HEAD_PACK, axis=0), 0.0))
    s_new = each(lambda p: col(p, 5) * s[p] + _bdot(kd[p], v_new[p]))
    outs = []
    for p in packs:
        for i in range(HEAD_PACK):
            s_ref[p * HEAD_PACK + i] = s_new[p][i * BLK:(i + 1) * BLK, :]
            outs.append(o[p][i * BLK:(i + 1) * BLK, :])
    return outs


def _delta_out(o, gate, ng_ref):
    return o * lax.rsqrt(jnp.mean(o * o, axis=-1, keepdims=True) + EPS) * ng_ref[...] * _silu(gate)


def _conv_silu(ext_ref, w_ref, t):
    acc = ext_ref[5:5 + t, :] * w_ref[0:1, :]
    for i in range(1, DN_CONV):
        acc = acc + ext_ref[5 + i:5 + i + t, :] * w_ref[i:i + 1, :]
    return _silu(acc)


def _delta_prompt_body(q_ref, k_ref, v_ref, gt_ref, ba_ref, wq_ref, wk_ref, wv_ref, alog_ref, dtb_ref, ng_ref,
                       y_ref, sout_ref, eq_ref, ek_ref, ev_ref, s_ref, *, nh, dk, nblk):
    c = pl.program_id(1)

    @pl.when(c == 0)
    def _():
        for e in (eq_ref, ek_ref, ev_ref):
            e[0:8, :] = jnp.zeros((8, e.shape[1]), F32)
        s_ref[...] = jnp.zeros(s_ref.shape, F32)

    xs = []
    for x_ref, e_ref, w_ref in ((q_ref, eq_ref, wq_ref), (k_ref, ek_ref, wk_ref), (v_ref, ev_ref, wv_ref)):
        e_ref[8:8 + BLK, :] = x_ref[...]
        xs.append(_conv_silu(e_ref, w_ref, BLK))
        e_ref[0:8, :] = x_ref[BLK - 8:, :]
    lane = lax.broadcasted_iota(jnp.int32, (nh, BLK), 1)
    valid = (c > 0) | (lane >= FRONT)
    outs = _delta_chunk(xs[0], xs[1], xs[2], ba_ref[...], valid, alog_ref, dtb_ref, s_ref, nh, dk)
    row = c * BLK + lax.broadcasted_iota(jnp.int32, (BLK, 1), 0)
    for h in range(nh):
        sl = slice(h * dk, (h + 1) * dk)
        y_ref[:, sl] = jnp.where(row >= FRONT, _delta_out(outs[h], gt_ref[:, sl], ng_ref), 0.0)

    @pl.when(c == nblk - 1)
    def _():
        sout_ref[0] = s_ref[...]


def _delta_prompt(p, ba, w_conv, a_log, dt_bias, norm_g, nseq, nblk, col_q, col_gate, nh, dk):
    hw = nh * dk
    cq, ck, cv, cg = col_q // hw, col_q // hw + 1, col_q // hw + 2, col_gate // hw
    rows = lambda cb: pl.BlockSpec((BLK, hw), lambda b, c: (b * nblk + c, cb))
    wcs = lambda cb: pl.BlockSpec((DN_CONV, hw), lambda b, c: (0, cb))
    small = lambda shape: pl.BlockSpec(shape, lambda b, c: (0, 0))
    return pl.pallas_call(
        functools.partial(_delta_prompt_body, nh=nh, dk=dk, nblk=nblk), grid=(nseq, nblk),
        in_specs=[rows(cq), rows(ck), rows(cv), rows(cg),
                  pl.BlockSpec((BLK, BLK), lambda b, c: (b * nblk + c, 0)),
                  wcs(0), wcs(1), wcs(2), small((nh, 1)), small((nh, 1)), small((1, dk))],
        out_specs=[pl.BlockSpec((BLK, hw), lambda b, c: (b * nblk + c, 0)),
                   pl.BlockSpec((1, nh, dk, dk), lambda b, c: (b, 0, 0, 0))],
        out_shape=[jax.ShapeDtypeStruct((p.shape[0], hw), F32),
                   jax.ShapeDtypeStruct((nseq, nh, dk, dk), F32)],
        scratch_shapes=[pltpu.VMEM((8 + BLK, hw), F32)] * 3 + [pltpu.VMEM((nh, dk, dk), F32)],
        compiler_params=_cparams("parallel", "arbitrary"), name="delta_prompt",
    )(p, p, p, p, ba, w_conv, w_conv, w_conv, a_log.reshape(nh, 1), dt_bias.reshape(nh, 1),
      norm_g.reshape(1, dk))


def _delta_sample_body(q_ref, k_ref, v_ref, gt_ref, ba_ref, hq_ref, hk_ref, hv_ref, s0_ref,
                       wq_ref, wk_ref, wv_ref, alog_ref, dtb_ref, ng_ref, ybuf_ref,
                       y_ref, sout_ref, eq_ref, ek_ref, ev_ref, s_ref, *, nh, dk):
    del ybuf_ref
    xs = []
    for x_ref, h_ref, e_ref, w_ref in ((q_ref, hq_ref, eq_ref, wq_ref), (k_ref, hk_ref, ek_ref, wk_ref),
                                       (v_ref, hv_ref, ev_ref, wv_ref)):
        e_ref[5:8, :] = h_ref[0]
        e_ref[8:8 + DEC_T, :] = x_ref[...]
        xs.append(_conv_silu(e_ref, w_ref, DEC_T))
    s_ref[...] = s0_ref[0]
    outs = _delta_chunk_small(xs[0], xs[1], xs[2], ba_ref[...], alog_ref, dtb_ref, s_ref, nh, dk)
    for h in range(nh):
        sl = slice(h * dk, (h + 1) * dk)
        y_ref[:, sl] = _delta_out(outs[h], gt_ref[:, sl], ng_ref)
    sout_ref[0] = s_ref[...]


def _delta_sample(p, ba, conv_hist, s0, w_conv, a_log, dt_bias, norm_g, ybuf, rows_per_seq,
                  col_q, col_gate, nh, dk):
    nb = s0.shape[0]
    hw = nh * dk
    cq, ck, cv, cg = col_q // hw, col_q // hw + 1, col_q // hw + 2, col_gate // hw
    rowmap = lambda cb: (lambda s: (_sample_row_block(s, rows_per_seq), cb))
    rows = lambda cb: pl.BlockSpec((DEC_T, hw), rowmap(cb))
    hist = lambda cb: pl.BlockSpec((1, DN_CONV - 1, hw), lambda s: (s, 0, cb))
    wcs = lambda cb: pl.BlockSpec((DN_CONV, hw), lambda s: (0, cb))
    small = lambda shape: pl.BlockSpec(shape, lambda s: (0, 0))
    return pl.pallas_call(
        functools.partial(_delta_sample_body, nh=nh, dk=dk), grid=(nb,),
        in_specs=[rows(cq), rows(ck), rows(cv), rows(cg), pl.BlockSpec((DEC_T, BLK), rowmap(0)),
                  hist(0), hist(1), hist(2),
                  pl.BlockSpec((1, nh, dk, dk), lambda s: (s, 0, 0, 0)),
                  wcs(0), wcs(1), wcs(2), small((nh, 1)), small((nh, 1)), small((1, dk)),
                  pl.BlockSpec(memory_space=pl.ANY)],
        out_specs=[pl.BlockSpec((DEC_T, hw), rowmap(0)),
                   pl.BlockSpec((1, nh, dk, dk), lambda s: (s, 0, 0, 0))],
        out_shape=[jax.ShapeDtypeStruct(ybuf.shape, F32), jax.ShapeDtypeStruct(s0.shape, F32)],
        scratch_shapes=[pltpu.VMEM((8 + DEC_T, hw), F32)] * 3 + [pltpu.VMEM((nh, dk, dk), F32)],
        input_output_aliases={15: 0},
        compiler_params=_cparams("arbitrary"), name="delta_sample",
    )(p, p, p, p, ba, conv_hist, conv_hist, conv_hist, s0, w_conv, w_conv, w_conv,
      a_log.reshape(nh, 1), dt_bias.reshape(nh, 1), norm_g.reshape(1, dk), ybuf)


def _attn_prompt_body(*refs, nkv, grp, hd, window):
    q_refs = refs[:nkv]
    kp_ref, kc_ref, vp_ref, vc_ref, slope_ref, sink_ref, y_ref = refs[nkv:]
    n = pl.program_id(1)
    i = lax.broadcasted_iota(jnp.int32, (BLK, 2 * BLK), 0)
    j = lax.broadcasted_iota(jnp.int32, (BLK, 2 * BLK), 1)
    dist = BLK + i - j
    krow = (n - 1) * BLK + j
    valid = (dist >= 0) & (dist < window) & (krow >= FRONT)
    distf = dist.astype(F32)
    kk = jnp.concatenate([kp_ref[...], kc_ref[...]], axis=0).astype(BF16)
    vv = jnp.concatenate([vp_ref[...], vc_ref[...]], axis=0).astype(BF16)
    heads = [(kv, g) for kv in range(nkv) for g in range(grp)]
    each = lambda f: [f(t, kv, g) for t, (kv, g) in enumerate(heads)]
    s = each(lambda t, kv, g: _bdot_nt(q_refs[kv][:, g * hd:(g + 1) * hd], kk[:, kv * hd:(kv + 1) * hd]))
    s = each(lambda t, kv, g: jnp.where(valid, s[t] * (hd ** -0.5) - slope_ref[t] * distf, NEG_INF))
    m = each(lambda t, kv, g: jnp.maximum(jnp.max(s[t], axis=-1, keepdims=True), sink_ref[t]))
    pr = each(lambda t, kv, g: jnp.exp(s[t] - m[t]))
    den = each(lambda t, kv, g: jnp.sum(pr[t], axis=-1, keepdims=True) + jnp.exp(sink_ref[t] - m[t]))
    pv = each(lambda t, kv, g: _bdot(pr[t], vv[:, kv * hd:(kv + 1) * hd]))
    for t in range(len(heads)):
        y_ref[:, t * hd:(t + 1) * hd] = pv[t] / den[t]


def _attn_prompt(p, slopes, sinks, nseq, nblk, col_q, col_k, col_v, nkv, grp, hd, window):
    gw, kw = grp * hd, nkv * hd
    prev = lambda cb: (lambda b, n: (b * nblk + jnp.maximum(n - 1, 0), cb))
    cur = lambda cb: (lambda b, n: (b * nblk + n, cb))
    smem = pl.BlockSpec(memory_space=pltpu.SMEM)
    return pl.pallas_call(
        functools.partial(_attn_prompt_body, nkv=nkv, grp=grp, hd=hd, window=window), grid=(nseq, nblk),
        in_specs=[pl.BlockSpec((BLK, gw), cur(col_q // gw + kv)) for kv in range(nkv)]
        + [pl.BlockSpec((BLK, kw), prev(col_k // kw)), pl.BlockSpec((BLK, kw), cur(col_k // kw)),
           pl.BlockSpec((BLK, kw), prev(col_v // kw)), pl.BlockSpec((BLK, kw), cur(col_v // kw)),
           smem, smem],
        out_specs=pl.BlockSpec((BLK, nkv * gw), cur(0)),
        out_shape=jax.ShapeDtypeStruct((p.shape[0], nkv * gw), F32),
        compiler_params=_cparams("parallel", "arbitrary"), name="attn_prompt",
    )(*([p] * (nkv + 4)), slopes, sinks)


def _attn_sample_body(*refs, nkv, grp, hd, window):
    q_refs = refs[:nkv]
    k_ref, v_ref, ck_ref, cv_ref, slope_ref, sink_ref, ybuf_ref, y_ref = refs[nkv:]
    del ybuf_ref
    wc = ck_ref.shape[1]
    i = lax.broadcasted_iota(jnp.int32, (DEC_T, wc), 0)
    j = lax.broadcasted_iota(jnp.int32, (DEC_T, wc), 1)
    dist_c = wc + i - j
    valid_c = (dist_c >= 0) & (dist_c < window)
    i2 = lax.broadcasted_iota(jnp.int32, (DEC_T, DEC_T), 0)
    j2 = lax.broadcasted_iota(jnp.int32, (DEC_T, DEC_T), 1)
    dist_n = i2 - j2
    valid_n = (dist_n >= 0) & (dist_n < window)
    tile = lambda a: jnp.concatenate([a] * grp, axis=0)
    valid_c, valid_n = tile(valid_c), tile(valid_n)
    dist_c, dist_n = tile(dist_c).astype(F32), tile(dist_n).astype(F32)
    gi = lax.broadcasted_iota(jnp.int32, (grp * DEC_T, 1), 0) // DEC_T

    def per_row(ref, kv):
        out = jnp.full((grp * DEC_T, 1), ref[kv * grp], F32)
        for g in range(1, grp):
            out = jnp.where(gi == g, ref[kv * grp + g], out)
        return out

    each = lambda f: [f(kv) for kv in range(nkv)]
    cols = lambda ref, kv: ref[:, kv * hd:(kv + 1) * hd]
    q = each(lambda kv: jnp.concatenate([q_refs[kv][:, g * hd:(g + 1) * hd] for g in range(grp)], axis=0))
    slope = each(lambda kv: per_row(slope_ref, kv))
    sink = each(lambda kv: per_row(sink_ref, kv))
    sc = each(lambda kv: _bdot_nt(q[kv], ck_ref[0, :, kv * hd:(kv + 1) * hd]) * (hd ** -0.5))
    sn = each(lambda kv: _bdot_nt(q[kv], cols(k_ref, kv)) * (hd ** -0.5))
    sc = each(lambda kv: jnp.where(valid_c, sc[kv] - slope[kv] * dist_c, NEG_INF))
    sn = each(lambda kv: jnp.where(valid_n, sn[kv] - slope[kv] * dist_n, NEG_INF))
    m = each(lambda kv: jnp.maximum(jnp.maximum(jnp.max(sc[kv], axis=-1, keepdims=True),
                                                jnp.max(sn[kv], axis=-1, keepdims=True)), sink[kv]))
    pc = each(lambda kv: jnp.exp(sc[kv] - m[kv]))
    pn = each(lambda kv: jnp.exp(sn[kv] - m[kv]))
    den = each(lambda kv: jnp.sum(pc[kv], axis=-1, keepdims=True) + jnp.sum(pn[kv], axis=-1, keepdims=True)
               + jnp.exp(sink[kv] - m[kv]))
    o = each(lambda kv: (_bdot(pc[kv], cv_ref[0, :, kv * hd:(kv + 1) * hd]) + _bdot(pn[kv], cols(v_ref, kv)))
             / den[kv])
    for kv in range(nkv):
        for g in range(grp):
            head = kv * grp + g
            y_ref[:, head * hd:(head + 1) * hd] = o[kv][g * DEC_T:(g + 1) * DEC_T, :]


def _attn_sample(p, cache_k, cache_v, slopes, sinks, ybuf, rows_per_seq, col_q, col_k, col_v,
                 nkv, grp, hd, window):
    nb, wc = cache_k.shape[0], cache_k.shape[1]
    gw, kw = grp * hd, nkv * hd
    ck3 = cache_k.reshape(nb, wc, kw)
    cv3 = cache_v.reshape(nb, wc, kw)
    rowmap = lambda cb: (lambda s: (_sample_row_block(s, rows_per_seq), cb))
    smem = pl.BlockSpec(memory_space=pltpu.SMEM)
    return pl.pallas_call(
        functools.partial(_attn_sample_body, nkv=nkv, grp=grp, hd=hd, window=window), grid=(nb,),
        in_specs=[pl.BlockSpec((DEC_T, gw), rowmap(col_q // gw + kv)) for kv in range(nkv)]
        + [pl.BlockSpec((DEC_T, kw), rowmap(col_k // kw)),
           pl.BlockSpec((DEC_T, kw), rowmap(col_v // kw)),
           pl.BlockSpec((1, wc, kw), lambda s: (s, 0, 0)),
           pl.BlockSpec((1, wc, kw), lambda s: (s, 0, 0)),
           smem, smem, pl.BlockSpec(memory_space=pl.ANY)],
        out_specs=pl.BlockSpec((DEC_T, nkv * gw), rowmap(0)),
        out_shape=jax.ShapeDtypeStruct(ybuf.shape, F32),
        input_output_aliases={nkv + 6: 0},
        compiler_params=_cparams("arbitrary"), name="attn_sample",
    )(*([p] * (nkv + 2)), ck3, cv3, slopes, sinks, ybuf)


SUB = 8


def _bitonic_pairs(n, merge_only=False):
    out = []
    k = n if merge_only else 2
    while k <= n:
        j = k // 2
        while j >= 1:
            out += [(i, i ^ j, (i & k) == 0) for i in range(n) if (i ^ j) > i]
            j //= 2
        k *= 2
    return out


def _compare_exchange(v, pairs):
    v = list(v)
    for i, l, desc in pairs:
        hi, lo = jnp.maximum(v[i], v[l]), jnp.minimum(v[i], v[l])
        v[i], v[l] = (hi, lo) if desc else (lo, hi)
    return v


def _top16(rows):
    v = _compare_exchange(rows, _bitonic_pairs(TOPK))
    shift = SUB // 2
    while shift >= 1:
        w = [pltpu.roll(v[TOPK - 1 - r], shift, axis=0) for r in range(TOPK)]
        v = _compare_exchange([jnp.maximum(a, b) for a, b in zip(v, w)], _bitonic_pairs(TOPK, merge_only=True))
        shift //= 2
    return v


def _sublane_sum(x):
    shift = SUB // 2
    while shift >= 1:
        x = x + pltpu.roll(x, shift, axis=0)
        shift //= 2
    return x


def _on_sublanes(vs):
    sub = lax.broadcasted_iota(jnp.int32, vs[0].shape, 0)
    out = vs[SUB - 1]
    for j in range(SUB - 2, -1, -1):
        out = jnp.where(sub == j, vs[j], out)
    return out


def _peer_topk_body(q_ref, keys_ref, rk_ref, cut_ref, e1_ref, e2_ref, *, nh):
    nv = NKEYS // SUB
    for h in range(nh):
        sc = [_bdot_nt(keys_ref[2 * h + half], q_ref[:, (2 * h + half) * NKEYS:(2 * h + half + 1) * NKEYS])
              for half in (0, 1)]
        s1 = [sc[0][SUB * i:SUB * (i + 1), :] for i in range(nv)]
        s2 = [sc[1][SUB * i:SUB * (i + 1), :] for i in range(nv)]
        a = _top16(s1)
        b = _top16(s2)
        b_lo, b_hi, a_hi = _on_sublanes(b[:SUB]), _on_sublanes(b[SUB:]), _on_sublanes(a[SUB:])
        cand = [a[0] + b_lo, a[0] + b_hi, a_hi + b[0]] + [a[i] + b_lo for i in range(1, SUB)]
        cand += [jnp.full(cand[0].shape, LOWEST, F32)] * (TOPK - len(cand))
        top = _top16(cand)
        thr = top[TOPK - 1]
        zsum = jnp.exp(top[0] - top[0])
        for r in range(1, TOPK):
            zsum = zsum + jnp.exp(top[r] - top[0])
        rz = 1.0 / zsum
        height = [_sublane_sum(jnp.where(a[r] + b_lo >= thr, 1.0, 0.0) + jnp.where(a[r] + b_hi >= thr, 1.0, 0.0))
                  for r in range(TOPK)]
        cut, rank2 = [], []
        for i in range(nv):
            c = jnp.zeros(s1[i].shape, F32)
            for r in range(TOPK - 1, -1, -1):
                c = jnp.where(s1[i] == a[r], height[r], c)
            cut.append(c)
            k = jnp.where(b[0] > s2[i], 1.0, 0.0)
            for r in range(1, TOPK):
                k = k + jnp.where(b[r] > s2[i], 1.0, 0.0)
            rank2.append(k)
        rk_ref[h] = jnp.concatenate(rank2, axis=0).astype(BF16)
        cut_ref[h] = jnp.concatenate(cut, axis=0)
        e1_ref[h] = jnp.exp(sc[0] - a[0][0:1, :])
        e2_ref[h] = (jnp.exp(sc[1] - b[0][0:1, :]) * rz[0:1, :]).astype(BF16)


def _peer_topk(q, sub_keys, *, tm=128):
    n = q.shape[0]
    nh = sub_keys.shape[0]
    keys = sub_keys.reshape(2 * nh, NKEYS, sub_keys.shape[-1])
    sspec = pl.BlockSpec((nh, NKEYS, tm), lambda i: (0, 0, i))
    shape = lambda dt: jax.ShapeDtypeStruct((nh, NKEYS, n), dt)
    return pl.pallas_call(
        functools.partial(_peer_topk_body, nh=nh), grid=(n // tm,),
        in_specs=[pl.BlockSpec((tm, q.shape[1]), lambda i: (i, 0)),
                  pl.BlockSpec(keys.shape, lambda i: (0, 0, 0))],
        out_specs=[sspec] * 4,
        out_shape=[shape(BF16), shape(F32), shape(F32), shape(BF16)],
        compiler_params=_cparams("parallel"), name="peer_topk")(q, keys)


def _gelu(x):
    return 0.5 * x * (1.0 + lax.erf(x * (2.0 ** -0.5)))


PACK = 16


def _peer_expert_body(xn_ref, wd_ref, wu_ref, id_ref, iu_ref, rk_ref, cut_ref, e1_ref, e2_ref, y_ref,
                      xt_ref, ix_ref, *, nh, ei, sub):
    c = pl.program_id(1)
    tm = xt_ref.shape[1]
    tw = tm // sub
    slabs = [slice(t * tw, (t + 1) * tw) for t in range(sub)]
    each = lambda f: [f(t, tl) for t, tl in enumerate(slabs)]

    @pl.when(c == 0)
    def _():
        y_ref[...] = jnp.zeros(y_ref.shape, F32)
        x = xn_ref[...].astype(F32)
        sx = _amax_scale(x)
        xt_ref[...] = (x * sx).T.astype(FP8)
        ix_ref[...] = jnp.broadcast_to(1.0 / sx, ix_ref.shape)

    def gates(tl):
        def row16(ref, h, i1):
            return jnp.broadcast_to(ref[h, pl.ds(i1, 1), tl], (PACK, tw)).astype(BF16)
        pieces = []
        for ii in range(ei):
            i1 = c * ei + ii
            cut = [row16(cut_ref, h, i1) for h in range(nh)]
            e1 = [row16(e1_ref, h, i1) for h in range(nh)]
            for r in range(NKEYS // PACK):
                sl = slice(r * PACK, (r + 1) * PACK)
                gate = None
                for h in range(nh):
                    w = jnp.where(rk_ref[h, sl, tl] < cut[h], e1[h] * e2_ref[h, sl, tl], 0.0)
                    gate = w if gate is None else gate + w
                pieces.append(gate)
        return pieces

    inv_h = id_ref[0, 0:1, 0:1] * ix_ref[0:1, 0:1]
    inv_u = iu_ref[0, 0:1, 0:1]
    ht = each(lambda t, tl: jnp.dot(wd_ref[...], xt_ref[:, tl], preferred_element_type=F32) * inv_h)
    gate = each(lambda t, tl: gates(tl))
    act = each(lambda t, tl: _gelu(ht[t]))
    at = each(lambda t, tl: jnp.concatenate(
        [act[t][i * PACK:(i + 1) * PACK, :] * g.astype(F32) for i, g in enumerate(gate[t])], axis=0))
    sa = each(lambda t, tl: _amax_scale(at[t]))
    part = each(lambda t, tl: jnp.dot((at[t] * sa[t]).T.astype(FP8), wu_ref[...], preferred_element_type=F32))
    for t, tl in enumerate(slabs):
        y_ref[tl, :] += part[t] * (inv_u / sa[t])


PEER_TM, PEER_EI, PEER_SUB = 512, 4, 2


def _peer_expert(xn, w_down, w_up, inv_down, inv_up, rank2, cut, e1, e2, *, tm=PEER_TM, ei=PEER_EI, sub=PEER_SUB):
    n, d = xn.shape
    nh = rank2.shape[0]
    e = ei * NKEYS
    once = dict(pipeline_mode=pl.Buffered(1))
    sspec = pl.BlockSpec((nh, NKEYS, tm), lambda i, c: (0, 0, i), **once)
    inv = pl.BlockSpec((1, 8, BLK), lambda i, c: (c, 0, 0))
    return pl.pallas_call(
        functools.partial(_peer_expert_body, nh=nh, ei=ei, sub=sub), grid=(n // tm, w_down.shape[0] // e),
        in_specs=[pl.BlockSpec((tm, d), lambda i, c: (i, 0), **once),
                  pl.BlockSpec((e, d), lambda i, c: (c, 0)),
                  pl.BlockSpec((e, d), lambda i, c: (c, 0)),
                  inv, inv, sspec, sspec, sspec, sspec],
        out_specs=pl.BlockSpec((tm, d), lambda i, c: (i, 0), **once),
        out_shape=jax.ShapeDtypeStruct((n, d), F32),
        scratch_shapes=[pltpu.VMEM((d, tm), FP8), pltpu.VMEM((8, BLK), F32)],
        compiler_params=_cparams("parallel", "arbitrary"), name="peer_expert",
    )(xn, w_down, w_up, inv_down, inv_up, rank2, cut, e1, e2)


def _final_norm(h, y, g, nseq, nblk, row_blk, first_blk, nblk_out):
    d = h.shape[1]
    row = pl.BlockSpec((row_blk, d), lambda b, i: (b * nblk + first_blk + i, 0))
    return pl.pallas_call(
        _final_norm_body, grid=(nseq, nblk_out),
        in_specs=[row, row, pl.BlockSpec((1, d), lambda b, i: (0, 0))],
        out_specs=pl.BlockSpec((1, row_blk, d), lambda b, i: (b, i, 0)),
        out_shape=jax.ShapeDtypeStruct((nseq, nblk_out * row_blk, d), F32),
        compiler_params=_cparams("parallel", "parallel"), name="final_norm",
    )(h, y, g.reshape(1, d).astype(F32))


def _final_norm_body(h_ref, y_ref, g_ref, o_ref):
    x = h_ref[...] + y_ref[...]
    ms = jnp.mean(x * x, axis=-1, keepdims=True)
    o_ref[0] = x * lax.rsqrt(ms + EPS) * g_ref[...]


def kernel(x_prompt, x_sample, state_pool, state_conv, state_delta, cache_k, cache_v, meta_tokens, norm1_g,
           w_in, w_pool, s_pool, w_conv, a_log, dt_bias, dn_norm_g, attn_sinks, w_out, norm2_g,
           peer_w_query, peer_sub_keys, peer_w_down, peer_w_up, final_norm_g):
    nseq, seq, d = x_prompt.shape
    nsamp, dec_t, _ = x_sample.shape
    depth = w_in.shape[0]
    pool_w = w_pool.shape[1] * w_pool.shape[2]
    dn_qkv = w_conv.shape[2]
    nh, dk = state_delta.shape[2], state_delta.shape[3]
    wc, nkv, hd = cache_k.shape[2], cache_k.shape[3], cache_k.shape[4]
    nq = attn_sinks.shape[1]
    grp = nq // nkv
    window = wc
    assert dec_t == DEC_T and nsamp == nseq * SLOTS and (N_META + seq) % BLK == N_META
    assert SLOTS * DEC_T + POOL_HIST <= FRONT and dn_qkv == 3 * nh * dk and wc == BLK
    rows_per_seq = FRONT + N_META + seq
    nblk = rows_per_seq // BLK
    past_len = 16384
    col_qkv = pool_w
    col_gate = col_qkv + dn_qkv
    col_q = col_gate + nh * dk
    col_k = col_q + nq * hd
    col_v = col_k + nkv * hd
    src_ba = pool_w + dn_qkv

    xs = x_sample.reshape(nseq, SLOTS * DEC_T, d)
    zeros = jnp.zeros((nseq, FRONT - SLOTS * DEC_T, d), F32)
    meta = jnp.broadcast_to(meta_tokens[None], (nseq, N_META, d))
    h = jnp.concatenate([xs, zeros, meta, x_prompt], axis=1).reshape(nseq * rows_per_seq, d)

    slopes = jnp.exp2(-8.0 * (jnp.arange(nq, dtype=F32) + 1.0) / nq)
    new_p = [[] for _ in range(5)]
    new_s = [[] for _ in range(5)]
    y_peer = None
    for l in range(depth):
        w_main, w_ba = _cast_w_in(w_in, l, src_ba, 2 * nh)
        h, xn = _addnorm(h, y_peer, norm1_g[l])
        p = _mm(xn, w_main)
        ba = _mm(xn, w_ba)

        y_pool = _pool_prompt(p, w_pool[l], s_pool[l].reshape(1, pool_w), nseq, nblk)
        y_pool = _pool_sample(p, state_pool[l], w_pool[l], s_pool[l].reshape(1, pool_w), y_pool,
                              rows_per_seq, past_len)
        y_dn, s_p = _delta_prompt(p, ba, w_conv[l], a_log[l], dt_bias[l], dn_norm_g[l], nseq, nblk,
                                  col_qkv, col_gate, nh, dk)
        y_dn, s_s = _delta_sample(p, ba, state_conv[l], state_delta[l], w_conv[l], a_log[l], dt_bias[l],
                                  dn_norm_g[l], y_dn, rows_per_seq, col_qkv, col_gate, nh, dk)
        y_att = _attn_prompt(p, slopes, attn_sinks[l], nseq, nblk, col_q, col_k, col_v, nkv, grp, hd, window)
        y_att = _attn_sample(p, cache_k[l], cache_v[l], slopes, attn_sinks[l], y_att, rows_per_seq,
                             col_q, col_k, col_v, nkv, grp, hd, window)
        h = _outproj(y_pool, y_dn, y_att, _cast(w_out, l), h)

        _, xn2 = _addnorm(h, None, norm2_g[l])
        q = _mm(xn2, _cast(peer_w_query, l))
        rank2, cut, e1, e2 = _peer_topk(q, peer_sub_keys[l])
        wd8, inv_d = _quant_fp8(peer_w_down, l, tr=PEER_EI * NKEYS)
        wu8, inv_u = _quant_fp8(peer_w_up, l, tr=PEER_EI * NKEYS)
        y_peer = _peer_expert(xn2, wd8, wu8, inv_d, inv_u, rank2, cut, e1, e2)

        p3 = p.reshape(nseq, rows_per_seq, p.shape[1])
        ps = p3[:, :SLOTS * DEC_T].reshape(nsamp, DEC_T, p.shape[1])
        new_p[0].append(p3[:, -POOL_HIST:, :pool_w])
        new_s[0].append(jnp.concatenate([state_pool[l], ps[:, :, :pool_w]], axis=1)[:, -POOL_HIST:])
        new_p[1].append(p3[:, -(DN_CONV - 1):, col_qkv:col_gate])
        new_s[1].append(jnp.concatenate([state_conv[l], ps[:, :, col_qkv:col_gate]], axis=1)[:, -(DN_CONV - 1):])
        new_p[2].append(s_p)
        new_s[2].append(s_s)
        new_p[3].append(p3[:, -window:, col_k:col_v].reshape(nseq, window, nkv, hd))
        new_s[3].append(jnp.concatenate([cache_k[l], ps[:, :, col_k:col_v].reshape(nsamp, DEC_T, nkv, hd)],
                                        axis=1)[:, -wc:])
        new_p[4].append(p3[:, -window:, col_v:].reshape(nseq, window, nkv, hd))
        new_s[4].append(jnp.concatenate([cache_v[l], ps[:, :, col_v:].reshape(nsamp, DEC_T, nkv, hd)],
                                        axis=1)[:, -wc:])

    y_prompt = _final_norm(h, y_peer, final_norm_g, nseq, nblk, BLK, 1, nblk - 1)
    y_sample = _final_norm(h, y_peer, final_norm_g, nseq, rows_per_seq // (SLOTS * DEC_T), SLOTS * DEC_T, 0, 1)
    y_sample = y_sample.reshape(nsamp, DEC_T, d)
    pool_p, conv_p, delta_p, k_p, v_p = (jnp.stack(a) for a in new_p)
    pool_s, conv_s, delta_s, k_s, v_s = (jnp.stack(a) for a in new_s)
    return (y_prompt, y_sample, pool_p, pool_s, conv_p, conv_s, delta_p, delta_s, k_p, k_s, v_p, v_s)
```

```python
import functools

import jax
import jax.numpy as jnp
from jax import lax
from jax.experimental import pallas as pl
from jax.experimental.pallas import tpu as pltpu

F32 = jnp.float32
BF16 = jnp.bfloat16

EPS = 1e-6
NEG_INF = -1e30
LOWEST = -3.0e38

N_META = 16
BLK = 128
FRONT = BLK - N_META
DEC_T = 8
SLOTS = 8
POOL_WINDOWS = (2, 4, 8, 16)
POOL_HIST = 15
DN_CONV = 4
TOPK = 16
NKEYS = 128
VMEM_LIMIT = 56 * 1024 * 1024


def _cparams(*sem):
    return pltpu.CompilerParams(dimension_semantics=sem, vmem_limit_bytes=VMEM_LIMIT)


def _bdot(a, b):
    return jnp.dot(a.astype(BF16), b.astype(BF16), preferred_element_type=F32)


def _bdot_nt(a, b):
    return lax.dot_general(a.astype(BF16), b.astype(BF16), (((1,), (1,)), ((), ())),
                           preferred_element_type=F32)


def _silu(x):
    return x * (1.0 / (1.0 + jnp.exp(-x)))


def _addnorm_body(*refs, add):
    if add:
        h_ref, y_ref, g_ref, hs_ref, xn_ref = refs
        x = h_ref[...] + y_ref[...]
        hs_ref[...] = x
    else:
        h_ref, g_ref, xn_ref = refs
        x = h_ref[...]
    ms = jnp.mean(x * x, axis=-1, keepdims=True)
    xn_ref[...] = (x * lax.rsqrt(ms + EPS) * g_ref[...]).astype(xn_ref.dtype)


def _addnorm(h, y, g, *, tm=256, out_dtype=BF16):
    n, d = h.shape
    row = pl.BlockSpec((tm, d), lambda i: (i, 0))
    gspec = pl.BlockSpec((1, d), lambda i: (0, 0))
    g2 = g.reshape(1, d).astype(F32)
    if y is None:
        xn = pl.pallas_call(
            functools.partial(_addnorm_body, add=False),
            grid=(n // tm,), in_specs=[row, gspec], out_specs=row,
            out_shape=jax.ShapeDtypeStruct((n, d), out_dtype),
            compiler_params=_cparams("parallel"), name="norm")(h, g2)
        return h, xn
    hs, xn = pl.pallas_call(
        functools.partial(_addnorm_body, add=True),
        grid=(n // tm,), in_specs=[row, row, gspec], out_specs=[row, row],
        out_shape=[jax.ShapeDtypeStruct((n, d), F32), jax.ShapeDtypeStruct((n, d), out_dtype)],
        compiler_params=_cparams("parallel"), name="add_norm")(h, y, g2)
    return hs, xn


def _cast_body(x_ref, o_ref):
    o_ref[...] = x_ref[0].astype(o_ref.dtype)


def _cast(x, l, dtype=BF16, *, tr=512):
    _, r, c = x.shape
    return pl.pallas_call(
        _cast_body, grid=(r // tr,),
        in_specs=[pl.BlockSpec((1, tr, c), lambda i: (l, i, 0))],
        out_specs=pl.BlockSpec((tr, c), lambda i: (i, 0)),
        out_shape=jax.ShapeDtypeStruct((r, c), dtype),
        compiler_params=_cparams("parallel"), name="cast")(x)


FP8 = jnp.float8_e4m3fn
FP8_TARGET = 224.0


def _amax_scale(x):
    a = jnp.max(jnp.max(jnp.abs(x), axis=1, keepdims=True), axis=0, keepdims=True)
    return jnp.where(a > 0.0, FP8_TARGET / a, 1.0)


def _quant_body(x_ref, o_ref, inv_ref):
    x = x_ref[0]
    s = _amax_scale(x)
    o_ref[...] = (x * s).astype(FP8)
    inv_ref[0] = jnp.broadcast_to(1.0 / s, inv_ref.shape[1:])


def _quant_fp8(x, l, *, tr):
    _, r, c = x.shape
    return pl.pallas_call(
        _quant_body, grid=(r // tr,),
        in_specs=[pl.BlockSpec((1, tr, c), lambda i: (l, i, 0))],
        out_specs=[pl.BlockSpec((tr, c), lambda i: (i, 0)), pl.BlockSpec((1, 8, BLK), lambda i: (i, 0, 0))],
        out_shape=[jax.ShapeDtypeStruct((r, c), FP8), jax.ShapeDtypeStruct((r // tr, 8, BLK), F32)],
        compiler_params=_cparams("parallel"), name="quant_fp8")(x)


def _cast_w_in_body(x_ref, main_ref, ba_ref, *, lo, nba):
    x = x_ref[0]
    main_ref[:, :lo] = x[:, :lo].astype(BF16)
    main_ref[:, lo:] = x[:, lo + nba:].astype(BF16)
    ba_ref[...] = jnp.concatenate(
        [x[:, lo:lo + nba], jnp.zeros((x.shape[0], BLK - nba), F32)], axis=1).astype(BF16)


def _cast_w_in(w_in, l, lo, nba, *, tr=256):
    _, r, c = w_in.shape
    return pl.pallas_call(
        functools.partial(_cast_w_in_body, lo=lo, nba=nba), grid=(r // tr,),
        in_specs=[pl.BlockSpec((1, tr, c), lambda i: (l, i, 0))],
        out_specs=[pl.BlockSpec((tr, c - nba), lambda i: (i, 0)), pl.BlockSpec((tr, BLK), lambda i: (i, 0))],
        out_shape=[jax.ShapeDtypeStruct((r, c - nba), BF16), jax.ShapeDtypeStruct((r, BLK), BF16)],
        compiler_params=_cparams("parallel"), name="cast_w_in")(w_in)


def _mm_body(x_ref, w_ref, o_ref):
    o_ref[...] = jnp.dot(x_ref[...], w_ref[...], preferred_element_type=F32)


def _mm(x, w, *, tm=512, tn=1024):
    m, k = x.shape
    n = w.shape[1]
    tn = min(tn, n)
    return pl.pallas_call(
        _mm_body, grid=(n // tn, m // tm),
        in_specs=[pl.BlockSpec((tm, k), lambda j, i: (i, 0)),
                  pl.BlockSpec((k, tn), lambda j, i: (0, j))],
        out_specs=pl.BlockSpec((tm, tn), lambda j, i: (i, j)),
        out_shape=jax.ShapeDtypeStruct((m, n), F32),
        compiler_params=_cparams("parallel", "parallel"), name="matmul")(x, w)


def _outproj_body(yp_ref, yd_ref, ya_ref, w_ref, h_ref, o_ref, *, wp, wd):
    acc = h_ref[...]
    acc += jnp.dot(yp_ref[...].astype(BF16), w_ref[0:wp, :], preferred_element_type=F32)
    acc += jnp.dot(yd_ref[...].astype(BF16), w_ref[wp:wp + wd, :], preferred_element_type=F32)
    acc += jnp.dot(ya_ref[...].astype(BF16), w_ref[wp + wd:, :], preferred_element_type=F32)
    o_ref[...] = acc


def _outproj(yp, yd, ya, w, h, *, tm=512, tn=1024):
    m, d = h.shape
    wp, wd, wa = yp.shape[1], yd.shape[1], ya.shape[1]
    k = wp + wd + wa
    return pl.pallas_call(
        functools.partial(_outproj_body, wp=wp, wd=wd), grid=(d // tn, m // tm),
        in_specs=[pl.BlockSpec((tm, wp), lambda j, i: (i, 0)),
                  pl.BlockSpec((tm, wd), lambda j, i: (i, 0)),
                  pl.BlockSpec((tm, wa), lambda j, i: (i, 0)),
                  pl.BlockSpec((k, tn), lambda j, i: (0, j)),
                  pl.BlockSpec((tm, tn), lambda j, i: (i, j))],
        out_specs=pl.BlockSpec((tm, tn), lambda j, i: (i, j)),
        out_shape=jax.ShapeDtypeStruct((m, d), F32),
        compiler_params=_cparams("parallel", "parallel"), name="out_proj")(yp, yd, ya, w, h)


def _pool_windows(ext_ref, u, t, pos, w_ref, s_ref, gw):
    outs = []
    for gi, w in enumerate(POOL_WINDOWS):
        sl = slice(gi * gw, (gi + 1) * gw)
        win = u[:, sl]
        for k in range(1, w):
            win = win + ext_ref[16 - k:16 - k + t, sl]
        cnt = jnp.clip(pos + 1, 1, w).astype(F32)
        d = win / cnt - u[:, sl]
        outs.append(_bdot(d, w_ref[gi]))
    return jnp.concatenate(outs, axis=-1) * s_ref[...]


def _pool_prompt_body(u_ref, w_ref, s_ref, y_ref, ext_ref, *, gw):
    n = pl.program_id(1)

    @pl.when(n == 0)
    def _():
        ext_ref[0:16, :] = jnp.zeros((16, ext_ref.shape[1]), F32)

    u = u_ref[...]
    ext_ref[16:16 + BLK, :] = u
    row = n * BLK + lax.broadcasted_iota(jnp.int32, (BLK, 1), 0)
    pos = row - FRONT
    y = _pool_windows(ext_ref, u, BLK, pos, w_ref, s_ref, gw)
    y_ref[...] = jnp.where(pos >= 0, y, 0.0)
    ext_ref[0:16, :] = u[BLK - 16:, :]


def _pool_prompt(p, w_pool, s_pool, nseq, nblk):
    pw = w_pool.shape[0] * w_pool.shape[1]
    gw = w_pool.shape[1]
    return pl.pallas_call(
        functools.partial(_pool_prompt_body, gw=gw), grid=(nseq, nblk),
        in_specs=[pl.BlockSpec((BLK, pw), lambda b, n: (b * nblk + n, 0)),
                  pl.BlockSpec(w_pool.shape, lambda b, n: (0, 0, 0)),
                  pl.BlockSpec((1, pw), lambda b, n: (0, 0))],
        out_specs=pl.BlockSpec((BLK, pw), lambda b, n: (b * nblk + n, 0)),
        out_shape=jax.ShapeDtypeStruct((p.shape[0], pw), F32),
        scratch_shapes=[pltpu.VMEM((16 + BLK, pw), F32)],
        compiler_params=_cparams("parallel", "arbitrary"), name="pool_prompt")(p, w_pool, s_pool)


def _pool_sample_body(u_ref, hist_ref, w_ref, s_ref, ybuf_ref, y_ref, ext_ref, *, gw, pos0):
    del ybuf_ref
    u = u_ref[...]
    ext_ref[0:1, :] = jnp.zeros((1, ext_ref.shape[1]), F32)
    ext_ref[1:16, :] = hist_ref[0]
    ext_ref[16:16 + DEC_T, :] = u
    pos = pos0 + lax.broadcasted_iota(jnp.int32, (DEC_T, 1), 0)
    y_ref[...] = _pool_windows(ext_ref, u, DEC_T, pos, w_ref, s_ref, gw)


def _sample_row_block(s, rows_per_seq):
    return (s // SLOTS) * (rows_per_seq // DEC_T) + s % SLOTS


def _pool_sample(p, hist, w_pool, s_pool, ybuf, rows_per_seq, pos0):
    nb = hist.shape[0]
    pw = hist.shape[2]
    gw = w_pool.shape[1]
    rowmap = lambda s: (_sample_row_block(s, rows_per_seq), 0)
    return pl.pallas_call(
        functools.partial(_pool_sample_body, gw=gw, pos0=pos0), grid=(nb,),
        in_specs=[pl.BlockSpec((DEC_T, pw), rowmap),
                  pl.BlockSpec((1, POOL_HIST, pw), lambda s: (s, 0, 0)),
                  pl.BlockSpec(w_pool.shape, lambda s: (0, 0, 0)),
                  pl.BlockSpec((1, pw), lambda s: (0, 0)),
                  pl.BlockSpec(memory_space=pl.ANY)],
        out_specs=pl.BlockSpec((DEC_T, pw), rowmap),
        out_shape=jax.ShapeDtypeStruct(ybuf.shape, F32),
        scratch_shapes=[pltpu.VMEM((16 + DEC_T, pw), F32)],
        input_output_aliases={4: 0},
        compiler_params=_cparams("arbitrary"), name="pool_sample")(p, hist, w_pool, s_pool, ybuf)


def _cumsum_lanes(x):
    lane = lax.broadcasted_iota(jnp.int32, x.shape, 1)
    s = 1
    while s < x.shape[1]:
        x = x + jnp.where(lane >= s, pltpu.roll(x, s, axis=1), 0.0)
        s *= 2
    return x


HEAD_PACK = 2


def _unit_lower_inverse(mats, ii, jj):
    eye = jnp.where(ii == jj, 1.0, 0.0).astype(F32)
    pair = ((ii // 2) == (jj // 2)) & (ii % 2 == 1) & (jj % 2 == 0)
    xs = [eye - jnp.where(pair, a, 0.0) for a in mats]
    s = 2
    while s < BLK:
        mask = ((ii // (2 * s)) == (jj // (2 * s))) & ((ii // s) % 2 == 1) & ((jj // s) % 2 == 0)
        ts = [_bdot(jnp.where(mask, a, 0.0), x) for a, x in zip(mats, xs)]
        xs = [x - _bdot(x, t) for x, t in zip(xs, ts)]
        s *= 2
    return xs


def _delta_scalars(ba, valid, alog_ref, dtb_ref, nh):
    bat = ba.T
    beta = jnp.where(valid, 1.0 / (1.0 + jnp.exp(-bat[0:nh])), 0.0)
    z = bat[nh:2 * nh] + dtb_ref[...]
    softplus = jnp.maximum(z, 0.0) + jnp.log(1.0 + jnp.exp(-jnp.abs(z)))
    g = jnp.where(valid, -jnp.exp(alog_ref[...]) * softplus, 0.0)
    gc = _cumsum_lanes(g)
    glast = jnp.broadcast_to(gc[:, BLK - 1:BLK], gc.shape)
    eg = jnp.exp(gc)
    rows = jnp.concatenate(
        [gc, eg, beta, beta * eg, jnp.exp(glast - gc), jnp.exp(glast),
         jnp.zeros((BLK - 6 * nh, BLK), F32)], axis=0)
    return gc, rows.T


def _l2n(x):
    return x * lax.rsqrt(jnp.sum(x * x, axis=-1, keepdims=True) + EPS)


def _delta_chunk_small(xq, xk, xv, ba, alog_ref, dtb_ref, s_ref, nh, dk):
    t = xq.shape[0]
    zrows = jnp.zeros((BLK - t, BLK), F32)
    lane = lax.broadcasted_iota(jnp.int32, (nh, BLK), 1)
    gc, cols = _delta_scalars(jnp.concatenate([ba, zrows], axis=0), lane < t, alog_ref, dtb_ref, nh)
    ii = lax.broadcasted_iota(jnp.int32, (t, BLK), 0)
    jj = lax.broadcasted_iota(jnp.int32, (t, BLK), 1)
    incl = ii >= jj
    strict = ii > jj
    each = lambda f: [f(h) for h in range(nh)]
    col = lambda h, qi: cols[0:t, qi * nh + h:qi * nh + h + 1]
    pad = lambda x: jnp.concatenate([x, zrows], axis=0)
    q = each(lambda h: _l2n(xq[:, h * dk:(h + 1) * dk]) * (dk ** -0.5))
    k = each(lambda h: _l2n(xk[:, h * dk:(h + 1) * dk]))
    s = each(lambda h: s_ref[h])
    kpad = each(lambda h: pad(k[h]))
    decay = each(lambda h: jnp.where(incl, jnp.exp(jnp.where(incl, col(h, 0) - gc[h:h + 1, :], 0.0)), 0.0))
    a_mat = each(lambda h: jnp.where(strict, _bdot_nt(k[h], kpad[h]) * decay[h] * col(h, 2), 0.0))
    qk = each(lambda h: _bdot_nt(q[h], kpad[h]) * decay[h])
    wu = each(lambda h: jnp.concatenate([col(h, 3) * k[h], col(h, 2) * xv[:, h * dk:(h + 1) * dk]], axis=-1))
    for j in range(t - 1):
        wu = each(lambda h: wu[h] - a_mat[h][:, j:j + 1] * wu[h][j:j + 1, :])
    v_new = each(lambda h: wu[h][:, dk:] - _bdot(wu[h][:, :dk], s[h]))
    o = each(lambda h: col(h, 1) * _bdot(q[h], s[h]))
    for j in range(t):
        o = each(lambda h: o[h] + qk[h][:, j:j + 1] * v_new[h][j:j + 1, :])
    kd = each(lambda h: pad(k[h] * col(h, 4)).T)
    s_new = each(lambda h: cols[:, 5 * nh + h:5 * nh + h + 1] * s[h] + _bdot(kd[h], pad(v_new[h])))
    for h in range(nh):
        s_ref[h] = s_new[h]
    return o


def _delta_chunk(xq, xk, xv, ba, valid, alog_ref, dtb_ref, s_ref, nh, dk):
    assert dk == BLK and nh % HEAD_PACK == 0
    gc, cols = _delta_scalars(ba, valid, alog_ref, dtb_ref, nh)
    n = HEAD_PACK * BLK
    ii = lax.broadcasted_iota(jnp.int32, (n, n), 0)
    jj = lax.broadcasted_iota(jnp.int32, (n, n), 1)
    same = (ii // BLK) == (jj // BLK)
    incl = same & (ii >= jj)
    strict = same & (ii > jj)
    packs = range(nh // HEAD_PACK)
    each = lambda f: [f(p) for p in packs]
    stack = lambda p, f: jnp.concatenate([f(h) for h in range(p * HEAD_PACK, (p + 1) * HEAD_PACK)], axis=0)
    col = lambda p, qi: stack(p, lambda h: cols[:, qi * nh + h:qi * nh + h + 1])
    diag = lambda m: jnp.where(same, jnp.concatenate([m] * HEAD_PACK, axis=1), 0.0)
    q = each(lambda p: stack(p, lambda h: _l2n(xq[:, h * dk:(h + 1) * dk]) * (dk ** -0.5)))
    k = each(lambda p: stack(p, lambda h: _l2n(xk[:, h * dk:(h + 1) * dk])))
    v = each(lambda p: stack(p, lambda h: xv[:, h * dk:(h + 1) * dk]))
    s = each(lambda p: stack(p, lambda h: s_ref[h]))
    diff = each(lambda p: col(p, 0) - jnp.concatenate(
        [gc[h:h + 1, :] for h in range(p * HEAD_PACK, (p + 1) * HEAD_PACK)], axis=1))
    decay = each(lambda p: jnp.where(incl, jnp.exp(jnp.where(incl, diff[p], 0.0)), 0.0))
    a_mat = each(lambda p: jnp.where(strict, _bdot_nt(k[p], k[p]) * decay[p] * col(p, 2), 0.0))
    qk = each(lambda p: _bdot_nt(q[p], k[p]) * decay[p])
    x = _unit_lower_inverse(a_mat, ii, jj)
    rhs = each(lambda p: jnp.concatenate([col(p, 3) * k[p], col(p, 2) * v[p]], axis=-1))
    wu = each(lambda p: _bdot(x[p], rhs[p]))
    v_new = each(lambda p: wu[p][:, dk:] - _bdot(diag(wu[p][:, :dk]), s[p]))
    qs = each(lambda p: _bdot(diag(q[p]), s[p]))
    o = each(lambda p: col(p, 1) * qs[p] + _bdot(qk[p], v_new[p]))
    kd = each(lambda p: jnp.where(same, jnp.concatenate([(k[p] * col(p, 4)).T] * HEAD_PACK, axis=0), 0.0))
    s_new = each(lambda p: col(p, 5) * s[p] + _bdot(kd[p], v_new[p]))
    outs = []
    for p in packs:
        for i in range(HEAD_PACK):
            s_ref[p * HEAD_PACK + i] = s_new[p][i * BLK:(i + 1) * BLK, :]
            outs.append(o[p][i * BLK:(i + 1) * BLK, :])
    return outs


def _delta_out(o, gate, ng_ref):
    return o * lax.rsqrt(jnp.mean(o * o, axis=-1, keepdims=True) + EPS) * ng_ref[...] * _silu(gate)


def _conv_silu(ext_ref, w_ref, t):
    acc = ext_ref[5:5 + t, :] * w_ref[0:1, :]
    for i in range(1, DN_CONV):
        acc = acc + ext_ref[5 + i:5 + i + t, :] * w_ref[i:i + 1, :]
    return _silu(acc)


def _delta_prompt_body(q_ref, k_ref, v_ref, gt_ref, ba_ref, wq_ref, wk_ref, wv_ref, alog_ref, dtb_ref, ng_ref,
                       y_ref, sout_ref, eq_ref, ek_ref, ev_ref, s_ref, *, nh, dk, nblk):
    c = pl.program_id(1)

    @pl.when(c == 0)
    def _():
        for e in (eq_ref, ek_ref, ev_ref):
            e[0:8, :] = jnp.zeros((8, e.shape[1]), F32)
        s_ref[...] = jnp.zeros(s_ref.shape, F32)

    xs = []
    for x_ref, e_ref, w_ref in ((q_ref, eq_ref, wq_ref), (k_ref, ek_ref, wk_ref), (v_ref, ev_ref, wv_ref)):
        e_ref[8:8 + BLK, :] = x_ref[...]
        xs.append(_conv_silu(e_ref, w_ref, BLK))
        e_ref[0:8, :] = x_ref[BLK - 8:, :]
    lane = lax.broadcasted_iota(jnp.int32, (nh, BLK), 1)
    valid = (c > 0) | (lane >= FRONT)
    outs = _delta_chunk(xs[0], xs[1], xs[2], ba_ref[...], valid, alog_ref, dtb_ref, s_ref, nh, dk)
    row = c * BLK + lax.broadcasted_iota(jnp.int32, (BLK, 1), 0)
    for h in range(nh):
        sl = slice(h * dk, (h + 1) * dk)
        y_ref[:, sl] = jnp.where(row >= FRONT, _delta_out(outs[h], gt_ref[:, sl], ng_ref), 0.0)

    @pl.when(c == nblk - 1)
    def _():
        sout_ref[0] = s_ref[...]


def _delta_prompt(p, ba, w_conv, a_log, dt_bias, norm_g, nseq, nblk, col_q, col_gate, nh, dk):
    hw = nh * dk
    cq, ck, cv, cg = col_q // hw, col_q // hw + 1, col_q // hw + 2, col_gate // hw
    rows = lambda cb: pl.BlockSpec((BLK, hw), lambda b, c: (b * nblk + c, cb))
    wcs = lambda cb: pl.BlockSpec((DN_CONV, hw), lambda b, c: (0, cb))
    small = lambda shape: pl.BlockSpec(shape, lambda b, c: (0, 0))
    return pl.pallas_call(
        functools.partial(_delta_prompt_body, nh=nh, dk=dk, nblk=nblk), grid=(nseq, nblk),
        in_specs=[rows(cq), rows(ck), rows(cv), rows(cg),
                  pl.BlockSpec((BLK, BLK), lambda b, c: (b * nblk + c, 0)),
                  wcs(0), wcs(1), wcs(2), small((nh, 1)), small((nh, 1)), small((1, dk))],
        out_specs=[pl.BlockSpec((BLK, hw), lambda b, c: (b * nblk + c, 0)),
                   pl.BlockSpec((1, nh, dk, dk), lambda b, c: (b, 0, 0, 0))],
        out_shape=[jax.ShapeDtypeStruct((p.shape[0], hw), F32),
                   jax.ShapeDtypeStruct((nseq, nh, dk, dk), F32)],
        scratch_shapes=[pltpu.VMEM((8 + BLK, hw), F32)] * 3 + [pltpu.VMEM((nh, dk, dk), F32)],
        compiler_params=_cparams("parallel", "arbitrary"), name="delta_prompt",
    )(p, p, p, p, ba, w_conv, w_conv, w_conv, a_log.reshape(nh, 1), dt_bias.reshape(nh, 1),
      norm_g.reshape(1, dk))


def _delta_sample_body(q_ref, k_ref, v_ref, gt_ref, ba_ref, hq_ref, hk_ref, hv_ref, s0_ref,
                       wq_ref, wk_ref, wv_ref, alog_ref, dtb_ref, ng_ref, ybuf_ref,
                       y_ref, sout_ref, eq_ref, ek_ref, ev_ref, s_ref, *, nh, dk):
    del ybuf_ref
    xs = []
    for x_ref, h_ref, e_ref, w_ref in ((q_ref, hq_ref, eq_ref, wq_ref), (k_ref, hk_ref, ek_ref, wk_ref),
                                       (v_ref, hv_ref, ev_ref, wv_ref)):
        e_ref[5:8, :] = h_ref[0]
        e_ref[8:8 + DEC_T, :] = x_ref[...]
        xs.append(_conv_silu(e_ref, w_ref, DEC_T))
    s_ref[...] = s0_ref[0]
    outs = _delta_chunk_small(xs[0], xs[1], xs[2], ba_ref[...], alog_ref, dtb_ref, s_ref, nh, dk)
    for h in range(nh):
        sl = slice(h * dk, (h + 1) * dk)
        y_ref[:, sl] = _delta_out(outs[h], gt_ref[:, sl], ng_ref)
    sout_ref[0] = s_ref[...]


def _delta_sample(p, ba, conv_hist, s0, w_conv, a_log, dt_bias, norm_g, ybuf, rows_per_seq,
                  col_q, col_gate, nh, dk):
    nb = s0.shape[0]
    hw = nh * dk
    cq, ck, cv, cg = col_q // hw, col_q // hw + 1, col_q // hw + 2, col_gate // hw
    rowmap = lambda cb: (lambda s: (_sample_row_block(s, rows_per_seq), cb))
    rows = lambda cb: pl.BlockSpec((DEC_T, hw), rowmap(cb))
    hist = lambda cb: pl.BlockSpec((1, DN_CONV - 1, hw), lambda s: (s, 0, cb))
    wcs = lambda cb: pl.BlockSpec((DN_CONV, hw), lambda s: (0, cb))
    small = lambda shape: pl.BlockSpec(shape, lambda s: (0, 0))
    return pl.pallas_call(
        functools.partial(_delta_sample_body, nh=nh, dk=dk), grid=(nb,),
        in_specs=[rows(cq), rows(ck), rows(cv), rows(cg), pl.BlockSpec((DEC_T, BLK), rowmap(0)),
                  hist(0), hist(1), hist(2),
                  pl.BlockSpec((1, nh, dk, dk), lambda s: (s, 0, 0, 0)),
                  wcs(0), wcs(1), wcs(2), small((nh, 1)), small((nh, 1)), small((1, dk)),
                  pl.BlockSpec(memory_space=pl.ANY)],
        out_specs=[pl.BlockSpec((DEC_T, hw), rowmap(0)),
                   pl.BlockSpec((1, nh, dk, dk), lambda s: (s, 0, 0, 0))],
        out_shape=[jax.ShapeDtypeStruct(ybuf.shape, F32), jax.ShapeDtypeStruct(s0.shape, F32)],
        scratch_shapes=[pltpu.VMEM((8 + DEC_T, hw), F32)] * 3 + [pltpu.VMEM((nh, dk, dk), F32)],
        input_output_aliases={15: 0},
        compiler_params=_cparams("arbitrary"), name="delta_sample",
    )(p, p, p, p, ba, conv_hist, conv_hist, conv_hist, s0, w_conv, w_conv, w_conv,
      a_log.reshape(nh, 1), dt_bias.reshape(nh, 1), norm_g.reshape(1, dk), ybuf)


def _attn_prompt_body(*refs, nkv, grp, hd, window):
    q_refs = refs[:nkv]
    kp_ref, kc_ref, vp_ref, vc_ref, slope_ref, sink_ref, y_ref = refs[nkv:]
    n = pl.program_id(1)
    i = lax.broadcasted_iota(jnp.int32, (BLK, 2 * BLK), 0)
    j = lax.broadcasted_iota(jnp.int32, (BLK, 2 * BLK), 1)
    dist = BLK + i - j
    krow = (n - 1) * BLK + j
    valid = (dist >= 0) & (dist < window) & (krow >= FRONT)
    distf = dist.astype(F32)
    kk = jnp.concatenate([kp_ref[...], kc_ref[...]], axis=0).astype(BF16)
    vv = jnp.concatenate([vp_ref[...], vc_ref[...]], axis=0).astype(BF16)
    heads = [(kv, g) for kv in range(nkv) for g in range(grp)]
    each = lambda f: [f(t, kv, g) for t, (kv, g) in enumerate(heads)]
    s = each(lambda t, kv, g: _bdot_nt(q_refs[kv][:, g * hd:(g + 1) * hd], kk[:, kv * hd:(kv + 1) * hd]))
    s = each(lambda t, kv, g: jnp.where(valid, s[t] * (hd ** -0.5) - slope_ref[t] * distf, NEG_INF))
    m = each(lambda t, kv, g: jnp.maximum(jnp.max(s[t], axis=-1, keepdims=True), sink_ref[t]))
    pr = each(lambda t, kv, g: jnp.exp(s[t] - m[t]))
    den = each(lambda t, kv, g: jnp.sum(pr[t], axis=-1, keepdims=True) + jnp.exp(sink_ref[t] - m[t]))
    pv = each(lambda t, kv, g: _bdot(pr[t], vv[:, kv * hd:(kv + 1) * hd]))
    for t in range(len(heads)):
        y_ref[:, t * hd:(t + 1) * hd] = pv[t] / den[t]


def _attn_prompt(p, slopes, sinks, nseq, nblk, col_q, col_k, col_v, nkv, grp, hd, window):
    gw, kw = grp * hd, nkv * hd
    prev = lambda cb: (lambda b, n: (b * nblk + jnp.maximum(n - 1, 0), cb))
    cur = lambda cb: (lambda b, n: (b * nblk + n, cb))
    smem = pl.BlockSpec(memory_space=pltpu.SMEM)
    return pl.pallas_call(
        functools.partial(_attn_prompt_body, nkv=nkv, grp=grp, hd=hd, window=window), grid=(nseq, nblk),
        in_specs=[pl.BlockSpec((BLK, gw), cur(col_q // gw + kv)) for kv in range(nkv)]
        + [pl.BlockSpec((BLK, kw), prev(col_k // kw)), pl.BlockSpec((BLK, kw), cur(col_k // kw)),
           pl.BlockSpec((BLK, kw), prev(col_v // kw)), pl.BlockSpec((BLK, kw), cur(col_v // kw)),
           smem, smem],
        out_specs=pl.BlockSpec((BLK, nkv * gw), cur(0)),
        out_shape=jax.ShapeDtypeStruct((p.shape[0], nkv * gw), F32),
        compiler_params=_cparams("parallel", "arbitrary"), name="attn_prompt",
    )(*([p] * (nkv + 4)), slopes, sinks)


def _attn_sample_body(*refs, nkv, grp, hd, window):
    q_refs = refs[:nkv]
    k_ref, v_ref, ck_ref, cv_ref, slope_ref, sink_ref, ybuf_ref, y_ref = refs[nkv:]
    del ybuf_ref
    wc = ck_ref.shape[1]
    i = lax.broadcasted_iota(jnp.int32, (DEC_T, wc), 0)
    j = lax.broadcasted_iota(jnp.int32, (DEC_T, wc), 1)
    dist_c = wc + i - j
    valid_c = (dist_c >= 0) & (dist_c < window)
    i2 = lax.broadcasted_iota(jnp.int32, (DEC_T, DEC_T), 0)
    j2 = lax.broadcasted_iota(jnp.int32, (DEC_T, DEC_T), 1)
    dist_n = i2 - j2
    valid_n = (dist_n >= 0) & (dist_n < window)
    tile = lambda a: jnp.concatenate([a] * grp, axis=0)
    valid_c, valid_n = tile(valid_c), tile(valid_n)
    dist_c, dist_n = tile(dist_c).astype(F32), tile(dist_n).astype(F32)
    gi = lax.broadcasted_iota(jnp.int32, (grp * DEC_T, 1), 0) // DEC_T

    def per_row(ref, kv):
        out = jnp.full((grp * DEC_T, 1), ref[kv * grp], F32)
        for g in range(1, grp):
            out = jnp.where(gi == g, ref[kv * grp + g], out)
        return out

    each = lambda f: [f(kv) for kv in range(nkv)]
    cols = lambda ref, kv: ref[:, kv * hd:(kv + 1) * hd]
    q = each(lambda kv: jnp.concatenate([q_refs[kv][:, g * hd:(g + 1) * hd] for g in range(grp)], axis=0))
    slope = each(lambda kv: per_row(slope_ref, kv))
    sink = each(lambda kv: per_row(sink_ref, kv))
    sc = each(lambda kv: _bdot_nt(q[kv], ck_ref[0, :, kv * hd:(kv + 1) * hd]) * (hd ** -0.5))
    sn = each(lambda kv: _bdot_nt(q[kv], cols(k_ref, kv)) * (hd ** -0.5))
    sc = each(lambda kv: jnp.where(valid_c, sc[kv] - slope[kv] * dist_c, NEG_INF))
    sn = each(lambda kv: jnp.where(valid_n, sn[kv] - slope[kv] * dist_n, NEG_INF))
    m = each(lambda kv: jnp.maximum(jnp.maximum(jnp.max(sc[kv], axis=-1, keepdims=True),
                                                jnp.max(sn[kv], axis=-1, keepdims=True)), sink[kv]))
    pc = each(lambda kv: jnp.exp(sc[kv] - m[kv]))
    pn = each(lambda kv: jnp.exp(sn[kv] - m[kv]))
    den = each(lambda kv: jnp.sum(pc[kv], axis=-1, keepdims=True) + jnp.sum(pn[kv], axis=-1, keepdims=True)
               + jnp.exp(sink[kv] - m[kv]))
    o = each(lambda kv: (_bdot(pc[kv], cv_ref[0, :, kv * hd:(kv + 1) * hd]) + _bdot(pn[kv], cols(v_ref, kv)))
             / den[kv])
    for kv in range(nkv):
        for g in range(grp):
            head = kv * grp + g
            y_ref[:, head * hd:(head + 1) * hd] = o[kv][g * DEC_T:(g + 1) * DEC_T, :]


def _attn_sample(p, cache_k, cache_v, slopes, sinks, ybuf, rows_per_seq, col_q, col_k, col_v,
                 nkv, grp, hd, window):
    nb, wc = cache_k.shape[0], cache_k.shape[1]
    gw, kw = grp * hd, nkv * hd
    ck3 = cache_k.reshape(nb, wc, kw)
    cv3 = cache_v.reshape(nb, wc, kw)
    rowmap = lambda cb: (lambda s: (_sample_row_block(s, rows_per_seq), cb))
    smem = pl.BlockSpec(memory_space=pltpu.SMEM)
    return pl.pallas_call(
        functools.partial(_attn_sample_body, nkv=nkv, grp=grp, hd=hd, window=window), grid=(nb,),
        in_specs=[pl.BlockSpec((DEC_T, gw), rowmap(col_q // gw + kv)) for kv in range(nkv)]
        + [pl.BlockSpec((DEC_T, kw), rowmap(col_k // kw)),
           pl.BlockSpec((DEC_T, kw), rowmap(col_v // kw)),
           pl.BlockSpec((1, wc, kw), lambda s: (s, 0, 0)),
           pl.BlockSpec((1, wc, kw), lambda s: (s, 0, 0)),
           smem, smem, pl.BlockSpec(memory_space=pl.ANY)],
        out_specs=pl.BlockSpec((DEC_T, nkv * gw), rowmap(0)),
        out_shape=jax.ShapeDtypeStruct(ybuf.shape, F32),
        input_output_aliases={nkv + 6: 0},
        compiler_params=_cparams("arbitrary"), name="attn_sample",
    )(*([p] * (nkv + 2)), ck3, cv3, slopes, sinks, ybuf)


SUB = 8


def _bitonic_pairs(n, merge_only=False):
    out = []
    k = n if merge_only else 2
    while k <= n:
        j = k // 2
        while j >= 1:
            out += [(i, i ^ j, (i & k) == 0) for i in range(n) if (i ^ j) > i]
            j //= 2
        k *= 2
    return out


def _compare_exchange(v, pairs):
    v = list(v)
    for i, l, desc in pairs:
        hi, lo = jnp.maximum(v[i], v[l]), jnp.minimum(v[i], v[l])
        v[i], v[l] = (hi, lo) if desc else (lo, hi)
    return v


def _top16(rows):
    v = _compare_exchange(rows, _bitonic_pairs(TOPK))
    shift = SUB // 2
    while shift >= 1:
        w = [pltpu.roll(v[TOPK - 1 - r], shift, axis=0) for r in range(TOPK)]
        v = _compare_exchange([jnp.maximum(a, b) for a, b in zip(v, w)], _bitonic_pairs(TOPK, merge_only=True))
        shift //= 2
    return v


def _sublane_sum(x):
    shift = SUB // 2
    while shift >= 1:
        x = x + pltpu.roll(x, shift, axis=0)
        shift //= 2
    return x


def _on_sublanes(vs):
    sub = lax.broadcasted_iota(jnp.int32, vs[0].shape, 0)
    out = vs[SUB - 1]
    for j in range(SUB - 2, -1, -1):
        out = jnp.where(sub == j, vs[j], out)
    return out


def _peer_topk_body(q_ref, keys_ref, rk_ref, cut_ref, e1_ref, e2_ref, *, nh):
    nv = NKEYS // SUB
    for h in range(nh):
        sc = [_bdot_nt(keys_ref[2 * h + half], q_ref[:, (2 * h + half) * NKEYS:(2 * h + half + 1) * NKEYS])
              for half in (0, 1)]
        s1 = [sc[0][SUB * i:SUB * (i + 1), :] for i in range(nv)]
        s2 = [sc[1][SUB * i:SUB * (i + 1), :] for i in range(nv)]
        a = _top16(s1)
        b = _top16(s2)
        b_lo, b_hi, a_hi = _on_sublanes(b[:SUB]), _on_sublanes(b[SUB:]), _on_sublanes(a[SUB:])
        cand = [a[0] + b_lo, a[0] + b_hi, a_hi + b[0]] + [a[i] + b_lo for i in range(1, SUB)]
        cand += [jnp.full(cand[0].shape, LOWEST, F32)] * (TOPK - len(cand))
        top = _top16(cand)
        thr = top[TOPK - 1]
        zsum = jnp.exp(top[0] - top[0])
        for r in range(1, TOPK):
            zsum = zsum + jnp.exp(top[r] - top[0])
        rz = 1.0 / zsum
        height = [_sublane_sum(jnp.where(a[r] + b_lo >= thr, 1.0, 0.0) + jnp.where(a[r] + b_hi >= thr, 1.0, 0.0))
                  for r in range(TOPK)]
        cut, rank2 = [], []
        for i in range(nv):
            c = jnp.zeros(s1[i].shape, F32)
            for r in range(TOPK - 1, -1, -1):
                c = jnp.where(s1[i] == a[r], height[r], c)
            cut.append(c)
            k = jnp.where(b[0] > s2[i], 1.0, 0.0)
            for r in range(1, TOPK):
                k = k + jnp.where(b[r] > s2[i], 1.0, 0.0)
            rank2.append(k)
        rk_ref[h] = jnp.concatenate(rank2, axis=0).astype(BF16)
        cut_ref[h] = jnp.concatenate(cut, axis=0)
        e1_ref[h] = jnp.exp(sc[0] - a[0][0:1, :])
        e2_ref[h] = (jnp.exp(sc[1] - b[0][0:1, :]) * rz[0:1, :]).astype(BF16)


def _peer_topk(q, sub_keys, *, tm=128):
    n = q.shape[0]
    nh = sub_keys.shape[0]
    keys = sub_keys.reshape(2 * nh, NKEYS, sub_keys.shape[-1])
    sspec = pl.BlockSpec((nh, NKEYS, tm), lambda i: (0, 0, i))
    shape = lambda dt: jax.ShapeDtypeStruct((nh, NKEYS, n), dt)
    return pl.pallas_call(
        functools.partial(_peer_topk_body, nh=nh), grid=(n // tm,),
        in_specs=[pl.BlockSpec((tm, q.shape[1]), lambda i: (i, 0)),
                  pl.BlockSpec(keys.shape, lambda i: (0, 0, 0))],
        out_specs=[sspec] * 4,
        out_shape=[shape(BF16), shape(F32), shape(F32), shape(BF16)],
        compiler_params=_cparams("parallel"), name="peer_topk")(q, keys)


def _gelu(x):
    return 0.5 * x * (1.0 + lax.erf(x * (2.0 ** -0.5)))


PACK = 16


def _peer_expert_body(xn_ref, wd_ref, wu_ref, id_ref, iu_ref, rk_ref, cut_ref, e1_ref, e2_ref, y_ref,
                      xt_ref, ix_ref, *, nh, ei, sub):
    c = pl.program_id(1)
    tm = xt_ref.shape[1]
    tw = tm // sub
    slabs = [slice(t * tw, (t + 1) * tw) for t in range(sub)]
    each = lambda f: [f(t, tl) for t, tl in enumerate(slabs)]

    @pl.when(c == 0)
    def _():
        y_ref[...] = jnp.zeros(y_ref.shape, F32)
        x = xn_ref[...].astype(F32)
        sx = _amax_scale(x)
        xt_ref[...] = (x * sx).T.astype(FP8)
        ix_ref[...] = jnp.broadcast_to(1.0 / sx, ix_ref.shape)

    def gates(tl):
        def row16(ref, h, i1):
            return jnp.broadcast_to(ref[h, pl.ds(i1, 1), tl], (PACK, tw)).astype(BF16)
        pieces = []
        for ii in range(ei):
            i1 = c * ei + ii
            cut = [row16(cut_ref, h, i1) for h in range(nh)]
            e1 = [row16(e1_ref, h, i1) for h in range(nh)]
            for r in range(NKEYS // PACK):
                sl = slice(r * PACK, (r + 1) * PACK)
                gate = pieces[-1] * 0.0 if pieces else None
                for h in range(nh):
                    w = jnp.where(rk_ref[h, sl, tl] < cut[h], e1[h] * e2_ref[h, sl, tl], 0.0)
                    gate = w if gate is None else gate + w
                pieces.append(gate)
        return pieces

    inv_h = id_ref[0, 0:1, 0:1] * ix_ref[0:1, 0:1]
    inv_u = iu_ref[0, 0:1, 0:1]
    ht = each(lambda t, tl: jnp.dot(wd_ref[...], xt_ref[:, tl], preferred_element_type=F32) * inv_h)
    gate = each(lambda t, tl: gates(tl))
    act = each(lambda t, tl: _gelu(ht[t]))
    at = each(lambda t, tl: jnp.concatenate(
        [act[t][i * PACK:(i + 1) * PACK, :] * g.astype(F32) for i, g in enumerate(gate[t])], axis=0))
    sa = each(lambda t, tl: _amax_scale(at[t]))
    part = each(lambda t, tl: jnp.dot((at[t] * sa[t]).T.astype(FP8), wu_ref[...], preferred_element_type=F32))
    for t, tl in enumerate(slabs):
        y_ref[tl, :] += part[t] * (inv_u / sa[t])


PEER_TM, PEER_EI, PEER_SUB = 512, 4, 2


def _peer_expert(xn, w_down, w_up, inv_down, inv_up, rank2, cut, e1, e2, *, tm=PEER_TM, ei=PEER_EI, sub=PEER_SUB):
    n, d = xn.shape
    nh = rank2.shape[0]
    e = ei * NKEYS
    once = dict(pipeline_mode=pl.Buffered(1))
    sspec = pl.BlockSpec((nh, NKEYS, tm), lambda i, c: (0, 0, i), **once)
    inv = pl.BlockSpec((1, 8, BLK), lambda i, c: (c, 0, 0))
    return pl.pallas_call(
        functools.partial(_peer_expert_body, nh=nh, ei=ei, sub=sub), grid=(n // tm, w_down.shape[0] // e),
        in_specs=[pl.BlockSpec((tm, d), lambda i, c: (i, 0), **once),
                  pl.BlockSpec((e, d), lambda i, c: (c, 0)),
                  pl.BlockSpec((e, d), lambda i, c: (c, 0)),
                  inv, inv, sspec, sspec, sspec, sspec],
        out_specs=pl.BlockSpec((tm, d), lambda i, c: (i, 0), **once),
        out_shape=jax.ShapeDtypeStruct((n, d), F32),
        scratch_shapes=[pltpu.VMEM((d, tm), FP8), pltpu.VMEM((8, BLK), F32)],
        compiler_params=_cparams("parallel", "arbitrary"), name="peer_expert",
    )(xn, w_down, w_up, inv_down, inv_up, rank2, cut, e1, e2)


def _final_norm(h, y, g, nseq, nblk, row_blk, first_blk, nblk_out):
    d = h.shape[1]
    row = pl.BlockSpec((row_blk, d), lambda b, i: (b * nblk + first_blk + i, 0))
    return pl.pallas_call(
        _final_norm_body, grid=(nseq, nblk_out),
        in_specs=[row, row, pl.BlockSpec((1, d), lambda b, i: (0, 0))],
        out_specs=pl.BlockSpec((1, row_blk, d), lambda b, i: (b, i, 0)),
        out_shape=jax.ShapeDtypeStruct((nseq, nblk_out * row_blk, d), F32),
        compiler_params=_cparams("parallel", "parallel"), name="final_norm",
    )(h, y, g.reshape(1, d).astype(F32))


def _final_norm_body(h_ref, y_ref, g_ref, o_ref):
    x = h_ref[...] + y_ref[...]
    ms = jnp.mean(x * x, axis=-1, keepdims=True)
    o_ref[0] = x * lax.rsqrt(ms + EPS) * g_ref[...]


def kernel(x_prompt, x_sample, state_pool, state_conv, state_delta, cache_k, cache_v, meta_tokens, norm1_g,
           w_in, w_pool, s_pool, w_conv, a_log, dt_bias, dn_norm_g, attn_sinks, w_out, norm2_g,
           peer_w_query, peer_sub_keys, peer_w_down, peer_w_up, final_norm_g):
    nseq, seq, d = x_prompt.shape
    nsamp, dec_t, _ = x_sample.shape
    depth = w_in.shape[0]
    pool_w = w_pool.shape[1] * w_pool.shape[2]
    dn_qkv = w_conv.shape[2]
    nh, dk = state_delta.shape[2], state_delta.shape[3]
    wc, nkv, hd = cache_k.shape[2], cache_k.shape[3], cache_k.shape[4]
    nq = attn_sinks.shape[1]
    grp = nq // nkv
    window = wc
    assert dec_t == DEC_T and nsamp == nseq * SLOTS and (N_META + seq) % BLK == N_META
    assert SLOTS * DEC_T + POOL_HIST <= FRONT and dn_qkv == 3 * nh * dk and wc == BLK
    rows_per_seq = FRONT + N_META + seq
    nblk = rows_per_seq // BLK
    past_len = 16384
    col_qkv = pool_w
    col_gate = col_qkv + dn_qkv
    col_q = col_gate + nh * dk
    col_k = col_q + nq * hd
    col_v = col_k + nkv * hd
    src_ba = pool_w + dn_qkv

    xs = x_sample.reshape(nseq, SLOTS * DEC_T, d)
    zeros = jnp.zeros((nseq, FRONT - SLOTS * DEC_T, d), F32)
    meta = jnp.broadcast_to(meta_tokens[None], (nseq, N_META, d))
    h = jnp.concatenate([xs, zeros, meta, x_prompt], axis=1).reshape(nseq * rows_per_seq, d)

    slopes = jnp.exp2(-8.0 * (jnp.arange(nq, dtype=F32) + 1.0) / nq)
    new_p = [[] for _ in range(5)]
    new_s = [[] for _ in range(5)]
    y_peer = None
    for l in range(depth):
        w_main, w_ba = _cast_w_in(w_in, l, src_ba, 2 * nh)
        h, xn = _addnorm(h, y_peer, norm1_g[l])
        p = _mm(xn, w_main)
        ba = _mm(xn, w_ba)

        y_pool = _pool_prompt(p, w_pool[l], s_pool[l].reshape(1, pool_w), nseq, nblk)
        y_pool = _pool_sample(p, state_pool[l], w_pool[l], s_pool[l].reshape(1, pool_w), y_pool,
                              rows_per_seq, past_len)
        y_dn, s_p = _delta_prompt(p, ba, w_conv[l], a_log[l], dt_bias[l], dn_norm_g[l], nseq, nblk,
                                  col_qkv, col_gate, nh, dk)
        y_dn, s_s = _delta_sample(p, ba, state_conv[l], state_delta[l], w_conv[l], a_log[l], dt_bias[l],
                                  dn_norm_g[l], y_dn, rows_per_seq, col_qkv, col_gate, nh, dk)
        y_att = _attn_prompt(p, slopes, attn_sinks[l], nseq, nblk, col_q, col_k, col_v, nkv, grp, hd, window)
        y_att = _attn_sample(p, cache_k[l], cache_v[l], slopes, attn_sinks[l], y_att, rows_per_seq,
                             col_q, col_k, col_v, nkv, grp, hd, window)
        h = _outproj(y_pool, y_dn, y_att, _cast(w_out, l), h)

        _, xn2 = _addnorm(h, None, norm2_g[l])
        q = _mm(xn2, _cast(peer_w_query, l))
        rank2, cut, e1, e2 = _peer_topk(q, peer_sub_keys[l])
        wd8, inv_d = _quant_fp8(peer_w_down, l, tr=PEER_EI * NKEYS)
        wu8, inv_u = _quant_fp8(peer_w_up, l, tr=PEER_EI * NKEYS)
        y_peer = _peer_expert(xn2, wd8, wu8, inv_d, inv_u, rank2, cut, e1, e2)

        p3 = p.reshape(nseq, rows_per_seq, p.shape[1])
        ps = p3[:, :SLOTS * DEC_T].reshape(nsamp, DEC_T, p.shape[1])
        new_p[0].append(p3[:, -POOL_HIST:, :pool_w])
        new_s[0].append(jnp.concatenate([state_pool[l], ps[:, :, :pool_w]], axis=1)[:, -POOL_HIST:])
        new_p[1].append(p3[:, -(DN_CONV - 1):, col_qkv:col_gate])
        new_s[1].append(jnp.concatenate([state_conv[l], ps[:, :, col_qkv:col_gate]], axis=1)[:, -(DN_CONV - 1):])
        new_p[2].append(s_p)
        new_s[2].append(s_s)
        new_p[3].append(p3[:, -window:, col_k:col_v].reshape(nseq, window, nkv, hd))
        new_s[3].append(jnp.concatenate([cache_k[l], ps[:, :, col_k:col_v].reshape(nsamp, DEC_T, nkv, hd)],
                                        axis=1)[:, -wc:])
        new_p[4].append(p3[:, -window:, col_v:].reshape(nseq, window, nkv, hd))
        new_s[4].append(jnp.concatenate([cache_v[l], ps[:, :, col_v:].reshape(nsamp, DEC_T, nkv, hd)],
                                        axis=1)[:, -wc:])

    y_prompt = _final_norm(h, y_peer, final_norm_g, nseq, nblk, BLK, 1, nblk - 1)
    y_sample = _final_norm(h, y_peer, final_norm_g, nseq, rows_per_seq // (SLOTS * DEC_T), SLOTS * DEC_T, 0, 1)
    y_sample = y_sample.reshape(nsamp, DEC_T, d)
    pool_p, conv_p, delta_p, k_p, v_p = (jnp.stack(a) for a in new_p)
    pool_s, conv_s, delta_s, k_s, v_s = (jnp.stack(a) for a in new_s)
    return (y_prompt, y_sample, pool_p, pool_s, conv_p, conv_s, delta_p, delta_s, k_p, k_s, v_p, v_s)
```

```python
import functools

import jax
import jax.numpy as jnp
from jax import lax
from jax.experimental import pallas as pl
from jax.experimental.pallas import tpu as pltpu

F32 = jnp.float32
BF16 = jnp.bfloat16

EPS = 1e-6
NEG_INF = -1e30
LOWEST = -3.0e38

N_META = 16
BLK = 128
FRONT = BLK - N_META
DEC_T = 8
SLOTS = 8
POOL_WINDOWS = (2, 4, 8, 16)
POOL_HIST = 15
DN_CONV = 4
TOPK = 16
NKEYS = 128
VMEM_LIMIT = 56 * 1024 * 1024


def _cparams(*sem):
    return pltpu.CompilerParams(dimension_semantics=sem, vmem_limit_bytes=VMEM_LIMIT)


def _bdot(a, b):
    return jnp.dot(a.astype(BF16), b.astype(BF16), preferred_element_type=F32)


def _bdot_nt(a, b):
    return lax.dot_general(a.astype(BF16), b.astype(BF16), (((1,), (1,)), ((), ())),
                           preferred_element_type=F32)


def _silu(x):
    return x * (1.0 / (1.0 + jnp.exp(-x)))


def _addnorm_body(*refs, add):
    if add:
        h_ref, y_ref, g_ref, hs_ref, xn_ref = refs
        x = h_ref[...] + y_ref[...]
        hs_ref[...] = x
    else:
        h_ref, g_ref, xn_ref = refs
        x = h_ref[...]
    ms = jnp.mean(x * x, axis=-1, keepdims=True)
    xn_ref[...] = (x * lax.rsqrt(ms + EPS) * g_ref[...]).astype(xn_ref.dtype)


def _addnorm(h, y, g, *, tm=256, out_dtype=BF16):
    n, d = h.shape
    row = pl.BlockSpec((tm, d), lambda i: (i, 0))
    gspec = pl.BlockSpec((1, d), lambda i: (0, 0))
    g2 = g.reshape(1, d).astype(F32)
    if y is None:
        xn = pl.pallas_call(
            functools.partial(_addnorm_body, add=False),
            grid=(n // tm,), in_specs=[row, gspec], out_specs=row,
            out_shape=jax.ShapeDtypeStruct((n, d), out_dtype),
            compiler_params=_cparams("parallel"), name="norm")(h, g2)
        return h, xn
    hs, xn = pl.pallas_call(
        functools.partial(_addnorm_body, add=True),
        grid=(n // tm,), in_specs=[row, row, gspec], out_specs=[row, row],
        out_shape=[jax.ShapeDtypeStruct((n, d), F32), jax.ShapeDtypeStruct((n, d), out_dtype)],
        compiler_params=_cparams("parallel"), name="add_norm")(h, y, g2)
    return hs, xn


def _cast_body(x_ref, o_ref):
    o_ref[...] = x_ref[0].astype(o_ref.dtype)


def _cast(x, l, dtype=BF16, *, tr=512):
    _, r, c = x.shape
    return pl.pallas_call(
        _cast_body, grid=(r // tr,),
        in_specs=[pl.BlockSpec((1, tr, c), lambda i: (l, i, 0))],
        out_specs=pl.BlockSpec((tr, c), lambda i: (i, 0)),
        out_shape=jax.ShapeDtypeStruct((r, c), dtype),
        compiler_params=_cparams("parallel"), name="cast")(x)


FP8 = jnp.float8_e4m3fn
FP8_TARGET = 224.0


def _amax_scale(x):
    a = jnp.max(jnp.max(jnp.abs(x), axis=1, keepdims=True), axis=0, keepdims=True)
    return jnp.where(a > 0.0, FP8_TARGET / a, 1.0)


def _quant_body(x_ref, o_ref, inv_ref):
    x = x_ref[0]
    s = _amax_scale(x)
    o_ref[...] = (x * s).astype(FP8)
    inv_ref[0] = jnp.broadcast_to(1.0 / s, inv_ref.shape[1:])


def _quant_fp8(x, l, *, tr):
    _, r, c = x.shape
    return pl.pallas_call(
        _quant_body, grid=(r // tr,),
        in_specs=[pl.BlockSpec((1, tr, c), lambda i: (l, i, 0))],
        out_specs=[pl.BlockSpec((tr, c), lambda i: (i, 0)), pl.BlockSpec((1, 8, BLK), lambda i: (i, 0, 0))],
        out_shape=[jax.ShapeDtypeStruct((r, c), FP8), jax.ShapeDtypeStruct((r // tr, 8, BLK), F32)],
        compiler_params=_cparams("parallel"), name="quant_fp8")(x)


def _cast_w_in_body(x_ref, main_ref, ba_ref, *, lo, nba):
    x = x_ref[0]
    main_ref[:, :lo] = x[:, :lo].astype(BF16)
    main_ref[:, lo:] = x[:, lo + nba:].astype(BF16)
    ba_ref[...] = jnp.concatenate(
        [x[:, lo:lo + nba], jnp.zeros((x.shape[0], BLK - nba), F32)], axis=1).astype(BF16)


def _cast_w_in(w_in, l, lo, nba, *, tr=256):
    _, r, c = w_in.shape
    return pl.pallas_call(
        functools.partial(_cast_w_in_body, lo=lo, nba=nba), grid=(r // tr,),
        in_specs=[pl.BlockSpec((1, tr, c), lambda i: (l, i, 0))],
        out_specs=[pl.BlockSpec((tr, c - nba), lambda i: (i, 0)), pl.BlockSpec((tr, BLK), lambda i: (i, 0))],
        out_shape=[jax.ShapeDtypeStruct((r, c - nba), BF16), jax.ShapeDtypeStruct((r, BLK), BF16)],
        compiler_params=_cparams("parallel"), name="cast_w_in")(w_in)


def _mm_body(x_ref, w_ref, o_ref):
    o_ref[...] = jnp.dot(x_ref[...], w_ref[...], preferred_element_type=F32)


def _mm(x, w, *, tm=512, tn=1024):
    m, k = x.shape
    n = w.shape[1]
    tn = min(tn, n)
    return pl.pallas_call(
        _mm_body, grid=(n // tn, m // tm),
        in_specs=[pl.BlockSpec((tm, k), lambda j, i: (i, 0)),
                  pl.BlockSpec((k, tn), lambda j, i: (0, j))],
        out_specs=pl.BlockSpec((tm, tn), lambda j, i: (i, j)),
        out_shape=jax.ShapeDtypeStruct((m, n), F32),
        compiler_params=_cparams("parallel", "parallel"), name="matmul")(x, w)


def _outproj_body(yp_ref, yd_ref, ya_ref, w_ref, h_ref, o_ref, *, wp, wd):
    acc = h_ref[...]
    acc += jnp.dot(yp_ref[...].astype(BF16), w_ref[0:wp, :], preferred_element_type=F32)
    acc += jnp.dot(yd_ref[...].astype(BF16), w_ref[wp:wp + wd, :], preferred_element_type=F32)
    acc += jnp.dot(ya_ref[...].astype(BF16), w_ref[wp + wd:, :], preferred_element_type=F32)
    o_ref[...] = acc


def _outproj(yp, yd, ya, w, h, *, tm=512, tn=1024):
    m, d = h.shape
    wp, wd, wa = yp.shape[1], yd.shape[1], ya.shape[1]
    k = wp + wd + wa
    return pl.pallas_call(
        functools.partial(_outproj_body, wp=wp, wd=wd), grid=(d // tn, m // tm),
        in_specs=[pl.BlockSpec((tm, wp), lambda j, i: (i, 0)),
                  pl.BlockSpec((tm, wd), lambda j, i: (i, 0)),
                  pl.BlockSpec((tm, wa), lambda j, i: (i, 0)),
                  pl.BlockSpec((k, tn), lambda j, i: (0, j)),
                  pl.BlockSpec((tm, tn), lambda j, i: (i, j))],
        out_specs=pl.BlockSpec((tm, tn), lambda j, i: (i, j)),
        out_shape=jax.ShapeDtypeStruct((m, d), F32),
        compiler_params=_cparams("parallel", "parallel"), name="out_proj")(yp, yd, ya, w, h)


def _pool_windows(ext_ref, u, t, pos, w_ref, s_ref, gw):
    outs = []
    for gi, w in enumerate(POOL_WINDOWS):
        sl = slice(gi * gw, (gi + 1) * gw)
        win = u[:, sl]
        for k in range(1, w):
            win = win + ext_ref[16 - k:16 - k + t, sl]
        cnt = jnp.clip(pos + 1, 1, w).astype(F32)
        d = win / cnt - u[:, sl]
        outs.append(_bdot(d, w_ref[gi]))
    return jnp.concatenate(outs, axis=-1) * s_ref[...]


def _pool_prompt_body(u_ref, w_ref, s_ref, y_ref, ext_ref, *, gw):
    n = pl.program_id(1)

    @pl.when(n == 0)
    def _():
        ext_ref[0:16, :] = jnp.zeros((16, ext_ref.shape[1]), F32)

    u = u_ref[...]
    ext_ref[16:16 + BLK, :] = u
    row = n * BLK + lax.broadcasted_iota(jnp.int32, (BLK, 1), 0)
    pos = row - FRONT
    y = _pool_windows(ext_ref, u, BLK, pos, w_ref, s_ref, gw)
    y_ref[...] = jnp.where(pos >= 0, y, 0.0)
    ext_ref[0:16, :] = u[BLK - 16:, :]


def _pool_prompt(p, w_pool, s_pool, nseq, nblk):
    pw = w_pool.shape[0] * w_pool.shape[1]
    gw = w_pool.shape[1]
    return pl.pallas_call(
        functools.partial(_pool_prompt_body, gw=gw), grid=(nseq, nblk),
        in_specs=[pl.BlockSpec((BLK, pw), lambda b, n: (b * nblk + n, 0)),
                  pl.BlockSpec(w_pool.shape, lambda b, n: (0, 0, 0)),
                  pl.BlockSpec((1, pw), lambda b, n: (0, 0))],
        out_specs=pl.BlockSpec((BLK, pw), lambda b, n: (b * nblk + n, 0)),
        out_shape=jax.ShapeDtypeStruct((p.shape[0], pw), F32),
        scratch_shapes=[pltpu.VMEM((16 + BLK, pw), F32)],
        compiler_params=_cparams("parallel", "arbitrary"), name="pool_prompt")(p, w_pool, s_pool)


def _pool_sample_body(u_ref, hist_ref, w_ref, s_ref, ybuf_ref, y_ref, ext_ref, *, gw, pos0):
    del ybuf_ref
    u = u_ref[...]
    ext_ref[0:1, :] = jnp.zeros((1, ext_ref.shape[1]), F32)
    ext_ref[1:16, :] = hist_ref[0]
    ext_ref[16:16 + DEC_T, :] = u
    pos = pos0 + lax.broadcasted_iota(jnp.int32, (DEC_T, 1), 0)
    y_ref[...] = _pool_windows(ext_ref, u, DEC_T, pos, w_ref, s_ref, gw)


def _sample_row_block(s, rows_per_seq):
    return (s // SLOTS) * (rows_per_seq // DEC_T) + s % SLOTS


def _pool_sample(p, hist, w_pool, s_pool, ybuf, rows_per_seq, pos0):
    nb = hist.shape[0]
    pw = hist.shape[2]
    gw = w_pool.shape[1]
    rowmap = lambda s: (_sample_row_block(s, rows_per_seq), 0)
    return pl.pallas_call(
        functools.partial(_pool_sample_body, gw=gw, pos0=pos0), grid=(nb,),
        in_specs=[pl.BlockSpec((DEC_T, pw), rowmap),
                  pl.BlockSpec((1, POOL_HIST, pw), lambda s: (s, 0, 0)),
                  pl.BlockSpec(w_pool.shape, lambda s: (0, 0, 0)),
                  pl.BlockSpec((1, pw), lambda s: (0, 0)),
                  pl.BlockSpec(memory_space=pl.ANY)],
        out_specs=pl.BlockSpec((DEC_T, pw), rowmap),
        out_shape=jax.ShapeDtypeStruct(ybuf.shape, F32),
        scratch_shapes=[pltpu.VMEM((16 + DEC_T, pw), F32)],
        input_output_aliases={4: 0},
        compiler_params=_cparams("arbitrary"), name="pool_sample")(p, hist, w_pool, s_pool, ybuf)


def _cumsum_lanes(x):
    lane = lax.broadcasted_iota(jnp.int32, x.shape, 1)
    s = 1
    while s < x.shape[1]:
        x = x + jnp.where(lane >= s, pltpu.roll(x, s, axis=1), 0.0)
        s *= 2
    return x


HEAD_PACK = 2


def _unit_lower_inverse(mats, ii, jj):
    eye = jnp.where(ii == jj, 1.0, 0.0).astype(F32)
    pair = ((ii // 2) == (jj // 2)) & (ii % 2 == 1) & (jj % 2 == 0)
    xs = [eye - jnp.where(pair, a, 0.0) for a in mats]
    s = 2
    while s < BLK:
        mask = ((ii // (2 * s)) == (jj // (2 * s))) & ((ii // s) % 2 == 1) & ((jj // s) % 2 == 0)
        ts = [_bdot(jnp.where(mask, a, 0.0), x) for a, x in zip(mats, xs)]
        xs = [x - _bdot(x, t) for x, t in zip(xs, ts)]
        s *= 2
    return xs


def _delta_scalars(ba, valid, alog_ref, dtb_ref, nh):
    bat = ba.T
    beta = jnp.where(valid, 1.0 / (1.0 + jnp.exp(-bat[0:nh])), 0.0)
    z = bat[nh:2 * nh] + dtb_ref[...]
    softplus = jnp.maximum(z, 0.0) + jnp.log(1.0 + jnp.exp(-jnp.abs(z)))
    g = jnp.where(valid, -jnp.exp(alog_ref[...]) * softplus, 0.0)
    gc = _cumsum_lanes(g)
    glast = jnp.broadcast_to(gc[:, BLK - 1:BLK], gc.shape)
    eg = jnp.exp(gc)
    rows = jnp.concatenate(
        [gc, eg, beta, beta * eg, jnp.exp(glast - gc), jnp.exp(glast),
         jnp.zeros((BLK - 6 * nh, BLK), F32)], axis=0)
    return gc, rows.T


def _l2n(x):
    return x * lax.rsqrt(jnp.sum(x * x, axis=-1, keepdims=True) + EPS)


def _delta_chunk_small(xq, xk, xv, ba, alog_ref, dtb_ref, s_ref, nh, dk):
    t = xq.shape[0]
    zrows = jnp.zeros((BLK - t, BLK), F32)
    lane = lax.broadcasted_iota(jnp.int32, (nh, BLK), 1)
    gc, cols = _delta_scalars(jnp.concatenate([ba, zrows], axis=0), lane < t, alog_ref, dtb_ref, nh)
    ii = lax.broadcasted_iota(jnp.int32, (t, BLK), 0)
    jj = lax.broadcasted_iota(jnp.int32, (t, BLK), 1)
    incl = ii >= jj
    strict = ii > jj
    each = lambda f: [f(h) for h in range(nh)]
    col = lambda h, qi: cols[0:t, qi * nh + h:qi * nh + h + 1]
    pad = lambda x: jnp.concatenate([x, zrows], axis=0)
    q = each(lambda h: _l2n(xq[:, h * dk:(h + 1) * dk]) * (dk ** -0.5))
    k = each(lambda h: _l2n(xk[:, h * dk:(h + 1) * dk]))
    s = each(lambda h: s_ref[h])
    kpad = each(lambda h: pad(k[h]))
    decay = each(lambda h: jnp.where(incl, jnp.exp(jnp.where(incl, col(h, 0) - gc[h:h + 1, :], 0.0)), 0.0))
    a_mat = each(lambda h: jnp.where(strict, _bdot_nt(k[h], kpad[h]) * decay[h] * col(h, 2), 0.0))
    qk = each(lambda h: _bdot_nt(q[h], kpad[h]) * decay[h])
    wu = each(lambda h: jnp.concatenate([col(h, 3) * k[h], col(h, 2) * xv[:, h * dk:(h + 1) * dk]], axis=-1))
    for j in range(t - 1):
        wu = each(lambda h: wu[h] - a_mat[h][:, j:j + 1] * wu[h][j:j + 1, :])
    v_new = each(lambda h: wu[h][:, dk:] - _bdot(wu[h][:, :dk], s[h]))
    o = each(lambda h: col(h, 1) * _bdot(q[h], s[h]))
    for j in range(t):
        o = each(lambda h: o[h] + qk[h][:, j:j + 1] * v_new[h][j:j + 1, :])
    kd = each(lambda h: pad(k[h] * col(h, 4)).T)
    s_new = each(lambda h: cols[:, 5 * nh + h:5 * nh + h + 1] * s[h] + _bdot(kd[h], pad(v_new[h])))
    for h in range(nh):
        s_ref[h] = s_new[h]
    return o


def _delta_chunk(xq, xk, xv, ba, valid, alog_ref, dtb_ref, s_ref, nh, dk):
    assert dk == BLK and nh % HEAD_PACK == 0
    gc, cols = _delta_scalars(ba, valid, alog_ref, dtb_ref, nh)
    n = HEAD_PACK * BLK
    ii = lax.broadcasted_iota(jnp.int32, (n, n), 0)
    jj = lax.broadcasted_iota(jnp.int32, (n, n), 1)
    same = (ii // BLK) == (jj // BLK)
    incl = same & (ii >= jj)
    strict = same & (ii > jj)
    packs = range(nh // HEAD_PACK)
    each = lambda f: [f(p) for p in packs]
    stack = lambda p, f: jnp.concatenate([f(h) for h in range(p * HEAD_PACK, (p + 1) * HEAD_PACK)], axis=0)
    col = lambda p, qi: stack(p, lambda h: cols[:, qi * nh + h:qi * nh + h + 1])
    diag = lambda m: jnp.where(same, jnp.concatenate([m] * HEAD_PACK, axis=1), 0.0)
    q = each(lambda p: stack(p, lambda h: _l2n(xq[:, h * dk:(h + 1) * dk]) * (dk ** -0.5)))
    k = each(lambda p: stack(p, lambda h: _l2n(xk[:, h * dk:(h + 1) * dk])))
    v = each(lambda p: stack(p, lambda h: xv[:, h * dk:(h + 1) * dk]))
    s = each(lambda p: stack(p, lambda h: s_ref[h]))
    diff = each(lambda p: col(p, 0) - jnp.concatenate(
        [gc[h:h + 1, :] for h in range(p * HEAD_PACK, (p + 1) * HEAD_PACK)], axis=1))
    decay = each(lambda p: jnp.where(incl, jnp.exp(jnp.where(incl, diff[p], 0.0)), 0.0))
    a_mat = each(lambda p: jnp.where(strict, _bdot_nt(k[p], k[p]) * decay[p] * col(p, 2), 0.0))
    qk = each(lambda p: _bdot_nt(q[p], k[p]) * decay[p])
    x = _unit_lower_inverse(a_mat, ii, jj)
    rhs = each(lambda p: jnp.concatenate([col(p, 3) * k[p], col(p, 2) * v[p]], axis=-1))
    wu = each(lambda p: _bdot(x[p], rhs[p]))
    v_new = each(lambda p: wu[p][:, dk:] - _bdot(diag(wu[p][:, :dk]), s[p]))
    qs = each(lambda p: _bdot(diag(q[p]), s[p]))
    o = each(lambda p: col(p, 1) * qs[p] + _bdot(qk[p], v_new[p]))
    kd = each(lambda p: jnp.where(same, jnp.concatenate([(k[p] * col(p, 4)).T] * HEAD_PACK, axis=0), 0.0))
    s_new = each(lambda p: col(p, 5) * s[p] + _bdot(kd[p], v_new[p]))
    outs = []
    for p in packs:
        for i in range(HEAD_PACK):
            s_ref[p * HEAD_PACK + i] = s_new[p][i * BLK:(i + 1) * BLK, :]
            outs.append(o[p][i * BLK:(i + 1) * BLK, :])
    return outs


def _delta_out(o, gate, ng_ref):
    return o * lax.rsqrt(jnp.mean(o * o, axis=-1, keepdims=True) + EPS) * ng_ref[...] * _silu(gate)


def _conv_silu(ext_ref, w_ref, t):
    acc = ext_ref[5:5 + t, :] * w_ref[0:1, :]
    for i in range(1, DN_CONV):
        acc = acc + ext_ref[5 + i:5 + i + t, :] * w_ref[i:i + 1, :]
    return _silu(acc)


def _delta_prompt_body(q_ref, k_ref, v_ref, gt_ref, ba_ref, wq_ref, wk_ref, wv_ref, alog_ref, dtb_ref, ng_ref,
                       y_ref, sout_ref, eq_ref, ek_ref, ev_ref, s_ref, *, nh, dk, nblk):
    c = pl.program_id(1)

    @pl.when(c == 0)
    def _():
        for e in (eq_ref, ek_ref, ev_ref):
            e[0:8, :] = jnp.zeros((8, e.shape[1]), F32)
        s_ref[...] = jnp.zeros(s_ref.shape, F32)

    xs = []
    for x_ref, e_ref, w_ref in ((q_ref, eq_ref, wq_ref), (k_ref, ek_ref, wk_ref), (v_ref, ev_ref, wv_ref)):
        e_ref[8:8 + BLK, :] = x_ref[...]
        xs.append(_conv_silu(e_ref, w_ref, BLK))
        e_ref[0:8, :] = x_ref[BLK - 8:, :]
    lane = lax.broadcasted_iota(jnp.int32, (nh, BLK), 1)
    valid = (c > 0) | (lane >= FRONT)
    outs = _delta_chunk(xs[0], xs[1], xs[2], ba_ref[...], valid, alog_ref, dtb_ref, s_ref, nh, dk)
    row = c * BLK + lax.broadcasted_iota(jnp.int32, (BLK, 1), 0)
    for h in range(nh):
        sl = slice(h * dk, (h + 1) * dk)
        y_ref[:, sl] = jnp.where(row >= FRONT, _delta_out(outs[h], gt_ref[:, sl], ng_ref), 0.0)

    @pl.when(c == nblk - 1)
    def _():
        sout_ref[0] = s_ref[...]


def _delta_prompt(p, ba, w_conv, a_log, dt_bias, norm_g, nseq, nblk, col_q, col_gate, nh, dk):
    hw = nh * dk
    cq, ck, cv, cg = col_q // hw, col_q // hw + 1, col_q // hw + 2, col_gate // hw
    rows = lambda cb: pl.BlockSpec((BLK, hw), lambda b, c: (b * nblk + c, cb))
    wcs = lambda cb: pl.BlockSpec((DN_CONV, hw), lambda b, c: (0, cb))
    small = lambda shape: pl.BlockSpec(shape, lambda b, c: (0, 0))
    return pl.pallas_call(
        functools.partial(_delta_prompt_body, nh=nh, dk=dk, nblk=nblk), grid=(nseq, nblk),
        in_specs=[rows(cq), rows(ck), rows(cv), rows(cg),
                  pl.BlockSpec((BLK, BLK), lambda b, c: (b * nblk + c, 0)),
                  wcs(0), wcs(1), wcs(2), small((nh, 1)), small((nh, 1)), small((1, dk))],
        out_specs=[pl.BlockSpec((BLK, hw), lambda b, c: (b * nblk + c, 0)),
                   pl.BlockSpec((1, nh, dk, dk), lambda b, c: (b, 0, 0, 0))],
        out_shape=[jax.ShapeDtypeStruct((p.shape[0], hw), F32),
                   jax.ShapeDtypeStruct((nseq, nh, dk, dk), F32)],
        scratch_shapes=[pltpu.VMEM((8 + BLK, hw), F32)] * 3 + [pltpu.VMEM((nh, dk, dk), F32)],
        compiler_params=_cparams("parallel", "arbitrary"), name="delta_prompt",
    )(p, p, p, p, ba, w_conv, w_conv, w_conv, a_log.reshape(nh, 1), dt_bias.reshape(nh, 1),
      norm_g.reshape(1, dk))


def _delta_sample_body(q_ref, k_ref, v_ref, gt_ref, ba_ref, hq_ref, hk_ref, hv_ref, s0_ref,
                       wq_ref, wk_ref, wv_ref, alog_ref, dtb_ref, ng_ref, ybuf_ref,
                       y_ref, sout_ref, eq_ref, ek_ref, ev_ref, s_ref, *, nh, dk):
    del ybuf_ref
    xs = []
    for x_ref, h_ref, e_ref, w_ref in ((q_ref, hq_ref, eq_ref, wq_ref), (k_ref, hk_ref, ek_ref, wk_ref),
                                       (v_ref, hv_ref, ev_ref, wv_ref)):
        e_ref[5:8, :] = h_ref[0]
        e_ref[8:8 + DEC_T, :] = x_ref[...]
        xs.append(_conv_silu(e_ref, w_ref, DEC_T))
    s_ref[...] = s0_ref[0]
    outs = _delta_chunk_small(xs[0], xs[1], xs[2], ba_ref[...], alog_ref, dtb_ref, s_ref, nh, dk)
    for h in range(nh):
        sl = slice(h * dk, (h + 1) * dk)
        y_ref[:, sl] = _delta_out(outs[h], gt_ref[:, sl], ng_ref)
    sout_ref[0] = s_ref[...]


def _delta_sample(p, ba, conv_hist, s0, w_conv, a_log, dt_bias, norm_g, ybuf, rows_per_seq,
                  col_q, col_gate, nh, dk):
    nb = s0.shape[0]
    hw = nh * dk
    cq, ck, cv, cg = col_q // hw, col_q // hw + 1, col_q // hw + 2, col_gate // hw
    rowmap = lambda cb: (lambda s: (_sample_row_block(s, rows_per_seq), cb))
    rows = lambda cb: pl.BlockSpec((DEC_T, hw), rowmap(cb))
    hist = lambda cb: pl.BlockSpec((1, DN_CONV - 1, hw), lambda s: (s, 0, cb))
    wcs = lambda cb: pl.BlockSpec((DN_CONV, hw), lambda s: (0, cb))
    small = lambda shape: pl.BlockSpec(shape, lambda s: (0, 0))
    return pl.pallas_call(
        functools.partial(_delta_sample_body, nh=nh, dk=dk), grid=(nb,),
        in_specs=[rows(cq), rows(ck), rows(cv), rows(cg), pl.BlockSpec((DEC_T, BLK), rowmap(0)),
                  hist(0), hist(1), hist(2),
                  pl.BlockSpec((1, nh, dk, dk), lambda s: (s, 0, 0, 0)),
                  wcs(0), wcs(1), wcs(2), small((nh, 1)), small((nh, 1)), small((1, dk)),
                  pl.BlockSpec(memory_space=pl.ANY)],
        out_specs=[pl.BlockSpec((DEC_T, hw), rowmap(0)),
                   pl.BlockSpec((1, nh, dk, dk), lambda s: (s, 0, 0, 0))],
        out_shape=[jax.ShapeDtypeStruct(ybuf.shape, F32), jax.ShapeDtypeStruct(s0.shape, F32)],
        scratch_shapes=[pltpu.VMEM((8 + DEC_T, hw), F32)] * 3 + [pltpu.VMEM((nh, dk, dk), F32)],
        input_output_aliases={15: 0},
        compiler_params=_cparams("arbitrary"), name="delta_sample",
    )(p, p, p, p, ba, conv_hist, conv_hist, conv_hist, s0, w_conv, w_conv, w_conv,
      a_log.reshape(nh, 1), dt_bias.reshape(nh, 1), norm_g.reshape(1, dk), ybuf)


def _attn_prompt_body(*refs, nkv, grp, hd, window):
    q_refs = refs[:nkv]
    kp_ref, kc_ref, vp_ref, vc_ref, slope_ref, sink_ref, y_ref = refs[nkv:]
    n = pl.program_id(1)
    i = lax.broadcasted_iota(jnp.int32, (BLK, 2 * BLK), 0)
    j = lax.broadcasted_iota(jnp.int32, (BLK, 2 * BLK), 1)
    dist = BLK + i - j
    krow = (n - 1) * BLK + j
    valid = (dist >= 0) & (dist < window) & (krow >= FRONT)
    distf = dist.astype(F32)
    kk = jnp.concatenate([kp_ref[...], kc_ref[...]], axis=0).astype(BF16)
    vv = jnp.concatenate([vp_ref[...], vc_ref[...]], axis=0).astype(BF16)
    heads = [(kv, g) for kv in range(nkv) for g in range(grp)]
    each = lambda f: [f(t, kv, g) for t, (kv, g) in enumerate(heads)]
    s = each(lambda t, kv, g: _bdot_nt(q_refs[kv][:, g * hd:(g + 1) * hd], kk[:, kv * hd:(kv + 1) * hd]))
    s = each(lambda t, kv, g: jnp.where(valid, s[t] * (hd ** -0.5) - slope_ref[t] * distf, NEG_INF))
    m = each(lambda t, kv, g: jnp.maximum(jnp.max(s[t], axis=-1, keepdims=True), sink_ref[t]))
    pr = each(lambda t, kv, g: jnp.exp(s[t] - m[t]))
    den = each(lambda t, kv, g: jnp.sum(pr[t], axis=-1, keepdims=True) + jnp.exp(sink_ref[t] - m[t]))
    pv = each(lambda t, kv, g: _bdot(pr[t], vv[:, kv * hd:(kv + 1) * hd]))
    for t in range(len(heads)):
        y_ref[:, t * hd:(t + 1) * hd] = pv[t] / den[t]


def _attn_prompt(p, slopes, sinks, nseq, nblk, col_q, col_k, col_v, nkv, grp, hd, window):
    gw, kw = grp * hd, nkv * hd
    prev = lambda cb: (lambda b, n: (b * nblk + jnp.maximum(n - 1, 0), cb))
    cur = lambda cb: (lambda b, n: (b * nblk + n, cb))
    smem = pl.BlockSpec(memory_space=pltpu.SMEM)
    return pl.pallas_call(
        functools.partial(_attn_prompt_body, nkv=nkv, grp=grp, hd=hd, window=window), grid=(nseq, nblk),
        in_specs=[pl.BlockSpec((BLK, gw), cur(col_q // gw + kv)) for kv in range(nkv)]
        + [pl.BlockSpec((BLK, kw), prev(col_k // kw)), pl.BlockSpec((BLK, kw), cur(col_k // kw)),
           pl.BlockSpec((BLK, kw), prev(col_v // kw)), pl.BlockSpec((BLK, kw), cur(col_v // kw)),
           smem, smem],
        out_specs=pl.BlockSpec((BLK, nkv * gw), cur(0)),
        out_shape=jax.ShapeDtypeStruct((p.shape[0], nkv * gw), F32),
        compiler_params=_cparams("parallel", "arbitrary"), name="attn_prompt",
    )(*([p] * (nkv + 4)), slopes, sinks)


def _attn_sample_body(*refs, nkv, grp, hd, window):
    q_refs = refs[:nkv]
    k_ref, v_ref, ck_ref, cv_ref, slope_ref, sink_ref, ybuf_ref, y_ref = refs[nkv:]
    del ybuf_ref
    wc = ck_ref.shape[1]
    i = lax.broadcasted_iota(jnp.int32, (DEC_T, wc), 0)
    j = lax.broadcasted_iota(jnp.int32, (DEC_T, wc), 1)
    dist_c = wc + i - j
    valid_c = (dist_c >= 0) & (dist_c < window)
    i2 = lax.broadcasted_iota(jnp.int32, (DEC_T, DEC_T), 0)
    j2 = lax.broadcasted_iota(jnp.int32, (DEC_T, DEC_T), 1)
    dist_n = i2 - j2
    valid_n = (dist_n >= 0) & (dist_n < window)
    tile = lambda a: jnp.concatenate([a] * grp, axis=0)
    valid_c, valid_n = tile(valid_c), tile(valid_n)
    dist_c, dist_n = tile(dist_c).astype(F32), tile(dist_n).astype(F32)
    gi = lax.broadcasted_iota(jnp.int32, (grp * DEC_T, 1), 0) // DEC_T

    def per_row(ref, kv):
        out = jnp.full((grp * DEC_T, 1), ref[kv * grp], F32)
        for g in range(1, grp):
            out = jnp.where(gi == g, ref[kv * grp + g], out)
        return out

    each = lambda f: [f(kv) for kv in range(nkv)]
    cols = lambda ref, kv: ref[:, kv * hd:(kv + 1) * hd]
    q = each(lambda kv: jnp.concatenate([q_refs[kv][:, g * hd:(g + 1) * hd] for g in range(grp)], axis=0))
    slope = each(lambda kv: per_row(slope_ref, kv))
    sink = each(lambda kv: per_row(sink_ref, kv))
    sc = each(lambda kv: _bdot_nt(q[kv], ck_ref[0, :, kv * hd:(kv + 1) * hd]) * (hd ** -0.5))
    sn = each(lambda kv: _bdot_nt(q[kv], cols(k_ref, kv)) * (hd ** -0.5))
    sc = each(lambda kv: jnp.where(valid_c, sc[kv] - slope[kv] * dist_c, NEG_INF))
    sn = each(lambda kv: jnp.where(valid_n, sn[kv] - slope[kv] * dist_n, NEG_INF))
    m = each(lambda kv: jnp.maximum(jnp.maximum(jnp.max(sc[kv], axis=-1, keepdims=True),
                                                jnp.max(sn[kv], axis=-1, keepdims=True)), sink[kv]))
    pc = each(lambda kv: jnp.exp(sc[kv] - m[kv]))
    pn = each(lambda kv: jnp.exp(sn[kv] - m[kv]))
    den = each(lambda kv: jnp.sum(pc[kv], axis=-1, keepdims=True) + jnp.sum(pn[kv], axis=-1, keepdims=True)
               + jnp.exp(sink[kv] - m[kv]))
    o = each(lambda kv: (_bdot(pc[kv], cv_ref[0, :, kv * hd:(kv + 1) * hd]) + _bdot(pn[kv], cols(v_ref, kv)))
             / den[kv])
    for kv in range(nkv):
        for g in range(grp):
            head = kv * grp + g
            y_ref[:, head * hd:(head + 1) * hd] = o[kv][g * DEC_T:(g + 1) * DEC_T, :]


def _attn_sample(p, cache_k, cache_v, slopes, sinks, ybuf, rows_per_seq, col_q, col_k, col_v,
                 nkv, grp, hd, window):
    nb, wc = cache_k.shape[0], cache_k.shape[1]
    gw, kw = grp * hd, nkv * hd
    ck3 = cache_k.reshape(nb, wc, kw)
    cv3 = cache_v.reshape(nb, wc, kw)
    rowmap = lambda cb: (lambda s: (_sample_row_block(s, rows_per_seq), cb))
    smem = pl.BlockSpec(memory_space=pltpu.SMEM)
    return pl.pallas_call(
        functools.partial(_attn_sample_body, nkv=nkv, grp=grp, hd=hd, window=window), grid=(nb,),
        in_specs=[pl.BlockSpec((DEC_T, gw), rowmap(col_q // gw + kv)) for kv in range(nkv)]
        + [pl.BlockSpec((DEC_T, kw), rowmap(col_k // kw)),
           pl.BlockSpec((DEC_T, kw), rowmap(col_v // kw)),
           pl.BlockSpec((1, wc, kw), lambda s: (s, 0, 0)),
           pl.BlockSpec((1, wc, kw), lambda s: (s, 0, 0)),
           smem, smem, pl.BlockSpec(memory_space=pl.ANY)],
        out_specs=pl.BlockSpec((DEC_T, nkv * gw), rowmap(0)),
        out_shape=jax.ShapeDtypeStruct(ybuf.shape, F32),
        input_output_aliases={nkv + 6: 0},
        compiler_params=_cparams("arbitrary"), name="attn_sample",
    )(*([p] * (nkv + 2)), ck3, cv3, slopes, sinks, ybuf)


SUB = 8


def _bitonic_pairs(n, merge_only=False):
    out = []
    k = n if merge_only else 2
    while k <= n:
        j = k // 2
        while j >= 1:
            out += [(i, i ^ j, (i & k) == 0) for i in range(n) if (i ^ j) > i]
            j //= 2
        k *= 2
    return out


def _compare_exchange(v, pairs):
    v = list(v)
    for i, l, desc in pairs:
        hi, lo = jnp.maximum(v[i], v[l]), jnp.minimum(v[i], v[l])
        v[i], v[l] = (hi, lo) if desc else (lo, hi)
    return v


def _top16(rows):
    v = _compare_exchange(rows, _bitonic_pairs(TOPK))
    shift = SUB // 2
    while shift >= 1:
        w = [pltpu.roll(v[TOPK - 1 - r], shift, axis=0) for r in range(TOPK)]
        v = _compare_exchange([jnp.maximum(a, b) for a, b in zip(v, w)], _bitonic_pairs(TOPK, merge_only=True))
        shift //= 2
    return v


def _sublane_sum(x):
    shift = SUB // 2
    while shift >= 1:
        x = x + pltpu.roll(x, shift, axis=0)
        shift //= 2
    return x


def _on_sublanes(vs):
    sub = lax.broadcasted_iota(jnp.int32, vs[0].shape, 0)
    out = vs[SUB - 1]
    for j in range(SUB - 2, -1, -1):
        out = jnp.where(sub == j, vs[j], out)
    return out


def _peer_topk_body(q_ref, keys_ref, rk_ref, cut_ref, e1_ref, e2_ref, *, nh):
    nv = NKEYS // SUB
    for h in range(nh):
        sc = [_bdot_nt(keys_ref[2 * h + half], q_ref[:, (2 * h + half) * NKEYS:(2 * h + half + 1) * NKEYS])
              for half in (0, 1)]
        s1 = [sc[0][SUB * i:SUB * (i + 1), :] for i in range(nv)]
        s2 = [sc[1][SUB * i:SUB * (i + 1), :] for i in range(nv)]
        a = _top16(s1)
        b = _top16(s2)
        b_lo, b_hi, a_hi = _on_sublanes(b[:SUB]), _on_sublanes(b[SUB:]), _on_sublanes(a[SUB:])
        cand = [a[0] + b_lo, a[0] + b_hi, a_hi + b[0]] + [a[i] + b_lo for i in range(1, SUB)]
        cand += [jnp.full(cand[0].shape, LOWEST, F32)] * (TOPK - len(cand))
        top = _top16(cand)
        thr = top[TOPK - 1]
        zsum = jnp.exp(top[0] - top[0])
        for r in range(1, TOPK):
            zsum = zsum + jnp.exp(top[r] - top[0])
        rz = 1.0 / zsum
        height = [_sublane_sum(jnp.where(a[r] + b_lo >= thr, 1.0, 0.0) + jnp.where(a[r] + b_hi >= thr, 1.0, 0.0))
                  for r in range(TOPK)]
        cut, rank2 = [], []
        for i in range(nv):
            c = jnp.zeros(s1[i].shape, F32)
            for r in range(TOPK - 1, -1, -1):
                c = jnp.where(s1[i] == a[r], height[r], c)
            cut.append(c)
            k = jnp.where(b[0] > s2[i], 1.0, 0.0)
            for r in range(1, TOPK):
                k = k + jnp.where(b[r] > s2[i], 1.0, 0.0)
            rank2.append(k)
        rk_ref[h] = jnp.concatenate(rank2, axis=0).astype(BF16)
        cut_ref[h] = jnp.concatenate(cut, axis=0)
        e1_ref[h] = jnp.exp(sc[0] - a[0][0:1, :])
        e2_ref[h] = (jnp.exp(sc[1] - b[0][0:1, :]) * rz[0:1, :]).astype(BF16)


def _peer_topk(q, sub_keys, *, tm=128):
    n = q.shape[0]
    nh = sub_keys.shape[0]
    keys = sub_keys.reshape(2 * nh, NKEYS, sub_keys.shape[-1])
    sspec = pl.BlockSpec((nh, NKEYS, tm), lambda i: (0, 0, i))
    shape = lambda dt: jax.ShapeDtypeStruct((nh, NKEYS, n), dt)
    return pl.pallas_call(
        functools.partial(_peer_topk_body, nh=nh), grid=(n // tm,),
        in_specs=[pl.BlockSpec((tm, q.shape[1]), lambda i: (i, 0)),
                  pl.BlockSpec(keys.shape, lambda i: (0, 0, 0))],
        out_specs=[sspec] * 4,
        out_shape=[shape(BF16), shape(F32), shape(F32), shape(BF16)],
        compiler_params=_cparams("parallel"), name="peer_topk")(q, keys)


def _gelu(x):
    return 0.5 * x * (1.0 + lax.erf(x * (2.0 ** -0.5)))


PACK = 16


def _peer_expert_body(xn_ref, wd_ref, wu_ref, id_ref, iu_ref, rk_ref, cut_ref, e1_ref, e2_ref, y_ref,
                      xt_ref, ix_ref, *, nh, ei, sub):
    c = pl.program_id(1)
    tm = xt_ref.shape[1]
    tw = tm // sub
    slabs = [slice(t * tw, (t + 1) * tw) for t in range(sub)]
    each = lambda f: [f(t, tl) for t, tl in enumerate(slabs)]

    @pl.when(c == 0)
    def _():
        y_ref[...] = jnp.zeros(y_ref.shape, F32)
        x = xn_ref[...].astype(F32)
        sx = _amax_scale(x)
        xt_ref[...] = (x * sx).T.astype(FP8)
        ix_ref[...] = jnp.broadcast_to(1.0 / sx, ix_ref.shape)

    def gates(tl):
        def row16(ref, h, i1):
            return jnp.broadcast_to(ref[h, pl.ds(i1, 1), tl], (PACK, tw)).astype(BF16)
        pieces = []
        for ii in range(ei):
            i1 = c * ei + ii
            cut = [row16(cut_ref, h, i1) for h in range(nh)]
            e1 = [row16(e1_ref, h, i1) for h in range(nh)]
            for r in range(NKEYS // PACK):
                sl = slice(r * PACK, (r + 1) * PACK)
                gate = pieces[-1] * 0.0 if pieces else None
                for h in range(nh):
                    w = jnp.where(rk_ref[h, sl, tl] < cut[h], e1[h] * e2_ref[h, sl, tl], 0.0)
                    gate = w if gate is None else gate + w
                pieces.append(gate)
        return pieces

    inv_h = id_ref[0, 0:1, 0:1] * ix_ref[0:1, 0:1]
    inv_u = iu_ref[0, 0:1, 0:1]
    ht = each(lambda t, tl: jnp.dot(wd_ref[...], xt_ref[:, tl], preferred_element_type=F32) * inv_h)
    gate = each(lambda t, tl: gates(tl))
    act = each(lambda t, tl: _gelu(ht[t]))
    at = each(lambda t, tl: jnp.concatenate(
        [act[t][i * PACK:(i + 1) * PACK, :] * g.astype(F32) for i, g in enumerate(gate[t])], axis=0))
    sa = each(lambda t, tl: _amax_scale(at[t]))
    part = each(lambda t, tl: jnp.dot((at[t] * sa[t]).T.astype(FP8), wu_ref[...], preferred_element_type=F32))
    for t, tl in enumerate(slabs):
        y_ref[tl, :] += part[t] * (inv_u / sa[t])


PEER_TM, PEER_EI, PEER_SUB = 512, 8, 2


def _peer_expert(xn, w_down, w_up, inv_down, inv_up, rank2, cut, e1, e2, *, tm=PEER_TM, ei=PEER_EI, sub=PEER_SUB):
    n, d = xn.shape
    nh = rank2.shape[0]
    e = ei * NKEYS
    once = dict(pipeline_mode=pl.Buffered(1))
    sspec = pl.BlockSpec((nh, NKEYS, tm), lambda i, c: (0, 0, i), **once)
    inv = pl.BlockSpec((1, 8, BLK), lambda i, c: (c, 0, 0))
    return pl.pallas_call(
        functools.partial(_peer_expert_body, nh=nh, ei=ei, sub=sub), grid=(n // tm, w_down.shape[0] // e),
        in_specs=[pl.BlockSpec((tm, d), lambda i, c: (i, 0), **once),
                  pl.BlockSpec((e, d), lambda i, c: (c, 0)),
                  pl.BlockSpec((e, d), lambda i, c: (c, 0)),
                  inv, inv, sspec, sspec, sspec, sspec],
        out_specs=pl.BlockSpec((tm, d), lambda i, c: (i, 0), **once),
        out_shape=jax.ShapeDtypeStruct((n, d), F32),
        scratch_shapes=[pltpu.VMEM((d, tm), FP8), pltpu.VMEM((8, BLK), F32)],
        compiler_params=_cparams("parallel", "arbitrary"), name="peer_expert",
    )(xn, w_down, w_up, inv_down, inv_up, rank2, cut, e1, e2)


def _final_norm(h, y, g, nseq, nblk, row_blk, first_blk, nblk_out):
    d = h.shape[1]
    row = pl.BlockSpec((row_blk, d), lambda b, i: (b * nblk + first_blk + i, 0))
    return pl.pallas_call(
        _final_norm_body, grid=(nseq, nblk_out),
        in_specs=[row, row, pl.BlockSpec((1, d), lambda b, i: (0, 0))],
        out_specs=pl.BlockSpec((1, row_blk, d), lambda b, i: (b, i, 0)),
        out_shape=jax.ShapeDtypeStruct((nseq, nblk_out * row_blk, d), F32),
        compiler_params=_cparams("parallel", "parallel"), name="final_norm",
    )(h, y, g.reshape(1, d).astype(F32))


def _final_norm_body(h_ref, y_ref, g_ref, o_ref):
    x = h_ref[...] + y_ref[...]
    ms = jnp.mean(x * x, axis=-1, keepdims=True)
    o_ref[0] = x * lax.rsqrt(ms + EPS) * g_ref[...]


def kernel(x_prompt, x_sample, state_pool, state_conv, state_delta, cache_k, cache_v, meta_tokens, norm1_g,
           w_in, w_pool, s_pool, w_conv, a_log, dt_bias, dn_norm_g, attn_sinks, w_out, norm2_g,
           peer_w_query, peer_sub_keys, peer_w_down, peer_w_up, final_norm_g):
    nseq, seq, d = x_prompt.shape
    nsamp, dec_t, _ = x_sample.shape
    depth = w_in.shape[0]
    pool_w = w_pool.shape[1] * w_pool.shape[2]
    dn_qkv = w_conv.shape[2]
    nh, dk = state_delta.shape[2], state_delta.shape[3]
    wc, nkv, hd = cache_k.shape[2], cache_k.shape[3], cache_k.shape[4]
    nq = attn_sinks.shape[1]
    grp = nq // nkv
    window = wc
    assert dec_t == DEC_T and nsamp == nseq * SLOTS and (N_META + seq) % BLK == N_META
    assert SLOTS * DEC_T + POOL_HIST <= FRONT and dn_qkv == 3 * nh * dk and wc == BLK
    rows_per_seq = FRONT + N_META + seq
    nblk = rows_per_seq // BLK
    past_len = 16384
    col_qkv = pool_w
    col_gate = col_qkv + dn_qkv
    col_q = col_gate + nh * dk
    col_k = col_q + nq * hd
    col_v = col_k + nkv * hd
    src_ba = pool_w + dn_qkv

    xs = x_sample.reshape(nseq, SLOTS * DEC_T, d)
    zeros = jnp.zeros((nseq, FRONT - SLOTS * DEC_T, d), F32)
    meta = jnp.broadcast_to(meta_tokens[None], (nseq, N_META, d))
    h = jnp.concatenate([xs, zeros, meta, x_prompt], axis=1).reshape(nseq * rows_per_seq, d)

    slopes = jnp.exp2(-8.0 * (jnp.arange(nq, dtype=F32) + 1.0) / nq)
    new_p = [[] for _ in range(5)]
    new_s = [[] for _ in range(5)]
    y_peer = None
    for l in range(depth):
        w_main, w_ba = _cast_w_in(w_in, l, src_ba, 2 * nh)
        h, xn = _addnorm(h, y_peer, norm1_g[l])
        p = _mm(xn, w_main)
        ba = _mm(xn, w_ba)

        y_pool = _pool_prompt(p, w_pool[l], s_pool[l].reshape(1, pool_w), nseq, nblk)
        y_pool = _pool_sample(p, state_pool[l], w_pool[l], s_pool[l].reshape(1, pool_w), y_pool,
                              rows_per_seq, past_len)
        y_dn, s_p = _delta_prompt(p, ba, w_conv[l], a_log[l], dt_bias[l], dn_norm_g[l], nseq, nblk,
                                  col_qkv, col_gate, nh, dk)
        y_dn, s_s = _delta_sample(p, ba, state_conv[l], state_delta[l], w_conv[l], a_log[l], dt_bias[l],
                                  dn_norm_g[l], y_dn, rows_per_seq, col_qkv, col_gate, nh, dk)
        y_att = _attn_prompt(p, slopes, attn_sinks[l], nseq, nblk, col_q, col_k, col_v, nkv, grp, hd, window)
        y_att = _attn_sample(p, cache_k[l], cache_v[l], slopes, attn_sinks[l], y_att, rows_per_seq,
                             col_q, col_k, col_v, nkv, grp, hd, window)
        h = _outproj(y_pool, y_dn, y_att, _cast(w_out, l), h)

        _, xn2 = _addnorm(h, None, norm2_g[l])
        q = _mm(xn2, _cast(peer_w_query, l))
        rank2, cut, e1, e2 = _peer_topk(q, peer_sub_keys[l])
        wd8, inv_d = _quant_fp8(peer_w_down, l, tr=PEER_EI * NKEYS)
        wu8, inv_u = _quant_fp8(peer_w_up, l, tr=PEER_EI * NKEYS)
        y_peer = _peer_expert(xn2, wd8, wu8, inv_d, inv_u, rank2, cut, e1, e2)

        p3 = p.reshape(nseq, rows_per_seq, p.shape[1])
        ps = p3[:, :SLOTS * DEC_T].reshape(nsamp, DEC_T, p.shape[1])
        new_p[0].append(p3[:, -POOL_HIST:, :pool_w])
        new_s[0].append(jnp.concatenate([state_pool[l], ps[:, :, :pool_w]], axis=1)[:, -POOL_HIST:])
        new_p[1].append(p3[:, -(DN_CONV - 1):, col_qkv:col_gate])
        new_s[1].append(jnp.concatenate([state_conv[l], ps[:, :, col_qkv:col_gate]], axis=1)[:, -(DN_CONV - 1):])
        new_p[2].append(s_p)
        new_s[2].append(s_s)
        new_p[3].append(p3[:, -window:, col_k:col_v].reshape(nseq, window, nkv, hd))
        new_s[3].append(jnp.concatenate([cache_k[l], ps[:, :, col_k:col_v].reshape(nsamp, DEC_T, nkv, hd)],
                                        axis=1)[:, -wc:])
        new_p[4].append(p3[:, -window:, col_v:].reshape(nseq, window, nkv, hd))
        new_s[4].append(jnp.concatenate([cache_v[l], ps[:, :, col_v:].reshape(nsamp, DEC_T, nkv, hd)],
                                        axis=1)[:, -wc:])

    y_prompt = _final_norm(h, y_peer, final_norm_g, nseq, nblk, BLK, 1, nblk - 1)
    y_sample = _final_norm(h, y_peer, final_norm_g, nseq, rows_per_seq // (SLOTS * DEC_T), SLOTS * DEC_T, 0, 1)
    y_sample = y_sample.reshape(nsamp, DEC_T, d)
    pool_p, conv_p, delta_p, k_p, v_p = (jnp.stack(a) for a in new_p)
    pool_s, conv_s, delta_s, k_s, v_s = (jnp.stack(a) for a in new_s)
    return (y_prompt, y_sample, pool_p, pool_s, conv_p, conv_s, delta_p, delta_s, k_p, k_s, v_p, v_s)
```

```python
import functools

import jax
import jax.numpy as jnp
from jax import lax
from jax.experimental import pallas as pl
from jax.experimental.pallas import tpu as pltpu

F32 = jnp.float32
BF16 = jnp.bfloat16

EPS = 1e-6
NEG_INF = -1e30
LOWEST = -3.0e38

N_META = 16
BLK = 128
FRONT = BLK - N_META
DEC_T = 8
SLOTS = 8
POOL_WINDOWS = (2, 4, 8, 16)
POOL_HIST = 15
DN_CONV = 4
TOPK = 16
NKEYS = 128
VMEM_LIMIT = 56 * 1024 * 1024


def _cparams(*sem):
    return pltpu.CompilerParams(dimension_semantics=sem, vmem_limit_bytes=VMEM_LIMIT)


def _bdot(a, b):
    return jnp.dot(a.astype(BF16), b.astype(BF16), preferred_element_type=F32)


def _bdot_nt(a, b):
    return lax.dot_general(a.astype(BF16), b.astype(BF16), (((1,), (1,)), ((), ())),
                           preferred_element_type=F32)


def _silu(x):
    return x * (1.0 / (1.0 + jnp.exp(-x)))


def _addnorm_body(*refs, add):
    if add:
        h_ref, y_ref, g_ref, hs_ref, xn_ref = refs
        x = h_ref[...] + y_ref[...]
        hs_ref[...] = x
    else:
        h_ref, g_ref, xn_ref = refs
        x = h_ref[...]
    ms = jnp.mean(x * x, axis=-1, keepdims=True)
    xn_ref[...] = (x * lax.rsqrt(ms + EPS) * g_ref[...]).astype(xn_ref.dtype)


def _addnorm(h, y, g, *, tm=256, out_dtype=BF16):
    n, d = h.shape
    row = pl.BlockSpec((tm, d), lambda i: (i, 0))
    gspec = pl.BlockSpec((1, d), lambda i: (0, 0))
    g2 = g.reshape(1, d).astype(F32)
    if y is None:
        xn = pl.pallas_call(
            functools.partial(_addnorm_body, add=False),
            grid=(n // tm,), in_specs=[row, gspec], out_specs=row,
            out_shape=jax.ShapeDtypeStruct((n, d), out_dtype),
            compiler_params=_cparams("parallel"), name="norm")(h, g2)
        return h, xn
    hs, xn = pl.pallas_call(
        functools.partial(_addnorm_body, add=True),
        grid=(n // tm,), in_specs=[row, row, gspec], out_specs=[row, row],
        out_shape=[jax.ShapeDtypeStruct((n, d), F32), jax.ShapeDtypeStruct((n, d), out_dtype)],
        compiler_params=_cparams("parallel"), name="add_norm")(h, y, g2)
    return hs, xn


def _cast_body(x_ref, o_ref):
    o_ref[...] = x_ref[0].astype(o_ref.dtype)


def _cast(x, l, dtype=BF16, *, tr=512):
    _, r, c = x.shape
    return pl.pallas_call(
        _cast_body, grid=(r // tr,),
        in_specs=[pl.BlockSpec((1, tr, c), lambda i: (l, i, 0))],
        out_specs=pl.BlockSpec((tr, c), lambda i: (i, 0)),
        out_shape=jax.ShapeDtypeStruct((r, c), dtype),
        compiler_params=_cparams("parallel"), name="cast")(x)


FP8 = jnp.float8_e4m3fn
FP8_TARGET = 224.0


def _amax_scale(x):
    a = jnp.max(jnp.max(jnp.abs(x), axis=1, keepdims=True), axis=0, keepdims=True)
    return jnp.where(a > 0.0, FP8_TARGET / a, 1.0)


def _quant_body(x_ref, o_ref, inv_ref):
    x = x_ref[0]
    s = _amax_scale(x)
    o_ref[...] = (x * s).astype(FP8)
    inv_ref[0] = jnp.broadcast_to(1.0 / s, inv_ref.shape[1:])


def _quant_fp8(x, l, *, tr):
    _, r, c = x.shape
    return pl.pallas_call(
        _quant_body, grid=(r // tr,),
        in_specs=[pl.BlockSpec((1, tr, c), lambda i: (l, i, 0))],
        out_specs=[pl.BlockSpec((tr, c), lambda i: (i, 0)), pl.BlockSpec((1, 8, BLK), lambda i: (i, 0, 0))],
        out_shape=[jax.ShapeDtypeStruct((r, c), FP8), jax.ShapeDtypeStruct((r // tr, 8, BLK), F32)],
        compiler_params=_cparams("parallel"), name="quant_fp8")(x)


def _cast_w_in_body(a_ref, b_ref, main_ref, ba_ref, *, nlo, nba):
    j = pl.program_id(0)
    tr = a_ref.shape[1]

    @pl.when(j < nlo)
    def _():
        main_ref[...] = a_ref[0].astype(BF16)

    @pl.when(j >= nlo)
    def _():
        main_ref[0:tr - nba, :] = a_ref[0, nba:, :].astype(BF16)
        main_ref[tr - nba:, :] = b_ref[0].astype(BF16)

    @pl.when(j == nlo)
    def _():
        ba_ref[0:nba, :] = a_ref[0, 0:nba, :].astype(BF16)
        ba_ref[nba:, :] = jnp.zeros((ba_ref.shape[0] - nba, ba_ref.shape[1]), BF16)


def _cast_w_in(w_in_t, l, lo, nba, *, tr=1024):
    _, r, c = w_in_t.shape
    assert lo % tr == 0 and (r - nba) % tr == 0 and tr % nba == 0 and nba % 8 == 0
    return pl.pallas_call(
        functools.partial(_cast_w_in_body, nlo=lo // tr, nba=nba), grid=((r - nba) // tr,),
        in_specs=[pl.BlockSpec((1, tr, c), lambda j: (l, j, 0)),
                  pl.BlockSpec((1, nba, c), lambda j: (l, (j + 1) * (tr // nba), 0))],
        out_specs=[pl.BlockSpec((tr, c), lambda j: (j, 0)), pl.BlockSpec((BLK, c), lambda j: (0, 0))],
        out_shape=[jax.ShapeDtypeStruct((r - nba, c), BF16), jax.ShapeDtypeStruct((BLK, c), BF16)],
        compiler_params=_cparams("arbitrary"), name="cast_w_in")(w_in_t, w_in_t)


def _mm_body(x_ref, w_ref, o_ref, *, nt):
    if nt:
        o_ref[...] = lax.dot_general(x_ref[...], w_ref[...], (((1,), (1,)), ((), ())),
                                     preferred_element_type=F32)
    else:
        o_ref[...] = jnp.dot(x_ref[...], w_ref[...], preferred_element_type=F32)


def _mm(x, w, *, nt=False, tm=512, tn=1024):
    m, k = x.shape
    n = w.shape[0] if nt else w.shape[1]
    tn = min(tn, n)
    wspec = pl.BlockSpec((tn, k), lambda j, i: (j, 0)) if nt else pl.BlockSpec((k, tn), lambda j, i: (0, j))
    return pl.pallas_call(
        functools.partial(_mm_body, nt=nt), grid=(n // tn, m // tm),
        in_specs=[pl.BlockSpec((tm, k), lambda j, i: (i, 0)), wspec],
        out_specs=pl.BlockSpec((tm, tn), lambda j, i: (i, j)),
        out_shape=jax.ShapeDtypeStruct((m, n), F32),
        compiler_params=_cparams("parallel", "parallel"), name="matmul")(x, w)


def _outproj_body(yp_ref, yd_ref, ya_ref, w_ref, h_ref, o_ref, *, wp, wd):
    acc = h_ref[...]
    acc += jnp.dot(yp_ref[...].astype(BF16), w_ref[0:wp, :], preferred_element_type=F32)
    acc += jnp.dot(yd_ref[...].astype(BF16), w_ref[wp:wp + wd, :], preferred_element_type=F32)
    acc += jnp.dot(ya_ref[...].astype(BF16), w_ref[wp + wd:, :], preferred_element_type=F32)
    o_ref[...] = acc


def _outproj(yp, yd, ya, w, h, *, tm=512, tn=1024):
    m, d = h.shape
    wp, wd, wa = yp.shape[1], yd.shape[1], ya.shape[1]
    k = wp + wd + wa
    return pl.pallas_call(
        functools.partial(_outproj_body, wp=wp, wd=wd), grid=(d // tn, m // tm),
        in_specs=[pl.BlockSpec((tm, wp), lambda j, i: (i, 0)),
                  pl.BlockSpec((tm, wd), lambda j, i: (i, 0)),
                  pl.BlockSpec((tm, wa), lambda j, i: (i, 0)),
                  pl.BlockSpec((k, tn), lambda j, i: (0, j)),
                  pl.BlockSpec((tm, tn), lambda j, i: (i, j))],
        out_specs=pl.BlockSpec((tm, tn), lambda j, i: (i, j)),
        out_shape=jax.ShapeDtypeStruct((m, d), F32),
        compiler_params=_cparams("parallel", "parallel"), name="out_proj")(yp, yd, ya, w, h)


def _pool_windows(ext_ref, u, t, pos, w_ref, s_ref, gw):
    outs = []
    for gi, w in enumerate(POOL_WINDOWS):
        sl = slice(gi * gw, (gi + 1) * gw)
        win = u[:, sl]
        for k in range(1, w):
            win = win + ext_ref[16 - k:16 - k + t, sl]
        cnt = jnp.clip(pos + 1, 1, w).astype(F32)
        d = win / cnt - u[:, sl]
        outs.append(_bdot(d, w_ref[gi]))
    return jnp.concatenate(outs, axis=-1) * s_ref[...]


def _pool_prompt_body(u_ref, w_ref, s_ref, y_ref, ext_ref, *, gw):
    n = pl.program_id(1)

    @pl.when(n == 0)
    def _():
        ext_ref[0:16, :] = jnp.zeros((16, ext_ref.shape[1]), F32)

    u = u_ref[...]
    ext_ref[16:16 + BLK, :] = u
    row = n * BLK + lax.broadcasted_iota(jnp.int32, (BLK, 1), 0)
    pos = row - FRONT
    y = _pool_windows(ext_ref, u, BLK, pos, w_ref, s_ref, gw)
    y_ref[...] = jnp.where(pos >= 0, y, 0.0)
    ext_ref[0:16, :] = u[BLK - 16:, :]


def _pool_prompt(p, w_pool, s_pool, nseq, nblk):
    pw = w_pool.shape[0] * w_pool.shape[1]
    gw = w_pool.shape[1]
    return pl.pallas_call(
        functools.partial(_pool_prompt_body, gw=gw), grid=(nseq, nblk),
        in_specs=[pl.BlockSpec((BLK, pw), lambda b, n: (b * nblk + n, 0)),
                  pl.BlockSpec(w_pool.shape, lambda b, n: (0, 0, 0)),
                  pl.BlockSpec((1, pw), lambda b, n: (0, 0))],
        out_specs=pl.BlockSpec((BLK, pw), lambda b, n: (b * nblk + n, 0)),
        out_shape=jax.ShapeDtypeStruct((p.shape[0], pw), F32),
        scratch_shapes=[pltpu.VMEM((16 + BLK, pw), F32)],
        compiler_params=_cparams("parallel", "arbitrary"), name="pool_prompt")(p, w_pool, s_pool)


def _pool_sample_body(u_ref, hist_ref, w_ref, s_ref, ybuf_ref, y_ref, ext_ref, *, gw, pos0):
    del ybuf_ref
    u = u_ref[...]
    ext_ref[0:1, :] = jnp.zeros((1, ext_ref.shape[1]), F32)
    ext_ref[1:16, :] = hist_ref[0]
    ext_ref[16:16 + DEC_T, :] = u
    pos = pos0 + lax.broadcasted_iota(jnp.int32, (DEC_T, 1), 0)
    y_ref[...] = _pool_windows(ext_ref, u, DEC_T, pos, w_ref, s_ref, gw)


def _sample_row_block(s, rows_per_seq):
    return (s // SLOTS) * (rows_per_seq // DEC_T) + s % SLOTS


def _pool_sample(p, hist, w_pool, s_pool, ybuf, rows_per_seq, pos0):
    nb = hist.shape[0]
    pw = hist.shape[2]
    gw = w_pool.shape[1]
    rowmap = lambda s: (_sample_row_block(s, rows_per_seq), 0)
    return pl.pallas_call(
        functools.partial(_pool_sample_body, gw=gw, pos0=pos0), grid=(nb,),
        in_specs=[pl.BlockSpec((DEC_T, pw), rowmap),
                  pl.BlockSpec((1, POOL_HIST, pw), lambda s: (s, 0, 0)),
                  pl.BlockSpec(w_pool.shape, lambda s: (0, 0, 0)),
                  pl.BlockSpec((1, pw), lambda s: (0, 0)),
                  pl.BlockSpec(memory_space=pl.ANY)],
        out_specs=pl.BlockSpec((DEC_T, pw), rowmap),
        out_shape=jax.ShapeDtypeStruct(ybuf.shape, F32),
        scratch_shapes=[pltpu.VMEM((16 + DEC_T, pw), F32)],
        input_output_aliases={4: 0},
        compiler_params=_cparams("arbitrary"), name="pool_sample")(p, hist, w_pool, s_pool, ybuf)


def _cumsum_lanes(x):
    lane = lax.broadcasted_iota(jnp.int32, x.shape, 1)
    s = 1
    while s < x.shape[1]:
        x = x + jnp.where(lane >= s, pltpu.roll(x, s, axis=1), 0.0)
        s *= 2
    return x


HEAD_PACK = 2


def _unit_lower_inverse(mats, ii, jj):
    eye = jnp.where(ii == jj, 1.0, 0.0).astype(F32)
    pair = ((ii // 2) == (jj // 2)) & (ii % 2 == 1) & (jj % 2 == 0)
    xs = [eye - jnp.where(pair, a, 0.0) for a in mats]
    s = 2
    while s < BLK:
        mask = ((ii // (2 * s)) == (jj // (2 * s))) & ((ii // s) % 2 == 1) & ((jj // s) % 2 == 0)
        ts = [_bdot(jnp.where(mask, a, 0.0), x) for a, x in zip(mats, xs)]
        xs = [x - _bdot(x, t) for x, t in zip(xs, ts)]
        s *= 2
    return xs


def _delta_scalars(ba, valid, alog_ref, dtb_ref, nh):
    bat = ba.T
    beta = jnp.where(valid, 1.0 / (1.0 + jnp.exp(-bat[0:nh])), 0.0)
    z = bat[nh:2 * nh] + dtb_ref[...]
    softplus = jnp.maximum(z, 0.0) + jnp.log(1.0 + jnp.exp(-jnp.abs(z)))
    g = jnp.where(valid, -jnp.exp(alog_ref[...]) * softplus, 0.0)
    gc = _cumsum_lanes(g)
    glast = jnp.broadcast_to(gc[:, BLK - 1:BLK], gc.shape)
    eg = jnp.exp(gc)
    rows = jnp.concatenate(
        [gc, eg, beta, beta * eg, jnp.exp(glast - gc), jnp.exp(glast),
         jnp.zeros((BLK - 6 * nh, BLK), F32)], axis=0)
    return gc, rows.T


def _l2n(x):
    return x * lax.rsqrt(jnp.sum(x * x, axis=-1, keepdims=True) + EPS)


def _delta_chunk_small(xq, xk, xv, ba, alog_ref, dtb_ref, s_ref, nh, dk):
    t = xq.shape[0]
    zrows = jnp.zeros((BLK - t, BLK), F32)
    lane = lax.broadcasted_iota(jnp.int32, (nh, BLK), 1)
    gc, cols = _delta_scalars(jnp.concatenate([ba, zrows], axis=0), lane < t, alog_ref, dtb_ref, nh)
    ii = lax.broadcasted_iota(jnp.int32, (t, BLK), 0)
    jj = lax.broadcasted_iota(jnp.int32, (t, BLK), 1)
    incl = ii >= jj
    strict = ii > jj
    each = lambda f: [f(h) for h in range(nh)]
    col = lambda h, qi: cols[0:t, qi * nh + h:qi * nh + h + 1]
    pad = lambda x: jnp.concatenate([x, zrows], axis=0)
    q = each(lambda h: _l2n(xq[:, h * dk:(h + 1) * dk]) * (dk ** -0.5))
    k = each(lambda h: _l2n(xk[:, h * dk:(h + 1) * dk]))
    s = each(lambda h: s_ref[h])
    kpad = each(lambda h: pad(k[h]))
    decay = each(lambda h: jnp.where(incl, jnp.exp(jnp.where(incl, col(h, 0) - gc[h:h + 1, :], 0.0)), 0.0))
    a_mat = each(lambda h: jnp.where(strict, _bdot_nt(k[h], kpad[h]) * decay[h] * col(h, 2), 0.0))
    qk = each(lambda h: _bdot_nt(q[h], kpad[h]) * decay[h])
    wu = each(lambda h: jnp.concatenate([col(h, 3) * k[h], col(h, 2) * xv[:, h * dk:(h + 1) * dk]], axis=-1))
    for j in range(t - 1):
        wu = each(lambda h: wu[h] - a_mat[h][:, j:j + 1] * wu[h][j:j + 1, :])
    v_new = each(lambda h: wu[h][:, dk:] - _bdot(wu[h][:, :dk], s[h]))
    o = each(lambda h: col(h, 1) * _bdot(q[h], s[h]))
    for j in range(t):
        o = each(lambda h: o[h] + qk[h][:, j:j + 1] * v_new[h][j:j + 1, :])
    kd = each(lambda h: pad(k[h] * col(h, 4)).T)
    s_new = each(lambda h: cols[:, 5 * nh + h:5 * nh + h + 1] * s[h] + _bdot(kd[h], pad(v_new[h])))
    for h in range(nh):
        s_ref[h] = s_new[h]
    return o


def _delta_chunk(xq, xk, xv, ba, valid, alog_ref, dtb_ref, s_ref, nh, dk):
    assert dk == BLK and nh % HEAD_PACK == 0
    gc, cols = _delta_scalars(ba, valid, alog_ref, dtb_ref, nh)
    n = HEAD_PACK * BLK
    ii = lax.broadcasted_iota(jnp.int32, (n, n), 0)
    jj = lax.broadcasted_iota(jnp.int32, (n, n), 1)
    same = (ii // BLK) == (jj // BLK)
    incl = same & (ii >= jj)
    strict = same & (ii > jj)
    packs = range(nh // HEAD_PACK)
    each = lambda f: [f(p) for p in packs]
    stack = lambda p, f: jnp.concatenate([f(h) for h in range(p * HEAD_PACK, (p + 1) * HEAD_PACK)], axis=0)
    col = lambda p, qi: stack(p, lambda h: cols[:, qi * nh + h:qi * nh + h + 1])
    diag = lambda m: jnp.where(same, jnp.concatenate([m] * HEAD_PACK, axis=1), 0.0)
    q = each(lambda p: stack(p, lambda h: _l2n(xq[:, h * dk:(h + 1) * dk]) * (dk ** -0.5)))
    k = each(lambda p: stack(p, lambda h: _l2n(xk[:, h * dk:(h + 1) * dk])))
    v = each(lambda p: stack(p, lambda h: xv[:, h * dk:(h + 1) * dk]))
    s = each(lambda p: stack(p, lambda h: s_ref[h]))
    diff = each(lambda p: col(p, 0) - jnp.concatenate(
        [gc[h:h + 1, :] for h in range(p * HEAD_PACK, (p + 1) * HEAD_PACK)], axis=1))
    decay = each(lambda p: jnp.where(incl, jnp.exp(jnp.where(incl, diff[p], 0.0)), 0.0))
    a_mat = each(lambda p: jnp.where(strict, _bdot_nt(k[p], k[p]) * decay[p] * col(p, 2), 0.0))
    qk = each(lambda p: _bdot_nt(q[p], k[p]) * decay[p])
    x = _unit_lower_inverse(a_mat, ii, jj)
    rhs = each(lambda p: jnp.concatenate([col(p, 3) * k[p], col(p, 2) * v[p]], axis=-1))
    wu = each(lambda p: _bdot(x[p], rhs[p]))
    v_new = each(lambda p: wu[p][:, dk:] - _bdot(diag(wu[p][:, :dk]), s[p]))
    qs = each(lambda p: _bdot(diag(q[p]), s[p]))
    o = each(lambda p: col(p, 1) * qs[p] + _bdot(qk[p], v_new[p]))
    kd = each(lambda p: jnp.where(same, jnp.concatenate([(k[p] * col(p, 4)).T] * HEAD_PACK, axis=0), 0.0))
    s_new = each(lambda p: col(p, 5) * s[p] + _bdot(kd[p], v_new[p]))
    outs = []
    for p in packs:
        for i in range(HEAD_PACK):
            s_ref[p * HEAD_PACK + i] = s_new[p][i * BLK:(i + 1) * BLK, :]
            outs.append(o[p][i * BLK:(i + 1) * BLK, :])
    return outs


def _delta_out(o, gate, ng_ref):
    return o * lax.rsqrt(jnp.mean(o * o, axis=-1, keepdims=True) + EPS) * ng_ref[...] * _silu(gate)


def _conv_silu(ext_ref, w_ref, t):
    acc = ext_ref[5:5 + t, :] * w_ref[0:1, :]
    for i in range(1, DN_CONV):
        acc = acc + ext_ref[5 + i:5 + i + t, :] * w_ref[i:i + 1, :]
    return _silu(acc)


def _delta_prompt_body(q_ref, k_ref, v_ref, gt_ref, ba_ref, wq_ref, wk_ref, wv_ref, alog_ref, dtb_ref, ng_ref,
                       y_ref, sout_ref, eq_ref, ek_ref, ev_ref, s_ref, *, nh, dk, nblk):
    c = pl.program_id(1)

    @pl.when(c == 0)
    def _():
        for e in (eq_ref, ek_ref, ev_ref):
            e[0:8, :] = jnp.zeros((8, e.shape[1]), F32)
        s_ref[...] = jnp.zeros(s_ref.shape, F32)

    xs = []
    for x_ref, e_ref, w_ref in ((q_ref, eq_ref, wq_ref), (k_ref, ek_ref, wk_ref), (v_ref, ev_ref, wv_ref)):
        e_ref[8:8 + BLK, :] = x_ref[...]
        xs.append(_conv_silu(e_ref, w_ref, BLK))
        e_ref[0:8, :] = x_ref[BLK - 8:, :]
    lane = lax.broadcasted_iota(jnp.int32, (nh, BLK), 1)
    valid = (c > 0) | (lane >= FRONT)
    outs = _delta_chunk(xs[0], xs[1], xs[2], ba_ref[...], valid, alog_ref, dtb_ref, s_ref, nh, dk)
    row = c * BLK + lax.broadcasted_iota(jnp.int32, (BLK, 1), 0)
    for h in range(nh):
        sl = slice(h * dk, (h + 1) * dk)
        y_ref[:, sl] = jnp.where(row >= FRONT, _delta_out(outs[h], gt_ref[:, sl], ng_ref), 0.0)

    @pl.when(c == nblk - 1)
    def _():
        sout_ref[0] = s_ref[...]


def _delta_prompt(p, ba, w_conv, a_log, dt_bias, norm_g, nseq, nblk, col_q, col_gate, nh, dk):
    hw = nh * dk
    cq, ck, cv, cg = col_q // hw, col_q // hw + 1, col_q // hw + 2, col_gate // hw
    rows = lambda cb: pl.BlockSpec((BLK, hw), lambda b, c: (b * nblk + c, cb))
    wcs = lambda cb: pl.BlockSpec((DN_CONV, hw), lambda b, c: (0, cb))
    small = lambda shape: pl.BlockSpec(shape, lambda b, c: (0, 0))
    return pl.pallas_call(
        functools.partial(_delta_prompt_body, nh=nh, dk=dk, nblk=nblk), grid=(nseq, nblk),
        in_specs=[rows(cq), rows(ck), rows(cv), rows(cg),
                  pl.BlockSpec((BLK, BLK), lambda b, c: (b * nblk + c, 0)),
                  wcs(0), wcs(1), wcs(2), small((nh, 1)), small((nh, 1)), small((1, dk))],
        out_specs=[pl.BlockSpec((BLK, hw), lambda b, c: (b * nblk + c, 0)),
                   pl.BlockSpec((1, nh, dk, dk), lambda b, c: (b, 0, 0, 0))],
        out_shape=[jax.ShapeDtypeStruct((p.shape[0], hw), F32),
                   jax.ShapeDtypeStruct((nseq, nh, dk, dk), F32)],
        scratch_shapes=[pltpu.VMEM((8 + BLK, hw), F32)] * 3 + [pltpu.VMEM((nh, dk, dk), F32)],
        compiler_params=_cparams("parallel", "arbitrary"), name="delta_prompt",
    )(p, p, p, p, ba, w_conv, w_conv, w_conv, a_log.reshape(nh, 1), dt_bias.reshape(nh, 1),
      norm_g.reshape(1, dk))


def _delta_sample_body(q_ref, k_ref, v_ref, gt_ref, ba_ref, hq_ref, hk_ref, hv_ref, s0_ref,
                       wq_ref, wk_ref, wv_ref, alog_ref, dtb_ref, ng_ref, ybuf_ref,
                       y_ref, sout_ref, eq_ref, ek_ref, ev_ref, s_ref, *, nh, dk):
    del ybuf_ref
    xs = []
    for x_ref, h_ref, e_ref, w_ref in ((q_ref, hq_ref, eq_ref, wq_ref), (k_ref, hk_ref, ek_ref, wk_ref),
                                       (v_ref, hv_ref, ev_ref, wv_ref)):
        e_ref[5:8, :] = h_ref[0]
        e_ref[8:8 + DEC_T, :] = x_ref[...]
        xs.append(_conv_silu(e_ref, w_ref, DEC_T))
    s_ref[...] = s0_ref[0]
    outs = _delta_chunk_small(xs[0], xs[1], xs[2], ba_ref[...], alog_ref, dtb_ref, s_ref, nh, dk)
    for h in range(nh):
        sl = slice(h * dk, (h + 1) * dk)
        y_ref[:, sl] = _delta_out(outs[h], gt_ref[:, sl], ng_ref)
    sout_ref[0] = s_ref[...]


def _delta_sample(p, ba, conv_hist, s0, w_conv, a_log, dt_bias, norm_g, ybuf, rows_per_seq,
                  col_q, col_gate, nh, dk):
    nb = s0.shape[0]
    hw = nh * dk
    cq, ck, cv, cg = col_q // hw, col_q // hw + 1, col_q // hw + 2, col_gate // hw
    rowmap = lambda cb: (lambda s: (_sample_row_block(s, rows_per_seq), cb))
    rows = lambda cb: pl.BlockSpec((DEC_T, hw), rowmap(cb))
    hist = lambda cb: pl.BlockSpec((1, DN_CONV - 1, hw), lambda s: (s, 0, cb))
    wcs = lambda cb: pl.BlockSpec((DN_CONV, hw), lambda s: (0, cb))
    small = lambda shape: pl.BlockSpec(shape, lambda s: (0, 0))
    return pl.pallas_call(
        functools.partial(_delta_sample_body, nh=nh, dk=dk), grid=(nb,),
        in_specs=[rows(cq), rows(ck), rows(cv), rows(cg), pl.BlockSpec((DEC_T, BLK), rowmap(0)),
                  hist(0), hist(1), hist(2),
                  pl.BlockSpec((1, nh, dk, dk), lambda s: (s, 0, 0, 0)),
                  wcs(0), wcs(1), wcs(2), small((nh, 1)), small((nh, 1)), small((1, dk)),
                  pl.BlockSpec(memory_space=pl.ANY)],
        out_specs=[pl.BlockSpec((DEC_T, hw), rowmap(0)),
                   pl.BlockSpec((1, nh, dk, dk), lambda s: (s, 0, 0, 0))],
        out_shape=[jax.ShapeDtypeStruct(ybuf.shape, F32), jax.ShapeDtypeStruct(s0.shape, F32)],
        scratch_shapes=[pltpu.VMEM((8 + DEC_T, hw), F32)] * 3 + [pltpu.VMEM((nh, dk, dk), F32)],
        input_output_aliases={15: 0},
        compiler_params=_cparams("arbitrary"), name="delta_sample",
    )(p, p, p, p, ba, conv_hist, conv_hist, conv_hist, s0, w_conv, w_conv, w_conv,
      a_log.reshape(nh, 1), dt_bias.reshape(nh, 1), norm_g.reshape(1, dk), ybuf)


def _attn_prompt_body(*refs, nkv, grp, hd, window):
    q_refs = refs[:nkv]
    kp_ref, kc_ref, vp_ref, vc_ref, slope_ref, sink_ref, y_ref = refs[nkv:]
    n = pl.program_id(1)
    i = lax.broadcasted_iota(jnp.int32, (BLK, 2 * BLK), 0)
    j = lax.broadcasted_iota(jnp.int32, (BLK, 2 * BLK), 1)
    dist = BLK + i - j
    krow = (n - 1) * BLK + j
    valid = (dist >= 0) & (dist < window) & (krow >= FRONT)
    distf = dist.astype(F32)
    kk = jnp.concatenate([kp_ref[...], kc_ref[...]], axis=0).astype(BF16)
    vv = jnp.concatenate([vp_ref[...], vc_ref[...]], axis=0).astype(BF16)
    heads = [(kv, g) for kv in range(nkv) for g in range(grp)]
    each = lambda f: [f(t, kv, g) for t, (kv, g) in enumerate(heads)]
    s = each(lambda t, kv, g: _bdot_nt(q_refs[kv][:, g * hd:(g + 1) * hd], kk[:, kv * hd:(kv + 1) * hd]))
    s = each(lambda t, kv, g: jnp.where(valid, s[t] * (hd ** -0.5) - slope_ref[t] * distf, NEG_INF))
    m = each(lambda t, kv, g: jnp.maximum(jnp.max(s[t], axis=-1, keepdims=True), sink_ref[t]))
    pr = each(lambda t, kv, g: jnp.exp(s[t] - m[t]))
    den = each(lambda t, kv, g: jnp.sum(pr[t], axis=-1, keepdims=True) + jnp.exp(sink_ref[t] - m[t]))
    pv = each(lambda t, kv, g: _bdot(pr[t], vv[:, kv * hd:(kv + 1) * hd]))
    for t in range(len(heads)):
        y_ref[:, t * hd:(t + 1) * hd] = pv[t] / den[t]


def _attn_prompt(p, slopes, sinks, nseq, nblk, col_q, col_k, col_v, nkv, grp, hd, window):
    gw, kw = grp * hd, nkv * hd
    prev = lambda cb: (lambda b, n: (b * nblk + jnp.maximum(n - 1, 0), cb))
    cur = lambda cb: (lambda b, n: (b * nblk + n, cb))
    smem = pl.BlockSpec(memory_space=pltpu.SMEM)
    return pl.pallas_call(
        functools.partial(_attn_prompt_body, nkv=nkv, grp=grp, hd=hd, window=window), grid=(nseq, nblk),
        in_specs=[pl.BlockSpec((BLK, gw), cur(col_q // gw + kv)) for kv in range(nkv)]
        + [pl.BlockSpec((BLK, kw), prev(col_k // kw)), pl.BlockSpec((BLK, kw), cur(col_k // kw)),
           pl.BlockSpec((BLK, kw), prev(col_v // kw)), pl.BlockSpec((BLK, kw), cur(col_v // kw)),
           smem, smem],
        out_specs=pl.BlockSpec((BLK, nkv * gw), cur(0)),
        out_shape=jax.ShapeDtypeStruct((p.shape[0], nkv * gw), F32),
        compiler_params=_cparams("parallel", "arbitrary"), name="attn_prompt",
    )(*([p] * (nkv + 4)), slopes, sinks)


def _attn_sample_body(*refs, nkv, grp, hd, window):
    q_refs = refs[:nkv]
    k_ref, v_ref, ck_ref, cv_ref, slope_ref, sink_ref, ybuf_ref, y_ref = refs[nkv:]
    del ybuf_ref
    wc = ck_ref.shape[1]
    i = lax.broadcasted_iota(jnp.int32, (DEC_T, wc), 0)
    j = lax.broadcasted_iota(jnp.int32, (DEC_T, wc), 1)
    dist_c = wc + i - j
    valid_c = (dist_c >= 0) & (dist_c < window)
    i2 = lax.broadcasted_iota(jnp.int32, (DEC_T, DEC_T), 0)
    j2 = lax.broadcasted_iota(jnp.int32, (DEC_T, DEC_T), 1)
    dist_n = i2 - j2
    valid_n = (dist_n >= 0) & (dist_n < window)
    tile = lambda a: jnp.concatenate([a] * grp, axis=0)
    valid_c, valid_n = tile(valid_c), tile(valid_n)
    dist_c, dist_n = tile(dist_c).astype(F32), tile(dist_n).astype(F32)
    gi = lax.broadcasted_iota(jnp.int32, (grp * DEC_T, 1), 0) // DEC_T

    def per_row(ref, kv):
        out = jnp.full((grp * DEC_T, 1), ref[kv * grp], F32)
        for g in range(1, grp):
            out = jnp.where(gi == g, ref[kv * grp + g], out)
        return out

    each = lambda f: [f(kv) for kv in range(nkv)]
    cols = lambda ref, kv: ref[:, kv * hd:(kv + 1) * hd]
    q = each(lambda kv: jnp.concatenate([q_refs[kv][:, g * hd:(g + 1) * hd] for g in range(grp)], axis=0))
    slope = each(lambda kv: per_row(slope_ref, kv))
    sink = each(lambda kv: per_row(sink_ref, kv))
    sc = each(lambda kv: _bdot_nt(q[kv], ck_ref[0, :, kv * hd:(kv + 1) * hd]) * (hd ** -0.5))
    sn = each(lambda kv: _bdot_nt(q[kv], cols(k_ref, kv)) * (hd ** -0.5))
    sc = each(lambda kv: jnp.where(valid_c, sc[kv] - slope[kv] * dist_c, NEG_INF))
    sn = each(lambda kv: jnp.where(valid_n, sn[kv] - slope[kv] * dist_n, NEG_INF))
    m = each(lambda kv: jnp.maximum(jnp.maximum(jnp.max(sc[kv], axis=-1, keepdims=True),
                                                jnp.max(sn[kv], axis=-1, keepdims=True)), sink[kv]))
    pc = each(lambda kv: jnp.exp(sc[kv] - m[kv]))
    pn = each(lambda kv: jnp.exp(sn[kv] - m[kv]))
    den = each(lambda kv: jnp.sum(pc[kv], axis=-1, keepdims=True) + jnp.sum(pn[kv], axis=-1, keepdims=True)
               + jnp.exp(sink[kv] - m[kv]))
    o = each(lambda kv: (_bdot(pc[kv], cv_ref[0, :, kv * hd:(kv + 1) * hd]) + _bdot(pn[kv], cols(v_ref, kv)))
             / den[kv])
    for kv in range(nkv):
        for g in range(grp):
            head = kv * grp + g
            y_ref[:, head * hd:(head + 1) * hd] = o[kv][g * DEC_T:(g + 1) * DEC_T, :]


def _attn_sample(p, cache_k, cache_v, slopes, sinks, ybuf, rows_per_seq, col_q, col_k, col_v,
                 nkv, grp, hd, window):
    nb, wc = cache_k.shape[0], cache_k.shape[1]
    gw, kw = grp * hd, nkv * hd
    ck3 = cache_k.reshape(nb, wc, kw)
    cv3 = cache_v.reshape(nb, wc, kw)
    rowmap = lambda cb: (lambda s: (_sample_row_block(s, rows_per_seq), cb))
    smem = pl.BlockSpec(memory_space=pltpu.SMEM)
    return pl.pallas_call(
        functools.partial(_attn_sample_body, nkv=nkv, grp=grp, hd=hd, window=window), grid=(nb,),
        in_specs=[pl.BlockSpec((DEC_T, gw), rowmap(col_q // gw + kv)) for kv in range(nkv)]
        + [pl.BlockSpec((DEC_T, kw), rowmap(col_k // kw)),
           pl.BlockSpec((DEC_T, kw), rowmap(col_v // kw)),
           pl.BlockSpec((1, wc, kw), lambda s: (s, 0, 0)),
           pl.BlockSpec((1, wc, kw), lambda s: (s, 0, 0)),
           smem, smem, pl.BlockSpec(memory_space=pl.ANY)],
        out_specs=pl.BlockSpec((DEC_T, nkv * gw), rowmap(0)),
        out_shape=jax.ShapeDtypeStruct(ybuf.shape, F32),
        input_output_aliases={nkv + 6: 0},
        compiler_params=_cparams("arbitrary"), name="attn_sample",
    )(*([p] * (nkv + 2)), ck3, cv3, slopes, sinks, ybuf)


SUB = 8


def _bitonic_pairs(n, merge_only=False):
    out = []
    k = n if merge_only else 2
    while k <= n:
        j = k // 2
        while j >= 1:
            out += [(i, i ^ j, (i & k) == 0) for i in range(n) if (i ^ j) > i]
            j //= 2
        k *= 2
    return out


def _compare_exchange(v, pairs):
    v = list(v)
    for i, l, desc in pairs:
        hi, lo = jnp.maximum(v[i], v[l]), jnp.minimum(v[i], v[l])
        v[i], v[l] = (hi, lo) if desc else (lo, hi)
    return v


def _top16(rows):
    v = _compare_exchange(rows, _bitonic_pairs(TOPK))
    shift = SUB // 2
    while shift >= 1:
        w = [pltpu.roll(v[TOPK - 1 - r], shift, axis=0) for r in range(TOPK)]
        v = _compare_exchange([jnp.maximum(a, b) for a, b in zip(v, w)], _bitonic_pairs(TOPK, merge_only=True))
        shift //= 2
    return v


def _sublane_sum(x):
    shift = SUB // 2
    while shift >= 1:
        x = x + pltpu.roll(x, shift, axis=0)
        shift //= 2
    return x


def _on_sublanes(vs):
    sub = lax.broadcasted_iota(jnp.int32, vs[0].shape, 0)
    out = vs[SUB - 1]
    for j in range(SUB - 2, -1, -1):
        out = jnp.where(sub == j, vs[j], out)
    return out


def _peer_topk_body(q_ref, keys_ref, rk_ref, cut_ref, e1_ref, e2_ref, *, nh):
    nv = NKEYS // SUB
    for h in range(nh):
        sc = [_bdot_nt(keys_ref[2 * h + half], q_ref[:, (2 * h + half) * NKEYS:(2 * h + half + 1) * NKEYS])
              for half in (0, 1)]
        s1 = [sc[0][SUB * i:SUB * (i + 1), :] for i in range(nv)]
        s2 = [sc[1][SUB * i:SUB * (i + 1), :] for i in range(nv)]
        a = _top16(s1)
        b = _top16(s2)
        b_lo, b_hi, a_hi = _on_sublanes(b[:SUB]), _on_sublanes(b[SUB:]), _on_sublanes(a[SUB:])
        cand = [a[0] + b_lo, a[0] + b_hi, a_hi + b[0]] + [a[i] + b_lo for i in range(1, SUB)]
        cand += [jnp.full(cand[0].shape, LOWEST, F32)] * (TOPK - len(cand))
        top = _top16(cand)
        thr = top[TOPK - 1]
        zsum = jnp.exp(top[0] - top[0])
        for r in range(1, TOPK):
            zsum = zsum + jnp.exp(top[r] - top[0])
        rz = 1.0 / zsum
        height = [_sublane_sum(jnp.where(a[r] + b_lo >= thr, 1.0, 0.0) + jnp.where(a[r] + b_hi >= thr, 1.0, 0.0))
                  for r in range(TOPK)]
        cut, rank2 = [], []
        for i in range(nv):
            c = jnp.zeros(s1[i].shape, F32)
            for r in range(TOPK - 1, -1, -1):
                c = jnp.where(s1[i] == a[r], height[r], c)
            cut.append(c)
            k = jnp.where(b[0] > s2[i], 1.0, 0.0)
            for r in range(1, TOPK):
                k = k + jnp.where(b[r] > s2[i], 1.0, 0.0)
            rank2.append(k)
        rk_ref[h] = jnp.concatenate(rank2, axis=0).astype(BF16)
        cut_ref[h] = jnp.concatenate(cut, axis=0)
        e1_ref[h] = jnp.exp(sc[0] - a[0][0:1, :])
        e2_ref[h] = (jnp.exp(sc[1] - b[0][0:1, :]) * rz[0:1, :]).astype(BF16)


def _peer_topk(q, sub_keys, *, tm=128):
    n = q.shape[0]
    nh = sub_keys.shape[0]
    keys = sub_keys.reshape(2 * nh, NKEYS, sub_keys.shape[-1])
    sspec = pl.BlockSpec((nh, NKEYS, tm), lambda i: (0, 0, i))
    shape = lambda dt: jax.ShapeDtypeStruct((nh, NKEYS, n), dt)
    return pl.pallas_call(
        functools.partial(_peer_topk_body, nh=nh), grid=(n // tm,),
        in_specs=[pl.BlockSpec((tm, q.shape[1]), lambda i: (i, 0)),
                  pl.BlockSpec(keys.shape, lambda i: (0, 0, 0))],
        out_specs=[sspec] * 4,
        out_shape=[shape(BF16), shape(F32), shape(F32), shape(BF16)],
        compiler_params=_cparams("parallel"), name="peer_topk")(q, keys)


def _gelu(x):
    return 0.5 * x * (1.0 + lax.erf(x * (2.0 ** -0.5)))


PACK = 16


def _peer_expert_body(xn_ref, wd_ref, wu_ref, id_ref, iu_ref, rk_ref, cut_ref, e1_ref, e2_ref, y_ref,
                      xt_ref, ix_ref, *, nh, ei, sub):
    c = pl.program_id(1)
    tm = xt_ref.shape[1]
    tw = tm // sub
    slabs = [slice(t * tw, (t + 1) * tw) for t in range(sub)]
    each = lambda f: [f(t, tl) for t, tl in enumerate(slabs)]

    @pl.when(c == 0)
    def _():
        y_ref[...] = jnp.zeros(y_ref.shape, F32)
        x = xn_ref[...].astype(F32)
        sx = _amax_scale(x)
        xt_ref[...] = (x * sx).T.astype(FP8)
        ix_ref[...] = jnp.broadcast_to(1.0 / sx, ix_ref.shape)

    def gates(tl):
        def row16(ref, h, i1):
            return jnp.broadcast_to(ref[h, pl.ds(i1, 1), tl], (PACK, tw)).astype(BF16)
        pieces = []
        for ii in range(ei):
            i1 = c * ei + ii
            cut = [row16(cut_ref, h, i1) for h in range(nh)]
            e1 = [row16(e1_ref, h, i1) for h in range(nh)]
            for r in range(NKEYS // PACK):
                sl = slice(r * PACK, (r + 1) * PACK)
                gate = pieces[-1] * 0.0 if pieces else None
                for h in range(nh):
                    w = jnp.where(rk_ref[h, sl, tl] < cut[h], e1[h] * e2_ref[h, sl, tl], 0.0)
                    gate = w if gate is None else gate + w
                pieces.append(gate)
        return pieces

    inv_h = id_ref[0, 0:1, 0:1] * ix_ref[0:1, 0:1]
    inv_u = iu_ref[0, 0:1, 0:1]
    ht = each(lambda t, tl: jnp.dot(wd_ref[...], xt_ref[:, tl], preferred_element_type=F32) * inv_h)
    gate = each(lambda t, tl: gates(tl))
    act = each(lambda t, tl: _gelu(ht[t]))
    at = each(lambda t, tl: jnp.concatenate(
        [act[t][i * PACK:(i + 1) * PACK, :] * g.astype(F32) for i, g in enumerate(gate[t])], axis=0))
    sa = each(lambda t, tl: _amax_scale(at[t]))
    part = each(lambda t, tl: jnp.dot((at[t] * sa[t]).T.astype(FP8), wu_ref[...], preferred_element_type=F32))
    for t, tl in enumerate(slabs):
        y_ref[tl, :] += part[t] * (inv_u / sa[t])


PEER_TM, PEER_EI, PEER_SUB = 512, 8, 2


def _peer_expert(xn, w_down, w_up, inv_down, inv_up, rank2, cut, e1, e2, *, tm=PEER_TM, ei=PEER_EI, sub=PEER_SUB):
    n, d = xn.shape
    nh = rank2.shape[0]
    e = ei * NKEYS
    once = dict(pipeline_mode=pl.Buffered(1))
    sspec = pl.BlockSpec((nh, NKEYS, tm), lambda i, c: (0, 0, i), **once)
    inv = pl.BlockSpec((1, 8, BLK), lambda i, c: (c, 0, 0))
    return pl.pallas_call(
        functools.partial(_peer_expert_body, nh=nh, ei=ei, sub=sub), grid=(n // tm, w_down.shape[0] // e),
        in_specs=[pl.BlockSpec((tm, d), lambda i, c: (i, 0), **once),
                  pl.BlockSpec((e, d), lambda i, c: (c, 0)),
                  pl.BlockSpec((e, d), lambda i, c: (c, 0)),
                  inv, inv, sspec, sspec, sspec, sspec],
        out_specs=pl.BlockSpec((tm, d), lambda i, c: (i, 0), **once),
        out_shape=jax.ShapeDtypeStruct((n, d), F32),
        scratch_shapes=[pltpu.VMEM((d, tm), FP8), pltpu.VMEM((8, BLK), F32)],
        compiler_params=_cparams("parallel", "arbitrary"), name="peer_expert",
    )(xn, w_down, w_up, inv_down, inv_up, rank2, cut, e1, e2)


def _final_norm(h, y, g, nseq, nblk, row_blk, first_blk, nblk_out):
    d = h.shape[1]
    row = pl.BlockSpec((row_blk, d), lambda b, i: (b * nblk + first_blk + i, 0))
    return pl.pallas_call(
        _final_norm_body, grid=(nseq, nblk_out),
        in_specs=[row, row, pl.BlockSpec((1, d), lambda b, i: (0, 0))],
        out_specs=pl.BlockSpec((1, row_blk, d), lambda b, i: (b, i, 0)),
        out_shape=jax.ShapeDtypeStruct((nseq, nblk_out * row_blk, d), F32),
        compiler_params=_cparams("parallel", "parallel"), name="final_norm",
    )(h, y, g.reshape(1, d).astype(F32))


def _final_norm_body(h_ref, y_ref, g_ref, o_ref):
    x = h_ref[...] + y_ref[...]
    ms = jnp.mean(x * x, axis=-1, keepdims=True)
    o_ref[0] = x * lax.rsqrt(ms + EPS) * g_ref[...]


def kernel(x_prompt, x_sample, state_pool, state_conv, state_delta, cache_k, cache_v, meta_tokens, norm1_g,
           w_in, w_pool, s_pool, w_conv, a_log, dt_bias, dn_norm_g, attn_sinks, w_out, norm2_g,
           peer_w_query, peer_sub_keys, peer_w_down, peer_w_up, final_norm_g):
    nseq, seq, d = x_prompt.shape
    nsamp, dec_t, _ = x_sample.shape
    depth = w_in.shape[0]
    pool_w = w_pool.shape[1] * w_pool.shape[2]
    dn_qkv = w_conv.shape[2]
    nh, dk = state_delta.shape[2], state_delta.shape[3]
    wc, nkv, hd = cache_k.shape[2], cache_k.shape[3], cache_k.shape[4]
    nq = attn_sinks.shape[1]
    grp = nq // nkv
    window = wc
    assert dec_t == DEC_T and nsamp == nseq * SLOTS and (N_META + seq) % BLK == N_META
    assert SLOTS * DEC_T + POOL_HIST <= FRONT and dn_qkv == 3 * nh * dk and wc == BLK
    rows_per_seq = FRONT + N_META + seq
    nblk = rows_per_seq // BLK
    past_len = 16384
    col_qkv = pool_w
    col_gate = col_qkv + dn_qkv
    col_q = col_gate + nh * dk
    col_k = col_q + nq * hd
    col_v = col_k + nkv * hd
    src_ba = pool_w + dn_qkv

    xs = x_sample.reshape(nseq, SLOTS * DEC_T, d)
    zeros = jnp.zeros((nseq, FRONT - SLOTS * DEC_T, d), F32)
    meta = jnp.broadcast_to(meta_tokens[None], (nseq, N_META, d))
    h = jnp.concatenate([xs, zeros, meta, x_prompt], axis=1).reshape(nseq * rows_per_seq, d)

    slopes = jnp.exp2(-8.0 * (jnp.arange(nq, dtype=F32) + 1.0) / nq)
    new_p = [[] for _ in range(5)]
    new_s = [[] for _ in range(5)]
    y_peer = None
    w_in_t = jnp.swapaxes(w_in, 1, 2)
    for l in range(depth):
        w_main, w_ba = _cast_w_in(w_in_t, l, src_ba, 2 * nh)
        h, xn = _addnorm(h, y_peer, norm1_g[l])
        p = _mm(xn, w_main, nt=True)
        ba = _mm(xn, w_ba, nt=True)

        y_pool = _pool_prompt(p, w_pool[l], s_pool[l].reshape(1, pool_w), nseq, nblk)
        y_pool = _pool_sample(p, state_pool[l], w_pool[l], s_pool[l].reshape(1, pool_w), y_pool,
                              rows_per_seq, past_len)
        y_dn, s_p = _delta_prompt(p, ba, w_conv[l], a_log[l], dt_bias[l], dn_norm_g[l], nseq, nblk,
                                  col_qkv, col_gate, nh, dk)
        y_dn, s_s = _delta_sample(p, ba, state_conv[l], state_delta[l], w_conv[l], a_log[l], dt_bias[l],
                                  dn_norm_g[l], y_dn, rows_per_seq, col_qkv, col_gate, nh, dk)
        y_att = _attn_prompt(p, slopes, attn_sinks[l], nseq, nblk, col_q, col_k, col_v, nkv, grp, hd, window)
        y_att = _attn_sample(p, cache_k[l], cache_v[l], slopes, attn_sinks[l], y_att, rows_per_seq,
                             col_q, col_k, col_v, nkv, grp, hd, window)
        h = _outproj(y_pool, y_dn, y_att, _cast(w_out, l), h)

        _, xn2 = _addnorm(h, None, norm2_g[l])
        q = _mm(xn2, _cast(peer_w_query, l))
        rank2, cut, e1, e2 = _peer_topk(q, peer_sub_keys[l])
        wd8, inv_d = _quant_fp8(peer_w_down, l, tr=PEER_EI * NKEYS)
        wu8, inv_u = _quant_fp8(peer_w_up, l, tr=PEER_EI * NKEYS)
        y_peer = _peer_expert(xn2, wd8, wu8, inv_d, inv_u, rank2, cut, e1, e2)

        p3 = p.reshape(nseq, rows_per_seq, p.shape[1])
        ps = p3[:, :SLOTS * DEC_T].reshape(nsamp, DEC_T, p.shape[1])
        new_p[0].append(p3[:, -POOL_HIST:, :pool_w])
        new_s[0].append(jnp.concatenate([state_pool[l], ps[:, :, :pool_w]], axis=1)[:, -POOL_HIST:])
        new_p[1].append(p3[:, -(DN_CONV - 1):, col_qkv:col_gate])
        new_s[1].append(jnp.concatenate([state_conv[l], ps[:, :, col_qkv:col_gate]], axis=1)[:, -(DN_CONV - 1):])
        new_p[2].append(s_p)
        new_s[2].append(s_s)
        new_p[3].append(p3[:, -window:, col_k:col_v].reshape(nseq, window, nkv, hd))
        new_s[3].append(jnp.concatenate([cache_k[l], ps[:, :, col_k:col_v].reshape(nsamp, DEC_T, nkv, hd)],
                                        axis=1)[:, -wc:])
        new_p[4].append(p3[:, -window:, col_v:].reshape(nseq, window, nkv, hd))
        new_s[4].append(jnp.concatenate([cache_v[l], ps[:, :, col_v:].reshape(nsamp, DEC_T, nkv, hd)],
                                        axis=1)[:, -wc:])

    y_prompt = _final_norm(h, y_peer, final_norm_g, nseq, nblk, BLK, 1, nblk - 1)
    y_sample = _final_norm(h, y_peer, final_norm_g, nseq, rows_per_seq // (SLOTS * DEC_T), SLOTS * DEC_T, 0, 1)
    y_sample = y_sample.reshape(nsamp, DEC_T, d)
    pool_p, conv_p, delta_p, k_p, v_p = (jnp.stack(a) for a in new_p)
    pool_s, conv_s, delta_s, k_s, v_s = (jnp.stack(a) for a in new_s)
    return (y_prompt, y_sample, pool_p, pool_s, conv_p, conv_s, delta_p, delta_s, k_p, k_s, v_p, v_s)
```

```python
import functools

import jax
import jax.numpy as jnp
from jax import lax
from jax.experimental import pallas as pl
from jax.experimental.pallas import tpu as pltpu

F32 = jnp.float32
BF16 = jnp.bfloat16

EPS = 1e-6
NEG_INF = -1e30
LOWEST = -3.0e38

PAST_LEN = 16384
N_META = 16
BLK = 128
FRONT = BLK - N_META
DEC_T = 8
SLOTS = 8
POOL_WINDOWS = (2, 4, 8, 16)
POOL_HIST = 15
DN_CONV = 4
TOPK = 16
NKEYS = 128
VMEM_LIMIT = 56 * 1024 * 1024


def _cparams(*sem):
    return pltpu.CompilerParams(dimension_semantics=sem, vmem_limit_bytes=VMEM_LIMIT)


def _bdot(a, b):
    return jnp.dot(a.astype(BF16), b.astype(BF16), preferred_element_type=F32)


def _bdot_nt(a, b):
    return lax.dot_general(a.astype(BF16), b.astype(BF16), (((1,), (1,)), ((), ())),
                           preferred_element_type=F32)


def _silu(x):
    return x * (1.0 / (1.0 + jnp.exp(-x)))


def _addnorm_body(*refs, add):
    if add:
        h_ref, y_ref, g_ref, hs_ref, xn_ref = refs
        x = h_ref[...] + y_ref[...]
        hs_ref[...] = x
    else:
        h_ref, g_ref, xn_ref = refs
        x = h_ref[...]
    ms = jnp.mean(x * x, axis=-1, keepdims=True)
    xn_ref[...] = (x * lax.rsqrt(ms + EPS) * g_ref[...]).astype(xn_ref.dtype)


def _addnorm(h, y, g, *, tm=256, out_dtype=BF16):
    n, d = h.shape
    row = pl.BlockSpec((tm, d), lambda i: (i, 0))
    gspec = pl.BlockSpec((1, d), lambda i: (0, 0))
    g2 = g.reshape(1, d).astype(F32)
    if y is None:
        xn = pl.pallas_call(
            functools.partial(_addnorm_body, add=False),
            grid=(n // tm,), in_specs=[row, gspec], out_specs=row,
            out_shape=jax.ShapeDtypeStruct((n, d), out_dtype),
            compiler_params=_cparams("parallel"), name="norm")(h, g2)
        return h, xn
    hs, xn = pl.pallas_call(
        functools.partial(_addnorm_body, add=True),
        grid=(n // tm,), in_specs=[row, row, gspec], out_specs=[row, row],
        out_shape=[jax.ShapeDtypeStruct((n, d), F32), jax.ShapeDtypeStruct((n, d), out_dtype)],
        compiler_params=_cparams("parallel"), name="add_norm")(h, y, g2)
    return hs, xn


def _assemble_norm_body(xs_ref, meta_ref, xp_ref, g_ref, h_ref, xn_ref):
    n = pl.program_id(1)

    @pl.when(n == 0)
    def _():
        ns = xs_ref.shape[1]
        h_ref[0:ns, :] = xs_ref[0]
        h_ref[ns:FRONT, :] = jnp.zeros((FRONT - ns, h_ref.shape[1]), F32)
        h_ref[FRONT:, :] = meta_ref[...]

    @pl.when(n > 0)
    def _():
        h_ref[...] = xp_ref[0]

    x = h_ref[...]
    ms = jnp.mean(x * x, axis=-1, keepdims=True)
    xn_ref[...] = (x * lax.rsqrt(ms + EPS) * g_ref[...]).astype(xn_ref.dtype)


def _assemble_norm(x_prompt, x_sample, meta_tokens, g, nblk):
    nseq, _, d = x_prompt.shape
    xs = x_sample.reshape(nseq, SLOTS * DEC_T, d)
    out = pl.BlockSpec((BLK, d), lambda b, n: (b * nblk + n, 0))
    return pl.pallas_call(
        _assemble_norm_body, grid=(nseq, nblk),
        in_specs=[pl.BlockSpec((1, SLOTS * DEC_T, d), lambda b, n: (b, 0, 0)),
                  pl.BlockSpec((N_META, d), lambda b, n: (0, 0)),
                  pl.BlockSpec((1, BLK, d), lambda b, n: (b, jnp.maximum(n - 1, 0), 0)),
                  pl.BlockSpec((1, d), lambda b, n: (0, 0))],
        out_specs=[out, out],
        out_shape=[jax.ShapeDtypeStruct((nseq * nblk * BLK, d), F32),
                   jax.ShapeDtypeStruct((nseq * nblk * BLK, d), BF16)],
        compiler_params=_cparams("parallel", "arbitrary"), name="assemble_norm",
    )(xs, meta_tokens, x_prompt, g.reshape(1, d).astype(F32))


def _cast_body(x_ref, o_ref):
    o_ref[...] = x_ref[0].astype(o_ref.dtype)


def _cast(x, l, dtype=BF16, *, tr=512):
    _, r, c = x.shape
    return pl.pallas_call(
        _cast_body, grid=(r // tr,),
        in_specs=[pl.BlockSpec((1, tr, c), lambda i: (l, i, 0))],
        out_specs=pl.BlockSpec((tr, c), lambda i: (i, 0)),
        out_shape=jax.ShapeDtypeStruct((r, c), dtype),
        compiler_params=_cparams("parallel"), name="cast")(x)


FP8 = jnp.float8_e4m3fn
FP8_TARGET = 224.0


def _amax_scale(x):
    a = jnp.max(jnp.max(jnp.abs(x), axis=1, keepdims=True), axis=0, keepdims=True)
    return jnp.where(a > 0.0, FP8_TARGET / a, 1.0)


def _quant_body(x_ref, o_ref, inv_ref):
    x = x_ref[0]
    s = _amax_scale(x)
    o_ref[...] = (x * s).astype(FP8)
    inv_ref[0] = jnp.broadcast_to(1.0 / s, inv_ref.shape[1:])


def _quant_fp8(x, l, *, tr):
    _, r, c = x.shape
    return pl.pallas_call(
        _quant_body, grid=(r // tr,),
        in_specs=[pl.BlockSpec((1, tr, c), lambda i: (l, i, 0))],
        out_specs=[pl.BlockSpec((tr, c), lambda i: (i, 0)), pl.BlockSpec((1, 8, BLK), lambda i: (i, 0, 0))],
        out_shape=[jax.ShapeDtypeStruct((r, c), FP8), jax.ShapeDtypeStruct((r // tr, 8, BLK), F32)],
        compiler_params=_cparams("parallel"), name="quant_fp8")(x)


def _cast_w_in_body(a_ref, b_ref, main_ref, ba_ref, *, nlo, nba):
    j = pl.program_id(0)
    tr = a_ref.shape[1]

    @pl.when(j < nlo)
    def _():
        main_ref[...] = a_ref[0].astype(BF16)

    @pl.when(j >= nlo)
    def _():
        main_ref[0:tr - nba, :] = a_ref[0, nba:, :].astype(BF16)
        main_ref[tr - nba:, :] = b_ref[0].astype(BF16)

    @pl.when(j == nlo)
    def _():
        ba_ref[0:nba, :] = a_ref[0, 0:nba, :].astype(BF16)
        ba_ref[nba:, :] = jnp.zeros((ba_ref.shape[0] - nba, ba_ref.shape[1]), BF16)


def _cast_w_in(w_in_t, l, lo, nba, *, tr=1024):
    _, r, c = w_in_t.shape
    assert lo % tr == 0 and (r - nba) % tr == 0 and tr % nba == 0 and nba % 8 == 0
    return pl.pallas_call(
        functools.partial(_cast_w_in_body, nlo=lo // tr, nba=nba), grid=((r - nba) // tr,),
        in_specs=[pl.BlockSpec((1, tr, c), lambda j: (l, j, 0)),
                  pl.BlockSpec((1, nba, c), lambda j: (l, (j + 1) * (tr // nba), 0))],
        out_specs=[pl.BlockSpec((tr, c), lambda j: (j, 0)), pl.BlockSpec((BLK, c), lambda j: (0, 0))],
        out_shape=[jax.ShapeDtypeStruct((r - nba, c), BF16), jax.ShapeDtypeStruct((BLK, c), BF16)],
        compiler_params=_cparams("arbitrary"), name="cast_w_in")(w_in_t, w_in_t)


def _mm_body(x_ref, w_ref, o_ref, *, nt):
    if nt:
        o_ref[...] = lax.dot_general(x_ref[...], w_ref[...], (((1,), (1,)), ((), ())),
                                     preferred_element_type=F32)
    else:
        o_ref[...] = jnp.dot(x_ref[...], w_ref[...], preferred_element_type=F32)


def _mm(x, w, *, nt=False, tm=512, tn=1024):
    m, k = x.shape
    n = w.shape[0] if nt else w.shape[1]
    tn = min(tn, n)
    wspec = pl.BlockSpec((tn, k), lambda j, i: (j, 0)) if nt else pl.BlockSpec((k, tn), lambda j, i: (0, j))
    return pl.pallas_call(
        functools.partial(_mm_body, nt=nt), grid=(n // tn, m // tm),
        in_specs=[pl.BlockSpec((tm, k), lambda j, i: (i, 0)), wspec],
        out_specs=pl.BlockSpec((tm, tn), lambda j, i: (i, j)),
        out_shape=jax.ShapeDtypeStruct((m, n), F32),
        compiler_params=_cparams("parallel", "parallel"), name="matmul")(x, w)


def _outproj_body(yp_ref, yd_ref, ya_ref, w_ref, h_ref, o_ref, *, wp, wd):
    acc = h_ref[...]
    acc += jnp.dot(yp_ref[...].astype(BF16), w_ref[0:wp, :], preferred_element_type=F32)
    acc += jnp.dot(yd_ref[...].astype(BF16), w_ref[wp:wp + wd, :], preferred_element_type=F32)
    acc += jnp.dot(ya_ref[...].astype(BF16), w_ref[wp + wd:, :], preferred_element_type=F32)
    o_ref[...] = acc


def _outproj(yp, yd, ya, w, h, *, tm=512, tn=1024):
    m, d = h.shape
    wp, wd, wa = yp.shape[1], yd.shape[1], ya.shape[1]
    k = wp + wd + wa
    return pl.pallas_call(
        functools.partial(_outproj_body, wp=wp, wd=wd), grid=(d // tn, m // tm),
        in_specs=[pl.BlockSpec((tm, wp), lambda j, i: (i, 0)),
                  pl.BlockSpec((tm, wd), lambda j, i: (i, 0)),
                  pl.BlockSpec((tm, wa), lambda j, i: (i, 0)),
                  pl.BlockSpec((k, tn), lambda j, i: (0, j)),
                  pl.BlockSpec((tm, tn), lambda j, i: (i, j))],
        out_specs=pl.BlockSpec((tm, tn), lambda j, i: (i, j)),
        out_shape=jax.ShapeDtypeStruct((m, d), F32),
        compiler_params=_cparams("parallel", "parallel"), name="out_proj")(yp, yd, ya, w, h)


def _pool_windows(ext_ref, u, t, pos, w_ref, s_ref, gw):
    outs = []
    for gi, w in enumerate(POOL_WINDOWS):
        sl = slice(gi * gw, (gi + 1) * gw)
        win = u[:, sl]
        for k in range(1, w):
            win = win + ext_ref[16 - k:16 - k + t, sl]
        cnt = jnp.clip(pos + 1, 1, w).astype(F32)
        d = win / cnt - u[:, sl]
        outs.append(_bdot(d, w_ref[gi]))
    return jnp.concatenate(outs, axis=-1) * s_ref[...]


def _pool_prompt_body(u_ref, w_ref, s_ref, y_ref, ext_ref, *, gw):
    n = pl.program_id(1)

    @pl.when(n == 0)
    def _():
        ext_ref[0:16, :] = jnp.zeros((16, ext_ref.shape[1]), F32)

    u = u_ref[...]
    ext_ref[16:16 + BLK, :] = u
    row = n * BLK + lax.broadcasted_iota(jnp.int32, (BLK, 1), 0)
    pos = row - FRONT
    y = _pool_windows(ext_ref, u, BLK, pos, w_ref, s_ref, gw)
    y_ref[...] = jnp.where(pos >= 0, y, 0.0)
    ext_ref[0:16, :] = u[BLK - 16:, :]


def _pool_prompt(p, w_pool, s_pool, nseq, nblk):
    pw = w_pool.shape[0] * w_pool.shape[1]
    gw = w_pool.shape[1]
    return pl.pallas_call(
        functools.partial(_pool_prompt_body, gw=gw), grid=(nseq, nblk),
        in_specs=[pl.BlockSpec((BLK, pw), lambda b, n: (b * nblk + n, 0)),
                  pl.BlockSpec(w_pool.shape, lambda b, n: (0, 0, 0)),
                  pl.BlockSpec((1, pw), lambda b, n: (0, 0))],
        out_specs=pl.BlockSpec((BLK, pw), lambda b, n: (b * nblk + n, 0)),
        out_shape=jax.ShapeDtypeStruct((p.shape[0], pw), F32),
        scratch_shapes=[pltpu.VMEM((16 + BLK, pw), F32)],
        compiler_params=_cparams("parallel", "arbitrary"), name="pool_prompt")(p, w_pool, s_pool)


def _pool_sample_body(u_ref, hist_ref, w_ref, s_ref, ybuf_ref, y_ref, ext_ref, *, gw, pos0):
    del ybuf_ref
    u = u_ref[...]
    ext_ref[0:1, :] = jnp.zeros((1, ext_ref.shape[1]), F32)
    ext_ref[1:16, :] = hist_ref[0]
    ext_ref[16:16 + DEC_T, :] = u
    pos = pos0 + lax.broadcasted_iota(jnp.int32, (DEC_T, 1), 0)
    y_ref[...] = _pool_windows(ext_ref, u, DEC_T, pos, w_ref, s_ref, gw)


def _sample_row_block(s, rows_per_seq):
    return (s // SLOTS) * (rows_per_seq // DEC_T) + s % SLOTS


def _pool_sample(p, hist, w_pool, s_pool, ybuf, rows_per_seq, pos0):
    nb = hist.shape[0]
    pw = hist.shape[2]
    gw = w_pool.shape[1]
    rowmap = lambda s: (_sample_row_block(s, rows_per_seq), 0)
    return pl.pallas_call(
        functools.partial(_pool_sample_body, gw=gw, pos0=pos0), grid=(nb,),
        in_specs=[pl.BlockSpec((DEC_T, pw), rowmap),
                  pl.BlockSpec((1, POOL_HIST, pw), lambda s: (s, 0, 0)),
                  pl.BlockSpec(w_pool.shape, lambda s: (0, 0, 0)),
                  pl.BlockSpec((1, pw), lambda s: (0, 0)),
                  pl.BlockSpec(memory_space=pl.ANY)],
        out_specs=pl.BlockSpec((DEC_T, pw), rowmap),
        out_shape=jax.ShapeDtypeStruct(ybuf.shape, F32),
        scratch_shapes=[pltpu.VMEM((16 + DEC_T, pw), F32)],
        input_output_aliases={4: 0},
        compiler_params=_cparams("arbitrary"), name="pool_sample")(p, hist, w_pool, s_pool, ybuf)


def _cumsum_lanes(x):
    lane = lax.broadcasted_iota(jnp.int32, x.shape, 1)
    s = 1
    while s < x.shape[1]:
        x = x + jnp.where(lane >= s, pltpu.roll(x, s, axis=1), 0.0)
        s *= 2
    return x


HEAD_PACK = 2


def _unit_lower_inverse(mats, ii, jj):
    eye = jnp.where(ii == jj, 1.0, 0.0).astype(F32)
    pair = ((ii // 2) == (jj // 2)) & (ii % 2 == 1) & (jj % 2 == 0)
    xs = [eye - jnp.where(pair, a, 0.0) for a in mats]
    s = 2
    while s < BLK:
        mask = ((ii // (2 * s)) == (jj // (2 * s))) & ((ii // s) % 2 == 1) & ((jj // s) % 2 == 0)
        ts = [_bdot(jnp.where(mask, a, 0.0), x) for a, x in zip(mats, xs)]
        xs = [x - _bdot(x, t) for x, t in zip(xs, ts)]
        s *= 2
    return xs


def _delta_scalars(ba, valid, alog_ref, dtb_ref, nh):
    bat = ba.T
    beta = jnp.where(valid, 1.0 / (1.0 + jnp.exp(-bat[0:nh])), 0.0)
    z = bat[nh:2 * nh] + dtb_ref[...]
    softplus = jnp.maximum(z, 0.0) + jnp.log(1.0 + jnp.exp(-jnp.abs(z)))
    g = jnp.where(valid, -jnp.exp(alog_ref[...]) * softplus, 0.0)
    gc = _cumsum_lanes(g)
    glast = jnp.broadcast_to(gc[:, BLK - 1:BLK], gc.shape)
    eg = jnp.exp(gc)
    rows = jnp.concatenate(
        [gc, eg, beta, beta * eg, jnp.exp(glast - gc), jnp.exp(glast),
         jnp.zeros((BLK - 6 * nh, BLK), F32)], axis=0)
    return gc, rows.T


def _l2n(x):
    return x * lax.rsqrt(jnp.sum(x * x, axis=-1, keepdims=True) + EPS)


def _delta_chunk_small(xq, xk, xv, ba, alog_ref, dtb_ref, s_ref, nh, dk):
    t = xq.shape[0]
    zrows = jnp.zeros((BLK - t, BLK), F32)
    lane = lax.broadcasted_iota(jnp.int32, (nh, BLK), 1)
    gc, cols = _delta_scalars(jnp.concatenate([ba, zrows], axis=0), lane < t, alog_ref, dtb_ref, nh)
    ii = lax.broadcasted_iota(jnp.int32, (t, BLK), 0)
    jj = lax.broadcasted_iota(jnp.int32, (t, BLK), 1)
    incl = ii >= jj
    strict = ii > jj
    each = lambda f: [f(h) for h in range(nh)]
    col = lambda h, qi: cols[0:t, qi * nh + h:qi * nh + h + 1]
    pad = lambda x: jnp.concatenate([x, zrows], axis=0)
    q = each(lambda h: _l2n(xq[:, h * dk:(h + 1) * dk]) * (dk ** -0.5))
    k = each(lambda h: _l2n(xk[:, h * dk:(h + 1) * dk]))
    s = each(lambda h: s_ref[h])
    kpad = each(lambda h: pad(k[h]))
    decay = each(lambda h: jnp.where(incl, jnp.exp(jnp.where(incl, col(h, 0) - gc[h:h + 1, :], 0.0)), 0.0))
    a_mat = each(lambda h: jnp.where(strict, _bdot_nt(k[h], kpad[h]) * decay[h] * col(h, 2), 0.0))
    qk = each(lambda h: _bdot_nt(q[h], kpad[h]) * decay[h])
    wu = each(lambda h: jnp.concatenate([col(h, 3) * k[h], col(h, 2) * xv[:, h * dk:(h + 1) * dk]], axis=-1))
    for j in range(t - 1):
        wu = each(lambda h: wu[h] - a_mat[h][:, j:j + 1] * wu[h][j:j + 1, :])
    v_new = each(lambda h: wu[h][:, dk:] - _bdot(wu[h][:, :dk], s[h]))
    o = each(lambda h: col(h, 1) * _bdot(q[h], s[h]))
    for j in range(t):
        o = each(lambda h: o[h] + qk[h][:, j:j + 1] * v_new[h][j:j + 1, :])
    kd = each(lambda h: pad(k[h] * col(h, 4)).T)
    s_new = each(lambda h: cols[:, 5 * nh + h:5 * nh + h + 1] * s[h] + _bdot(kd[h], pad(v_new[h])))
    for h in range(nh):
        s_ref[h] = s_new[h]
    return o


def _delta_chunk(xq, xk, xv, ba, valid, alog_ref, dtb_ref, s_ref, nh, dk):
    assert dk == BLK and nh % HEAD_PACK == 0
    gc, cols = _delta_scalars(ba, valid, alog_ref, dtb_ref, nh)
    n = HEAD_PACK * BLK
    ii = lax.broadcasted_iota(jnp.int32, (n, n), 0)
    jj = lax.broadcasted_iota(jnp.int32, (n, n), 1)
    same = (ii // BLK) == (jj // BLK)
    incl = same & (ii >= jj)
    strict = same & (ii > jj)
    packs = range(nh // HEAD_PACK)
    each = lambda f: [f(p) for p in packs]
    stack = lambda p, f: jnp.concatenate([f(h) for h in range(p * HEAD_PACK, (p + 1) * HEAD_PACK)], axis=0)
    col = lambda p, qi: stack(p, lambda h: cols[:, qi * nh + h:qi * nh + h + 1])
    diag = lambda m: jnp.where(same, jnp.concatenate([m] * HEAD_PACK, axis=1), 0.0)
    q = each(lambda p: stack(p, lambda h: _l2n(xq[:, h * dk:(h + 1) * dk]) * (dk ** -0.5)))
    k = each(lambda p: stack(p, lambda h: _l2n(xk[:, h * dk:(h + 1) * dk])))
    v = each(lambda p: stack(p, lambda h: xv[:, h * dk:(h + 1) * dk]))
    s = each(lambda p: stack(p, lambda h: s_ref[h]))
    diff = each(lambda p: col(p, 0) - jnp.concatenate(
        [gc[h:h + 1, :] for h in range(p * HEAD_PACK, (p + 1) * HEAD_PACK)], axis=1))
    decay = each(lambda p: jnp.where(incl, jnp.exp(jnp.where(incl, diff[p], 0.0)), 0.0))
    a_mat = each(lambda p: jnp.where(strict, _bdot_nt(k[p], k[p]) * decay[p] * col(p, 2), 0.0))
    qk = each(lambda p: _bdot_nt(q[p], k[p]) * decay[p])
    x = _unit_lower_inverse(a_mat, ii, jj)
    rhs = each(lambda p: jnp.concatenate([col(p, 3) * k[p], col(p, 2) * v[p]], axis=-1))
    wu = each(lambda p: _bdot(x[p], rhs[p]))
    v_new = each(lambda p: wu[p][:, dk:] - _bdot(diag(wu[p][:, :dk]), s[p]))
    qs = each(lambda p: _bdot(diag(q[p]), s[p]))
    o = each(lambda p: col(p, 1) * qs[p] + _bdot(qk[p], v_new[p]))
    kd = each(lambda p: jnp.where(same, jnp.concatenate([(k[p] * col(p, 4)).T] * HEAD_PACK, axis=0), 0.0))
    s_new = each(lambda p: col(p, 5) * s[p] + _bdot(kd[p], v_new[p]))
    outs = []
    for p in packs:
        for i in range(HEAD_PACK):
            s_ref[p * HEAD_PACK + i] = s_new[p][i * BLK:(i + 1) * BLK, :]
            outs.append(o[p][i * BLK:(i + 1) * BLK, :])
    return outs


def _delta_out(o, gate, ng_ref):
    return o * lax.rsqrt(jnp.mean(o * o, axis=-1, keepdims=True) + EPS) * ng_ref[...] * _silu(gate)


def _conv_silu(ext_ref, w_ref, t):
    acc = ext_ref[5:5 + t, :] * w_ref[0:1, :]
    for i in range(1, DN_CONV):
        acc = acc + ext_ref[5 + i:5 + i + t, :] * w_ref[i:i + 1, :]
    return _silu(acc)


def _delta_prompt_body(q_ref, k_ref, v_ref, gt_ref, ba_ref, wq_ref, wk_ref, wv_ref, alog_ref, dtb_ref, ng_ref,
                       y_ref, sout_ref, eq_ref, ek_ref, ev_ref, s_ref, *, nh, dk, nblk):
    c = pl.program_id(1)

    @pl.when(c == 0)
    def _():
        for e in (eq_ref, ek_ref, ev_ref):
            e[0:8, :] = jnp.zeros((8, e.shape[1]), F32)
        s_ref[...] = jnp.zeros(s_ref.shape, F32)

    xs = []
    for x_ref, e_ref, w_ref in ((q_ref, eq_ref, wq_ref), (k_ref, ek_ref, wk_ref), (v_ref, ev_ref, wv_ref)):
        e_ref[8:8 + BLK, :] = x_ref[...]
        xs.append(_conv_silu(e_ref, w_ref, BLK))
        e_ref[0:8, :] = x_ref[BLK - 8:, :]
    lane = lax.broadcasted_iota(jnp.int32, (nh, BLK), 1)
    valid = (c > 0) | (lane >= FRONT)
    outs = _delta_chunk(xs[0], xs[1], xs[2], ba_ref[...], valid, alog_ref, dtb_ref, s_ref, nh, dk)
    row = c * BLK + lax.broadcasted_iota(jnp.int32, (BLK, 1), 0)
    for h in range(nh):
        sl = slice(h * dk, (h + 1) * dk)
        y_ref[:, sl] = jnp.where(row >= FRONT, _delta_out(outs[h], gt_ref[:, sl], ng_ref), 0.0)

    @pl.when(c == nblk - 1)
    def _():
        sout_ref[0] = s_ref[...]


def _delta_prompt(p, ba, w_conv, a_log, dt_bias, norm_g, nseq, nblk, col_q, col_gate, nh, dk):
    hw = nh * dk
    cq, ck, cv, cg = col_q // hw, col_q // hw + 1, col_q // hw + 2, col_gate // hw
    rows = lambda cb: pl.BlockSpec((BLK, hw), lambda b, c: (b * nblk + c, cb))
    wcs = lambda cb: pl.BlockSpec((DN_CONV, hw), lambda b, c: (0, cb))
    small = lambda shape: pl.BlockSpec(shape, lambda b, c: (0, 0))
    return pl.pallas_call(
        functools.partial(_delta_prompt_body, nh=nh, dk=dk, nblk=nblk), grid=(nseq, nblk),
        in_specs=[rows(cq), rows(ck), rows(cv), rows(cg),
                  pl.BlockSpec((BLK, BLK), lambda b, c: (b * nblk + c, 0)),
                  wcs(0), wcs(1), wcs(2), small((nh, 1)), small((nh, 1)), small((1, dk))],
        out_specs=[pl.BlockSpec((BLK, hw), lambda b, c: (b * nblk + c, 0)),
                   pl.BlockSpec((1, nh, dk, dk), lambda b, c: (b, 0, 0, 0))],
        out_shape=[jax.ShapeDtypeStruct((p.shape[0], hw), F32),
                   jax.ShapeDtypeStruct((nseq, nh, dk, dk), F32)],
        scratch_shapes=[pltpu.VMEM((8 + BLK, hw), F32)] * 3 + [pltpu.VMEM((nh, dk, dk), F32)],
        compiler_params=_cparams("parallel", "arbitrary"), name="delta_prompt",
    )(p, p, p, p, ba, w_conv, w_conv, w_conv, a_log.reshape(nh, 1), dt_bias.reshape(nh, 1),
      norm_g.reshape(1, dk))


def _delta_sample_body(q_ref, k_ref, v_ref, gt_ref, ba_ref, hq_ref, hk_ref, hv_ref, s0_ref,
                       wq_ref, wk_ref, wv_ref, alog_ref, dtb_ref, ng_ref, ybuf_ref,
                       y_ref, sout_ref, eq_ref, ek_ref, ev_ref, s_ref, *, nh, dk):
    del ybuf_ref
    xs = []
    for x_ref, h_ref, e_ref, w_ref in ((q_ref, hq_ref, eq_ref, wq_ref), (k_ref, hk_ref, ek_ref, wk_ref),
                                       (v_ref, hv_ref, ev_ref, wv_ref)):
        e_ref[5:8, :] = h_ref[0]
        e_ref[8:8 + DEC_T, :] = x_ref[...]
        xs.append(_conv_silu(e_ref, w_ref, DEC_T))
    s_ref[...] = s0_ref[0]
    outs = _delta_chunk_small(xs[0], xs[1], xs[2], ba_ref[...], alog_ref, dtb_ref, s_ref, nh, dk)
    for h in range(nh):
        sl = slice(h * dk, (h + 1) * dk)
        y_ref[:, sl] = _delta_out(outs[h], gt_ref[:, sl], ng_ref)
    sout_ref[0] = s_ref[...]


def _delta_sample(p, ba, conv_hist, s0, w_conv, a_log, dt_bias, norm_g, ybuf, rows_per_seq,
                  col_q, col_gate, nh, dk):
    nb = s0.shape[0]
    hw = nh * dk
    cq, ck, cv, cg = col_q // hw, col_q // hw + 1, col_q // hw + 2, col_gate // hw
    rowmap = lambda cb: (lambda s: (_sample_row_block(s, rows_per_seq), cb))
    rows = lambda cb: pl.BlockSpec((DEC_T, hw), rowmap(cb))
    hist = lambda cb: pl.BlockSpec((1, DN_CONV - 1, hw), lambda s: (s, 0, cb))
    wcs = lambda cb: pl.BlockSpec((DN_CONV, hw), lambda s: (0, cb))
    small = lambda shape: pl.BlockSpec(shape, lambda s: (0, 0))
    return pl.pallas_call(
        functools.partial(_delta_sample_body, nh=nh, dk=dk), grid=(nb,),
        in_specs=[rows(cq), rows(ck), rows(cv), rows(cg), pl.BlockSpec((DEC_T, BLK), rowmap(0)),
                  hist(0), hist(1), hist(2),
                  pl.BlockSpec((1, nh, dk, dk), lambda s: (s, 0, 0, 0)),
                  wcs(0), wcs(1), wcs(2), small((nh, 1)), small((nh, 1)), small((1, dk)),
                  pl.BlockSpec(memory_space=pl.ANY)],
        out_specs=[pl.BlockSpec((DEC_T, hw), rowmap(0)),
                   pl.BlockSpec((1, nh, dk, dk), lambda s: (s, 0, 0, 0))],
        out_shape=[jax.ShapeDtypeStruct(ybuf.shape, F32), jax.ShapeDtypeStruct(s0.shape, F32)],
        scratch_shapes=[pltpu.VMEM((8 + DEC_T, hw), F32)] * 3 + [pltpu.VMEM((nh, dk, dk), F32)],
        input_output_aliases={15: 0},
        compiler_params=_cparams("arbitrary"), name="delta_sample",
    )(p, p, p, p, ba, conv_hist, conv_hist, conv_hist, s0, w_conv, w_conv, w_conv,
      a_log.reshape(nh, 1), dt_bias.reshape(nh, 1), norm_g.reshape(1, dk), ybuf)


def _attn_prompt_body(*refs, nkv, grp, hd, window):
    q_refs = refs[:nkv]
    kp_ref, kc_ref, vp_ref, vc_ref, slope_ref, sink_ref, y_ref = refs[nkv:]
    n = pl.program_id(1)
    i = lax.broadcasted_iota(jnp.int32, (BLK, 2 * BLK), 0)
    j = lax.broadcasted_iota(jnp.int32, (BLK, 2 * BLK), 1)
    dist = BLK + i - j
    krow = (n - 1) * BLK + j
    valid = (dist >= 0) & (dist < window) & (krow >= FRONT)
    distf = dist.astype(F32)
    kk = jnp.concatenate([kp_ref[...], kc_ref[...]], axis=0).astype(BF16)
    vv = jnp.concatenate([vp_ref[...], vc_ref[...]], axis=0).astype(BF16)
    heads = [(kv, g) for kv in range(nkv) for g in range(grp)]
    each = lambda f: [f(t, kv, g) for t, (kv, g) in enumerate(heads)]
    s = each(lambda t, kv, g: _bdot_nt(q_refs[kv][:, g * hd:(g + 1) * hd], kk[:, kv * hd:(kv + 1) * hd]))
    s = each(lambda t, kv, g: jnp.where(valid, s[t] * (hd ** -0.5) - slope_ref[t] * distf, NEG_INF))
    m = each(lambda t, kv, g: jnp.maximum(jnp.max(s[t], axis=-1, keepdims=True), sink_ref[t]))
    pr = each(lambda t, kv, g: jnp.exp(s[t] - m[t]))
    den = each(lambda t, kv, g: jnp.sum(pr[t], axis=-1, keepdims=True) + jnp.exp(sink_ref[t] - m[t]))
    pv = each(lambda t, kv, g: _bdot(pr[t], vv[:, kv * hd:(kv + 1) * hd]))
    for t in range(len(heads)):
        y_ref[:, t * hd:(t + 1) * hd] = pv[t] / den[t]


def _attn_prompt(p, slopes, sinks, nseq, nblk, col_q, col_k, col_v, nkv, grp, hd, window):
    gw, kw = grp * hd, nkv * hd
    prev = lambda cb: (lambda b, n: (b * nblk + jnp.maximum(n - 1, 0), cb))
    cur = lambda cb: (lambda b, n: (b * nblk + n, cb))
    smem = pl.BlockSpec(memory_space=pltpu.SMEM)
    return pl.pallas_call(
        functools.partial(_attn_prompt_body, nkv=nkv, grp=grp, hd=hd, window=window), grid=(nseq, nblk),
        in_specs=[pl.BlockSpec((BLK, gw), cur(col_q // gw + kv)) for kv in range(nkv)]
        + [pl.BlockSpec((BLK, kw), prev(col_k // kw)), pl.BlockSpec((BLK, kw), cur(col_k // kw)),
           pl.BlockSpec((BLK, kw), prev(col_v // kw)), pl.BlockSpec((BLK, kw), cur(col_v // kw)),
           smem, smem],
        out_specs=pl.BlockSpec((BLK, nkv * gw), cur(0)),
        out_shape=jax.ShapeDtypeStruct((p.shape[0], nkv * gw), F32),
        compiler_params=_cparams("parallel", "arbitrary"), name="attn_prompt",
    )(*([p] * (nkv + 4)), slopes, sinks)


def _attn_sample_body(*refs, nkv, grp, hd, window):
    q_refs = refs[:nkv]
    k_ref, v_ref, ck_ref, cv_ref, slope_ref, sink_ref, ybuf_ref, y_ref = refs[nkv:]
    del ybuf_ref
    wc = ck_ref.shape[1]
    i = lax.broadcasted_iota(jnp.int32, (DEC_T, wc), 0)
    j = lax.broadcasted_iota(jnp.int32, (DEC_T, wc), 1)
    dist_c = wc + i - j
    valid_c = (dist_c >= 0) & (dist_c < window)
    i2 = lax.broadcasted_iota(jnp.int32, (DEC_T, DEC_T), 0)
    j2 = lax.broadcasted_iota(jnp.int32, (DEC_T, DEC_T), 1)
    dist_n = i2 - j2
    valid_n = (dist_n >= 0) & (dist_n < window)
    tile = lambda a: jnp.concatenate([a] * grp, axis=0)
    valid_c, valid_n = tile(valid_c), tile(valid_n)
    dist_c, dist_n = tile(dist_c).astype(F32), tile(dist_n).astype(F32)
    gi = lax.broadcasted_iota(jnp.int32, (grp * DEC_T, 1), 0) // DEC_T

    def per_row(ref, kv):
        out = jnp.full((grp * DEC_T, 1), ref[kv * grp], F32)
        for g in range(1, grp):
            out = jnp.where(gi == g, ref[kv * grp + g], out)
        return out

    each = lambda f: [f(kv) for kv in range(nkv)]
    cols = lambda ref, kv: ref[:, kv * hd:(kv + 1) * hd]
    q = each(lambda kv: jnp.concatenate([q_refs[kv][:, g * hd:(g + 1) * hd] for g in range(grp)], axis=0))
    slope = each(lambda kv: per_row(slope_ref, kv))
    sink = each(lambda kv: per_row(sink_ref, kv))
    sc = each(lambda kv: _bdot_nt(q[kv], ck_ref[0, :, kv * hd:(kv + 1) * hd]) * (hd ** -0.5))
    sn = each(lambda kv: _bdot_nt(q[kv], cols(k_ref, kv)) * (hd ** -0.5))
    sc = each(lambda kv: jnp.where(valid_c, sc[kv] - slope[kv] * dist_c, NEG_INF))
    sn = each(lambda kv: jnp.where(valid_n, sn[kv] - slope[kv] * dist_n, NEG_INF))
    m = each(lambda kv: jnp.maximum(jnp.maximum(jnp.max(sc[kv], axis=-1, keepdims=True),
                                                jnp.max(sn[kv], axis=-1, keepdims=True)), sink[kv]))
    pc = each(lambda kv: jnp.exp(sc[kv] - m[kv]))
    pn = each(lambda kv: jnp.exp(sn[kv] - m[kv]))
    den = each(lambda kv: jnp.sum(pc[kv], axis=-1, keepdims=True) + jnp.sum(pn[kv], axis=-1, keepdims=True)
               + jnp.exp(sink[kv] - m[kv]))
    o = each(lambda kv: (_bdot(pc[kv], cv_ref[0, :, kv * hd:(kv + 1) * hd]) + _bdot(pn[kv], cols(v_ref, kv)))
             / den[kv])
    for kv in range(nkv):
        for g in range(grp):
            head = kv * grp + g
            y_ref[:, head * hd:(head + 1) * hd] = o[kv][g * DEC_T:(g + 1) * DEC_T, :]


def _attn_sample(p, cache_k, cache_v, slopes, sinks, ybuf, rows_per_seq, col_q, col_k, col_v,
                 nkv, grp, hd, window):
    nb, wc = cache_k.shape[0], cache_k.shape[1]
    gw, kw = grp * hd, nkv * hd
    ck3 = cache_k.reshape(nb, wc, kw)
    cv3 = cache_v.reshape(nb, wc, kw)
    rowmap = lambda cb: (lambda s: (_sample_row_block(s, rows_per_seq), cb))
    smem = pl.BlockSpec(memory_space=pltpu.SMEM)
    return pl.pallas_call(
        functools.partial(_attn_sample_body, nkv=nkv, grp=grp, hd=hd, window=window), grid=(nb,),
        in_specs=[pl.BlockSpec((DEC_T, gw), rowmap(col_q // gw + kv)) for kv in range(nkv)]
        + [pl.BlockSpec((DEC_T, kw), rowmap(col_k // kw)),
           pl.BlockSpec((DEC_T, kw), rowmap(col_v // kw)),
           pl.BlockSpec((1, wc, kw), lambda s: (s, 0, 0)),
           pl.BlockSpec((1, wc, kw), lambda s: (s, 0, 0)),
           smem, smem, pl.BlockSpec(memory_space=pl.ANY)],
        out_specs=pl.BlockSpec((DEC_T, nkv * gw), rowmap(0)),
        out_shape=jax.ShapeDtypeStruct(ybuf.shape, F32),
        input_output_aliases={nkv + 6: 0},
        compiler_params=_cparams("arbitrary"), name="attn_sample",
    )(*([p] * (nkv + 2)), ck3, cv3, slopes, sinks, ybuf)


SUB = 8


def _bitonic_pairs(n, merge_only=False):
    out = []
    k = n if merge_only else 2
    while k <= n:
        j = k // 2
        while j >= 1:
            out += [(i, i ^ j, (i & k) == 0) for i in range(n) if (i ^ j) > i]
            j //= 2
        k *= 2
    return out


def _compare_exchange(v, pairs):
    v = list(v)
    for i, l, desc in pairs:
        hi, lo = jnp.maximum(v[i], v[l]), jnp.minimum(v[i], v[l])
        v[i], v[l] = (hi, lo) if desc else (lo, hi)
    return v


def _top16(rows):
    v = _compare_exchange(rows, _bitonic_pairs(TOPK))
    shift = SUB // 2
    while shift >= 1:
        w = [pltpu.roll(v[TOPK - 1 - r], shift, axis=0) for r in range(TOPK)]
        v = _compare_exchange([jnp.maximum(a, b) for a, b in zip(v, w)], _bitonic_pairs(TOPK, merge_only=True))
        shift //= 2
    return v


def _sublane_sum(x):
    shift = SUB // 2
    while shift >= 1:
        x = x + pltpu.roll(x, shift, axis=0)
        shift //= 2
    return x


def _on_sublanes(vs):
    sub = lax.broadcasted_iota(jnp.int32, vs[0].shape, 0)
    out = vs[SUB - 1]
    for j in range(SUB - 2, -1, -1):
        out = jnp.where(sub == j, vs[j], out)
    return out


def _peer_topk_body(q_ref, keys_ref, rk_ref, cut_ref, e1_ref, e2_ref, *, nh):
    nv = NKEYS // SUB
    for h in range(nh):
        sc = [_bdot_nt(keys_ref[2 * h + half], q_ref[:, (2 * h + half) * NKEYS:(2 * h + half + 1) * NKEYS])
              for half in (0, 1)]
        s1 = [sc[0][SUB * i:SUB * (i + 1), :] for i in range(nv)]
        s2 = [sc[1][SUB * i:SUB * (i + 1), :] for i in range(nv)]
        a = _top16(s1)
        b = _top16(s2)
        b_lo, b_hi, a_hi = _on_sublanes(b[:SUB]), _on_sublanes(b[SUB:]), _on_sublanes(a[SUB:])
        cand = [a[0] + b_lo, a[0] + b_hi, a_hi + b[0]] + [a[i] + b_lo for i in range(1, SUB)]
        cand += [jnp.full(cand[0].shape, LOWEST, F32)] * (TOPK - len(cand))
        top = _top16(cand)
        thr = top[TOPK - 1]
        zsum = jnp.exp(top[0] - top[0])
        for r in range(1, TOPK):
            zsum = zsum + jnp.exp(top[r] - top[0])
        rz = 1.0 / zsum
        height = [_sublane_sum(jnp.where(a[r] + b_lo >= thr, 1.0, 0.0) + jnp.where(a[r] + b_hi >= thr, 1.0, 0.0))
                  for r in range(TOPK)]
        cut, rank2 = [], []
        for i in range(nv):
            c = jnp.zeros(s1[i].shape, F32)
            for r in range(TOPK - 1, -1, -1):
                c = jnp.where(s1[i] == a[r], height[r], c)
            cut.append(c)
            k = jnp.where(b[0] > s2[i], 1.0, 0.0)
            for r in range(1, TOPK):
                k = k + jnp.where(b[r] > s2[i], 1.0, 0.0)
            rank2.append(k)
        rk_ref[h] = jnp.concatenate(rank2, axis=0).astype(BF16)
        cut_ref[h] = jnp.concatenate(cut, axis=0)
        e1_ref[h] = jnp.exp(sc[0] - a[0][0:1, :])
        e2_ref[h] = (jnp.exp(sc[1] - b[0][0:1, :]) * rz[0:1, :]).astype(BF16)


def _peer_topk(q, sub_keys, *, tm=128):
    n = q.shape[0]
    nh = sub_keys.shape[0]
    keys = sub_keys.reshape(2 * nh, NKEYS, sub_keys.shape[-1])
    sspec = pl.BlockSpec((nh, NKEYS, tm), lambda i: (0, 0, i))
    shape = lambda dt: jax.ShapeDtypeStruct((nh, NKEYS, n), dt)
    return pl.pallas_call(
        functools.partial(_peer_topk_body, nh=nh), grid=(n // tm,),
        in_specs=[pl.BlockSpec((tm, q.shape[1]), lambda i: (i, 0)),
                  pl.BlockSpec(keys.shape, lambda i: (0, 0, 0))],
        out_specs=[sspec] * 4,
        out_shape=[shape(BF16), shape(F32), shape(F32), shape(BF16)],
        compiler_params=_cparams("parallel"), name="peer_topk")(q, keys)


def _gelu(x):
    return 0.5 * x * (1.0 + lax.erf(x * (2.0 ** -0.5)))


PACK = 16


def _peer_expert_body(xn_ref, wd_ref, wu_ref, id_ref, iu_ref, rk_ref, cut_ref, e1_ref, e2_ref, y_ref,
                      xt_ref, ix_ref, *, nh, ei, sub):
    c = pl.program_id(1)
    tm = xt_ref.shape[1]
    tw = tm // sub
    slabs = [slice(t * tw, (t + 1) * tw) for t in range(sub)]
    each = lambda f: [f(t, tl) for t, tl in enumerate(slabs)]

    @pl.when(c == 0)
    def _():
        y_ref[...] = jnp.zeros(y_ref.shape, F32)
        x = xn_ref[...].astype(F32)
        sx = _amax_scale(x)
        xt_ref[...] = (x * sx).T.astype(FP8)
        ix_ref[...] = jnp.broadcast_to(1.0 / sx, ix_ref.shape)

    def gates(tl):
        def row16(ref, h, i1):
            return jnp.broadcast_to(ref[h, pl.ds(i1, 1), tl], (PACK, tw)).astype(BF16)
        pieces = []
        for ii in range(ei):
            i1 = c * ei + ii
            cut = [row16(cut_ref, h, i1) for h in range(nh)]
            e1 = [row16(e1_ref, h, i1) for h in range(nh)]
            for r in range(NKEYS // PACK):
                sl = slice(r * PACK, (r + 1) * PACK)
                gate = pieces[-1] * 0.0 if pieces else None
                for h in range(nh):
                    w = jnp.where(rk_ref[h, sl, tl] < cut[h], e1[h] * e2_ref[h, sl, tl], 0.0)
                    gate = w if gate is None else gate + w
                pieces.append(gate)
        return pieces

    inv_h = id_ref[0, 0:1, 0:1] * ix_ref[0:1, 0:1]
    inv_u = iu_ref[0, 0:1, 0:1]
    ht = each(lambda t, tl: jnp.dot(wd_ref[...], xt_ref[:, tl], preferred_element_type=F32) * inv_h)
    gate = each(lambda t, tl: gates(tl))
    act = each(lambda t, tl: _gelu(ht[t]))
    at = each(lambda t, tl: jnp.concatenate(
        [act[t][i * PACK:(i + 1) * PACK, :] * g.astype(F32) for i, g in enumerate(gate[t])], axis=0))
    sa = each(lambda t, tl: _amax_scale(at[t]))
    part = each(lambda t, tl: jnp.dot((at[t] * sa[t]).T.astype(FP8), wu_ref[...], preferred_element_type=F32))
    for t, tl in enumerate(slabs):
        y_ref[tl, :] += part[t] * (inv_u / sa[t])


PEER_TM, PEER_EI, PEER_SUB = 512, 8, 2


def _peer_expert(xn, w_down, w_up, inv_down, inv_up, rank2, cut, e1, e2, *, tm=PEER_TM, ei=PEER_EI, sub=PEER_SUB):
    n, d = xn.shape
    nh = rank2.shape[0]
    e = ei * NKEYS
    once = dict(pipeline_mode=pl.Buffered(1))
    sspec = pl.BlockSpec((nh, NKEYS, tm), lambda i, c: (0, 0, i), **once)
    inv = pl.BlockSpec((1, 8, BLK), lambda i, c: (c, 0, 0))
    return pl.pallas_call(
        functools.partial(_peer_expert_body, nh=nh, ei=ei, sub=sub), grid=(n // tm, w_down.shape[0] // e),
        in_specs=[pl.BlockSpec((tm, d), lambda i, c: (i, 0), **once),
                  pl.BlockSpec((e, d), lambda i, c: (c, 0)),
                  pl.BlockSpec((e, d), lambda i, c: (c, 0)),
                  inv, inv, sspec, sspec, sspec, sspec],
        out_specs=pl.BlockSpec((tm, d), lambda i, c: (i, 0), **once),
        out_shape=jax.ShapeDtypeStruct((n, d), F32),
        scratch_shapes=[pltpu.VMEM((d, tm), FP8), pltpu.VMEM((8, BLK), F32)],
        compiler_params=_cparams("parallel", "arbitrary"), name="peer_expert",
    )(xn, w_down, w_up, inv_down, inv_up, rank2, cut, e1, e2)


def _final_norm(h, y, g, nseq, nblk, row_blk, first_blk, nblk_out):
    d = h.shape[1]
    row = pl.BlockSpec((row_blk, d), lambda b, i: (b * nblk + first_blk + i, 0))
    return pl.pallas_call(
        _final_norm_body, grid=(nseq, nblk_out),
        in_specs=[row, row, pl.BlockSpec((1, d), lambda b, i: (0, 0))],
        out_specs=pl.BlockSpec((1, row_blk, d), lambda b, i: (b, i, 0)),
        out_shape=jax.ShapeDtypeStruct((nseq, nblk_out * row_blk, d), F32),
        compiler_params=_cparams("parallel", "parallel"), name="final_norm",
    )(h, y, g.reshape(1, d).astype(F32))


def _final_norm_body(h_ref, y_ref, g_ref, o_ref):
    x = h_ref[...] + y_ref[...]
    ms = jnp.mean(x * x, axis=-1, keepdims=True)
    o_ref[0] = x * lax.rsqrt(ms + EPS) * g_ref[...]


def kernel(x_prompt, x_sample, state_pool, state_conv, state_delta, cache_k, cache_v, meta_tokens, norm1_g,
           w_in, w_pool, s_pool, w_conv, a_log, dt_bias, dn_norm_g, attn_sinks, w_out, norm2_g,
           peer_w_query, peer_sub_keys, peer_w_down, peer_w_up, final_norm_g):
    nseq, seq, d = x_prompt.shape
    nsamp, dec_t, _ = x_sample.shape
    depth = w_in.shape[0]
    pool_w = w_pool.shape[1] * w_pool.shape[2]
    dn_qkv = w_conv.shape[2]
    nh, dk = state_delta.shape[2], state_delta.shape[3]
    wc, nkv, hd = cache_k.shape[2], cache_k.shape[3], cache_k.shape[4]
    nq = attn_sinks.shape[1]
    grp = nq // nkv
    window = wc
    assert dec_t == DEC_T and nsamp == nseq * SLOTS and (N_META + seq) % BLK == N_META
    assert SLOTS * DEC_T + POOL_HIST <= FRONT and dn_qkv == 3 * nh * dk and wc == BLK
    rows_per_seq = FRONT + N_META + seq
    nblk = rows_per_seq // BLK
    col_qkv = pool_w
    col_gate = col_qkv + dn_qkv
    col_q = col_gate + nh * dk
    col_k = col_q + nq * hd
    col_v = col_k + nkv * hd
    src_ba = pool_w + dn_qkv

    slopes = jnp.exp2(-8.0 * (jnp.arange(nq, dtype=F32) + 1.0) / nq)
    new_p = [[] for _ in range(5)]
    new_s = [[] for _ in range(5)]
    y_peer = None
    w_in_t = jnp.swapaxes(w_in, 1, 2)
    for l in range(depth):
        w_main, w_ba = _cast_w_in(w_in_t, l, src_ba, 2 * nh)
        if l == 0:
            h, xn = _assemble_norm(x_prompt, x_sample, meta_tokens, norm1_g[l], nblk)
        else:
            h, xn = _addnorm(h, y_peer, norm1_g[l])
        p = _mm(xn, w_main, nt=True)
        ba = _mm(xn, w_ba, nt=True)

        y_pool = _pool_prompt(p, w_pool[l], s_pool[l].reshape(1, pool_w), nseq, nblk)
        y_pool = _pool_sample(p, state_pool[l], w_pool[l], s_pool[l].reshape(1, pool_w), y_pool,
                              rows_per_seq, PAST_LEN)
        y_dn, s_p = _delta_prompt(p, ba, w_conv[l], a_log[l], dt_bias[l], dn_norm_g[l], nseq, nblk,
                                  col_qkv, col_gate, nh, dk)
        y_dn, s_s = _delta_sample(p, ba, state_conv[l], state_delta[l], w_conv[l], a_log[l], dt_bias[l],
                                  dn_norm_g[l], y_dn, rows_per_seq, col_qkv, col_gate, nh, dk)
        y_att = _attn_prompt(p, slopes, attn_sinks[l], nseq, nblk, col_q, col_k, col_v, nkv, grp, hd, window)
        y_att = _attn_sample(p, cache_k[l], cache_v[l], slopes, attn_sinks[l], y_att, rows_per_seq,
                             col_q, col_k, col_v, nkv, grp, hd, window)
        h = _outproj(y_pool, y_dn, y_att, _cast(w_out, l), h)

        _, xn2 = _addnorm(h, None, norm2_g[l])
        q = _mm(xn2, _cast(peer_w_query, l))
        rank2, cut, e1, e2 = _peer_topk(q, peer_sub_keys[l])
        wd8, inv_d = _quant_fp8(peer_w_down, l, tr=PEER_EI * NKEYS)
        wu8, inv_u = _quant_fp8(peer_w_up, l, tr=PEER_EI * NKEYS)
        y_peer = _peer_expert(xn2, wd8, wu8, inv_d, inv_u, rank2, cut, e1, e2)

        p3 = p.reshape(nseq, rows_per_seq, p.shape[1])
        ps = p3[:, :SLOTS * DEC_T].reshape(nsamp, DEC_T, p.shape[1])
        new_p[0].append(p3[:, -POOL_HIST:, :pool_w])
        new_s[0].append(jnp.concatenate([state_pool[l], ps[:, :, :pool_w]], axis=1)[:, -POOL_HIST:])
        new_p[1].append(p3[:, -(DN_CONV - 1):, col_qkv:col_gate])
        new_s[1].append(jnp.concatenate([state_conv[l], ps[:, :, col_qkv:col_gate]], axis=1)[:, -(DN_CONV - 1):])
        new_p[2].append(s_p)
        new_s[2].append(s_s)
        new_p[3].append(p3[:, -window:, col_k:col_v].reshape(nseq, window, nkv, hd))
        new_s[3].append(jnp.concatenate([cache_k[l], ps[:, :, col_k:col_v].reshape(nsamp, DEC_T, nkv, hd)],
                                        axis=1)[:, -wc:])
        new_p[4].append(p3[:, -window:, col_v:].reshape(nseq, window, nkv, hd))
        new_s[4].append(jnp.concatenate([cache_v[l], ps[:, :, col_v:].reshape(nsamp, DEC_T, nkv, hd)],
                                        axis=1)[:, -wc:])

    y_prompt = _final_norm(h, y_peer, final_norm_g, nseq, nblk, BLK, 1, nblk - 1)
    y_sample = _final_norm(h, y_peer, final_norm_g, nseq, rows_per_seq // (SLOTS * DEC_T), SLOTS * DEC_T, 0, 1)
    y_sample = y_sample.reshape(nsamp, DEC_T, d)
    pool_p, conv_p, delta_p, k_p, v_p = (jnp.stack(a) for a in new_p)
    pool_s, conv_s, delta_s, k_s, v_s = (jnp.stack(a) for a in new_s)
    return (y_prompt, y_sample, pool_p, pool_s, conv_p, conv_s, delta_p, delta_s, k_p, k_s, v_p, v_s)
```

```python
import functools

import jax
import jax.numpy as jnp
from jax import lax
from jax.experimental import pallas as pl
from jax.experimental.pallas import tpu as pltpu

F32 = jnp.float32
BF16 = jnp.bfloat16

EPS = 1e-6
NEG_INF = -1e30
LOWEST = -3.0e38

PAST_LEN = 16384
N_META = 16
BLK = 128
FRONT = BLK - N_META
DEC_T = 8
SLOTS = 8
POOL_WINDOWS = (2, 4, 8, 16)
POOL_HIST = 15
DN_CONV = 4
TOPK = 16
NKEYS = 128
VMEM_LIMIT = 56 * 1024 * 1024


def _cparams(*sem):
    return pltpu.CompilerParams(dimension_semantics=sem, vmem_limit_bytes=VMEM_LIMIT)


def _bdot(a, b):
    return jnp.dot(a.astype(BF16), b.astype(BF16), preferred_element_type=F32)


def _bdot_nt(a, b):
    return lax.dot_general(a.astype(BF16), b.astype(BF16), (((1,), (1,)), ((), ())),
                           preferred_element_type=F32)


def _silu(x):
    return x * (1.0 / (1.0 + jnp.exp(-x)))


def _addnorm_body(*refs, add):
    if add:
        h_ref, y_ref, g_ref, hs_ref, xn_ref = refs
        x = h_ref[...] + y_ref[...]
        hs_ref[...] = x
    else:
        h_ref, g_ref, xn_ref = refs
        x = h_ref[...]
    ms = jnp.mean(x * x, axis=-1, keepdims=True)
    xn_ref[...] = (x * lax.rsqrt(ms + EPS) * g_ref[...]).astype(xn_ref.dtype)


def _addnorm(h, y, g, *, tm=256, out_dtype=BF16):
    n, d = h.shape
    row = pl.BlockSpec((tm, d), lambda i: (i, 0))
    gspec = pl.BlockSpec((1, d), lambda i: (0, 0))
    g2 = g.reshape(1, d).astype(F32)
    if y is None:
        xn = pl.pallas_call(
            functools.partial(_addnorm_body, add=False),
            grid=(n // tm,), in_specs=[row, gspec], out_specs=row,
            out_shape=jax.ShapeDtypeStruct((n, d), out_dtype),
            compiler_params=_cparams("parallel"), name="norm")(h, g2)
        return h, xn
    hs, xn = pl.pallas_call(
        functools.partial(_addnorm_body, add=True),
        grid=(n // tm,), in_specs=[row, row, gspec], out_specs=[row, row],
        out_shape=[jax.ShapeDtypeStruct((n, d), F32), jax.ShapeDtypeStruct((n, d), out_dtype)],
        compiler_params=_cparams("parallel"), name="add_norm")(h, y, g2)
    return hs, xn


def _assemble_norm_body(xs_ref, meta_ref, xp_ref, g_ref, h_ref, xn_ref):
    n = pl.program_id(1)

    @pl.when(n == 0)
    def _():
        ns = xs_ref.shape[1]
        h_ref[0:ns, :] = xs_ref[0]
        h_ref[ns:FRONT, :] = jnp.zeros((FRONT - ns, h_ref.shape[1]), F32)
        h_ref[FRONT:, :] = meta_ref[...]

    @pl.when(n > 0)
    def _():
        h_ref[...] = xp_ref[0]

    x = h_ref[...]
    ms = jnp.mean(x * x, axis=-1, keepdims=True)
    xn_ref[...] = (x * lax.rsqrt(ms + EPS) * g_ref[...]).astype(xn_ref.dtype)


def _assemble_norm(x_prompt, x_sample, meta_tokens, g, nblk):
    nseq, _, d = x_prompt.shape
    xs = x_sample.reshape(nseq, SLOTS * DEC_T, d)
    out = pl.BlockSpec((BLK, d), lambda b, n: (b * nblk + n, 0))
    return pl.pallas_call(
        _assemble_norm_body, grid=(nseq, nblk),
        in_specs=[pl.BlockSpec((1, SLOTS * DEC_T, d), lambda b, n: (b, 0, 0)),
                  pl.BlockSpec((N_META, d), lambda b, n: (0, 0)),
                  pl.BlockSpec((1, BLK, d), lambda b, n: (b, jnp.maximum(n - 1, 0), 0)),
                  pl.BlockSpec((1, d), lambda b, n: (0, 0))],
        out_specs=[out, out],
        out_shape=[jax.ShapeDtypeStruct((nseq * nblk * BLK, d), F32),
                   jax.ShapeDtypeStruct((nseq * nblk * BLK, d), BF16)],
        compiler_params=_cparams("parallel", "arbitrary"), name="assemble_norm",
    )(xs, meta_tokens, x_prompt, g.reshape(1, d).astype(F32))


def _cast_body(x_ref, o_ref):
    o_ref[...] = x_ref[0].astype(o_ref.dtype)


def _cast(x, l, dtype=BF16, *, tr=512):
    _, r, c = x.shape
    return pl.pallas_call(
        _cast_body, grid=(r // tr,),
        in_specs=[pl.BlockSpec((1, tr, c), lambda i: (l, i, 0))],
        out_specs=pl.BlockSpec((tr, c), lambda i: (i, 0)),
        out_shape=jax.ShapeDtypeStruct((r, c), dtype),
        compiler_params=_cparams("parallel"), name="cast")(x)


FP8 = jnp.float8_e4m3fn
FP8_TARGET = 224.0


def _amax_scale(x):
    a = jnp.max(jnp.max(jnp.abs(x), axis=1, keepdims=True), axis=0, keepdims=True)
    return jnp.where(a > 0.0, FP8_TARGET / a, 1.0)


def _quant_body(x_ref, o_ref, inv_ref):
    x = x_ref[0]
    s = _amax_scale(x)
    o_ref[...] = (x * s).astype(FP8)
    inv_ref[0] = jnp.broadcast_to(1.0 / s, inv_ref.shape[1:])


def _quant_fp8(x, l, *, tr):
    _, r, c = x.shape
    return pl.pallas_call(
        _quant_body, grid=(r // tr,),
        in_specs=[pl.BlockSpec((1, tr, c), lambda i: (l, i, 0))],
        out_specs=[pl.BlockSpec((tr, c), lambda i: (i, 0)), pl.BlockSpec((1, 8, BLK), lambda i: (i, 0, 0))],
        out_shape=[jax.ShapeDtypeStruct((r, c), FP8), jax.ShapeDtypeStruct((r // tr, 8, BLK), F32)],
        compiler_params=_cparams("parallel"), name="quant_fp8")(x)


def _cast_w_in_body(a_ref, b_ref, main_ref, ba_ref, *, nlo, nba):
    j = pl.program_id(0)
    tr = a_ref.shape[1]

    @pl.when(j < nlo)
    def _():
        main_ref[...] = a_ref[0].astype(BF16)

    @pl.when(j >= nlo)
    def _():
        main_ref[0:tr - nba, :] = a_ref[0, nba:, :].astype(BF16)
        main_ref[tr - nba:, :] = b_ref[0].astype(BF16)

    @pl.when(j == nlo)
    def _():
        ba_ref[0:nba, :] = a_ref[0, 0:nba, :].astype(BF16)
        ba_ref[nba:, :] = jnp.zeros((ba_ref.shape[0] - nba, ba_ref.shape[1]), BF16)


def _cast_w_in(w_in_t, l, lo, nba, *, tr=1024):
    _, r, c = w_in_t.shape
    assert lo % tr == 0 and (r - nba) % tr == 0 and tr % nba == 0 and nba % 8 == 0
    return pl.pallas_call(
        functools.partial(_cast_w_in_body, nlo=lo // tr, nba=nba), grid=((r - nba) // tr,),
        in_specs=[pl.BlockSpec((1, tr, c), lambda j: (l, j, 0)),
                  pl.BlockSpec((1, nba, c), lambda j: (l, (j + 1) * (tr // nba), 0))],
        out_specs=[pl.BlockSpec((tr, c), lambda j: (j, 0)), pl.BlockSpec((BLK, c), lambda j: (0, 0))],
        out_shape=[jax.ShapeDtypeStruct((r - nba, c), BF16), jax.ShapeDtypeStruct((BLK, c), BF16)],
        compiler_params=_cparams("arbitrary"), name="cast_w_in")(w_in_t, w_in_t)


def _mm_body(x_ref, w_ref, o_ref, *, nt):
    if nt:
        o_ref[...] = lax.dot_general(x_ref[...], w_ref[...], (((1,), (1,)), ((), ())),
                                     preferred_element_type=F32)
    else:
        o_ref[...] = jnp.dot(x_ref[...], w_ref[...], preferred_element_type=F32)


def _mm(x, w, *, nt=False, tm=512, tn=1024):
    m, k = x.shape
    n = w.shape[0] if nt else w.shape[1]
    tn = min(tn, n)
    wspec = pl.BlockSpec((tn, k), lambda j, i: (j, 0)) if nt else pl.BlockSpec((k, tn), lambda j, i: (0, j))
    return pl.pallas_call(
        functools.partial(_mm_body, nt=nt), grid=(n // tn, m // tm),
        in_specs=[pl.BlockSpec((tm, k), lambda j, i: (i, 0)), wspec],
        out_specs=pl.BlockSpec((tm, tn), lambda j, i: (i, j)),
        out_shape=jax.ShapeDtypeStruct((m, n), F32),
        compiler_params=_cparams("parallel", "parallel"), name="matmul")(x, w)


def _outproj_body(yp_ref, yd_ref, ya_ref, w_ref, h_ref, o_ref, *, wp, wd):
    acc = h_ref[...]
    acc += jnp.dot(yp_ref[...].astype(BF16), w_ref[0:wp, :], preferred_element_type=F32)
    acc += jnp.dot(yd_ref[...].astype(BF16), w_ref[wp:wp + wd, :], preferred_element_type=F32)
    acc += jnp.dot(ya_ref[...].astype(BF16), w_ref[wp + wd:, :], preferred_element_type=F32)
    o_ref[...] = acc


def _outproj(yp, yd, ya, w, h, *, tm=512, tn=1024):
    m, d = h.shape
    wp, wd, wa = yp.shape[1], yd.shape[1], ya.shape[1]
    k = wp + wd + wa
    return pl.pallas_call(
        functools.partial(_outproj_body, wp=wp, wd=wd), grid=(d // tn, m // tm),
        in_specs=[pl.BlockSpec((tm, wp), lambda j, i: (i, 0)),
                  pl.BlockSpec((tm, wd), lambda j, i: (i, 0)),
                  pl.BlockSpec((tm, wa), lambda j, i: (i, 0)),
                  pl.BlockSpec((k, tn), lambda j, i: (0, j)),
                  pl.BlockSpec((tm, tn), lambda j, i: (i, j))],
        out_specs=pl.BlockSpec((tm, tn), lambda j, i: (i, j)),
        out_shape=jax.ShapeDtypeStruct((m, d), F32),
        compiler_params=_cparams("parallel", "parallel"), name="out_proj")(yp, yd, ya, w, h)


def _pool_windows(ext_ref, u, t, pos, w_ref, s_ref, gw):
    outs = []
    for gi, w in enumerate(POOL_WINDOWS):
        sl = slice(gi * gw, (gi + 1) * gw)
        win = u[:, sl]
        for k in range(1, w):
            win = win + ext_ref[16 - k:16 - k + t, sl]
        cnt = jnp.clip(pos + 1, 1, w).astype(F32)
        d = win / cnt - u[:, sl]
        outs.append(_bdot(d, w_ref[gi]))
    return jnp.concatenate(outs, axis=-1) * s_ref[...]


def _pool_prompt_body(u_ref, w_ref, s_ref, y_ref, ext_ref, *, gw):
    n = pl.program_id(1)

    @pl.when(n == 0)
    def _():
        ext_ref[0:16, :] = jnp.zeros((16, ext_ref.shape[1]), F32)

    u = u_ref[...]
    ext_ref[16:16 + BLK, :] = u
    row = n * BLK + lax.broadcasted_iota(jnp.int32, (BLK, 1), 0)
    pos = row - FRONT
    y = _pool_windows(ext_ref, u, BLK, pos, w_ref, s_ref, gw)
    y_ref[...] = jnp.where(pos >= 0, y, 0.0)
    ext_ref[0:16, :] = u[BLK - 16:, :]


def _pool_prompt(p, w_pool, s_pool, nseq, nblk):
    pw = w_pool.shape[0] * w_pool.shape[1]
    gw = w_pool.shape[1]
    return pl.pallas_call(
        functools.partial(_pool_prompt_body, gw=gw), grid=(nseq, nblk),
        in_specs=[pl.BlockSpec((BLK, pw), lambda b, n: (b * nblk + n, 0)),
                  pl.BlockSpec(w_pool.shape, lambda b, n: (0, 0, 0)),
                  pl.BlockSpec((1, pw), lambda b, n: (0, 0))],
        out_specs=pl.BlockSpec((BLK, pw), lambda b, n: (b * nblk + n, 0)),
        out_shape=jax.ShapeDtypeStruct((p.shape[0], pw), F32),
        scratch_shapes=[pltpu.VMEM((16 + BLK, pw), F32)],
        compiler_params=_cparams("parallel", "arbitrary"), name="pool_prompt")(p, w_pool, s_pool)


def _pool_sample_body(u_ref, hist_ref, w_ref, s_ref, ybuf_ref, y_ref, ext_ref, *, gw, pos0):
    del ybuf_ref
    u = u_ref[...]
    ext_ref[0:1, :] = jnp.zeros((1, ext_ref.shape[1]), F32)
    ext_ref[1:16, :] = hist_ref[0]
    ext_ref[16:16 + DEC_T, :] = u
    pos = pos0 + lax.broadcasted_iota(jnp.int32, (DEC_T, 1), 0)
    y_ref[...] = _pool_windows(ext_ref, u, DEC_T, pos, w_ref, s_ref, gw)


def _sample_row_block(s, rows_per_seq):
    return (s // SLOTS) * (rows_per_seq // DEC_T) + s % SLOTS


def _pool_sample(p, hist, w_pool, s_pool, ybuf, rows_per_seq, pos0):
    nb = hist.shape[0]
    pw = hist.shape[2]
    gw = w_pool.shape[1]
    rowmap = lambda s: (_sample_row_block(s, rows_per_seq), 0)
    return pl.pallas_call(
        functools.partial(_pool_sample_body, gw=gw, pos0=pos0), grid=(nb,),
        in_specs=[pl.BlockSpec((DEC_T, pw), rowmap),
                  pl.BlockSpec((1, POOL_HIST, pw), lambda s: (s, 0, 0)),
                  pl.BlockSpec(w_pool.shape, lambda s: (0, 0, 0)),
                  pl.BlockSpec((1, pw), lambda s: (0, 0)),
                  pl.BlockSpec(memory_space=pl.ANY)],
        out_specs=pl.BlockSpec((DEC_T, pw), rowmap),
        out_shape=jax.ShapeDtypeStruct(ybuf.shape, F32),
        scratch_shapes=[pltpu.VMEM((16 + DEC_T, pw), F32)],
        input_output_aliases={4: 0},
        compiler_params=_cparams("arbitrary"), name="pool_sample")(p, hist, w_pool, s_pool, ybuf)


def _cumsum_lanes(x):
    lane = lax.broadcasted_iota(jnp.int32, x.shape, 1)
    s = 1
    while s < x.shape[1]:
        x = x + jnp.where(lane >= s, pltpu.roll(x, s, axis=1), 0.0)
        s *= 2
    return x


HEAD_PACK = 2


def _unit_lower_inverse(mats, ii, jj):
    eye = jnp.where(ii == jj, 1.0, 0.0).astype(F32)
    pair = ((ii // 2) == (jj // 2)) & (ii % 2 == 1) & (jj % 2 == 0)
    xs = [eye - jnp.where(pair, a, 0.0) for a in mats]
    s = 2
    while s < BLK:
        mask = ((ii // (2 * s)) == (jj // (2 * s))) & ((ii // s) % 2 == 1) & ((jj // s) % 2 == 0)
        ts = [_bdot(jnp.where(mask, a, 0.0), x) for a, x in zip(mats, xs)]
        xs = [x - _bdot(x, t) for x, t in zip(xs, ts)]
        s *= 2
    return xs


def _delta_scalars(ba, valid, alog_ref, dtb_ref, nh):
    bat = ba.T
    beta = jnp.where(valid, 1.0 / (1.0 + jnp.exp(-bat[0:nh])), 0.0)
    z = bat[nh:2 * nh] + dtb_ref[...]
    softplus = jnp.maximum(z, 0.0) + jnp.log(1.0 + jnp.exp(-jnp.abs(z)))
    g = jnp.where(valid, -jnp.exp(alog_ref[...]) * softplus, 0.0)
    gc = _cumsum_lanes(g)
    glast = jnp.broadcast_to(gc[:, BLK - 1:BLK], gc.shape)
    eg = jnp.exp(gc)
    rows = jnp.concatenate(
        [gc, eg, beta, beta * eg, jnp.exp(glast - gc), jnp.exp(glast),
         jnp.zeros((BLK - 6 * nh, BLK), F32)], axis=0)
    return gc, rows.T


def _l2n(x):
    return x * lax.rsqrt(jnp.sum(x * x, axis=-1, keepdims=True) + EPS)


def _delta_chunk_small(xq, xk, xv, ba, alog_ref, dtb_ref, s_ref, nh, dk):
    t = xq.shape[0]
    zrows = jnp.zeros((BLK - t, BLK), F32)
    lane = lax.broadcasted_iota(jnp.int32, (nh, BLK), 1)
    gc, cols = _delta_scalars(jnp.concatenate([ba, zrows], axis=0), lane < t, alog_ref, dtb_ref, nh)
    ii = lax.broadcasted_iota(jnp.int32, (t, BLK), 0)
    jj = lax.broadcasted_iota(jnp.int32, (t, BLK), 1)
    incl = ii >= jj
    strict = ii > jj
    each = lambda f: [f(h) for h in range(nh)]
    col = lambda h, qi: cols[0:t, qi * nh + h:qi * nh + h + 1]
    pad = lambda x: jnp.concatenate([x, zrows], axis=0)
    q = each(lambda h: _l2n(xq[:, h * dk:(h + 1) * dk]) * (dk ** -0.5))
    k = each(lambda h: _l2n(xk[:, h * dk:(h + 1) * dk]))
    s = each(lambda h: s_ref[h])
    kpad = each(lambda h: pad(k[h]))
    decay = each(lambda h: jnp.where(incl, jnp.exp(jnp.where(incl, col(h, 0) - gc[h:h + 1, :], 0.0)), 0.0))
    a_mat = each(lambda h: jnp.where(strict, _bdot_nt(k[h], kpad[h]) * decay[h] * col(h, 2), 0.0))
    qk = each(lambda h: _bdot_nt(q[h], kpad[h]) * decay[h])
    wu = each(lambda h: jnp.concatenate([col(h, 3) * k[h], col(h, 2) * xv[:, h * dk:(h + 1) * dk]], axis=-1))
    for j in range(t - 1):
        wu = each(lambda h: wu[h] - a_mat[h][:, j:j + 1] * wu[h][j:j + 1, :])
    v_new = each(lambda h: wu[h][:, dk:] - _bdot(wu[h][:, :dk], s[h]))
    o = each(lambda h: col(h, 1) * _bdot(q[h], s[h]))
    for j in range(t):
        o = each(lambda h: o[h] + qk[h][:, j:j + 1] * v_new[h][j:j + 1, :])
    kd = each(lambda h: pad(k[h] * col(h, 4)).T)
    s_new = each(lambda h: cols[:, 5 * nh + h:5 * nh + h + 1] * s[h] + _bdot(kd[h], pad(v_new[h])))
    for h in range(nh):
        s_ref[h] = s_new[h]
    return o


def _delta_chunk(xq, xk, xv, ba, valid, alog_ref, dtb_ref, s_ref, nh, dk):
    assert dk == BLK and nh % HEAD_PACK == 0
    gc, cols = _delta_scalars(ba, valid, alog_ref, dtb_ref, nh)
    n = HEAD_PACK * BLK
    ii = lax.broadcasted_iota(jnp.int32, (n, n), 0)
    jj = lax.broadcasted_iota(jnp.int32, (n, n), 1)
    same = (ii // BLK) == (jj // BLK)
    incl = same & (ii >= jj)
    strict = same & (ii > jj)
    packs = range(nh // HEAD_PACK)
    each = lambda f: [f(p) for p in packs]
    stack = lambda p, f: jnp.concatenate([f(h) for h in range(p * HEAD_PACK, (p + 1) * HEAD_PACK)], axis=0)
    col = lambda p, qi: stack(p, lambda h: cols[:, qi * nh + h:qi * nh + h + 1])
    diag = lambda m: jnp.where(same, jnp.concatenate([m] * HEAD_PACK, axis=1), 0.0)
    q = each(lambda p: stack(p, lambda h: _l2n(xq[:, h * dk:(h + 1) * dk]) * (dk ** -0.5)))
    k = each(lambda p: stack(p, lambda h: _l2n(xk[:, h * dk:(h + 1) * dk])))
    v = each(lambda p: stack(p, lambda h: xv[:, h * dk:(h + 1) * dk]))
    s = each(lambda p: stack(p, lambda h: s_ref[h]))
    diff = each(lambda p: col(p, 0) - jnp.concatenate(
        [gc[h:h + 1, :] for h in range(p * HEAD_PACK, (p + 1) * HEAD_PACK)], axis=1))
    decay = each(lambda p: jnp.where(incl, jnp.exp(jnp.where(incl, diff[p], 0.0)), 0.0))
    a_mat = each(lambda p: jnp.where(strict, _bdot_nt(k[p], k[p]) * decay[p] * col(p, 2), 0.0))
    qk = each(lambda p: _bdot_nt(q[p], k[p]) * decay[p])
    x = _unit_lower_inverse(a_mat, ii, jj)
    rhs = each(lambda p: jnp.concatenate([col(p, 3) * k[p], col(p, 2) * v[p]], axis=-1))
    wu = each(lambda p: _bdot(x[p], rhs[p]))
    v_new = each(lambda p: wu[p][:, dk:] - _bdot(diag(wu[p][:, :dk]), s[p]))
    qs = each(lambda p: _bdot(diag(q[p]), s[p]))
    o = each(lambda p: col(p, 1) * qs[p] + _bdot(qk[p], v_new[p]))
    kd = each(lambda p: jnp.where(same, jnp.concatenate([(k[p] * col(p, 4)).T] * HEAD_PACK, axis=0), 0.0))
    s_new = each(lambda p: col(p, 5) * s[p] + _bdot(kd[p], v_new[p]))
    outs = []
    for p in packs:
        for i in range(HEAD_PACK):
            s_ref[p * HEAD_PACK + i] = s_new[p][i * BLK:(i + 1) * BLK, :]
            outs.append(o[p][i * BLK:(i + 1) * BLK, :])
    return outs


def _delta_out(o, gate, ng_ref):
    return o * lax.rsqrt(jnp.mean(o * o, axis=-1, keepdims=True) + EPS) * ng_ref[...] * _silu(gate)


def _conv_silu(ext_ref, w_ref, t):
    acc = ext_ref[5:5 + t, :] * w_ref[0:1, :]
    for i in range(1, DN_CONV):
        acc = acc + ext_ref[5 + i:5 + i + t, :] * w_ref[i:i + 1, :]
    return _silu(acc)


def _delta_prompt_body(q_ref, k_ref, v_ref, gt_ref, ba_ref, wq_ref, wk_ref, wv_ref, alog_ref, dtb_ref, ng_ref,
                       y_ref, sout_ref, eq_ref, ek_ref, ev_ref, s_ref, *, nh, dk, nblk):
    c = pl.program_id(1)

    @pl.when(c == 0)
    def _():
        for e in (eq_ref, ek_ref, ev_ref):
            e[0:8, :] = jnp.zeros((8, e.shape[1]), F32)
        s_ref[...] = jnp.zeros(s_ref.shape, F32)

    xs = []
    for x_ref, e_ref, w_ref in ((q_ref, eq_ref, wq_ref), (k_ref, ek_ref, wk_ref), (v_ref, ev_ref, wv_ref)):
        e_ref[8:8 + BLK, :] = x_ref[...]
        xs.append(_conv_silu(e_ref, w_ref, BLK))
        e_ref[0:8, :] = x_ref[BLK - 8:, :]
    lane = lax.broadcasted_iota(jnp.int32, (nh, BLK), 1)
    valid = (c > 0) | (lane >= FRONT)
    outs = _delta_chunk(xs[0], xs[1], xs[2], ba_ref[...], valid, alog_ref, dtb_ref, s_ref, nh, dk)
    row = c * BLK + lax.broadcasted_iota(jnp.int32, (BLK, 1), 0)
    for h in range(nh):
        sl = slice(h * dk, (h + 1) * dk)
        y_ref[:, sl] = jnp.where(row >= FRONT, _delta_out(outs[h], gt_ref[:, sl], ng_ref), 0.0)

    @pl.when(c == nblk - 1)
    def _():
        sout_ref[0] = s_ref[...]


def _delta_prompt(p, ba, w_conv, a_log, dt_bias, norm_g, nseq, nblk, col_q, col_gate, nh, dk):
    hw = nh * dk
    cq, ck, cv, cg = col_q // hw, col_q // hw + 1, col_q // hw + 2, col_gate // hw
    rows = lambda cb: pl.BlockSpec((BLK, hw), lambda b, c: (b * nblk + c, cb))
    wcs = lambda cb: pl.BlockSpec((DN_CONV, hw), lambda b, c: (0, cb))
    small = lambda shape: pl.BlockSpec(shape, lambda b, c: (0, 0))
    return pl.pallas_call(
        functools.partial(_delta_prompt_body, nh=nh, dk=dk, nblk=nblk), grid=(nseq, nblk),
        in_specs=[rows(cq), rows(ck), rows(cv), rows(cg),
                  pl.BlockSpec((BLK, BLK), lambda b, c: (b * nblk + c, 0)),
                  wcs(0), wcs(1), wcs(2), small((nh, 1)), small((nh, 1)), small((1, dk))],
        out_specs=[pl.BlockSpec((BLK, hw), lambda b, c: (b * nblk + c, 0)),
                   pl.BlockSpec((1, nh, dk, dk), lambda b, c: (b, 0, 0, 0))],
        out_shape=[jax.ShapeDtypeStruct((p.shape[0], hw), F32),
                   jax.ShapeDtypeStruct((nseq, nh, dk, dk), F32)],
        scratch_shapes=[pltpu.VMEM((8 + BLK, hw), F32)] * 3 + [pltpu.VMEM((nh, dk, dk), F32)],
        compiler_params=_cparams("parallel", "arbitrary"), name="delta_prompt",
    )(p, p, p, p, ba, w_conv, w_conv, w_conv, a_log.reshape(nh, 1), dt_bias.reshape(nh, 1),
      norm_g.reshape(1, dk))


def _delta_sample_body(q_ref, k_ref, v_ref, gt_ref, ba_ref, hq_ref, hk_ref, hv_ref, s0_ref,
                       wq_ref, wk_ref, wv_ref, alog_ref, dtb_ref, ng_ref, ybuf_ref,
                       y_ref, sout_ref, eq_ref, ek_ref, ev_ref, s_ref, *, nh, dk):
    del ybuf_ref
    xs = []
    for x_ref, h_ref, e_ref, w_ref in ((q_ref, hq_ref, eq_ref, wq_ref), (k_ref, hk_ref, ek_ref, wk_ref),
                                       (v_ref, hv_ref, ev_ref, wv_ref)):
        e_ref[5:8, :] = h_ref[0]
        e_ref[8:8 + DEC_T, :] = x_ref[...]
        xs.append(_conv_silu(e_ref, w_ref, DEC_T))
    s_ref[...] = s0_ref[0]
    outs = _delta_chunk_small(xs[0], xs[1], xs[2], ba_ref[...], alog_ref, dtb_ref, s_ref, nh, dk)
    for h in range(nh):
        sl = slice(h * dk, (h + 1) * dk)
        y_ref[:, sl] = _delta_out(outs[h], gt_ref[:, sl], ng_ref)
    sout_ref[0] = s_ref[...]


def _delta_sample(p, ba, conv_hist, s0, w_conv, a_log, dt_bias, norm_g, ybuf, rows_per_seq,
                  col_q, col_gate, nh, dk):
    nb = s0.shape[0]
    hw = nh * dk
    cq, ck, cv, cg = col_q // hw, col_q // hw + 1, col_q // hw + 2, col_gate // hw
    rowmap = lambda cb: (lambda s: (_sample_row_block(s, rows_per_seq), cb))
    rows = lambda cb: pl.BlockSpec((DEC_T, hw), rowmap(cb))
    hist = lambda cb: pl.BlockSpec((1, DN_CONV - 1, hw), lambda s: (s, 0, cb))
    wcs = lambda cb: pl.BlockSpec((DN_CONV, hw), lambda s: (0, cb))
    small = lambda shape: pl.BlockSpec(shape, lambda s: (0, 0))
    return pl.pallas_call(
        functools.partial(_delta_sample_body, nh=nh, dk=dk), grid=(nb,),
        in_specs=[rows(cq), rows(ck), rows(cv), rows(cg), pl.BlockSpec((DEC_T, BLK), rowmap(0)),
                  hist(0), hist(1), hist(2),
                  pl.BlockSpec((1, nh, dk, dk), lambda s: (s, 0, 0, 0)),
                  wcs(0), wcs(1), wcs(2), small((nh, 1)), small((nh, 1)), small((1, dk)),
                  pl.BlockSpec(memory_space=pl.ANY)],
        out_specs=[pl.BlockSpec((DEC_T, hw), rowmap(0)),
                   pl.BlockSpec((1, nh, dk, dk), lambda s: (s, 0, 0, 0))],
        out_shape=[jax.ShapeDtypeStruct(ybuf.shape, F32), jax.ShapeDtypeStruct(s0.shape, F32)],
        scratch_shapes=[pltpu.VMEM((8 + DEC_T, hw), F32)] * 3 + [pltpu.VMEM((nh, dk, dk), F32)],
        input_output_aliases={15: 0},
        compiler_params=_cparams("arbitrary"), name="delta_sample",
    )(p, p, p, p, ba, conv_hist, conv_hist, conv_hist, s0, w_conv, w_conv, w_conv,
      a_log.reshape(nh, 1), dt_bias.reshape(nh, 1), norm_g.reshape(1, dk), ybuf)


def _attn_prompt_body(*refs, nkv, grp, hd, window):
    q_refs = refs[:nkv]
    kp_ref, kc_ref, vp_ref, vc_ref, slope_ref, sink_ref, y_ref = refs[nkv:]
    n = pl.program_id(1)
    i = lax.broadcasted_iota(jnp.int32, (BLK, 2 * BLK), 0)
    j = lax.broadcasted_iota(jnp.int32, (BLK, 2 * BLK), 1)
    dist = BLK + i - j
    krow = (n - 1) * BLK + j
    valid = (dist >= 0) & (dist < window) & (krow >= FRONT)
    distf = dist.astype(F32)
    kk = jnp.concatenate([kp_ref[...], kc_ref[...]], axis=0).astype(BF16)
    vv = jnp.concatenate([vp_ref[...], vc_ref[...]], axis=0).astype(BF16)
    heads = [(kv, g) for kv in range(nkv) for g in range(grp)]
    each = lambda f: [f(t, kv, g) for t, (kv, g) in enumerate(heads)]
    s = each(lambda t, kv, g: _bdot_nt(q_refs[kv][:, g * hd:(g + 1) * hd], kk[:, kv * hd:(kv + 1) * hd]))
    s = each(lambda t, kv, g: jnp.where(valid, s[t] * (hd ** -0.5) - slope_ref[t] * distf, NEG_INF))
    m = each(lambda t, kv, g: jnp.maximum(jnp.max(s[t], axis=-1, keepdims=True), sink_ref[t]))
    pr = each(lambda t, kv, g: jnp.exp(s[t] - m[t]))
    den = each(lambda t, kv, g: jnp.sum(pr[t], axis=-1, keepdims=True) + jnp.exp(sink_ref[t] - m[t]))
    pv = each(lambda t, kv, g: _bdot(pr[t], vv[:, kv * hd:(kv + 1) * hd]))
    for t in range(len(heads)):
        y_ref[:, t * hd:(t + 1) * hd] = pv[t] / den[t]


def _attn_prompt(p, slopes, sinks, nseq, nblk, col_q, col_k, col_v, nkv, grp, hd, window):
    gw, kw = grp * hd, nkv * hd
    prev = lambda cb: (lambda b, n: (b * nblk + jnp.maximum(n - 1, 0), cb))
    cur = lambda cb: (lambda b, n: (b * nblk + n, cb))
    smem = pl.BlockSpec(memory_space=pltpu.SMEM)
    return pl.pallas_call(
        functools.partial(_attn_prompt_body, nkv=nkv, grp=grp, hd=hd, window=window), grid=(nseq, nblk),
        in_specs=[pl.BlockSpec((BLK, gw), cur(col_q // gw + kv)) for kv in range(nkv)]
        + [pl.BlockSpec((BLK, kw), prev(col_k // kw)), pl.BlockSpec((BLK, kw), cur(col_k // kw)),
           pl.BlockSpec((BLK, kw), prev(col_v // kw)), pl.BlockSpec((BLK, kw), cur(col_v // kw)),
           smem, smem],
        out_specs=pl.BlockSpec((BLK, nkv * gw), cur(0)),
        out_shape=jax.ShapeDtypeStruct((p.shape[0], nkv * gw), F32),
        compiler_params=_cparams("parallel", "arbitrary"), name="attn_prompt",
    )(*([p] * (nkv + 4)), slopes, sinks)


def _attn_sample_body(*refs, nkv, grp, hd, window):
    q_refs = refs[:nkv]
    k_ref, v_ref, ck_ref, cv_ref, slope_ref, sink_ref, ybuf_ref, y_ref = refs[nkv:]
    del ybuf_ref
    wc = ck_ref.shape[1]
    i = lax.broadcasted_iota(jnp.int32, (DEC_T, wc), 0)
    j = lax.broadcasted_iota(jnp.int32, (DEC_T, wc), 1)
    dist_c = wc + i - j
    valid_c = (dist_c >= 0) & (dist_c < window)
    i2 = lax.broadcasted_iota(jnp.int32, (DEC_T, DEC_T), 0)
    j2 = lax.broadcasted_iota(jnp.int32, (DEC_T, DEC_T), 1)
    dist_n = i2 - j2
    valid_n = (dist_n >= 0) & (dist_n < window)
    tile = lambda a: jnp.concatenate([a] * grp, axis=0)
    valid_c, valid_n = tile(valid_c), tile(valid_n)
    dist_c, dist_n = tile(dist_c).astype(F32), tile(dist_n).astype(F32)
    gi = lax.broadcasted_iota(jnp.int32, (grp * DEC_T, 1), 0) // DEC_T

    def per_row(ref, kv):
        out = jnp.full((grp * DEC_T, 1), ref[kv * grp], F32)
        for g in range(1, grp):
            out = jnp.where(gi == g, ref[kv * grp + g], out)
        return out

    each = lambda f: [f(kv) for kv in range(nkv)]
    cols = lambda ref, kv: ref[:, kv * hd:(kv + 1) * hd]
    q = each(lambda kv: jnp.concatenate([q_refs[kv][:, g * hd:(g + 1) * hd] for g in range(grp)], axis=0))
    slope = each(lambda kv: per_row(slope_ref, kv))
    sink = each(lambda kv: per_row(sink_ref, kv))
    sc = each(lambda kv: _bdot_nt(q[kv], ck_ref[0, :, kv * hd:(kv + 1) * hd]) * (hd ** -0.5))
    sn = each(lambda kv: _bdot_nt(q[kv], cols(k_ref, kv)) * (hd ** -0.5))
    sc = each(lambda kv: jnp.where(valid_c, sc[kv] - slope[kv] * dist_c, NEG_INF))
    sn = each(lambda kv: jnp.where(valid_n, sn[kv] - slope[kv] * dist_n, NEG_INF))
    m = each(lambda kv: jnp.maximum(jnp.maximum(jnp.max(sc[kv], axis=-1, keepdims=True),
                                                jnp.max(sn[kv], axis=-1, keepdims=True)), sink[kv]))
    pc = each(lambda kv: jnp.exp(sc[kv] - m[kv]))
    pn = each(lambda kv: jnp.exp(sn[kv] - m[kv]))
    den = each(lambda kv: jnp.sum(pc[kv], axis=-1, keepdims=True) + jnp.sum(pn[kv], axis=-1, keepdims=True)
               + jnp.exp(sink[kv] - m[kv]))
    o = each(lambda kv: (_bdot(pc[kv], cv_ref[0, :, kv * hd:(kv + 1) * hd]) + _bdot(pn[kv], cols(v_ref, kv)))
             / den[kv])
    for kv in range(nkv):
        for g in range(grp):
            head = kv * grp + g
            y_ref[:, head * hd:(head + 1) * hd] = o[kv][g * DEC_T:(g + 1) * DEC_T, :]


def _attn_sample(p, cache_k, cache_v, slopes, sinks, ybuf, rows_per_seq, col_q, col_k, col_v,
                 nkv, grp, hd, window):
    nb, wc = cache_k.shape[0], cache_k.shape[1]
    gw, kw = grp * hd, nkv * hd
    ck3 = cache_k.reshape(nb, wc, kw)
    cv3 = cache_v.reshape(nb, wc, kw)
    rowmap = lambda cb: (lambda s: (_sample_row_block(s, rows_per_seq), cb))
    smem = pl.BlockSpec(memory_space=pltpu.SMEM)
    return pl.pallas_call(
        functools.partial(_attn_sample_body, nkv=nkv, grp=grp, hd=hd, window=window), grid=(nb,),
        in_specs=[pl.BlockSpec((DEC_T, gw), rowmap(col_q // gw + kv)) for kv in range(nkv)]
        + [pl.BlockSpec((DEC_T, kw), rowmap(col_k // kw)),
           pl.BlockSpec((DEC_T, kw), rowmap(col_v // kw)),
           pl.BlockSpec((1, wc, kw), lambda s: (s, 0, 0)),
           pl.BlockSpec((1, wc, kw), lambda s: (s, 0, 0)),
           smem, smem, pl.BlockSpec(memory_space=pl.ANY)],
        out_specs=pl.BlockSpec((DEC_T, nkv * gw), rowmap(0)),
        out_shape=jax.ShapeDtypeStruct(ybuf.shape, F32),
        input_output_aliases={nkv + 6: 0},
        compiler_params=_cparams("arbitrary"), name="attn_sample",
    )(*([p] * (nkv + 2)), ck3, cv3, slopes, sinks, ybuf)


SUB = 8


def _bitonic_pairs(n, merge_only=False):
    out = []
    k = n if merge_only else 2
    while k <= n:
        j = k // 2
        while j >= 1:
            out += [(i, i ^ j, (i & k) == 0) for i in range(n) if (i ^ j) > i]
            j //= 2
        k *= 2
    return out


def _compare_exchange(v, pairs):
    v = list(v)
    for i, l, desc in pairs:
        hi, lo = jnp.maximum(v[i], v[l]), jnp.minimum(v[i], v[l])
        v[i], v[l] = (hi, lo) if desc else (lo, hi)
    return v


def _top16(rows):
    v = _compare_exchange(rows, _bitonic_pairs(TOPK))
    shift = SUB // 2
    while shift >= 1:
        w = [pltpu.roll(v[TOPK - 1 - r], shift, axis=0) for r in range(TOPK)]
        v = _compare_exchange([jnp.maximum(a, b) for a, b in zip(v, w)], _bitonic_pairs(TOPK, merge_only=True))
        shift //= 2
    return v


def _sublane_sum(x):
    shift = SUB // 2
    while shift >= 1:
        x = x + pltpu.roll(x, shift, axis=0)
        shift //= 2
    return x


def _on_sublanes(vs):
    sub = lax.broadcasted_iota(jnp.int32, vs[0].shape, 0)
    out = vs[SUB - 1]
    for j in range(SUB - 2, -1, -1):
        out = jnp.where(sub == j, vs[j], out)
    return out


def _peer_topk_body(q_ref, keys_ref, rk_ref, cut_ref, e1_ref, e2_ref, *, nh):
    nv = NKEYS // SUB
    for h in range(nh):
        sc = [_bdot_nt(keys_ref[2 * h + half], q_ref[:, (2 * h + half) * NKEYS:(2 * h + half + 1) * NKEYS])
              for half in (0, 1)]
        s1 = [sc[0][SUB * i:SUB * (i + 1), :] for i in range(nv)]
        s2 = [sc[1][SUB * i:SUB * (i + 1), :] for i in range(nv)]
        a = _top16(s1)
        b = _top16(s2)
        b_lo, b_hi, a_hi = _on_sublanes(b[:SUB]), _on_sublanes(b[SUB:]), _on_sublanes(a[SUB:])
        cand = [a[0] + b_lo, a[0] + b_hi, a_hi + b[0]] + [a[i] + b_lo for i in range(1, SUB)]
        cand += [jnp.full(cand[0].shape, LOWEST, F32)] * (TOPK - len(cand))
        top = _top16(cand)
        thr = top[TOPK - 1]
        zsum = jnp.exp(top[0] - top[0])
        for r in range(1, TOPK):
            zsum = zsum + jnp.exp(top[r] - top[0])
        rz = 1.0 / zsum
        height = [_sublane_sum(jnp.where(a[r] + b_lo >= thr, 1.0, 0.0) + jnp.where(a[r] + b_hi >= thr, 1.0, 0.0))
                  for r in range(TOPK)]
        cut, rank2 = [], []
        for i in range(nv):
            c = jnp.zeros(s1[i].shape, F32)
            for r in range(TOPK - 1, -1, -1):
                c = jnp.where(s1[i] == a[r], height[r], c)
            cut.append(c)
            k = jnp.where(b[0] > s2[i], 1.0, 0.0)
            for r in range(1, TOPK):
                k = k + jnp.where(b[r] > s2[i], 1.0, 0.0)
            rank2.append(k)
        rk_ref[h] = jnp.concatenate(rank2, axis=0).astype(BF16)
        cut_ref[h] = jnp.concatenate(cut, axis=0)
        e1_ref[h] = jnp.exp(sc[0] - a[0][0:1, :])
        e2_ref[h] = (jnp.exp(sc[1] - b[0][0:1, :]) * rz[0:1, :]).astype(BF16)


def _peer_topk(q, sub_keys, *, tm=128):
    n = q.shape[0]
    nh = sub_keys.shape[0]
    keys = sub_keys.reshape(2 * nh, NKEYS, sub_keys.shape[-1])
    sspec = pl.BlockSpec((nh, NKEYS, tm), lambda i: (0, 0, i))
    shape = lambda dt: jax.ShapeDtypeStruct((nh, NKEYS, n), dt)
    return pl.pallas_call(
        functools.partial(_peer_topk_body, nh=nh), grid=(n // tm,),
        in_specs=[pl.BlockSpec((tm, q.shape[1]), lambda i: (i, 0)),
                  pl.BlockSpec(keys.shape, lambda i: (0, 0, 0))],
        out_specs=[sspec] * 4,
        out_shape=[shape(BF16), shape(F32), shape(F32), shape(BF16)],
        compiler_params=_cparams("parallel"), name="peer_topk")(q, keys)


def _gelu(x):
    return 0.5 * x * (1.0 + lax.erf(x * (2.0 ** -0.5)))


PACK = 16


def _peer_expert_body(xn_ref, wd_ref, wu_ref, id_ref, iu_ref, rk_ref, cut_ref, e1_ref, e2_ref, y_ref,
                      xt_ref, ix_ref, *, nh, ei, sub):
    c = pl.program_id(1)
    tm = xt_ref.shape[1]
    tw = tm // sub
    slabs = [slice(t * tw, (t + 1) * tw) for t in range(sub)]
    each = lambda f: [f(t, tl) for t, tl in enumerate(slabs)]

    @pl.when(c == 0)
    def _():
        y_ref[...] = jnp.zeros(y_ref.shape, F32)
        x = xn_ref[...].astype(F32)
        sx = _amax_scale(x)
        xt_ref[...] = (x * sx).T.astype(FP8)
        ix_ref[...] = jnp.broadcast_to(1.0 / sx, ix_ref.shape)

    def gates(tl):
        def row16(ref, h, i1):
            return jnp.broadcast_to(ref[h, pl.ds(i1, 1), tl], (PACK, tw)).astype(BF16)
        pieces = []
        for ii in range(ei):
            i1 = c * ei + ii
            cut = [row16(cut_ref, h, i1) for h in range(nh)]
            e1 = [row16(e1_ref, h, i1) for h in range(nh)]
            for r in range(NKEYS // PACK):
                sl = slice(r * PACK, (r + 1) * PACK)
                gate = pieces[-1] * 0.0 if pieces else None
                for h in range(nh):
                    w = jnp.where(rk_ref[h, sl, tl] < cut[h], e1[h] * e2_ref[h, sl, tl], 0.0)
                    gate = w if gate is None else gate + w
                pieces.append(gate)
        return pieces

    inv_h = id_ref[0, 0:1, 0:1] * ix_ref[0:1, 0:1]
    inv_u = iu_ref[0, 0:1, 0:1]
    ht = each(lambda t, tl: jnp.dot(wd_ref[...], xt_ref[:, tl], preferred_element_type=F32) * inv_h)
    gate = each(lambda t, tl: gates(tl))
    act = each(lambda t, tl: _gelu(ht[t]))
    at = each(lambda t, tl: jnp.concatenate(
        [act[t][i * PACK:(i + 1) * PACK, :] * g.astype(F32) for i, g in enumerate(gate[t])], axis=0))
    sa = each(lambda t, tl: _amax_scale(at[t]))
    part = each(lambda t, tl: jnp.dot((at[t] * sa[t]).T.astype(FP8), wu_ref[...], preferred_element_type=F32))
    for t, tl in enumerate(slabs):
        y_ref[tl, :] += part[t] * (inv_u / sa[t])


PEER_TM, PEER_EI, PEER_SUB = 512, 8, 2


def _peer_expert(xn, w_down, w_up, inv_down, inv_up, rank2, cut, e1, e2, *, tm=PEER_TM, ei=PEER_EI, sub=PEER_SUB):
    n, d = xn.shape
    nh = rank2.shape[0]
    e = ei * NKEYS
    once = dict(pipeline_mode=pl.Buffered(1))
    sspec = pl.BlockSpec((nh, NKEYS, tm), lambda i, c: (0, 0, i))
    inv = pl.BlockSpec((1, 8, BLK), lambda i, c: (c, 0, 0))
    return pl.pallas_call(
        functools.partial(_peer_expert_body, nh=nh, ei=ei, sub=sub), grid=(n // tm, w_down.shape[0] // e),
        in_specs=[pl.BlockSpec((tm, d), lambda i, c: (i, 0)),
                  pl.BlockSpec((e, d), lambda i, c: (c, 0)),
                  pl.BlockSpec((e, d), lambda i, c: (c, 0)),
                  inv, inv, sspec, sspec, sspec, sspec],
        out_specs=pl.BlockSpec((tm, d), lambda i, c: (i, 0), **once),
        out_shape=jax.ShapeDtypeStruct((n, d), F32),
        scratch_shapes=[pltpu.VMEM((d, tm), FP8), pltpu.VMEM((8, BLK), F32)],
        compiler_params=_cparams("parallel", "arbitrary"), name="peer_expert",
    )(xn, w_down, w_up, inv_down, inv_up, rank2, cut, e1, e2)


def _final_norm(h, y, g, nseq, nblk, row_blk, first_blk, nblk_out):
    d = h.shape[1]
    row = pl.BlockSpec((row_blk, d), lambda b, i: (b * nblk + first_blk + i, 0))
    return pl.pallas_call(
        _final_norm_body, grid=(nseq, nblk_out),
        in_specs=[row, row, pl.BlockSpec((1, d), lambda b, i: (0, 0))],
        out_specs=pl.BlockSpec((1, row_blk, d), lambda b, i: (b, i, 0)),
        out_shape=jax.ShapeDtypeStruct((nseq, nblk_out * row_blk, d), F32),
        compiler_params=_cparams("parallel", "parallel"), name="final_norm",
    )(h, y, g.reshape(1, d).astype(F32))


def _final_norm_body(h_ref, y_ref, g_ref, o_ref):
    x = h_ref[...] + y_ref[...]
    ms = jnp.mean(x * x, axis=-1, keepdims=True)
    o_ref[0] = x * lax.rsqrt(ms + EPS) * g_ref[...]


def kernel(x_prompt, x_sample, state_pool, state_conv, state_delta, cache_k, cache_v, meta_tokens, norm1_g,
           w_in, w_pool, s_pool, w_conv, a_log, dt_bias, dn_norm_g, attn_sinks, w_out, norm2_g,
           peer_w_query, peer_sub_keys, peer_w_down, peer_w_up, final_norm_g):
    nseq, seq, d = x_prompt.shape
    nsamp, dec_t, _ = x_sample.shape
    depth = w_in.shape[0]
    pool_w = w_pool.shape[1] * w_pool.shape[2]
    dn_qkv = w_conv.shape[2]
    nh, dk = state_delta.shape[2], state_delta.shape[3]
    wc, nkv, hd = cache_k.shape[2], cache_k.shape[3], cache_k.shape[4]
    nq = attn_sinks.shape[1]
    grp = nq // nkv
    window = wc
    assert dec_t == DEC_T and nsamp == nseq * SLOTS and (N_META + seq) % BLK == N_META
    assert SLOTS * DEC_T + POOL_HIST <= FRONT and dn_qkv == 3 * nh * dk and wc == BLK
    rows_per_seq = FRONT + N_META + seq
    nblk = rows_per_seq // BLK
    col_qkv = pool_w
    col_gate = col_qkv + dn_qkv
    col_q = col_gate + nh * dk
    col_k = col_q + nq * hd
    col_v = col_k + nkv * hd
    src_ba = pool_w + dn_qkv

    slopes = jnp.exp2(-8.0 * (jnp.arange(nq, dtype=F32) + 1.0) / nq)
    new_p = [[] for _ in range(5)]
    new_s = [[] for _ in range(5)]
    y_peer = None
    w_in_t = jnp.swapaxes(w_in, 1, 2)
    for l in range(depth):
        w_main, w_ba = _cast_w_in(w_in_t, l, src_ba, 2 * nh)
        if l == 0:
            h, xn = _assemble_norm(x_prompt, x_sample, meta_tokens, norm1_g[l], nblk)
        else:
            h, xn = _addnorm(h, y_peer, norm1_g[l])
        p = _mm(xn, w_main, nt=True)
        ba = _mm(xn, w_ba, nt=True)

        y_pool = _pool_prompt(p, w_pool[l], s_pool[l].reshape(1, pool_w), nseq, nblk)
        y_pool = _pool_sample(p, state_pool[l], w_pool[l], s_pool[l].reshape(1, pool_w), y_pool,
                              rows_per_seq, PAST_LEN)
        y_dn, s_p = _delta_prompt(p, ba, w_conv[l], a_log[l], dt_bias[l], dn_norm_g[l], nseq, nblk,
                                  col_qkv, col_gate, nh, dk)
        y_dn, s_s = _delta_sample(p, ba, state_conv[l], state_delta[l], w_conv[l], a_log[l], dt_bias[l],
                                  dn_norm_g[l], y_dn, rows_per_seq, col_qkv, col_gate, nh, dk)
        y_att = _attn_prompt(p, slopes, attn_sinks[l], nseq, nblk, col_q, col_k, col_v, nkv, grp, hd, window)
        y_att = _attn_sample(p, cache_k[l], cache_v[l], slopes, attn_sinks[l], y_att, rows_per_seq,
                             col_q, col_k, col_v, nkv, grp, hd, window)
        h = _outproj(y_pool, y_dn, y_att, _cast(w_out, l), h)

        _, xn2 = _addnorm(h, None, norm2_g[l])
        q = _mm(xn2, _cast(peer_w_query, l))
        rank2, cut, e1, e2 = _peer_topk(q, peer_sub_keys[l])
        wd8, inv_d = _quant_fp8(peer_w_down, l, tr=PEER_EI * NKEYS)
        wu8, inv_u = _quant_fp8(peer_w_up, l, tr=PEER_EI * NKEYS)
        y_peer = _peer_expert(xn2, wd8, wu8, inv_d, inv_u, rank2, cut, e1, e2)

        p3 = p.reshape(nseq, rows_per_seq, p.shape[1])
        ps = p3[:, :SLOTS * DEC_T].reshape(nsamp, DEC_T, p.shape[1])
        new_p[0].append(p3[:, -POOL_HIST:, :pool_w])
        new_s[0].append(jnp.concatenate([state_pool[l], ps[:, :, :pool_w]], axis=1)[:, -POOL_HIST:])
        new_p[1].append(p3[:, -(DN_CONV - 1):, col_qkv:col_gate])
        new_s[1].append(jnp.concatenate([state_conv[l], ps[:, :, col_qkv:col_gate]], axis=1)[:, -(DN_CONV - 1):])
        new_p[2].append(s_p)
        new_s[2].append(s_s)
        new_p[3].append(p3[:, -window:, col_k:col_v].reshape(nseq, window, nkv, hd))
        new_s[3].append(jnp.concatenate([cache_k[l], ps[:, :, col_k:col_v].reshape(nsamp, DEC_T, nkv, hd)],
                                        axis=1)[:, -wc:])
        new_p[4].append(p3[:, -window:, col_v:].reshape(nseq, window, nkv, hd))
        new_s[4].append(jnp.concatenate([cache_v[l], ps[:, :, col_v:].reshape(nsamp, DEC_T, nkv, hd)],
                                        axis=1)[:, -wc:])

    y_prompt = _final_norm(h, y_peer, final_norm_g, nseq, nblk, BLK, 1, nblk - 1)
    y_sample = _final_norm(h, y_peer, final_norm_g, nseq, rows_per_seq // (SLOTS * DEC_T), SLOTS * DEC_T, 0, 1)
    y_sample = y_sample.reshape(nsamp, DEC_T, d)
    pool_p, conv_p, delta_p, k_p, v_p = (jnp.stack(a) for a in new_p)
    pool_s, conv_s, delta_s, k_s, v_s = (jnp.stack(a) for a in new_s)
    return (y_prompt, y_sample, pool_p, pool_s, conv_p, conv_s, delta_p, delta_s, k_p, k_s, v_p, v_s)
```

```python
import functools

import jax
import jax.numpy as jnp
from jax import lax
from jax.experimental import pallas as pl
from jax.experimental.pallas import tpu as pltpu

F32 = jnp.float32
BF16 = jnp.bfloat16

EPS = 1e-6
NEG_INF = -1e30
LOWEST = -3.0e38

PAST_LEN = 16384
N_META = 16
BLK = 128
FRONT = BLK - N_META
DEC_T = 8
SLOTS = 8
POOL_WINDOWS = (2, 4, 8, 16)
POOL_HIST = 15
DN_CONV = 4
TOPK = 16
NKEYS = 128
VMEM_LIMIT = 56 * 1024 * 1024


def _cparams(*sem):
    return pltpu.CompilerParams(dimension_semantics=sem, vmem_limit_bytes=VMEM_LIMIT)


def _bdot(a, b):
    return jnp.dot(a.astype(BF16), b.astype(BF16), preferred_element_type=F32)


def _bdot_nt(a, b):
    return lax.dot_general(a.astype(BF16), b.astype(BF16), (((1,), (1,)), ((), ())),
                           preferred_element_type=F32)


def _silu(x):
    return x * (1.0 / (1.0 + jnp.exp(-x)))


def _addnorm_body(*refs, add):
    if add:
        h_ref, y_ref, g_ref, hs_ref, xn_ref = refs
        x = h_ref[...] + y_ref[...]
        hs_ref[...] = x
    else:
        h_ref, g_ref, xn_ref = refs
        x = h_ref[...]
    ms = jnp.mean(x * x, axis=-1, keepdims=True)
    xn_ref[...] = (x * lax.rsqrt(ms + EPS) * g_ref[...]).astype(xn_ref.dtype)


def _addnorm(h, y, g, *, tm=256, out_dtype=BF16):
    n, d = h.shape
    row = pl.BlockSpec((tm, d), lambda i: (i, 0))
    gspec = pl.BlockSpec((1, d), lambda i: (0, 0))
    g2 = g.reshape(1, d).astype(F32)
    if y is None:
        xn = pl.pallas_call(
            functools.partial(_addnorm_body, add=False),
            grid=(n // tm,), in_specs=[row, gspec], out_specs=row,
            out_shape=jax.ShapeDtypeStruct((n, d), out_dtype),
            compiler_params=_cparams("parallel"), name="norm")(h, g2)
        return h, xn
    hs, xn = pl.pallas_call(
        functools.partial(_addnorm_body, add=True),
        grid=(n // tm,), in_specs=[row, row, gspec], out_specs=[row, row],
        out_shape=[jax.ShapeDtypeStruct((n, d), F32), jax.ShapeDtypeStruct((n, d), out_dtype)],
        compiler_params=_cparams("parallel"), name="add_norm")(h, y, g2)
    return hs, xn


def _assemble_norm_body(xs_ref, meta_ref, xp_ref, g_ref, h_ref, xn_ref):
    n = pl.program_id(1)

    @pl.when(n == 0)
    def _():
        ns = xs_ref.shape[1]
        h_ref[0:ns, :] = xs_ref[0]
        h_ref[ns:FRONT, :] = jnp.zeros((FRONT - ns, h_ref.shape[1]), F32)
        h_ref[FRONT:, :] = meta_ref[...]

    @pl.when(n > 0)
    def _():
        h_ref[...] = xp_ref[0]

    x = h_ref[...]
    ms = jnp.mean(x * x, axis=-1, keepdims=True)
    xn_ref[...] = (x * lax.rsqrt(ms + EPS) * g_ref[...]).astype(xn_ref.dtype)


def _assemble_norm(x_prompt, x_sample, meta_tokens, g, nblk):
    nseq, _, d = x_prompt.shape
    xs = x_sample.reshape(nseq, SLOTS * DEC_T, d)
    out = pl.BlockSpec((BLK, d), lambda b, n: (b * nblk + n, 0))
    return pl.pallas_call(
        _assemble_norm_body, grid=(nseq, nblk),
        in_specs=[pl.BlockSpec((1, SLOTS * DEC_T, d), lambda b, n: (b, 0, 0)),
                  pl.BlockSpec((N_META, d), lambda b, n: (0, 0)),
                  pl.BlockSpec((1, BLK, d), lambda b, n: (b, jnp.maximum(n - 1, 0), 0)),
                  pl.BlockSpec((1, d), lambda b, n: (0, 0))],
        out_specs=[out, out],
        out_shape=[jax.ShapeDtypeStruct((nseq * nblk * BLK, d), F32),
                   jax.ShapeDtypeStruct((nseq * nblk * BLK, d), BF16)],
        compiler_params=_cparams("parallel", "arbitrary"), name="assemble_norm",
    )(xs, meta_tokens, x_prompt, g.reshape(1, d).astype(F32))


def _cast_body(x_ref, o_ref):
    o_ref[...] = x_ref[0].astype(o_ref.dtype)


def _cast(x, l, dtype=BF16, *, tr=512):
    _, r, c = x.shape
    return pl.pallas_call(
        _cast_body, grid=(r // tr,),
        in_specs=[pl.BlockSpec((1, tr, c), lambda i: (l, i, 0))],
        out_specs=pl.BlockSpec((tr, c), lambda i: (i, 0)),
        out_shape=jax.ShapeDtypeStruct((r, c), dtype),
        compiler_params=_cparams("parallel"), name="cast")(x)


FP8 = jnp.float8_e4m3fn
FP8_TARGET = 224.0


def _amax_scale(x):
    a = jnp.max(jnp.max(jnp.abs(x), axis=1, keepdims=True), axis=0, keepdims=True)
    return jnp.where(a > 0.0, FP8_TARGET / a, 1.0)


def _quant_body(x_ref, o_ref, inv_ref):
    x = x_ref[0]
    s = _amax_scale(x)
    o_ref[...] = (x * s).astype(FP8)
    inv_ref[0] = jnp.broadcast_to(1.0 / s, inv_ref.shape[1:])


def _quant_fp8(x, l, *, tr):
    _, r, c = x.shape
    return pl.pallas_call(
        _quant_body, grid=(r // tr,),
        in_specs=[pl.BlockSpec((1, tr, c), lambda i: (l, i, 0))],
        out_specs=[pl.BlockSpec((tr, c), lambda i: (i, 0)), pl.BlockSpec((1, 8, BLK), lambda i: (i, 0, 0))],
        out_shape=[jax.ShapeDtypeStruct((r, c), FP8), jax.ShapeDtypeStruct((r // tr, 8, BLK), F32)],
        compiler_params=_cparams("parallel"), name="quant_fp8")(x)


def _cast_w_in_body(a_ref, b_ref, main_ref, ba_ref, *, nlo, nba):
    j = pl.program_id(0)
    tr = a_ref.shape[1]

    @pl.when(j < nlo)
    def _():
        main_ref[...] = a_ref[0].astype(BF16)

    @pl.when(j >= nlo)
    def _():
        main_ref[0:tr - nba, :] = a_ref[0, nba:, :].astype(BF16)
        main_ref[tr - nba:, :] = b_ref[0].astype(BF16)

    @pl.when(j == nlo)
    def _():
        ba_ref[0:nba, :] = a_ref[0, 0:nba, :].astype(BF16)
        ba_ref[nba:, :] = jnp.zeros((ba_ref.shape[0] - nba, ba_ref.shape[1]), BF16)


def _cast_w_in(w_in_t, l, lo, nba, *, tr=1024):
    _, r, c = w_in_t.shape
    assert lo % tr == 0 and (r - nba) % tr == 0 and tr % nba == 0 and nba % 8 == 0
    return pl.pallas_call(
        functools.partial(_cast_w_in_body, nlo=lo // tr, nba=nba), grid=((r - nba) // tr,),
        in_specs=[pl.BlockSpec((1, tr, c), lambda j: (l, j, 0)),
                  pl.BlockSpec((1, nba, c), lambda j: (l, (j + 1) * (tr // nba), 0))],
        out_specs=[pl.BlockSpec((tr, c), lambda j: (j, 0)), pl.BlockSpec((BLK, c), lambda j: (0, 0))],
        out_shape=[jax.ShapeDtypeStruct((r - nba, c), BF16), jax.ShapeDtypeStruct((BLK, c), BF16)],
        compiler_params=_cparams("arbitrary"), name="cast_w_in")(w_in_t, w_in_t)


def _mm_body(x_ref, w_ref, o_ref, *, nt):
    if nt:
        o_ref[...] = lax.dot_general(x_ref[...], w_ref[...], (((1,), (1,)), ((), ())),
                                     preferred_element_type=F32)
    else:
        o_ref[...] = jnp.dot(x_ref[...], w_ref[...], preferred_element_type=F32)


def _mm(x, w, *, nt=False, tm=512, tn=2048):
    m, k = x.shape
    n = w.shape[0] if nt else w.shape[1]
    tn = min(tn, n)
    wspec = pl.BlockSpec((tn, k), lambda j, i: (j, 0)) if nt else pl.BlockSpec((k, tn), lambda j, i: (0, j))
    return pl.pallas_call(
        functools.partial(_mm_body, nt=nt), grid=(n // tn, m // tm),
        in_specs=[pl.BlockSpec((tm, k), lambda j, i: (i, 0)), wspec],
        out_specs=pl.BlockSpec((tm, tn), lambda j, i: (i, j)),
        out_shape=jax.ShapeDtypeStruct((m, n), F32),
        compiler_params=_cparams("parallel", "parallel"), name="matmul")(x, w)


def _outproj_body(yp_ref, yd_ref, ya_ref, w_ref, h_ref, o_ref, *, wp, wd):
    acc = h_ref[...]
    acc += jnp.dot(yp_ref[...].astype(BF16), w_ref[0:wp, :], preferred_element_type=F32)
    acc += jnp.dot(yd_ref[...].astype(BF16), w_ref[wp:wp + wd, :], preferred_element_type=F32)
    acc += jnp.dot(ya_ref[...].astype(BF16), w_ref[wp + wd:, :], preferred_element_type=F32)
    o_ref[...] = acc


def _outproj(yp, yd, ya, w, h, *, tm=512, tn=1024):
    m, d = h.shape
    wp, wd, wa = yp.shape[1], yd.shape[1], ya.shape[1]
    k = wp + wd + wa
    return pl.pallas_call(
        functools.partial(_outproj_body, wp=wp, wd=wd), grid=(d // tn, m // tm),
        in_specs=[pl.BlockSpec((tm, wp), lambda j, i: (i, 0)),
                  pl.BlockSpec((tm, wd), lambda j, i: (i, 0)),
                  pl.BlockSpec((tm, wa), lambda j, i: (i, 0)),
                  pl.BlockSpec((k, tn), lambda j, i: (0, j)),
                  pl.BlockSpec((tm, tn), lambda j, i: (i, j))],
        out_specs=pl.BlockSpec((tm, tn), lambda j, i: (i, j)),
        out_shape=jax.ShapeDtypeStruct((m, d), F32),
        compiler_params=_cparams("parallel", "parallel"), name="out_proj")(yp, yd, ya, w, h)


def _pool_windows(ext_ref, u, t, pos, w_ref, s_ref, gw):
    outs = []
    for gi, w in enumerate(POOL_WINDOWS):
        sl = slice(gi * gw, (gi + 1) * gw)
        win = u[:, sl]
        for k in range(1, w):
            win = win + ext_ref[16 - k:16 - k + t, sl]
        cnt = jnp.clip(pos + 1, 1, w).astype(F32)
        d = win / cnt - u[:, sl]
        outs.append(_bdot(d, w_ref[gi]))
    return jnp.concatenate(outs, axis=-1) * s_ref[...]


def _pool_prompt_body(u_ref, w_ref, s_ref, y_ref, ext_ref, *, gw):
    n = pl.program_id(1)

    @pl.when(n == 0)
    def _():
        ext_ref[0:16, :] = jnp.zeros((16, ext_ref.shape[1]), F32)

    u = u_ref[...]
    ext_ref[16:16 + BLK, :] = u
    row = n * BLK + lax.broadcasted_iota(jnp.int32, (BLK, 1), 0)
    pos = row - FRONT
    y = _pool_windows(ext_ref, u, BLK, pos, w_ref, s_ref, gw)
    y_ref[...] = jnp.where(pos >= 0, y, 0.0)
    ext_ref[0:16, :] = u[BLK - 16:, :]


def _pool_prompt(p, w_pool, s_pool, nseq, nblk):
    pw = w_pool.shape[0] * w_pool.shape[1]
    gw = w_pool.shape[1]
    return pl.pallas_call(
        functools.partial(_pool_prompt_body, gw=gw), grid=(nseq, nblk),
        in_specs=[pl.BlockSpec((BLK, pw), lambda b, n: (b * nblk + n, 0)),
                  pl.BlockSpec(w_pool.shape, lambda b, n: (0, 0, 0)),
                  pl.BlockSpec((1, pw), lambda b, n: (0, 0))],
        out_specs=pl.BlockSpec((BLK, pw), lambda b, n: (b * nblk + n, 0)),
        out_shape=jax.ShapeDtypeStruct((p.shape[0], pw), F32),
        scratch_shapes=[pltpu.VMEM((16 + BLK, pw), F32)],
        compiler_params=_cparams("parallel", "arbitrary"), name="pool_prompt")(p, w_pool, s_pool)


def _pool_sample_body(u_ref, hist_ref, w_ref, s_ref, ybuf_ref, y_ref, ext_ref, *, gw, pos0):
    del ybuf_ref
    u = u_ref[...]
    ext_ref[0:1, :] = jnp.zeros((1, ext_ref.shape[1]), F32)
    ext_ref[1:16, :] = hist_ref[0]
    ext_ref[16:16 + DEC_T, :] = u
    pos = pos0 + lax.broadcasted_iota(jnp.int32, (DEC_T, 1), 0)
    y_ref[...] = _pool_windows(ext_ref, u, DEC_T, pos, w_ref, s_ref, gw)


def _sample_row_block(s, rows_per_seq):
    return (s // SLOTS) * (rows_per_seq // DEC_T) + s % SLOTS


def _pool_sample(p, hist, w_pool, s_pool, ybuf, rows_per_seq, pos0):
    nb = hist.shape[0]
    pw = hist.shape[2]
    gw = w_pool.shape[1]
    rowmap = lambda s: (_sample_row_block(s, rows_per_seq), 0)
    return pl.pallas_call(
        functools.partial(_pool_sample_body, gw=gw, pos0=pos0), grid=(nb,),
        in_specs=[pl.BlockSpec((DEC_T, pw), rowmap),
                  pl.BlockSpec((1, POOL_HIST, pw), lambda s: (s, 0, 0)),
                  pl.BlockSpec(w_pool.shape, lambda s: (0, 0, 0)),
                  pl.BlockSpec((1, pw), lambda s: (0, 0)),
                  pl.BlockSpec(memory_space=pl.ANY)],
        out_specs=pl.BlockSpec((DEC_T, pw), rowmap),
        out_shape=jax.ShapeDtypeStruct(ybuf.shape, F32),
        scratch_shapes=[pltpu.VMEM((16 + DEC_T, pw), F32)],
        input_output_aliases={4: 0},
        compiler_params=_cparams("arbitrary"), name="pool_sample")(p, hist, w_pool, s_pool, ybuf)


def _cumsum_lanes(x):
    lane = lax.broadcasted_iota(jnp.int32, x.shape, 1)
    s = 1
    while s < x.shape[1]:
        x = x + jnp.where(lane >= s, pltpu.roll(x, s, axis=1), 0.0)
        s *= 2
    return x


HEAD_PACK = 2


def _unit_lower_inverse(mats, ii, jj):
    eye = jnp.where(ii == jj, 1.0, 0.0).astype(F32)
    pair = ((ii // 2) == (jj // 2)) & (ii % 2 == 1) & (jj % 2 == 0)
    xs = [eye - jnp.where(pair, a, 0.0) for a in mats]
    s = 2
    while s < BLK:
        mask = ((ii // (2 * s)) == (jj // (2 * s))) & ((ii // s) % 2 == 1) & ((jj // s) % 2 == 0)
        ts = [_bdot(jnp.where(mask, a, 0.0), x) for a, x in zip(mats, xs)]
        xs = [x - _bdot(x, t) for x, t in zip(xs, ts)]
        s *= 2
    return xs


def _delta_scalars(ba, valid, alog_ref, dtb_ref, nh):
    bat = ba.T
    beta = jnp.where(valid, 1.0 / (1.0 + jnp.exp(-bat[0:nh])), 0.0)
    z = bat[nh:2 * nh] + dtb_ref[...]
    softplus = jnp.maximum(z, 0.0) + jnp.log(1.0 + jnp.exp(-jnp.abs(z)))
    g = jnp.where(valid, -jnp.exp(alog_ref[...]) * softplus, 0.0)
    gc = _cumsum_lanes(g)
    glast = jnp.broadcast_to(gc[:, BLK - 1:BLK], gc.shape)
    eg = jnp.exp(gc)
    rows = jnp.concatenate(
        [gc, eg, beta, beta * eg, jnp.exp(glast - gc), jnp.exp(glast),
         jnp.zeros((BLK - 6 * nh, BLK), F32)], axis=0)
    return gc, rows.T


def _l2n(x):
    return x * lax.rsqrt(jnp.sum(x * x, axis=-1, keepdims=True) + EPS)


def _delta_chunk_small(xq, xk, xv, ba, alog_ref, dtb_ref, s_ref, nh, dk):
    t = xq.shape[0]
    zrows = jnp.zeros((BLK - t, BLK), F32)
    lane = lax.broadcasted_iota(jnp.int32, (nh, BLK), 1)
    gc, cols = _delta_scalars(jnp.concatenate([ba, zrows], axis=0), lane < t, alog_ref, dtb_ref, nh)
    ii = lax.broadcasted_iota(jnp.int32, (t, BLK), 0)
    jj = lax.broadcasted_iota(jnp.int32, (t, BLK), 1)
    incl = ii >= jj
    strict = ii > jj
    each = lambda f: [f(h) for h in range(nh)]
    col = lambda h, qi: cols[0:t, qi * nh + h:qi * nh + h + 1]
    pad = lambda x: jnp.concatenate([x, zrows], axis=0)
    q = each(lambda h: _l2n(xq[:, h * dk:(h + 1) * dk]) * (dk ** -0.5))
    k = each(lambda h: _l2n(xk[:, h * dk:(h + 1) * dk]))
    s = each(lambda h: s_ref[h])
    kpad = each(lambda h: pad(k[h]))
    decay = each(lambda h: jnp.where(incl, jnp.exp(jnp.where(incl, col(h, 0) - gc[h:h + 1, :], 0.0)), 0.0))
    a_mat = each(lambda h: jnp.where(strict, _bdot_nt(k[h], kpad[h]) * decay[h] * col(h, 2), 0.0))
    qk = each(lambda h: _bdot_nt(q[h], kpad[h]) * decay[h])
    wu = each(lambda h: jnp.concatenate([col(h, 3) * k[h], col(h, 2) * xv[:, h * dk:(h + 1) * dk]], axis=-1))
    for j in range(t - 1):
        wu = each(lambda h: wu[h] - a_mat[h][:, j:j + 1] * wu[h][j:j + 1, :])
    v_new = each(lambda h: wu[h][:, dk:] - _bdot(wu[h][:, :dk], s[h]))
    o = each(lambda h: col(h, 1) * _bdot(q[h], s[h]))
    for j in range(t):
        o = each(lambda h: o[h] + qk[h][:, j:j + 1] * v_new[h][j:j + 1, :])
    kd = each(lambda h: pad(k[h] * col(h, 4)).T)
    s_new = each(lambda h: cols[:, 5 * nh + h:5 * nh + h + 1] * s[h] + _bdot(kd[h], pad(v_new[h])))
    for h in range(nh):
        s_ref[h] = s_new[h]
    return o


def _delta_chunk(xq, xk, xv, ba, valid, alog_ref, dtb_ref, s_ref, nh, dk):
    assert dk == BLK and nh % HEAD_PACK == 0
    gc, cols = _delta_scalars(ba, valid, alog_ref, dtb_ref, nh)
    n = HEAD_PACK * BLK
    ii = lax.broadcasted_iota(jnp.int32, (n, n), 0)
    jj = lax.broadcasted_iota(jnp.int32, (n, n), 1)
    same = (ii // BLK) == (jj // BLK)
    incl = same & (ii >= jj)
    strict = same & (ii > jj)
    packs = range(nh // HEAD_PACK)
    each = lambda f: [f(p) for p in packs]
    stack = lambda p, f: jnp.concatenate([f(h) for h in range(p * HEAD_PACK, (p + 1) * HEAD_PACK)], axis=0)
    col = lambda p, qi: stack(p, lambda h: cols[:, qi * nh + h:qi * nh + h + 1])
    diag = lambda m: jnp.where(same, jnp.concatenate([m] * HEAD_PACK, axis=1), 0.0)
    q = each(lambda p: stack(p, lambda h: _l2n(xq[:, h * dk:(h + 1) * dk]) * (dk ** -0.5)))
    k = each(lambda p: stack(p, lambda h: _l2n(xk[:, h * dk:(h + 1) * dk])))
    v = each(lambda p: stack(p, lambda h: xv[:, h * dk:(h + 1) * dk]))
    s = each(lambda p: stack(p, lambda h: s_ref[h]))
    diff = each(lambda p: col(p, 0) - jnp.concatenate(
        [gc[h:h + 1, :] for h in range(p * HEAD_PACK, (p + 1) * HEAD_PACK)], axis=1))
    decay = each(lambda p: jnp.where(incl, jnp.exp(jnp.where(incl, diff[p], 0.0)), 0.0))
    a_mat = each(lambda p: jnp.where(strict, _bdot_nt(k[p], k[p]) * decay[p] * col(p, 2), 0.0))
    qk = each(lambda p: _bdot_nt(q[p], k[p]) * decay[p])
    x = _unit_lower_inverse(a_mat, ii, jj)
    rhs = each(lambda p: jnp.concatenate([col(p, 3) * k[p], col(p, 2) * v[p]], axis=-1))
    wu = each(lambda p: _bdot(x[p], rhs[p]))
    v_new = each(lambda p: wu[p][:, dk:] - _bdot(diag(wu[p][:, :dk]), s[p]))
    qs = each(lambda p: _bdot(diag(q[p]), s[p]))
    o = each(lambda p: col(p, 1) * qs[p] + _bdot(qk[p], v_new[p]))
    kd = each(lambda p: jnp.where(same, jnp.concatenate([(k[p] * col(p, 4)).T] * HEAD_PACK, axis=0), 0.0))
    s_new = each(lambda p: col(p, 5) * s[p] + _bdot(kd[p], v_new[p]))
    outs = []
    for p in packs:
        for i in range(HEAD_PACK):
            s_ref[p * HEAD_PACK + i] = s_new[p][i * BLK:(i + 1) * BLK, :]
            outs.append(o[p][i * BLK:(i + 1) * BLK, :])
    return outs


def _delta_out(o, gate, ng_ref):
    return o * lax.rsqrt(jnp.mean(o * o, axis=-1, keepdims=True) + EPS) * ng_ref[...] * _silu(gate)


def _conv_silu(ext_ref, w_ref, t):
    acc = ext_ref[5:5 + t, :] * w_ref[0:1, :]
    for i in range(1, DN_CONV):
        acc = acc + ext_ref[5 + i:5 + i + t, :] * w_ref[i:i + 1, :]
    return _silu(acc)


def _delta_prompt_body(q_ref, k_ref, v_ref, gt_ref, ba_ref, wq_ref, wk_ref, wv_ref, alog_ref, dtb_ref, ng_ref,
                       y_ref, sout_ref, eq_ref, ek_ref, ev_ref, s_ref, *, nh, dk, nblk):
    c = pl.program_id(1)

    @pl.when(c == 0)
    def _():
        for e in (eq_ref, ek_ref, ev_ref):
            e[0:8, :] = jnp.zeros((8, e.shape[1]), F32)
        s_ref[...] = jnp.zeros(s_ref.shape, F32)

    xs = []
    for x_ref, e_ref, w_ref in ((q_ref, eq_ref, wq_ref), (k_ref, ek_ref, wk_ref), (v_ref, ev_ref, wv_ref)):
        e_ref[8:8 + BLK, :] = x_ref[...]
        xs.append(_conv_silu(e_ref, w_ref, BLK))
        e_ref[0:8, :] = x_ref[BLK - 8:, :]
    lane = lax.broadcasted_iota(jnp.int32, (nh, BLK), 1)
    valid = (c > 0) | (lane >= FRONT)
    outs = _delta_chunk(xs[0], xs[1], xs[2], ba_ref[...], valid, alog_ref, dtb_ref, s_ref, nh, dk)
    row = c * BLK + lax.broadcasted_iota(jnp.int32, (BLK, 1), 0)
    for h in range(nh):
        sl = slice(h * dk, (h + 1) * dk)
        y_ref[:, sl] = jnp.where(row >= FRONT, _delta_out(outs[h], gt_ref[:, sl], ng_ref), 0.0)

    @pl.when(c == nblk - 1)
    def _():
        sout_ref[0] = s_ref[...]


def _delta_prompt(p, ba, w_conv, a_log, dt_bias, norm_g, nseq, nblk, col_q, col_gate, nh, dk):
    hw = nh * dk
    cq, ck, cv, cg = col_q // hw, col_q // hw + 1, col_q // hw + 2, col_gate // hw
    rows = lambda cb: pl.BlockSpec((BLK, hw), lambda b, c: (b * nblk + c, cb))
    wcs = lambda cb: pl.BlockSpec((DN_CONV, hw), lambda b, c: (0, cb))
    small = lambda shape: pl.BlockSpec(shape, lambda b, c: (0, 0))
    return pl.pallas_call(
        functools.partial(_delta_prompt_body, nh=nh, dk=dk, nblk=nblk), grid=(nseq, nblk),
        in_specs=[rows(cq), rows(ck), rows(cv), rows(cg),
                  pl.BlockSpec((BLK, BLK), lambda b, c: (b * nblk + c, 0)),
                  wcs(0), wcs(1), wcs(2), small((nh, 1)), small((nh, 1)), small((1, dk))],
        out_specs=[pl.BlockSpec((BLK, hw), lambda b, c: (b * nblk + c, 0)),
                   pl.BlockSpec((1, nh, dk, dk), lambda b, c: (b, 0, 0, 0))],
        out_shape=[jax.ShapeDtypeStruct((p.shape[0], hw), F32),
                   jax.ShapeDtypeStruct((nseq, nh, dk, dk), F32)],
        scratch_shapes=[pltpu.VMEM((8 + BLK, hw), F32)] * 3 + [pltpu.VMEM((nh, dk, dk), F32)],
        compiler_params=_cparams("parallel", "arbitrary"), name="delta_prompt",
    )(p, p, p, p, ba, w_conv, w_conv, w_conv, a_log.reshape(nh, 1), dt_bias.reshape(nh, 1),
      norm_g.reshape(1, dk))


def _delta_sample_body(q_ref, k_ref, v_ref, gt_ref, ba_ref, hq_ref, hk_ref, hv_ref, s0_ref,
                       wq_ref, wk_ref, wv_ref, alog_ref, dtb_ref, ng_ref, ybuf_ref,
                       y_ref, sout_ref, eq_ref, ek_ref, ev_ref, s_ref, *, nh, dk):
    del ybuf_ref
    xs = []
    for x_ref, h_ref, e_ref, w_ref in ((q_ref, hq_ref, eq_ref, wq_ref), (k_ref, hk_ref, ek_ref, wk_ref),
                                       (v_ref, hv_ref, ev_ref, wv_ref)):
        e_ref[5:8, :] = h_ref[0]
        e_ref[8:8 + DEC_T, :] = x_ref[...]
        xs.append(_conv_silu(e_ref, w_ref, DEC_T))
    s_ref[...] = s0_ref[0]
    outs = _delta_chunk_small(xs[0], xs[1], xs[2], ba_ref[...], alog_ref, dtb_ref, s_ref, nh, dk)
    for h in range(nh):
        sl = slice(h * dk, (h + 1) * dk)
        y_ref[:, sl] = _delta_out(outs[h], gt_ref[:, sl], ng_ref)
    sout_ref[0] = s_ref[...]


def _delta_sample(p, ba, conv_hist, s0, w_conv, a_log, dt_bias, norm_g, ybuf, rows_per_seq,
                  col_q, col_gate, nh, dk):
    nb = s0.shape[0]
    hw = nh * dk
    cq, ck, cv, cg = col_q // hw, col_q // hw + 1, col_q // hw + 2, col_gate // hw
    rowmap = lambda cb: (lambda s: (_sample_row_block(s, rows_per_seq), cb))
    rows = lambda cb: pl.BlockSpec((DEC_T, hw), rowmap(cb))
    hist = lambda cb: pl.BlockSpec((1, DN_CONV - 1, hw), lambda s: (s, 0, cb))
    wcs = lambda cb: pl.BlockSpec((DN_CONV, hw), lambda s: (0, cb))
    small = lambda shape: pl.BlockSpec(shape, lambda s: (0, 0))
    return pl.pallas_call(
        functools.partial(_delta_sample_body, nh=nh, dk=dk), grid=(nb,),
        in_specs=[rows(cq), rows(ck), rows(cv), rows(cg), pl.BlockSpec((DEC_T, BLK), rowmap(0)),
                  hist(0), hist(1), hist(2),
                  pl.BlockSpec((1, nh, dk, dk), lambda s: (s, 0, 0, 0)),
                  wcs(0), wcs(1), wcs(2), small((nh, 1)), small((nh, 1)), small((1, dk)),
                  pl.BlockSpec(memory_space=pl.ANY)],
        out_specs=[pl.BlockSpec((DEC_T, hw), rowmap(0)),
                   pl.BlockSpec((1, nh, dk, dk), lambda s: (s, 0, 0, 0))],
        out_shape=[jax.ShapeDtypeStruct(ybuf.shape, F32), jax.ShapeDtypeStruct(s0.shape, F32)],
        scratch_shapes=[pltpu.VMEM((8 + DEC_T, hw), F32)] * 3 + [pltpu.VMEM((nh, dk, dk), F32)],
        input_output_aliases={15: 0},
        compiler_params=_cparams("arbitrary"), name="delta_sample",
    )(p, p, p, p, ba, conv_hist, conv_hist, conv_hist, s0, w_conv, w_conv, w_conv,
      a_log.reshape(nh, 1), dt_bias.reshape(nh, 1), norm_g.reshape(1, dk), ybuf)


def _attn_prompt_body(*refs, nkv, grp, hd, window):
    q_refs = refs[:nkv]
    kp_ref, kc_ref, vp_ref, vc_ref, slope_ref, sink_ref, y_ref = refs[nkv:]
    n = pl.program_id(1)
    i = lax.broadcasted_iota(jnp.int32, (BLK, 2 * BLK), 0)
    j = lax.broadcasted_iota(jnp.int32, (BLK, 2 * BLK), 1)
    dist = BLK + i - j
    krow = (n - 1) * BLK + j
    valid = (dist >= 0) & (dist < window) & (krow >= FRONT)
    distf = dist.astype(F32)
    kk = jnp.concatenate([kp_ref[...], kc_ref[...]], axis=0).astype(BF16)
    vv = jnp.concatenate([vp_ref[...], vc_ref[...]], axis=0).astype(BF16)
    heads = [(kv, g) for kv in range(nkv) for g in range(grp)]
    each = lambda f: [f(t, kv, g) for t, (kv, g) in enumerate(heads)]
    s = each(lambda t, kv, g: _bdot_nt(q_refs[kv][:, g * hd:(g + 1) * hd], kk[:, kv * hd:(kv + 1) * hd]))
    s = each(lambda t, kv, g: jnp.where(valid, s[t] * (hd ** -0.5) - slope_ref[t] * distf, NEG_INF))
    m = each(lambda t, kv, g: jnp.maximum(jnp.max(s[t], axis=-1, keepdims=True), sink_ref[t]))
    pr = each(lambda t, kv, g: jnp.exp(s[t] - m[t]))
    den = each(lambda t, kv, g: jnp.sum(pr[t], axis=-1, keepdims=True) + jnp.exp(sink_ref[t] - m[t]))
    pv = each(lambda t, kv, g: _bdot(pr[t], vv[:, kv * hd:(kv + 1) * hd]))
    for t in range(len(heads)):
        y_ref[:, t * hd:(t + 1) * hd] = pv[t] / den[t]


def _attn_prompt(p, slopes, sinks, nseq, nblk, col_q, col_k, col_v, nkv, grp, hd, window):
    gw, kw = grp * hd, nkv * hd
    prev = lambda cb: (lambda b, n: (b * nblk + jnp.maximum(n - 1, 0), cb))
    cur = lambda cb: (lambda b, n: (b * nblk + n, cb))
    smem = pl.BlockSpec(memory_space=pltpu.SMEM)
    return pl.pallas_call(
        functools.partial(_attn_prompt_body, nkv=nkv, grp=grp, hd=hd, window=window), grid=(nseq, nblk),
        in_specs=[pl.BlockSpec((BLK, gw), cur(col_q // gw + kv)) for kv in range(nkv)]
        + [pl.BlockSpec((BLK, kw), prev(col_k // kw)), pl.BlockSpec((BLK, kw), cur(col_k // kw)),
           pl.BlockSpec((BLK, kw), prev(col_v // kw)), pl.BlockSpec((BLK, kw), cur(col_v // kw)),
           smem, smem],
        out_specs=pl.BlockSpec((BLK, nkv * gw), cur(0)),
        out_shape=jax.ShapeDtypeStruct((p.shape[0], nkv * gw), F32),
        compiler_params=_cparams("parallel", "arbitrary"), name="attn_prompt",
    )(*([p] * (nkv + 4)), slopes, sinks)


def _attn_sample_body(*refs, nkv, grp, hd, window):
    q_refs = refs[:nkv]
    k_ref, v_ref, ck_ref, cv_ref, slope_ref, sink_ref, ybuf_ref, y_ref = refs[nkv:]
    del ybuf_ref
    wc = ck_ref.shape[1]
    i = lax.broadcasted_iota(jnp.int32, (DEC_T, wc), 0)
    j = lax.broadcasted_iota(jnp.int32, (DEC_T, wc), 1)
    dist_c = wc + i - j
    valid_c = (dist_c >= 0) & (dist_c < window)
    i2 = lax.broadcasted_iota(jnp.int32, (DEC_T, DEC_T), 0)
    j2 = lax.broadcasted_iota(jnp.int32, (DEC_T, DEC_T), 1)
    dist_n = i2 - j2
    valid_n = (dist_n >= 0) & (dist_n < window)
    tile = lambda a: jnp.concatenate([a] * grp, axis=0)
    valid_c, valid_n = tile(valid_c), tile(valid_n)
    dist_c, dist_n = tile(dist_c).astype(F32), tile(dist_n).astype(F32)
    gi = lax.broadcasted_iota(jnp.int32, (grp * DEC_T, 1), 0) // DEC_T

    def per_row(ref, kv):
        out = jnp.full((grp * DEC_T, 1), ref[kv * grp], F32)
        for g in range(1, grp):
            out = jnp.where(gi == g, ref[kv * grp + g], out)
        return out

    each = lambda f: [f(kv) for kv in range(nkv)]
    cols = lambda ref, kv: ref[:, kv * hd:(kv + 1) * hd]
    q = each(lambda kv: jnp.concatenate([q_refs[kv][:, g * hd:(g + 1) * hd] for g in range(grp)], axis=0))
    slope = each(lambda kv: per_row(slope_ref, kv))
    sink = each(lambda kv: per_row(sink_ref, kv))
    sc = each(lambda kv: _bdot_nt(q[kv], ck_ref[0, :, kv * hd:(kv + 1) * hd]) * (hd ** -0.5))
    sn = each(lambda kv: _bdot_nt(q[kv], cols(k_ref, kv)) * (hd ** -0.5))
    sc = each(lambda kv: jnp.where(valid_c, sc[kv] - slope[kv] * dist_c, NEG_INF))
    sn = each(lambda kv: jnp.where(valid_n, sn[kv] - slope[kv] * dist_n, NEG_INF))
    m = each(lambda kv: jnp.maximum(jnp.maximum(jnp.max(sc[kv], axis=-1, keepdims=True),
                                                jnp.max(sn[kv], axis=-1, keepdims=True)), sink[kv]))
    pc = each(lambda kv: jnp.exp(sc[kv] - m[kv]))
    pn = each(lambda kv: jnp.exp(sn[kv] - m[kv]))
    den = each(lambda kv: jnp.sum(pc[kv], axis=-1, keepdims=True) + jnp.sum(pn[kv], axis=-1, keepdims=True)
               + jnp.exp(sink[kv] - m[kv]))
    o = each(lambda kv: (_bdot(pc[kv], cv_ref[0, :, kv * hd:(kv + 1) * hd]) + _bdot(pn[kv], cols(v_ref, kv)))
             / den[kv])
    for kv in range(nkv):
        for g in range(grp):
            head = kv * grp + g
            y_ref[:, head * hd:(head + 1) * hd] = o[kv][g * DEC_T:(g + 1) * DEC_T, :]


def _attn_sample(p, cache_k, cache_v, slopes, sinks, ybuf, rows_per_seq, col_q, col_k, col_v,
                 nkv, grp, hd, window):
    nb, wc = cache_k.shape[0], cache_k.shape[1]
    gw, kw = grp * hd, nkv * hd
    ck3 = cache_k.reshape(nb, wc, kw)
    cv3 = cache_v.reshape(nb, wc, kw)
    rowmap = lambda cb: (lambda s: (_sample_row_block(s, rows_per_seq), cb))
    smem = pl.BlockSpec(memory_space=pltpu.SMEM)
    return pl.pallas_call(
        functools.partial(_attn_sample_body, nkv=nkv, grp=grp, hd=hd, window=window), grid=(nb,),
        in_specs=[pl.BlockSpec((DEC_T, gw), rowmap(col_q // gw + kv)) for kv in range(nkv)]
        + [pl.BlockSpec((DEC_T, kw), rowmap(col_k // kw)),
           pl.BlockSpec((DEC_T, kw), rowmap(col_v // kw)),
           pl.BlockSpec((1, wc, kw), lambda s: (s, 0, 0)),
           pl.BlockSpec((1, wc, kw), lambda s: (s, 0, 0)),
           smem, smem, pl.BlockSpec(memory_space=pl.ANY)],
        out_specs=pl.BlockSpec((DEC_T, nkv * gw), rowmap(0)),
        out_shape=jax.ShapeDtypeStruct(ybuf.shape, F32),
        input_output_aliases={nkv + 6: 0},
        compiler_params=_cparams("arbitrary"), name="attn_sample",
    )(*([p] * (nkv + 2)), ck3, cv3, slopes, sinks, ybuf)


SUB = 8


def _bitonic_pairs(n, merge_only=False):
    out = []
    k = n if merge_only else 2
    while k <= n:
        j = k // 2
        while j >= 1:
            out += [(i, i ^ j, (i & k) == 0) for i in range(n) if (i ^ j) > i]
            j //= 2
        k *= 2
    return out


def _compare_exchange(v, pairs):
    v = list(v)
    for i, l, desc in pairs:
        hi, lo = jnp.maximum(v[i], v[l]), jnp.minimum(v[i], v[l])
        v[i], v[l] = (hi, lo) if desc else (lo, hi)
    return v


def _top16(rows):
    v = _compare_exchange(rows, _bitonic_pairs(TOPK))
    shift = SUB // 2
    while shift >= 1:
        w = [pltpu.roll(v[TOPK - 1 - r], shift, axis=0) for r in range(TOPK)]
        v = _compare_exchange([jnp.maximum(a, b) for a, b in zip(v, w)], _bitonic_pairs(TOPK, merge_only=True))
        shift //= 2
    return v


def _sublane_sum(x):
    shift = SUB // 2
    while shift >= 1:
        x = x + pltpu.roll(x, shift, axis=0)
        shift //= 2
    return x


def _on_sublanes(vs):
    sub = lax.broadcasted_iota(jnp.int32, vs[0].shape, 0)
    out = vs[SUB - 1]
    for j in range(SUB - 2, -1, -1):
        out = jnp.where(sub == j, vs[j], out)
    return out


def _peer_topk_body(q_ref, keys_ref, rk_ref, cut_ref, e1_ref, e2_ref, *, nh):
    nv = NKEYS // SUB
    for h in range(nh):
        sc = [_bdot_nt(keys_ref[2 * h + half], q_ref[:, (2 * h + half) * NKEYS:(2 * h + half + 1) * NKEYS])
              for half in (0, 1)]
        s1 = [sc[0][SUB * i:SUB * (i + 1), :] for i in range(nv)]
        s2 = [sc[1][SUB * i:SUB * (i + 1), :] for i in range(nv)]
        a = _top16(s1)
        b = _top16(s2)
        b_lo, b_hi, a_hi = _on_sublanes(b[:SUB]), _on_sublanes(b[SUB:]), _on_sublanes(a[SUB:])
        cand = [a[0] + b_lo, a[0] + b_hi, a_hi + b[0]] + [a[i] + b_lo for i in range(1, SUB)]
        cand += [jnp.full(cand[0].shape, LOWEST, F32)] * (TOPK - len(cand))
        top = _top16(cand)
        thr = top[TOPK - 1]
        zsum = jnp.exp(top[0] - top[0])
        for r in range(1, TOPK):
            zsum = zsum + jnp.exp(top[r] - top[0])
        rz = 1.0 / zsum
        height = [_sublane_sum(jnp.where(a[r] + b_lo >= thr, 1.0, 0.0) + jnp.where(a[r] + b_hi >= thr, 1.0, 0.0))
                  for r in range(TOPK)]
        cut, rank2 = [], []
        for i in range(nv):
            c = jnp.zeros(s1[i].shape, F32)
            for r in range(TOPK - 1, -1, -1):
                c = jnp.where(s1[i] == a[r], height[r], c)
            cut.append(c)
            k = jnp.where(b[0] > s2[i], 1.0, 0.0)
            for r in range(1, TOPK):
                k = k + jnp.where(b[r] > s2[i], 1.0, 0.0)
            rank2.append(k)
        rk_ref[h] = jnp.concatenate(rank2, axis=0).astype(BF16)
        cut_ref[h] = jnp.concatenate(cut, axis=0)
        e1_ref[h] = jnp.exp(sc[0] - a[0][0:1, :])
        e2_ref[h] = (jnp.exp(sc[1] - b[0][0:1, :]) * rz[0:1, :]).astype(BF16)


def _peer_topk(q, sub_keys, *, tm=128):
    n = q.shape[0]
    nh = sub_keys.shape[0]
    keys = sub_keys.reshape(2 * nh, NKEYS, sub_keys.shape[-1])
    sspec = pl.BlockSpec((nh, NKEYS, tm), lambda i: (0, 0, i))
    shape = lambda dt: jax.ShapeDtypeStruct((nh, NKEYS, n), dt)
    return pl.pallas_call(
        functools.partial(_peer_topk_body, nh=nh), grid=(n // tm,),
        in_specs=[pl.BlockSpec((tm, q.shape[1]), lambda i: (i, 0)),
                  pl.BlockSpec(keys.shape, lambda i: (0, 0, 0))],
        out_specs=[sspec] * 4,
        out_shape=[shape(BF16), shape(F32), shape(F32), shape(BF16)],
        compiler_params=_cparams("parallel"), name="peer_topk")(q, keys)


def _gelu(x):
    return 0.5 * x * (1.0 + lax.erf(x * (2.0 ** -0.5)))


PACK = 16


def _peer_expert_body(xn_ref, wd_ref, wu_ref, id_ref, iu_ref, rk_ref, cut_ref, e1_ref, e2_ref, y_ref,
                      xt_ref, ix_ref, *, nh, ei, sub):
    c = pl.program_id(1)
    tm = xt_ref.shape[1]
    tw = tm // sub
    slabs = [slice(t * tw, (t + 1) * tw) for t in range(sub)]
    each = lambda f: [f(t, tl) for t, tl in enumerate(slabs)]

    @pl.when(c == 0)
    def _():
        y_ref[...] = jnp.zeros(y_ref.shape, F32)
        x = xn_ref[...].astype(F32)
        sx = _amax_scale(x)
        xt_ref[...] = (x * sx).T.astype(FP8)
        ix_ref[...] = jnp.broadcast_to(1.0 / sx, ix_ref.shape)

    def gates(tl):
        def row16(ref, h, i1):
            return jnp.broadcast_to(ref[h, pl.ds(i1, 1), tl], (PACK, tw)).astype(BF16)
        pieces = []
        for ii in range(ei):
            i1 = c * ei + ii
            cut = [row16(cut_ref, h, i1) for h in range(nh)]
            e1 = [row16(e1_ref, h, i1) for h in range(nh)]
            for r in range(NKEYS // PACK):
                sl = slice(r * PACK, (r + 1) * PACK)
                gate = pieces[-1] * 0.0 if pieces else None
                for h in range(nh):
                    w = jnp.where(rk_ref[h, sl, tl] < cut[h], e1[h] * e2_ref[h, sl, tl], 0.0)
                    gate = w if gate is None else gate + w
                pieces.append(gate)
        return pieces

    inv_h = id_ref[0, 0:1, 0:1] * ix_ref[0:1, 0:1]
    inv_u = iu_ref[0, 0:1, 0:1]
    ht = each(lambda t, tl: jnp.dot(wd_ref[...], xt_ref[:, tl], preferred_element_type=F32) * inv_h)
    gate = each(lambda t, tl: gates(tl))
    act = each(lambda t, tl: _gelu(ht[t]))
    at = each(lambda t, tl: jnp.concatenate(
        [act[t][i * PACK:(i + 1) * PACK, :] * g.astype(F32) for i, g in enumerate(gate[t])], axis=0))
    sa = each(lambda t, tl: _amax_scale(at[t]))
    part = each(lambda t, tl: jnp.dot((at[t] * sa[t]).T.astype(FP8), wu_ref[...], preferred_element_type=F32))
    for t, tl in enumerate(slabs):
        y_ref[tl, :] += part[t] * (inv_u / sa[t])


PEER_TM, PEER_EI, PEER_SUB = 512, 8, 2


def _peer_expert(xn, w_down, w_up, inv_down, inv_up, rank2, cut, e1, e2, *, tm=PEER_TM, ei=PEER_EI, sub=PEER_SUB):
    n, d = xn.shape
    nh = rank2.shape[0]
    e = ei * NKEYS
    once = dict(pipeline_mode=pl.Buffered(1))
    sspec = pl.BlockSpec((nh, NKEYS, tm), lambda i, c: (0, 0, i))
    inv = pl.BlockSpec((1, 8, BLK), lambda i, c: (c, 0, 0))
    return pl.pallas_call(
        functools.partial(_peer_expert_body, nh=nh, ei=ei, sub=sub), grid=(n // tm, w_down.shape[0] // e),
        in_specs=[pl.BlockSpec((tm, d), lambda i, c: (i, 0)),
                  pl.BlockSpec((e, d), lambda i, c: (c, 0)),
                  pl.BlockSpec((e, d), lambda i, c: (c, 0)),
                  inv, inv, sspec, sspec, sspec, sspec],
        out_specs=pl.BlockSpec((tm, d), lambda i, c: (i, 0), **once),
        out_shape=jax.ShapeDtypeStruct((n, d), F32),
        scratch_shapes=[pltpu.VMEM((d, tm), FP8), pltpu.VMEM((8, BLK), F32)],
        compiler_params=_cparams("parallel", "arbitrary"), name="peer_expert",
    )(xn, w_down, w_up, inv_down, inv_up, rank2, cut, e1, e2)


def _final_norm(h, y, g, nseq, nblk, row_blk, first_blk, nblk_out):
    d = h.shape[1]
    row = pl.BlockSpec((row_blk, d), lambda b, i: (b * nblk + first_blk + i, 0))
    return pl.pallas_call(
        _final_norm_body, grid=(nseq, nblk_out),
        in_specs=[row, row, pl.BlockSpec((1, d), lambda b, i: (0, 0))],
        out_specs=pl.BlockSpec((1, row_blk, d), lambda b, i: (b, i, 0)),
        out_shape=jax.ShapeDtypeStruct((nseq, nblk_out * row_blk, d), F32),
        compiler_params=_cparams("parallel", "parallel"), name="final_norm",
    )(h, y, g.reshape(1, d).astype(F32))


def _final_norm_body(h_ref, y_ref, g_ref, o_ref):
    x = h_ref[...] + y_ref[...]
    ms = jnp.mean(x * x, axis=-1, keepdims=True)
    o_ref[0] = x * lax.rsqrt(ms + EPS) * g_ref[...]


def kernel(x_prompt, x_sample, state_pool, state_conv, state_delta, cache_k, cache_v, meta_tokens, norm1_g,
           w_in, w_pool, s_pool, w_conv, a_log, dt_bias, dn_norm_g, attn_sinks, w_out, norm2_g,
           peer_w_query, peer_sub_keys, peer_w_down, peer_w_up, final_norm_g):
    nseq, seq, d = x_prompt.shape
    nsamp, dec_t, _ = x_sample.shape
    depth = w_in.shape[0]
    pool_w = w_pool.shape[1] * w_pool.shape[2]
    dn_qkv = w_conv.shape[2]
    nh, dk = state_delta.shape[2], state_delta.shape[3]
    wc, nkv, hd = cache_k.shape[2], cache_k.shape[3], cache_k.shape[4]
    nq = attn_sinks.shape[1]
    grp = nq // nkv
    window = wc
    assert dec_t == DEC_T and nsamp == nseq * SLOTS and (N_META + seq) % BLK == N_META
    assert SLOTS * DEC_T + POOL_HIST <= FRONT and dn_qkv == 3 * nh * dk and wc == BLK
    rows_per_seq = FRONT + N_META + seq
    nblk = rows_per_seq // BLK
    col_qkv = pool_w
    col_gate = col_qkv + dn_qkv
    col_q = col_gate + nh * dk
    col_k = col_q + nq * hd
    col_v = col_k + nkv * hd
    src_ba = pool_w + dn_qkv

    slopes = jnp.exp2(-8.0 * (jnp.arange(nq, dtype=F32) + 1.0) / nq)
    new_p = [[] for _ in range(5)]
    new_s = [[] for _ in range(5)]
    y_peer = None
    w_in_t = jnp.swapaxes(w_in, 1, 2)
    for l in range(depth):
        w_main, w_ba = _cast_w_in(w_in_t, l, src_ba, 2 * nh)
        if l == 0:
            h, xn = _assemble_norm(x_prompt, x_sample, meta_tokens, norm1_g[l], nblk)
        else:
            h, xn = _addnorm(h, y_peer, norm1_g[l])
        p = _mm(xn, w_main, nt=True)
        ba = _mm(xn, w_ba, nt=True)

        y_pool = _pool_prompt(p, w_pool[l], s_pool[l].reshape(1, pool_w), nseq, nblk)
        y_pool = _pool_sample(p, state_pool[l], w_pool[l], s_pool[l].reshape(1, pool_w), y_pool,
                              rows_per_seq, PAST_LEN)
        y_dn, s_p = _delta_prompt(p, ba, w_conv[l], a_log[l], dt_bias[l], dn_norm_g[l], nseq, nblk,
                                  col_qkv, col_gate, nh, dk)
        y_dn, s_s = _delta_sample(p, ba, state_conv[l], state_delta[l], w_conv[l], a_log[l], dt_bias[l],
                                  dn_norm_g[l], y_dn, rows_per_seq, col_qkv, col_gate, nh, dk)
        y_att = _attn_prompt(p, slopes, attn_sinks[l], nseq, nblk, col_q, col_k, col_v, nkv, grp, hd, window)
        y_att = _attn_sample(p, cache_k[l], cache_v[l], slopes, attn_sinks[l], y_att, rows_per_seq,
                             col_q, col_k, col_v, nkv, grp, hd, window)
        h = _outproj(y_pool, y_dn, y_att, _cast(w_out, l), h)

        _, xn2 = _addnorm(h, None, norm2_g[l])
        q = _mm(xn2, _cast(peer_w_query, l))
        rank2, cut, e1, e2 = _peer_topk(q, peer_sub_keys[l])
        wd8, inv_d = _quant_fp8(peer_w_down, l, tr=PEER_EI * NKEYS)
        wu8, inv_u = _quant_fp8(peer_w_up, l, tr=PEER_EI * NKEYS)
        y_peer = _peer_expert(xn2, wd8, wu8, inv_d, inv_u, rank2, cut, e1, e2)

        p3 = p.reshape(nseq, rows_per_seq, p.shape[1])
        ps = p3[:, :SLOTS * DEC_T].reshape(nsamp, DEC_T, p.shape[1])
        new_p[0].append(p3[:, -POOL_HIST:, :pool_w])
        new_s[0].append(jnp.concatenate([state_pool[l], ps[:, :, :pool_w]], axis=1)[:, -POOL_HIST:])
        new_p[1].append(p3[:, -(DN_CONV - 1):, col_qkv:col_gate])
        new_s[1].append(jnp.concatenate([state_conv[l], ps[:, :, col_qkv:col_gate]], axis=1)[:, -(DN_CONV - 1):])
        new_p[2].append(s_p)
        new_s[2].append(s_s)
        new_p[3].append(p3[:, -window:, col_k:col_v].reshape(nseq, window, nkv, hd))
        new_s[3].append(jnp.concatenate([cache_k[l], ps[:, :, col_k:col_v].reshape(nsamp, DEC_T, nkv, hd)],
                                        axis=1)[:, -wc:])
        new_p[4].append(p3[:, -window:, col_v:].reshape(nseq, window, nkv, hd))
        new_s[4].append(jnp.concatenate([cache_v[l], ps[:, :, col_v:].reshape(nsamp, DEC_T, nkv, hd)],
                                        axis=1)[:, -wc:])

    y_prompt = _final_norm(h, y_peer, final_norm_g, nseq, nblk, BLK, 1, nblk - 1)
    y_sample = _final_norm(h, y_peer, final_norm_g, nseq, rows_per_seq // (SLOTS * DEC_T), SLOTS * DEC_T, 0, 1)
    y_sample = y_sample.reshape(nsamp, DEC_T, d)
    pool_p, conv_p, delta_p, k_p, v_p = (jnp.stack(a) for a in new_p)
    pool_s, conv_s, delta_s, k_s, v_s = (jnp.stack(a) for a in new_s)
    return (y_prompt, y_sample, pool_p, pool_s, conv_p, conv_s, delta_p, delta_s, k_p, k_s, v_p, v_s)
```

```python
import functools

import jax
import jax.numpy as jnp
from jax import lax
from jax.experimental import pallas as pl
from jax.experimental.pallas import tpu as pltpu

F32 = jnp.float32
BF16 = jnp.bfloat16

EPS = 1e-6
NEG_INF = -1e30
LOWEST = -3.0e38

PAST_LEN = 16384
N_META = 16
BLK = 128
FRONT = BLK - N_META
DEC_T = 8
SLOTS = 8
POOL_WINDOWS = (2, 4, 8, 16)
POOL_HIST = 15
DN_CONV = 4
TOPK = 16
NKEYS = 128
VMEM_LIMIT = 56 * 1024 * 1024


def _cparams(*sem):
    return pltpu.CompilerParams(dimension_semantics=sem, vmem_limit_bytes=VMEM_LIMIT)


def _bdot(a, b):
    return jnp.dot(a.astype(BF16), b.astype(BF16), preferred_element_type=F32)


def _bdot_nt(a, b):
    return lax.dot_general(a.astype(BF16), b.astype(BF16), (((1,), (1,)), ((), ())),
                           preferred_element_type=F32)


def _silu(x):
    return x * (1.0 / (1.0 + jnp.exp(-x)))


def _addnorm_body(*refs, add):
    if add:
        h_ref, y_ref, g_ref, hs_ref, xn_ref = refs
        x = h_ref[...] + y_ref[...]
        hs_ref[...] = x
    else:
        h_ref, g_ref, xn_ref = refs
        x = h_ref[...]
    ms = jnp.mean(x * x, axis=-1, keepdims=True)
    xn_ref[...] = (x * lax.rsqrt(ms + EPS) * g_ref[...]).astype(xn_ref.dtype)


def _addnorm(h, y, g, *, tm=256, out_dtype=BF16):
    n, d = h.shape
    row = pl.BlockSpec((tm, d), lambda i: (i, 0))
    gspec = pl.BlockSpec((1, d), lambda i: (0, 0))
    g2 = g.reshape(1, d).astype(F32)
    if y is None:
        xn = pl.pallas_call(
            functools.partial(_addnorm_body, add=False),
            grid=(n // tm,), in_specs=[row, gspec], out_specs=row,
            out_shape=jax.ShapeDtypeStruct((n, d), out_dtype),
            compiler_params=_cparams("parallel"), name="norm")(h, g2)
        return h, xn
    hs, xn = pl.pallas_call(
        functools.partial(_addnorm_body, add=True),
        grid=(n // tm,), in_specs=[row, row, gspec], out_specs=[row, row],
        out_shape=[jax.ShapeDtypeStruct((n, d), F32), jax.ShapeDtypeStruct((n, d), out_dtype)],
        compiler_params=_cparams("parallel"), name="add_norm")(h, y, g2)
    return hs, xn


def _assemble_norm_body(xs_ref, meta_ref, xp_ref, g_ref, h_ref, xn_ref):
    n = pl.program_id(1)

    @pl.when(n == 0)
    def _():
        ns = xs_ref.shape[1]
        h_ref[0:ns, :] = xs_ref[0]
        h_ref[ns:FRONT, :] = jnp.zeros((FRONT - ns, h_ref.shape[1]), F32)
        h_ref[FRONT:, :] = meta_ref[...]

    @pl.when(n > 0)
    def _():
        h_ref[...] = xp_ref[0]

    x = h_ref[...]
    ms = jnp.mean(x * x, axis=-1, keepdims=True)
    xn_ref[...] = (x * lax.rsqrt(ms + EPS) * g_ref[...]).astype(xn_ref.dtype)


def _assemble_norm(x_prompt, x_sample, meta_tokens, g, nblk):
    nseq, _, d = x_prompt.shape
    xs = x_sample.reshape(nseq, SLOTS * DEC_T, d)
    out = pl.BlockSpec((BLK, d), lambda b, n: (b * nblk + n, 0))
    return pl.pallas_call(
        _assemble_norm_body, grid=(nseq, nblk),
        in_specs=[pl.BlockSpec((1, SLOTS * DEC_T, d), lambda b, n: (b, 0, 0)),
                  pl.BlockSpec((N_META, d), lambda b, n: (0, 0)),
                  pl.BlockSpec((1, BLK, d), lambda b, n: (b, jnp.maximum(n - 1, 0), 0)),
                  pl.BlockSpec((1, d), lambda b, n: (0, 0))],
        out_specs=[out, out],
        out_shape=[jax.ShapeDtypeStruct((nseq * nblk * BLK, d), F32),
                   jax.ShapeDtypeStruct((nseq * nblk * BLK, d), BF16)],
        compiler_params=_cparams("parallel", "arbitrary"), name="assemble_norm",
    )(xs, meta_tokens, x_prompt, g.reshape(1, d).astype(F32))


def _cast_body(x_ref, o_ref):
    o_ref[...] = x_ref[0].astype(o_ref.dtype)


def _cast(x, l, dtype=BF16, *, tr=512):
    _, r, c = x.shape
    return pl.pallas_call(
        _cast_body, grid=(r // tr,),
        in_specs=[pl.BlockSpec((1, tr, c), lambda i: (l, i, 0))],
        out_specs=pl.BlockSpec((tr, c), lambda i: (i, 0)),
        out_shape=jax.ShapeDtypeStruct((r, c), dtype),
        compiler_params=_cparams("parallel"), name="cast")(x)


FP8 = jnp.float8_e4m3fn
FP8_TARGET = 224.0


def _amax_scale(x):
    a = jnp.max(jnp.max(jnp.abs(x), axis=1, keepdims=True), axis=0, keepdims=True)
    return jnp.where(a > 0.0, FP8_TARGET / a, 1.0)


def _quant_body(x_ref, o_ref, inv_ref):
    x = x_ref[0]
    s = _amax_scale(x)
    o_ref[...] = (x * s).astype(FP8)
    inv_ref[0] = jnp.broadcast_to(1.0 / s, inv_ref.shape[1:])


def _quant_fp8(x, l, *, tr):
    _, r, c = x.shape
    return pl.pallas_call(
        _quant_body, grid=(r // tr,),
        in_specs=[pl.BlockSpec((1, tr, c), lambda i: (l, i, 0))],
        out_specs=[pl.BlockSpec((tr, c), lambda i: (i, 0)), pl.BlockSpec((1, 8, BLK), lambda i: (i, 0, 0))],
        out_shape=[jax.ShapeDtypeStruct((r, c), FP8), jax.ShapeDtypeStruct((r // tr, 8, BLK), F32)],
        compiler_params=_cparams("parallel"), name="quant_fp8")(x)


def _cast_w_in_body(a_ref, b_ref, main_ref, ba_ref, *, nlo, nba):
    j = pl.program_id(0)
    tr = a_ref.shape[1]

    @pl.when(j < nlo)
    def _():
        main_ref[...] = a_ref[0].astype(BF16)

    @pl.when(j >= nlo)
    def _():
        main_ref[0:tr - nba, :] = a_ref[0, nba:, :].astype(BF16)
        main_ref[tr - nba:, :] = b_ref[0].astype(BF16)

    @pl.when(j == nlo)
    def _():
        ba_ref[0:nba, :] = a_ref[0, 0:nba, :].astype(BF16)
        ba_ref[nba:, :] = jnp.zeros((ba_ref.shape[0] - nba, ba_ref.shape[1]), BF16)


def _cast_w_in(w_in_t, l, lo, nba, *, tr=1024):
    _, r, c = w_in_t.shape
    assert lo % tr == 0 and (r - nba) % tr == 0 and tr % nba == 0 and nba % 8 == 0
    return pl.pallas_call(
        functools.partial(_cast_w_in_body, nlo=lo // tr, nba=nba), grid=((r - nba) // tr,),
        in_specs=[pl.BlockSpec((1, tr, c), lambda j: (l, j, 0)),
                  pl.BlockSpec((1, nba, c), lambda j: (l, (j + 1) * (tr // nba), 0))],
        out_specs=[pl.BlockSpec((tr, c), lambda j: (j, 0)), pl.BlockSpec((BLK, c), lambda j: (0, 0))],
        out_shape=[jax.ShapeDtypeStruct((r - nba, c), BF16), jax.ShapeDtypeStruct((BLK, c), BF16)],
        compiler_params=_cparams("arbitrary"), name="cast_w_in")(w_in_t, w_in_t)


def _mm_body(x_ref, w_ref, o_ref, *, nt):
    if nt:
        o_ref[...] = lax.dot_general(x_ref[...], w_ref[...], (((1,), (1,)), ((), ())),
                                     preferred_element_type=F32)
    else:
        o_ref[...] = jnp.dot(x_ref[...], w_ref[...], preferred_element_type=F32)


def _mm(x, w, *, nt=False, tm=512, tn=2048):
    m, k = x.shape
    n = w.shape[0] if nt else w.shape[1]
    tn = min(tn, n)
    wspec = pl.BlockSpec((tn, k), lambda j, i: (j, 0)) if nt else pl.BlockSpec((k, tn), lambda j, i: (0, j))
    return pl.pallas_call(
        functools.partial(_mm_body, nt=nt), grid=(n // tn, m // tm),
        in_specs=[pl.BlockSpec((tm, k), lambda j, i: (i, 0)), wspec],
        out_specs=pl.BlockSpec((tm, tn), lambda j, i: (i, j)),
        out_shape=jax.ShapeDtypeStruct((m, n), F32),
        compiler_params=_cparams("parallel", "parallel"), name="matmul")(x, w)


def _outproj_body(yp_ref, yd_ref, ya_ref, w_ref, h_ref, o_ref, *, wp, wd):
    acc = h_ref[...]
    acc += jnp.dot(yp_ref[...].astype(BF16), w_ref[0:wp, :], preferred_element_type=F32)
    acc += jnp.dot(yd_ref[...].astype(BF16), w_ref[wp:wp + wd, :], preferred_element_type=F32)
    acc += jnp.dot(ya_ref[...].astype(BF16), w_ref[wp + wd:, :], preferred_element_type=F32)
    o_ref[...] = acc


def _outproj(yp, yd, ya, w, h, *, tm=512, tn=1024):
    m, d = h.shape
    wp, wd, wa = yp.shape[1], yd.shape[1], ya.shape[1]
    k = wp + wd + wa
    return pl.pallas_call(
        functools.partial(_outproj_body, wp=wp, wd=wd), grid=(d // tn, m // tm),
        in_specs=[pl.BlockSpec((tm, wp), lambda j, i: (i, 0)),
                  pl.BlockSpec((tm, wd), lambda j, i: (i, 0)),
                  pl.BlockSpec((tm, wa), lambda j, i: (i, 0)),
                  pl.BlockSpec((k, tn), lambda j, i: (0, j)),
                  pl.BlockSpec((tm, tn), lambda j, i: (i, j))],
        out_specs=pl.BlockSpec((tm, tn), lambda j, i: (i, j)),
        out_shape=jax.ShapeDtypeStruct((m, d), F32),
        compiler_params=_cparams("parallel", "parallel"), name="out_proj")(yp, yd, ya, w, h)


def _pool_windows(ext_ref, u, t, pos, w_ref, s_ref, gw):
    outs = []
    for gi, w in enumerate(POOL_WINDOWS):
        sl = slice(gi * gw, (gi + 1) * gw)
        win = u[:, sl]
        for k in range(1, w):
            win = win + ext_ref[16 - k:16 - k + t, sl]
        cnt = jnp.clip(pos + 1, 1, w).astype(F32)
        d = win / cnt - u[:, sl]
        outs.append(_bdot(d, w_ref[gi]))
    return jnp.concatenate(outs, axis=-1) * s_ref[...]


def _pool_prompt_body(u_ref, w_ref, s_ref, y_ref, ext_ref, *, gw):
    n = pl.program_id(1)

    @pl.when(n == 0)
    def _():
        ext_ref[0:16, :] = jnp.zeros((16, ext_ref.shape[1]), F32)

    u = u_ref[...]
    ext_ref[16:16 + BLK, :] = u
    row = n * BLK + lax.broadcasted_iota(jnp.int32, (BLK, 1), 0)
    pos = row - FRONT
    y = _pool_windows(ext_ref, u, BLK, pos, w_ref, s_ref, gw)
    y_ref[...] = jnp.where(pos >= 0, y, 0.0)
    ext_ref[0:16, :] = u[BLK - 16:, :]


def _pool_prompt(p, w_pool, s_pool, nseq, nblk):
    pw = w_pool.shape[0] * w_pool.shape[1]
    gw = w_pool.shape[1]
    return pl.pallas_call(
        functools.partial(_pool_prompt_body, gw=gw), grid=(nseq, nblk),
        in_specs=[pl.BlockSpec((BLK, pw), lambda b, n: (b * nblk + n, 0)),
                  pl.BlockSpec(w_pool.shape, lambda b, n: (0, 0, 0)),
                  pl.BlockSpec((1, pw), lambda b, n: (0, 0))],
        out_specs=pl.BlockSpec((BLK, pw), lambda b, n: (b * nblk + n, 0)),
        out_shape=jax.ShapeDtypeStruct((p.shape[0], pw), F32),
        scratch_shapes=[pltpu.VMEM((16 + BLK, pw), F32)],
        compiler_params=_cparams("parallel", "arbitrary"), name="pool_prompt")(p, w_pool, s_pool)


def _pool_sample_body(u_ref, hist_ref, w_ref, s_ref, ybuf_ref, y_ref, ext_ref, *, gw, pos0):
    del ybuf_ref
    u = u_ref[...]
    ext_ref[0:1, :] = jnp.zeros((1, ext_ref.shape[1]), F32)
    ext_ref[1:16, :] = hist_ref[0]
    ext_ref[16:16 + DEC_T, :] = u
    pos = pos0 + lax.broadcasted_iota(jnp.int32, (DEC_T, 1), 0)
    y_ref[...] = _pool_windows(ext_ref, u, DEC_T, pos, w_ref, s_ref, gw)


def _sample_row_block(s, rows_per_seq):
    return (s // SLOTS) * (rows_per_seq // DEC_T) + s % SLOTS


def _pool_sample(p, hist, w_pool, s_pool, ybuf, rows_per_seq, pos0):
    nb = hist.shape[0]
    pw = hist.shape[2]
    gw = w_pool.shape[1]
    rowmap = lambda s: (_sample_row_block(s, rows_per_seq), 0)
    return pl.pallas_call(
        functools.partial(_pool_sample_body, gw=gw, pos0=pos0), grid=(nb,),
        in_specs=[pl.BlockSpec((DEC_T, pw), rowmap),
                  pl.BlockSpec((1, POOL_HIST, pw), lambda s: (s, 0, 0)),
                  pl.BlockSpec(w_pool.shape, lambda s: (0, 0, 0)),
                  pl.BlockSpec((1, pw), lambda s: (0, 0)),
                  pl.BlockSpec(memory_space=pl.ANY)],
        out_specs=pl.BlockSpec((DEC_T, pw), rowmap),
        out_shape=jax.ShapeDtypeStruct(ybuf.shape, F32),
        scratch_shapes=[pltpu.VMEM((16 + DEC_T, pw), F32)],
        input_output_aliases={4: 0},
        compiler_params=_cparams("arbitrary"), name="pool_sample")(p, hist, w_pool, s_pool, ybuf)


def _cumsum_lanes(x):
    lane = lax.broadcasted_iota(jnp.int32, x.shape, 1)
    s = 1
    while s < x.shape[1]:
        x = x + jnp.where(lane >= s, pltpu.roll(x, s, axis=1), 0.0)
        s *= 2
    return x


HEAD_PACK = 2


def _unit_lower_inverse(mats, ii, jj):
    eye = jnp.where(ii == jj, 1.0, 0.0).astype(F32)
    pair = ((ii // 2) == (jj // 2)) & (ii % 2 == 1) & (jj % 2 == 0)
    xs = [eye - jnp.where(pair, a, 0.0) for a in mats]
    s = 2
    while s < BLK:
        mask = ((ii // (2 * s)) == (jj // (2 * s))) & ((ii // s) % 2 == 1) & ((jj // s) % 2 == 0)
        ts = [_bdot(jnp.where(mask, a, 0.0), x) for a, x in zip(mats, xs)]
        xs = [x - _bdot(x, t) for x, t in zip(xs, ts)]
        s *= 2
    return xs


def _delta_scalars(ba, valid, alog_ref, dtb_ref, nh):
    bat = ba.T
    beta = jnp.where(valid, 1.0 / (1.0 + jnp.exp(-bat[0:nh])), 0.0)
    z = bat[nh:2 * nh] + dtb_ref[...]
    softplus = jnp.maximum(z, 0.0) + jnp.log(1.0 + jnp.exp(-jnp.abs(z)))
    g = jnp.where(valid, -jnp.exp(alog_ref[...]) * softplus, 0.0)
    gc = _cumsum_lanes(g)
    glast = jnp.broadcast_to(gc[:, BLK - 1:BLK], gc.shape)
    eg = jnp.exp(gc)
    rows = jnp.concatenate(
        [gc, eg, beta, beta * eg, jnp.exp(glast - gc), jnp.exp(glast),
         jnp.zeros((BLK - 6 * nh, BLK), F32)], axis=0)
    return gc, rows.T


def _l2n(x):
    return x * lax.rsqrt(jnp.sum(x * x, axis=-1, keepdims=True) + EPS)


def _delta_chunk_small(xq, xk, xv, ba, alog_ref, dtb_ref, s_ref, nh, dk):
    t = xq.shape[0]
    zrows = jnp.zeros((BLK - t, BLK), F32)
    lane = lax.broadcasted_iota(jnp.int32, (nh, BLK), 1)
    gc, cols = _delta_scalars(jnp.concatenate([ba, zrows], axis=0), lane < t, alog_ref, dtb_ref, nh)
    ii = lax.broadcasted_iota(jnp.int32, (t, BLK), 0)
    jj = lax.broadcasted_iota(jnp.int32, (t, BLK), 1)
    incl = ii >= jj
    strict = ii > jj
    each = lambda f: [f(h) for h in range(nh)]
    col = lambda h, qi: cols[0:t, qi * nh + h:qi * nh + h + 1]
    pad = lambda x: jnp.concatenate([x, zrows], axis=0)
    q = each(lambda h: _l2n(xq[:, h * dk:(h + 1) * dk]) * (dk ** -0.5))
    k = each(lambda h: _l2n(xk[:, h * dk:(h + 1) * dk]))
    s = each(lambda h: s_ref[h])
    kpad = each(lambda h: pad(k[h]))
    decay = each(lambda h: jnp.where(incl, jnp.exp(jnp.where(incl, col(h, 0) - gc[h:h + 1, :], 0.0)), 0.0))
    a_mat = each(lambda h: jnp.where(strict, _bdot_nt(k[h], kpad[h]) * decay[h] * col(h, 2), 0.0))
    qk = each(lambda h: _bdot_nt(q[h], kpad[h]) * decay[h])
    wu = each(lambda h: jnp.concatenate([col(h, 3) * k[h], col(h, 2) * xv[:, h * dk:(h + 1) * dk]], axis=-1))
    for j in range(t - 1):
        wu = each(lambda h: wu[h] - a_mat[h][:, j:j + 1] * wu[h][j:j + 1, :])
    v_new = each(lambda h: wu[h][:, dk:] - _bdot(wu[h][:, :dk], s[h]))
    o = each(lambda h: col(h, 1) * _bdot(q[h], s[h]))
    for j in range(t):
        o = each(lambda h: o[h] + qk[h][:, j:j + 1] * v_new[h][j:j + 1, :])
    kd = each(lambda h: pad(k[h] * col(h, 4)).T)
    s_new = each(lambda h: cols[:, 5 * nh + h:5 * nh + h + 1] * s[h] + _bdot(kd[h], pad(v_new[h])))
    for h in range(nh):
        s_ref[h] = s_new[h]
    return o


def _delta_chunk(xq, xk, xv, ba, valid, alog_ref, dtb_ref, s_ref, nh, dk):
    assert dk == BLK and nh % HEAD_PACK == 0
    gc, cols = _delta_scalars(ba, valid, alog_ref, dtb_ref, nh)
    n = HEAD_PACK * BLK
    ii = lax.broadcasted_iota(jnp.int32, (n, n), 0)
    jj = lax.broadcasted_iota(jnp.int32, (n, n), 1)
    same = (ii // BLK) == (jj // BLK)
    incl = same & (ii >= jj)
    strict = same & (ii > jj)
    packs = range(nh // HEAD_PACK)
    each = lambda f: [f(p) for p in packs]
    stack = lambda p, f: jnp.concatenate([f(h) for h in range(p * HEAD_PACK, (p + 1) * HEAD_PACK)], axis=0)
    col = lambda p, qi: stack(p, lambda h: cols[:, qi * nh + h:qi * nh + h + 1])
    diag = lambda m: jnp.where(same, jnp.concatenate([m] * HEAD_PACK, axis=1), 0.0)
    q = each(lambda p: stack(p, lambda h: _l2n(xq[:, h * dk:(h + 1) * dk]) * (dk ** -0.5)))
    k = each(lambda p: stack(p, lambda h: _l2n(xk[:, h * dk:(h + 1) * dk])))
    v = each(lambda p: stack(p, lambda h: xv[:, h * dk:(h + 1) * dk]))
    s = each(lambda p: stack(p, lambda h: s_ref[h]))
    diff = each(lambda p: col(p, 0) - jnp.concatenate(
        [gc[h:h + 1, :] for h in range(p * HEAD_PACK, (p + 1) * HEAD_PACK)], axis=1))
    decay = each(lambda p: jnp.where(incl, jnp.exp(jnp.where(incl, diff[p], 0.0)), 0.0))
    a_mat = each(lambda p: jnp.where(strict, _bdot_nt(k[p], k[p]) * decay[p] * col(p, 2), 0.0))
    qk = each(lambda p: _bdot_nt(q[p], k[p]) * decay[p])
    x = _unit_lower_inverse(a_mat, ii, jj)
    rhs = each(lambda p: jnp.concatenate([col(p, 3) * k[p], col(p, 2) * v[p]], axis=-1))
    wu = each(lambda p: _bdot(x[p], rhs[p]))
    v_new = each(lambda p: wu[p][:, dk:] - _bdot(diag(wu[p][:, :dk]), s[p]))
    qs = each(lambda p: _bdot(diag(q[p]), s[p]))
    o = each(lambda p: col(p, 1) * qs[p] + _bdot(qk[p], v_new[p]))
    kd = each(lambda p: jnp.where(same, jnp.concatenate([(k[p] * col(p, 4)).T] * HEAD_PACK, axis=0), 0.0))
    s_new = each(lambda p: col(p, 5) * s[p] + _bdot(kd[p], v_new[p]))
    outs = []
    for p in packs:
        for i in range(HEAD_PACK):
            s_ref[p * HEAD_PACK + i] = s_new[p][i * BLK:(i + 1) * BLK, :]
            outs.append(o[p][i * BLK:(i + 1) * BLK, :])
    return outs


def _delta_out(o, gate, ng_ref):
    return o * lax.rsqrt(jnp.mean(o * o, axis=-1, keepdims=True) + EPS) * ng_ref[...] * _silu(gate)


def _conv_silu(ext_ref, w_ref, t):
    acc = ext_ref[5:5 + t, :] * w_ref[0:1, :]
    for i in range(1, DN_CONV):
        acc = acc + ext_ref[5 + i:5 + i + t, :] * w_ref[i:i + 1, :]
    return _silu(acc)


def _delta_prompt_body(q_ref, k_ref, v_ref, gt_ref, ba_ref, wq_ref, wk_ref, wv_ref, alog_ref, dtb_ref, ng_ref,
                       y_ref, sout_ref, eq_ref, ek_ref, ev_ref, s_ref, *, nh, dk, nblk):
    c = pl.program_id(1)

    @pl.when(c == 0)
    def _():
        for e in (eq_ref, ek_ref, ev_ref):
            e[0:8, :] = jnp.zeros((8, e.shape[1]), F32)
        s_ref[...] = jnp.zeros(s_ref.shape, F32)

    xs = []
    for x_ref, e_ref, w_ref in ((q_ref, eq_ref, wq_ref), (k_ref, ek_ref, wk_ref), (v_ref, ev_ref, wv_ref)):
        e_ref[8:8 + BLK, :] = x_ref[...]
        xs.append(_conv_silu(e_ref, w_ref, BLK))
        e_ref[0:8, :] = x_ref[BLK - 8:, :]
    lane = lax.broadcasted_iota(jnp.int32, (nh, BLK), 1)
    valid = (c > 0) | (lane >= FRONT)
    outs = _delta_chunk(xs[0], xs[1], xs[2], ba_ref[...], valid, alog_ref, dtb_ref, s_ref, nh, dk)
    row = c * BLK + lax.broadcasted_iota(jnp.int32, (BLK, 1), 0)
    for h in range(nh):
        sl = slice(h * dk, (h + 1) * dk)
        y_ref[:, sl] = jnp.where(row >= FRONT, _delta_out(outs[h], gt_ref[:, sl], ng_ref), 0.0)

    @pl.when(c == nblk - 1)
    def _():
        sout_ref[0] = s_ref[...]


def _delta_prompt(p, ba, w_conv, a_log, dt_bias, norm_g, nseq, nblk, col_q, col_gate, nh, dk):
    hw = nh * dk
    cq, ck, cv, cg = col_q // hw, col_q // hw + 1, col_q // hw + 2, col_gate // hw
    rows = lambda cb: pl.BlockSpec((BLK, hw), lambda b, c: (b * nblk + c, cb))
    wcs = lambda cb: pl.BlockSpec((DN_CONV, hw), lambda b, c: (0, cb))
    small = lambda shape: pl.BlockSpec(shape, lambda b, c: (0, 0))
    return pl.pallas_call(
        functools.partial(_delta_prompt_body, nh=nh, dk=dk, nblk=nblk), grid=(nseq, nblk),
        in_specs=[rows(cq), rows(ck), rows(cv), rows(cg),
                  pl.BlockSpec((BLK, BLK), lambda b, c: (b * nblk + c, 0)),
                  wcs(0), wcs(1), wcs(2), small((nh, 1)), small((nh, 1)), small((1, dk))],
        out_specs=[pl.BlockSpec((BLK, hw), lambda b, c: (b * nblk + c, 0)),
                   pl.BlockSpec((1, nh, dk, dk), lambda b, c: (b, 0, 0, 0))],
        out_shape=[jax.ShapeDtypeStruct((p.shape[0], hw), F32),
                   jax.ShapeDtypeStruct((nseq, nh, dk, dk), F32)],
        scratch_shapes=[pltpu.VMEM((8 + BLK, hw), F32)] * 3 + [pltpu.VMEM((nh, dk, dk), F32)],
        compiler_params=_cparams("parallel", "arbitrary"), name="delta_prompt",
    )(p, p, p, p, ba, w_conv, w_conv, w_conv, a_log.reshape(nh, 1), dt_bias.reshape(nh, 1),
      norm_g.reshape(1, dk))


def _delta_sample_body(q_ref, k_ref, v_ref, gt_ref, ba_ref, hq_ref, hk_ref, hv_ref, s0_ref,
                       wq_ref, wk_ref, wv_ref, alog_ref, dtb_ref, ng_ref, ybuf_ref,
                       y_ref, sout_ref, eq_ref, ek_ref, ev_ref, s_ref, *, nh, dk):
    del ybuf_ref
    xs = []
    for x_ref, h_ref, e_ref, w_ref in ((q_ref, hq_ref, eq_ref, wq_ref), (k_ref, hk_ref, ek_ref, wk_ref),
                                       (v_ref, hv_ref, ev_ref, wv_ref)):
        e_ref[5:8, :] = h_ref[0]
        e_ref[8:8 + DEC_T, :] = x_ref[...]
        xs.append(_conv_silu(e_ref, w_ref, DEC_T))
    s_ref[...] = s0_ref[0]
    outs = _delta_chunk_small(xs[0], xs[1], xs[2], ba_ref[...], alog_ref, dtb_ref, s_ref, nh, dk)
    for h in range(nh):
        sl = slice(h * dk, (h + 1) * dk)
        y_ref[:, sl] = _delta_out(outs[h], gt_ref[:, sl], ng_ref)
    sout_ref[0] = s_ref[...]


def _delta_sample(p, ba, conv_hist, s0, w_conv, a_log, dt_bias, norm_g, ybuf, rows_per_seq,
                  col_q, col_gate, nh, dk):
    nb = s0.shape[0]
    hw = nh * dk
    cq, ck, cv, cg = col_q // hw, col_q // hw + 1, col_q // hw + 2, col_gate // hw
    rowmap = lambda cb: (lambda s: (_sample_row_block(s, rows_per_seq), cb))
    rows = lambda cb: pl.BlockSpec((DEC_T, hw), rowmap(cb))
    hist = lambda cb: pl.BlockSpec((1, DN_CONV - 1, hw), lambda s: (s, 0, cb))
    wcs = lambda cb: pl.BlockSpec((DN_CONV, hw), lambda s: (0, cb))
    small = lambda shape: pl.BlockSpec(shape, lambda s: (0, 0))
    return pl.pallas_call(
        functools.partial(_delta_sample_body, nh=nh, dk=dk), grid=(nb,),
        in_specs=[rows(cq), rows(ck), rows(cv), rows(cg), pl.BlockSpec((DEC_T, BLK), rowmap(0)),
                  hist(0), hist(1), hist(2),
                  pl.BlockSpec((1, nh, dk, dk), lambda s: (s, 0, 0, 0)),
                  wcs(0), wcs(1), wcs(2), small((nh, 1)), small((nh, 1)), small((1, dk)),
                  pl.BlockSpec(memory_space=pl.ANY)],
        out_specs=[pl.BlockSpec((DEC_T, hw), rowmap(0)),
                   pl.BlockSpec((1, nh, dk, dk), lambda s: (s, 0, 0, 0))],
        out_shape=[jax.ShapeDtypeStruct(ybuf.shape, F32), jax.ShapeDtypeStruct(s0.shape, F32)],
        scratch_shapes=[pltpu.VMEM((8 + DEC_T, hw), F32)] * 3 + [pltpu.VMEM((nh, dk, dk), F32)],
        input_output_aliases={15: 0},
        compiler_params=_cparams("arbitrary"), name="delta_sample",
    )(p, p, p, p, ba, conv_hist, conv_hist, conv_hist, s0, w_conv, w_conv, w_conv,
      a_log.reshape(nh, 1), dt_bias.reshape(nh, 1), norm_g.reshape(1, dk), ybuf)


def _attn_prompt_body(*refs, nkv, grp, hd, window):
    q_refs = refs[:nkv]
    kp_ref, kc_ref, vp_ref, vc_ref, slope_ref, sink_ref, y_ref = refs[nkv:]
    n = pl.program_id(1)
    i = lax.broadcasted_iota(jnp.int32, (BLK, 2 * BLK), 0)
    j = lax.broadcasted_iota(jnp.int32, (BLK, 2 * BLK), 1)
    dist = BLK + i - j
    krow = (n - 1) * BLK + j
    valid = (dist >= 0) & (dist < window) & (krow >= FRONT)
    distf = dist.astype(F32)
    kk = jnp.concatenate([kp_ref[...], kc_ref[...]], axis=0).astype(BF16)
    vv = jnp.concatenate([vp_ref[...], vc_ref[...]], axis=0).astype(BF16)
    heads = [(kv, g) for kv in range(nkv) for g in range(grp)]
    each = lambda f: [f(t, kv, g) for t, (kv, g) in enumerate(heads)]
    s = each(lambda t, kv, g: _bdot_nt(q_refs[kv][:, g * hd:(g + 1) * hd], kk[:, kv * hd:(kv + 1) * hd]))
    s = each(lambda t, kv, g: jnp.where(valid, s[t] * (hd ** -0.5) - slope_ref[t] * distf, NEG_INF))
    m = each(lambda t, kv, g: jnp.maximum(jnp.max(s[t], axis=-1, keepdims=True), sink_ref[t]))
    pr = each(lambda t, kv, g: jnp.exp(s[t] - m[t]))
    den = each(lambda t, kv, g: jnp.sum(pr[t], axis=-1, keepdims=True) + jnp.exp(sink_ref[t] - m[t]))
    pv = each(lambda t, kv, g: _bdot(pr[t], vv[:, kv * hd:(kv + 1) * hd]))
    real = n * BLK + lax.broadcasted_iota(jnp.int32, (BLK, 1), 0) >= FRONT
    for t in range(len(heads)):
        y_ref[:, t * hd:(t + 1) * hd] = jnp.where(real, pv[t] / den[t], 0.0)


def _attn_prompt(p, slopes, sinks, nseq, nblk, col_q, col_k, col_v, nkv, grp, hd, window):
    gw, kw = grp * hd, nkv * hd
    prev = lambda cb: (lambda b, n: (b * nblk + jnp.maximum(n - 1, 0), cb))
    cur = lambda cb: (lambda b, n: (b * nblk + n, cb))
    smem = pl.BlockSpec(memory_space=pltpu.SMEM)
    return pl.pallas_call(
        functools.partial(_attn_prompt_body, nkv=nkv, grp=grp, hd=hd, window=window), grid=(nseq, nblk),
        in_specs=[pl.BlockSpec((BLK, gw), cur(col_q // gw + kv)) for kv in range(nkv)]
        + [pl.BlockSpec((BLK, kw), prev(col_k // kw)), pl.BlockSpec((BLK, kw), cur(col_k // kw)),
           pl.BlockSpec((BLK, kw), prev(col_v // kw)), pl.BlockSpec((BLK, kw), cur(col_v // kw)),
           smem, smem],
        out_specs=pl.BlockSpec((BLK, nkv * gw), cur(0)),
        out_shape=jax.ShapeDtypeStruct((p.shape[0], nkv * gw), F32),
        compiler_params=_cparams("parallel", "arbitrary"), name="attn_prompt",
    )(*([p] * (nkv + 4)), slopes, sinks)


def _attn_sample_body(*refs, nkv, grp, hd, window):
    q_refs = refs[:nkv]
    k_ref, v_ref, ck_ref, cv_ref, slope_ref, sink_ref, ybuf_ref, y_ref = refs[nkv:]
    del ybuf_ref
    wc = ck_ref.shape[1]
    i = lax.broadcasted_iota(jnp.int32, (DEC_T, wc), 0)
    j = lax.broadcasted_iota(jnp.int32, (DEC_T, wc), 1)
    dist_c = wc + i - j
    valid_c = (dist_c >= 0) & (dist_c < window)
    i2 = lax.broadcasted_iota(jnp.int32, (DEC_T, DEC_T), 0)
    j2 = lax.broadcasted_iota(jnp.int32, (DEC_T, DEC_T), 1)
    dist_n = i2 - j2
    valid_n = (dist_n >= 0) & (dist_n < window)
    tile = lambda a: jnp.concatenate([a] * grp, axis=0)
    valid_c, valid_n = tile(valid_c), tile(valid_n)
    dist_c, dist_n = tile(dist_c).astype(F32), tile(dist_n).astype(F32)
    gi = lax.broadcasted_iota(jnp.int32, (grp * DEC_T, 1), 0) // DEC_T

    def per_row(ref, kv):
        out = jnp.full((grp * DEC_T, 1), ref[kv * grp], F32)
        for g in range(1, grp):
            out = jnp.where(gi == g, ref[kv * grp + g], out)
        return out

    each = lambda f: [f(kv) for kv in range(nkv)]
    cols = lambda ref, kv: ref[:, kv * hd:(kv + 1) * hd]
    q = each(lambda kv: jnp.concatenate([q_refs[kv][:, g * hd:(g + 1) * hd] for g in range(grp)], axis=0))
    slope = each(lambda kv: per_row(slope_ref, kv))
    sink = each(lambda kv: per_row(sink_ref, kv))
    sc = each(lambda kv: _bdot_nt(q[kv], ck_ref[0, :, kv * hd:(kv + 1) * hd]) * (hd ** -0.5))
    sn = each(lambda kv: _bdot_nt(q[kv], cols(k_ref, kv)) * (hd ** -0.5))
    sc = each(lambda kv: jnp.where(valid_c, sc[kv] - slope[kv] * dist_c, NEG_INF))
    sn = each(lambda kv: jnp.where(valid_n, sn[kv] - slope[kv] * dist_n, NEG_INF))
    m = each(lambda kv: jnp.maximum(jnp.maximum(jnp.max(sc[kv], axis=-1, keepdims=True),
                                                jnp.max(sn[kv], axis=-1, keepdims=True)), sink[kv]))
    pc = each(lambda kv: jnp.exp(sc[kv] - m[kv]))
    pn = each(lambda kv: jnp.exp(sn[kv] - m[kv]))
    den = each(lambda kv: jnp.sum(pc[kv], axis=-1, keepdims=True) + jnp.sum(pn[kv], axis=-1, keepdims=True)
               + jnp.exp(sink[kv] - m[kv]))
    o = each(lambda kv: (_bdot(pc[kv], cv_ref[0, :, kv * hd:(kv + 1) * hd]) + _bdot(pn[kv], cols(v_ref, kv)))
             / den[kv])
    for kv in range(nkv):
        for g in range(grp):
            head = kv * grp + g
            y_ref[:, head * hd:(head + 1) * hd] = o[kv][g * DEC_T:(g + 1) * DEC_T, :]


def _attn_sample(p, cache_k, cache_v, slopes, sinks, ybuf, rows_per_seq, col_q, col_k, col_v,
                 nkv, grp, hd, window):
    nb, wc = cache_k.shape[0], cache_k.shape[1]
    gw, kw = grp * hd, nkv * hd
    ck3 = cache_k.reshape(nb, wc, kw)
    cv3 = cache_v.reshape(nb, wc, kw)
    rowmap = lambda cb: (lambda s: (_sample_row_block(s, rows_per_seq), cb))
    smem = pl.BlockSpec(memory_space=pltpu.SMEM)
    return pl.pallas_call(
        functools.partial(_attn_sample_body, nkv=nkv, grp=grp, hd=hd, window=window), grid=(nb,),
        in_specs=[pl.BlockSpec((DEC_T, gw), rowmap(col_q // gw + kv)) for kv in range(nkv)]
        + [pl.BlockSpec((DEC_T, kw), rowmap(col_k // kw)),
           pl.BlockSpec((DEC_T, kw), rowmap(col_v // kw)),
           pl.BlockSpec((1, wc, kw), lambda s: (s, 0, 0)),
           pl.BlockSpec((1, wc, kw), lambda s: (s, 0, 0)),
           smem, smem, pl.BlockSpec(memory_space=pl.ANY)],
        out_specs=pl.BlockSpec((DEC_T, nkv * gw), rowmap(0)),
        out_shape=jax.ShapeDtypeStruct(ybuf.shape, F32),
        input_output_aliases={nkv + 6: 0},
        compiler_params=_cparams("arbitrary"), name="attn_sample",
    )(*([p] * (nkv + 2)), ck3, cv3, slopes, sinks, ybuf)


SUB = 8


def _bitonic_pairs(n, merge_only=False):
    out = []
    k = n if merge_only else 2
    while k <= n:
        j = k // 2
        while j >= 1:
            out += [(i, i ^ j, (i & k) == 0) for i in range(n) if (i ^ j) > i]
            j //= 2
        k *= 2
    return out


def _compare_exchange(v, pairs):
    v = list(v)
    for i, l, desc in pairs:
        hi, lo = jnp.maximum(v[i], v[l]), jnp.minimum(v[i], v[l])
        v[i], v[l] = (hi, lo) if desc else (lo, hi)
    return v


def _top16(rows):
    v = _compare_exchange(rows, _bitonic_pairs(TOPK))
    shift = SUB // 2
    while shift >= 1:
        w = [pltpu.roll(v[TOPK - 1 - r], shift, axis=0) for r in range(TOPK)]
        v = _compare_exchange([jnp.maximum(a, b) for a, b in zip(v, w)], _bitonic_pairs(TOPK, merge_only=True))
        shift //= 2
    return v


def _sublane_sum(x):
    shift = SUB // 2
    while shift >= 1:
        x = x + pltpu.roll(x, shift, axis=0)
        shift //= 2
    return x


def _on_sublanes(vs):
    sub = lax.broadcasted_iota(jnp.int32, vs[0].shape, 0)
    out = vs[SUB - 1]
    for j in range(SUB - 2, -1, -1):
        out = jnp.where(sub == j, vs[j], out)
    return out


def _peer_topk_body(q_ref, keys_ref, rk_ref, cut_ref, e1_ref, e2_ref, *, nh):
    nv = NKEYS // SUB
    for h in range(nh):
        sc = [_bdot_nt(keys_ref[2 * h + half], q_ref[:, (2 * h + half) * NKEYS:(2 * h + half + 1) * NKEYS])
              for half in (0, 1)]
        s1 = [sc[0][SUB * i:SUB * (i + 1), :] for i in range(nv)]
        s2 = [sc[1][SUB * i:SUB * (i + 1), :] for i in range(nv)]
        a = _top16(s1)
        b = _top16(s2)
        b_lo, b_hi, a_hi = _on_sublanes(b[:SUB]), _on_sublanes(b[SUB:]), _on_sublanes(a[SUB:])
        cand = [a[0] + b_lo, a[0] + b_hi, a_hi + b[0]] + [a[i] + b_lo for i in range(1, SUB)]
        cand += [jnp.full(cand[0].shape, LOWEST, F32)] * (TOPK - len(cand))
        top = _top16(cand)
        thr = top[TOPK - 1]
        zsum = jnp.exp(top[0] - top[0])
        for r in range(1, TOPK):
            zsum = zsum + jnp.exp(top[r] - top[0])
        rz = 1.0 / zsum
        height = [_sublane_sum(jnp.where(a[r] + b_lo >= thr, 1.0, 0.0) + jnp.where(a[r] + b_hi >= thr, 1.0, 0.0))
                  for r in range(TOPK)]
        cut, rank2 = [], []
        for i in range(nv):
            c = jnp.zeros(s1[i].shape, F32)
            for r in range(TOPK - 1, -1, -1):
                c = jnp.where(s1[i] == a[r], height[r], c)
            cut.append(c)
            k = jnp.where(b[0] > s2[i], 1.0, 0.0)
            for r in range(1, TOPK):
                k = k + jnp.where(b[r] > s2[i], 1.0, 0.0)
            rank2.append(k)
        rk_ref[h] = jnp.concatenate(rank2, axis=0).astype(BF16)
        cut_ref[h] = jnp.concatenate(cut, axis=0)
        e1_ref[h] = jnp.exp(sc[0] - a[0][0:1, :])
        e2_ref[h] = (jnp.exp(sc[1] - b[0][0:1, :]) * rz[0:1, :]).astype(BF16)


def _peer_topk(q, sub_keys, *, tm=128):
    n = q.shape[0]
    nh = sub_keys.shape[0]
    keys = sub_keys.reshape(2 * nh, NKEYS, sub_keys.shape[-1])
    sspec = pl.BlockSpec((nh, NKEYS, tm), lambda i: (0, 0, i))
    shape = lambda dt: jax.ShapeDtypeStruct((nh, NKEYS, n), dt)
    return pl.pallas_call(
        functools.partial(_peer_topk_body, nh=nh), grid=(n // tm,),
        in_specs=[pl.BlockSpec((tm, q.shape[1]), lambda i: (i, 0)),
                  pl.BlockSpec(keys.shape, lambda i: (0, 0, 0))],
        out_specs=[sspec] * 4,
        out_shape=[shape(BF16), shape(F32), shape(F32), shape(BF16)],
        compiler_params=_cparams("parallel"), name="peer_topk")(q, keys)


def _gelu(x):
    return 0.5 * x * (1.0 + lax.erf(x * (2.0 ** -0.5)))


PACK = 16


def _peer_expert_body(xn_ref, wd_ref, wu_ref, id_ref, iu_ref, rk_ref, cut_ref, e1_ref, e2_ref, y_ref,
                      xt_ref, ix_ref, *, nh, ei, sub):
    c = pl.program_id(1)
    tm = xt_ref.shape[1]
    tw = tm // sub
    slabs = [slice(t * tw, (t + 1) * tw) for t in range(sub)]
    each = lambda f: [f(t, tl) for t, tl in enumerate(slabs)]

    @pl.when(c == 0)
    def _():
        y_ref[...] = jnp.zeros(y_ref.shape, F32)
        x = xn_ref[...].astype(F32)
        sx = _amax_scale(x)
        xt_ref[...] = (x * sx).T.astype(FP8)
        ix_ref[...] = jnp.broadcast_to(1.0 / sx, ix_ref.shape)

    def gates(tl):
        def row16(ref, h, i1):
            return jnp.broadcast_to(ref[h, pl.ds(i1, 1), tl], (PACK, tw)).astype(BF16)
        pieces = []
        for ii in range(ei):
            i1 = c * ei + ii
            cut = [row16(cut_ref, h, i1) for h in range(nh)]
            e1 = [row16(e1_ref, h, i1) for h in range(nh)]
            for r in range(NKEYS // PACK):
                sl = slice(r * PACK, (r + 1) * PACK)
                gate = pieces[-1] * 0.0 if pieces else None
                for h in range(nh):
                    w = jnp.where(rk_ref[h, sl, tl] < cut[h], e1[h] * e2_ref[h, sl, tl], 0.0)
                    gate = w if gate is None else gate + w
                pieces.append(gate)
        return pieces

    inv_h = id_ref[0, 0:1, 0:1] * ix_ref[0:1, 0:1]
    inv_u = iu_ref[0, 0:1, 0:1]
    ht = each(lambda t, tl: jnp.dot(wd_ref[...], xt_ref[:, tl], preferred_element_type=F32) * inv_h)
    gate = each(lambda t, tl: gates(tl))
    act = each(lambda t, tl: _gelu(ht[t]))
    at = each(lambda t, tl: jnp.concatenate(
        [act[t][i * PACK:(i + 1) * PACK, :] * g.astype(F32) for i, g in enumerate(gate[t])], axis=0))
    sa = each(lambda t, tl: _amax_scale(at[t]))
    part = each(lambda t, tl: jnp.dot((at[t] * sa[t]).T.astype(FP8), wu_ref[...], preferred_element_type=F32))
    for t, tl in enumerate(slabs):
        y_ref[tl, :] += part[t] * (inv_u / sa[t])


PEER_TM, PEER_EI, PEER_SUB = 512, 8, 2


def _peer_expert(xn, w_down, w_up, inv_down, inv_up, rank2, cut, e1, e2, *, tm=PEER_TM, ei=PEER_EI, sub=PEER_SUB):
    n, d = xn.shape
    nh = rank2.shape[0]
    e = ei * NKEYS
    once = dict(pipeline_mode=pl.Buffered(1))
    sspec = pl.BlockSpec((nh, NKEYS, tm), lambda i, c: (0, 0, i))
    inv = pl.BlockSpec((1, 8, BLK), lambda i, c: (c, 0, 0))
    return pl.pallas_call(
        functools.partial(_peer_expert_body, nh=nh, ei=ei, sub=sub), grid=(n // tm, w_down.shape[0] // e),
        in_specs=[pl.BlockSpec((tm, d), lambda i, c: (i, 0)),
                  pl.BlockSpec((e, d), lambda i, c: (c, 0)),
                  pl.BlockSpec((e, d), lambda i, c: (c, 0)),
                  inv, inv, sspec, sspec, sspec, sspec],
        out_specs=pl.BlockSpec((tm, d), lambda i, c: (i, 0), **once),
        out_shape=jax.ShapeDtypeStruct((n, d), F32),
        scratch_shapes=[pltpu.VMEM((d, tm), FP8), pltpu.VMEM((8, BLK), F32)],
        compiler_params=_cparams("parallel", "arbitrary"), name="peer_expert",
    )(xn, w_down, w_up, inv_down, inv_up, rank2, cut, e1, e2)


def _final_norm(h, y, g, nseq, nblk, row_blk, first_blk, nblk_out):
    d = h.shape[1]
    row = pl.BlockSpec((row_blk, d), lambda b, i: (b * nblk + first_blk + i, 0))
    return pl.pallas_call(
        _final_norm_body, grid=(nseq, nblk_out),
        in_specs=[row, row, pl.BlockSpec((1, d), lambda b, i: (0, 0))],
        out_specs=pl.BlockSpec((1, row_blk, d), lambda b, i: (b, i, 0)),
        out_shape=jax.ShapeDtypeStruct((nseq, nblk_out * row_blk, d), F32),
        compiler_params=_cparams("parallel", "parallel"), name="final_norm",
    )(h, y, g.reshape(1, d).astype(F32))


def _final_norm_body(h_ref, y_ref, g_ref, o_ref):
    x = h_ref[...] + y_ref[...]
    ms = jnp.mean(x * x, axis=-1, keepdims=True)
    o_ref[0] = x * lax.rsqrt(ms + EPS) * g_ref[...]


def kernel(x_prompt, x_sample, state_pool, state_conv, state_delta, cache_k, cache_v, meta_tokens, norm1_g,
           w_in, w_pool, s_pool, w_conv, a_log, dt_bias, dn_norm_g, attn_sinks, w_out, norm2_g,
           peer_w_query, peer_sub_keys, peer_w_down, peer_w_up, final_norm_g):
    nseq, seq, d = x_prompt.shape
    nsamp, dec_t, _ = x_sample.shape
    depth = w_in.shape[0]
    pool_w = w_pool.shape[1] * w_pool.shape[2]
    dn_qkv = w_conv.shape[2]
    nh, dk = state_delta.shape[2], state_delta.shape[3]
    wc, nkv, hd = cache_k.shape[2], cache_k.shape[3], cache_k.shape[4]
    nq = attn_sinks.shape[1]
    grp = nq // nkv
    window = wc
    assert dec_t == DEC_T and nsamp == nseq * SLOTS and (N_META + seq) % BLK == N_META
    assert SLOTS * DEC_T + POOL_HIST <= FRONT and dn_qkv == 3 * nh * dk and wc == BLK
    rows_per_seq = FRONT + N_META + seq
    nblk = rows_per_seq // BLK
    col_qkv = pool_w
    col_gate = col_qkv + dn_qkv
    col_q = col_gate + nh * dk
    col_k = col_q + nq * hd
    col_v = col_k + nkv * hd
    src_ba = pool_w + dn_qkv

    slopes = jnp.exp2(-8.0 * (jnp.arange(nq, dtype=F32) + 1.0) / nq)
    new_p = [[] for _ in range(5)]
    new_s = [[] for _ in range(5)]
    y_peer = None
    w_in_t = jnp.swapaxes(w_in, 1, 2)
    for l in range(depth):
        w_main, w_ba = _cast_w_in(w_in_t, l, src_ba, 2 * nh)
        if l == 0:
            h, xn = _assemble_norm(x_prompt, x_sample, meta_tokens, norm1_g[l], nblk)
        else:
            h, xn = _addnorm(h, y_peer, norm1_g[l])
        p = _mm(xn, w_main, nt=True)
        ba = _mm(xn, w_ba, nt=True)

        y_pool = _pool_prompt(p, w_pool[l], s_pool[l].reshape(1, pool_w), nseq, nblk)
        y_pool = _pool_sample(p, state_pool[l], w_pool[l], s_pool[l].reshape(1, pool_w), y_pool,
                              rows_per_seq, PAST_LEN)
        y_dn, s_p = _delta_prompt(p, ba, w_conv[l], a_log[l], dt_bias[l], dn_norm_g[l], nseq, nblk,
                                  col_qkv, col_gate, nh, dk)
        y_dn, s_s = _delta_sample(p, ba, state_conv[l], state_delta[l], w_conv[l], a_log[l], dt_bias[l],
                                  dn_norm_g[l], y_dn, rows_per_seq, col_qkv, col_gate, nh, dk)
        y_att = _attn_prompt(p, slopes, attn_sinks[l], nseq, nblk, col_q, col_k, col_v, nkv, grp, hd, window)
        y_att = _attn_sample(p, cache_k[l], cache_v[l], slopes, attn_sinks[l], y_att, rows_per_seq,
                             col_q, col_k, col_v, nkv, grp, hd, window)
        h = _outproj(y_pool, y_dn, y_att, _cast(w_out, l), h)

        _, xn2 = _addnorm(h, None, norm2_g[l])
        q = _mm(xn2, _cast(peer_w_query, l))
        rank2, cut, e1, e2 = _peer_topk(q, peer_sub_keys[l])
        wd8, inv_d = _quant_fp8(peer_w_down, l, tr=PEER_EI * NKEYS)
        wu8, inv_u = _quant_fp8(peer_w_up, l, tr=PEER_EI * NKEYS)
        y_peer = _peer_expert(xn2, wd8, wu8, inv_d, inv_u, rank2, cut, e1, e2)

        p3 = p.reshape(nseq, rows_per_seq, p.shape[1])
        ps = p3[:, :SLOTS * DEC_T].reshape(nsamp, DEC_T, p.shape[1])
        new_p[0].append(p3[:, -POOL_HIST:, :pool_w])
        new_s[0].append(jnp.concatenate([state_pool[l], ps[:, :, :pool_w]], axis=1)[:, -POOL_HIST:])
        new_p[1].append(p3[:, -(DN_CONV - 1):, col_qkv:col_gate])
        new_s[1].append(jnp.concatenate([state_conv[l], ps[:, :, col_qkv:col_gate]], axis=1)[:, -(DN_CONV - 1):])
        new_p[2].append(s_p)
        new_s[2].append(s_s)
        new_p[3].append(p3[:, -window:, col_k:col_v].reshape(nseq, window, nkv, hd))
        new_s[3].append(jnp.concatenate([cache_k[l], ps[:, :, col_k:col_v].reshape(nsamp, DEC_T, nkv, hd)],
                                        axis=1)[:, -wc:])
        new_p[4].append(p3[:, -window:, col_v:].reshape(nseq, window, nkv, hd))
        new_s[4].append(jnp.concatenate([cache_v[l], ps[:, :, col_v:].reshape(nsamp, DEC_T, nkv, hd)],
                                        axis=1)[:, -wc:])

    y_prompt = _final_norm(h, y_peer, final_norm_g, nseq, nblk, BLK, 1, nblk - 1)
    y_sample = _final_norm(h, y_peer, final_norm_g, nseq, rows_per_seq // (SLOTS * DEC_T), SLOTS * DEC_T, 0, 1)
    y_sample = y_sample.reshape(nsamp, DEC_T, d)
    pool_p, conv_p, delta_p, k_p, v_p = (jnp.stack(a) for a in new_p)
    pool_s, conv_s, delta_s, k_s, v_s = (jnp.stack(a) for a in new_s)
    return (y_prompt, y_sample, pool_p, pool_s, conv_p, conv_s, delta_p, delta_s, k_p, k_s, v_p, v_s)
```

```python
import functools

import jax
import jax.numpy as jnp
from jax import lax
from jax.experimental import pallas as pl
from jax.experimental.pallas import tpu as pltpu

F32 = jnp.float32
BF16 = jnp.bfloat16

EPS = 1e-6
NEG_INF = -1e30
LOWEST = -3.0e38

PAST_LEN = 16384
N_META = 16
BLK = 128
FRONT = BLK - N_META
DEC_T = 8
SLOTS = 8
POOL_WINDOWS = (2, 4, 8, 16)
POOL_HIST = 15
DN_CONV = 4
TOPK = 16
NKEYS = 128
VMEM_LIMIT = 56 * 1024 * 1024


def _cparams(*sem):
    return pltpu.CompilerParams(dimension_semantics=sem, vmem_limit_bytes=VMEM_LIMIT)


def _bdot(a, b):
    return jnp.dot(a.astype(BF16), b.astype(BF16), preferred_element_type=F32)


def _bdot_nt(a, b):
    return lax.dot_general(a.astype(BF16), b.astype(BF16), (((1,), (1,)), ((), ())),
                           preferred_element_type=F32)


def _silu(x):
    return x * (1.0 / (1.0 + jnp.exp(-x)))


def _addnorm_body(*refs, add):
    if add:
        h_ref, y_ref, g_ref, hs_ref, xn_ref = refs
        x = h_ref[...] + y_ref[...]
        hs_ref[...] = x
    else:
        h_ref, g_ref, xn_ref = refs
        x = h_ref[...]
    ms = jnp.mean(x * x, axis=-1, keepdims=True)
    xn_ref[...] = (x * lax.rsqrt(ms + EPS) * g_ref[...]).astype(xn_ref.dtype)


def _addnorm(h, y, g, *, tm=256, out_dtype=BF16):
    n, d = h.shape
    row = pl.BlockSpec((tm, d), lambda i: (i, 0))
    gspec = pl.BlockSpec((1, d), lambda i: (0, 0))
    g2 = g.reshape(1, d).astype(F32)
    if y is None:
        xn = pl.pallas_call(
            functools.partial(_addnorm_body, add=False),
            grid=(n // tm,), in_specs=[row, gspec], out_specs=row,
            out_shape=jax.ShapeDtypeStruct((n, d), out_dtype),
            compiler_params=_cparams("parallel"), name="norm")(h, g2)
        return h, xn
    hs, xn = pl.pallas_call(
        functools.partial(_addnorm_body, add=True),
        grid=(n // tm,), in_specs=[row, row, gspec], out_specs=[row, row],
        out_shape=[jax.ShapeDtypeStruct((n, d), F32), jax.ShapeDtypeStruct((n, d), out_dtype)],
        compiler_params=_cparams("parallel"), name="add_norm")(h, y, g2)
    return hs, xn


def _assemble_norm_body(xs_ref, meta_ref, xp_ref, g_ref, h_ref, xn_ref):
    n = pl.program_id(1)

    @pl.when(n == 0)
    def _():
        ns = xs_ref.shape[1]
        h_ref[0:ns, :] = xs_ref[0]
        h_ref[ns:FRONT, :] = jnp.zeros((FRONT - ns, h_ref.shape[1]), F32)
        h_ref[FRONT:, :] = meta_ref[...]

    @pl.when(n > 0)
    def _():
        h_ref[...] = xp_ref[0]

    x = h_ref[...]
    ms = jnp.mean(x * x, axis=-1, keepdims=True)
    xn_ref[...] = (x * lax.rsqrt(ms + EPS) * g_ref[...]).astype(xn_ref.dtype)


def _assemble_norm(x_prompt, x_sample, meta_tokens, g, nblk):
    nseq, _, d = x_prompt.shape
    xs = x_sample.reshape(nseq, SLOTS * DEC_T, d)
    out = pl.BlockSpec((BLK, d), lambda b, n: (b * nblk + n, 0))
    return pl.pallas_call(
        _assemble_norm_body, grid=(nseq, nblk),
        in_specs=[pl.BlockSpec((1, SLOTS * DEC_T, d), lambda b, n: (b, 0, 0)),
                  pl.BlockSpec((N_META, d), lambda b, n: (0, 0)),
                  pl.BlockSpec((1, BLK, d), lambda b, n: (b, jnp.maximum(n - 1, 0), 0)),
                  pl.BlockSpec((1, d), lambda b, n: (0, 0))],
        out_specs=[out, out],
        out_shape=[jax.ShapeDtypeStruct((nseq * nblk * BLK, d), F32),
                   jax.ShapeDtypeStruct((nseq * nblk * BLK, d), BF16)],
        compiler_params=_cparams("parallel", "arbitrary"), name="assemble_norm",
    )(xs, meta_tokens, x_prompt, g.reshape(1, d).astype(F32))


def _cast_body(x_ref, o_ref):
    o_ref[...] = x_ref[0].astype(o_ref.dtype)


def _cast(x, l, dtype=BF16, *, tr=512):
    _, r, c = x.shape
    return pl.pallas_call(
        _cast_body, grid=(r // tr,),
        in_specs=[pl.BlockSpec((1, tr, c), lambda i: (l, i, 0))],
        out_specs=pl.BlockSpec((tr, c), lambda i: (i, 0)),
        out_shape=jax.ShapeDtypeStruct((r, c), dtype),
        compiler_params=_cparams("parallel"), name="cast")(x)


FP8 = jnp.float8_e4m3fn
FP8_TARGET = 224.0


def _amax_scale(x):
    a = jnp.max(jnp.max(jnp.abs(x), axis=1, keepdims=True), axis=0, keepdims=True)
    return jnp.where(a > 0.0, FP8_TARGET / a, 1.0)


def _quant_body(x_ref, o_ref, inv_ref):
    x = x_ref[0]
    s = _amax_scale(x)
    o_ref[...] = (x * s).astype(FP8)
    inv_ref[0] = jnp.broadcast_to(1.0 / s, inv_ref.shape[1:])


def _quant_fp8(x, l, *, tr):
    _, r, c = x.shape
    return pl.pallas_call(
        _quant_body, grid=(r // tr,),
        in_specs=[pl.BlockSpec((1, tr, c), lambda i: (l, i, 0))],
        out_specs=[pl.BlockSpec((tr, c), lambda i: (i, 0)), pl.BlockSpec((1, 8, BLK), lambda i: (i, 0, 0))],
        out_shape=[jax.ShapeDtypeStruct((r, c), FP8), jax.ShapeDtypeStruct((r // tr, 8, BLK), F32)],
        compiler_params=_cparams("parallel"), name="quant_fp8")(x)


def _cast_w_in_body(a_ref, b_ref, main_ref, ba_ref, *, nlo, nba):
    j = pl.program_id(0)
    tr = a_ref.shape[1]

    @pl.when(j < nlo)
    def _():
        main_ref[...] = a_ref[0].astype(BF16)

    @pl.when(j >= nlo)
    def _():
        main_ref[0:tr - nba, :] = a_ref[0, nba:, :].astype(BF16)
        main_ref[tr - nba:, :] = b_ref[0].astype(BF16)

    @pl.when(j == nlo)
    def _():
        ba_ref[0:nba, :] = a_ref[0, 0:nba, :].astype(BF16)
        ba_ref[nba:, :] = jnp.zeros((ba_ref.shape[0] - nba, ba_ref.shape[1]), BF16)


def _cast_w_in(w_in_t, l, lo, nba, *, tr=1024):
    _, r, c = w_in_t.shape
    assert lo % tr == 0 and (r - nba) % tr == 0 and tr % nba == 0 and nba % 8 == 0
    return pl.pallas_call(
        functools.partial(_cast_w_in_body, nlo=lo // tr, nba=nba), grid=((r - nba) // tr,),
        in_specs=[pl.BlockSpec((1, tr, c), lambda j: (l, j, 0)),
                  pl.BlockSpec((1, nba, c), lambda j: (l, (j + 1) * (tr // nba), 0))],
        out_specs=[pl.BlockSpec((tr, c), lambda j: (j, 0)), pl.BlockSpec((BLK, c), lambda j: (0, 0))],
        out_shape=[jax.ShapeDtypeStruct((r - nba, c), BF16), jax.ShapeDtypeStruct((BLK, c), BF16)],
        compiler_params=_cparams("arbitrary"), name="cast_w_in")(w_in_t, w_in_t)


def _mm_body(x_ref, w_ref, o_ref, *, nt):
    if nt:
        o_ref[...] = lax.dot_general(x_ref[...], w_ref[...], (((1,), (1,)), ((), ())),
                                     preferred_element_type=F32)
    else:
        o_ref[...] = jnp.dot(x_ref[...], w_ref[...], preferred_element_type=F32)


def _mm(x, w, *, nt=False, tm=512, tn=2048):
    m, k = x.shape
    n = w.shape[0] if nt else w.shape[1]
    tn = min(tn, n)
    wspec = pl.BlockSpec((tn, k), lambda j, i: (j, 0)) if nt else pl.BlockSpec((k, tn), lambda j, i: (0, j))
    return pl.pallas_call(
        functools.partial(_mm_body, nt=nt), grid=(n // tn, m // tm),
        in_specs=[pl.BlockSpec((tm, k), lambda j, i: (i, 0)), wspec],
        out_specs=pl.BlockSpec((tm, tn), lambda j, i: (i, j)),
        out_shape=jax.ShapeDtypeStruct((m, n), F32),
        compiler_params=_cparams("parallel", "parallel"), name="matmul")(x, w)


def _outproj_body(yp_ref, yd_ref, ya_ref, w_ref, h_ref, o_ref, *, wp, wd):
    acc = h_ref[...]
    acc += jnp.dot(yp_ref[...].astype(BF16), w_ref[0:wp, :], preferred_element_type=F32)
    acc += jnp.dot(yd_ref[...].astype(BF16), w_ref[wp:wp + wd, :], preferred_element_type=F32)
    acc += jnp.dot(ya_ref[...].astype(BF16), w_ref[wp + wd:, :], preferred_element_type=F32)
    o_ref[...] = acc


def _outproj(yp, yd, ya, w, h, *, tm=512, tn=1024):
    m, d = h.shape
    wp, wd, wa = yp.shape[1], yd.shape[1], ya.shape[1]
    k = wp + wd + wa
    return pl.pallas_call(
        functools.partial(_outproj_body, wp=wp, wd=wd), grid=(d // tn, m // tm),
        in_specs=[pl.BlockSpec((tm, wp), lambda j, i: (i, 0)),
                  pl.BlockSpec((tm, wd), lambda j, i: (i, 0)),
                  pl.BlockSpec((tm, wa), lambda j, i: (i, 0)),
                  pl.BlockSpec((k, tn), lambda j, i: (0, j)),
                  pl.BlockSpec((tm, tn), lambda j, i: (i, j))],
        out_specs=pl.BlockSpec((tm, tn), lambda j, i: (i, j)),
        out_shape=jax.ShapeDtypeStruct((m, d), F32),
        compiler_params=_cparams("parallel", "parallel"), name="out_proj")(yp, yd, ya, w, h)


def _pool_windows(ext_ref, u, t, pos, w_ref, s_ref, gw):
    outs = []
    for gi, w in enumerate(POOL_WINDOWS):
        sl = slice(gi * gw, (gi + 1) * gw)
        win = u[:, sl]
        for k in range(1, w):
            win = win + ext_ref[16 - k:16 - k + t, sl]
        cnt = jnp.clip(pos + 1, 1, w).astype(F32)
        d = win / cnt - u[:, sl]
        outs.append(_bdot(d, w_ref[gi]))
    return jnp.concatenate(outs, axis=-1) * s_ref[...]


def _pool_prompt_body(u_ref, w_ref, s_ref, y_ref, ext_ref, *, gw):
    n = pl.program_id(1)

    @pl.when(n == 0)
    def _():
        ext_ref[0:16, :] = jnp.zeros((16, ext_ref.shape[1]), F32)

    u = u_ref[...]
    ext_ref[16:16 + BLK, :] = u
    row = n * BLK + lax.broadcasted_iota(jnp.int32, (BLK, 1), 0)
    pos = row - FRONT
    y = _pool_windows(ext_ref, u, BLK, pos, w_ref, s_ref, gw)
    y_ref[...] = jnp.where(pos >= 0, y, 0.0)
    ext_ref[0:16, :] = u[BLK - 16:, :]


def _pool_prompt(p, w_pool, s_pool, nseq, nblk):
    pw = w_pool.shape[0] * w_pool.shape[1]
    gw = w_pool.shape[1]
    return pl.pallas_call(
        functools.partial(_pool_prompt_body, gw=gw), grid=(nseq, nblk),
        in_specs=[pl.BlockSpec((BLK, pw), lambda b, n: (b * nblk + n, 0)),
                  pl.BlockSpec(w_pool.shape, lambda b, n: (0, 0, 0)),
                  pl.BlockSpec((1, pw), lambda b, n: (0, 0))],
        out_specs=pl.BlockSpec((BLK, pw), lambda b, n: (b * nblk + n, 0)),
        out_shape=jax.ShapeDtypeStruct((p.shape[0], pw), F32),
        scratch_shapes=[pltpu.VMEM((16 + BLK, pw), F32)],
        compiler_params=_cparams("parallel", "arbitrary"), name="pool_prompt")(p, w_pool, s_pool)


def _pool_sample_body(u_ref, hist_ref, w_ref, s_ref, ybuf_ref, y_ref, ext_ref, *, gw, pos0):
    del ybuf_ref
    u = u_ref[...]
    ext_ref[0:1, :] = jnp.zeros((1, ext_ref.shape[1]), F32)
    ext_ref[1:16, :] = hist_ref[0]
    ext_ref[16:16 + DEC_T, :] = u
    pos = pos0 + lax.broadcasted_iota(jnp.int32, (DEC_T, 1), 0)
    y_ref[...] = _pool_windows(ext_ref, u, DEC_T, pos, w_ref, s_ref, gw)


def _sample_row_block(s, rows_per_seq):
    return (s // SLOTS) * (rows_per_seq // DEC_T) + s % SLOTS


def _pool_sample(p, hist, w_pool, s_pool, ybuf, rows_per_seq, pos0):
    nb = hist.shape[0]
    pw = hist.shape[2]
    gw = w_pool.shape[1]
    rowmap = lambda s: (_sample_row_block(s, rows_per_seq), 0)
    return pl.pallas_call(
        functools.partial(_pool_sample_body, gw=gw, pos0=pos0), grid=(nb,),
        in_specs=[pl.BlockSpec((DEC_T, pw), rowmap),
                  pl.BlockSpec((1, POOL_HIST, pw), lambda s: (s, 0, 0)),
                  pl.BlockSpec(w_pool.shape, lambda s: (0, 0, 0)),
                  pl.BlockSpec((1, pw), lambda s: (0, 0)),
                  pl.BlockSpec(memory_space=pl.ANY)],
        out_specs=pl.BlockSpec((DEC_T, pw), rowmap),
        out_shape=jax.ShapeDtypeStruct(ybuf.shape, F32),
        scratch_shapes=[pltpu.VMEM((16 + DEC_T, pw), F32)],
        input_output_aliases={4: 0},
        compiler_params=_cparams("arbitrary"), name="pool_sample")(p, hist, w_pool, s_pool, ybuf)


def _cumsum_lanes(x):
    lane = lax.broadcasted_iota(jnp.int32, x.shape, 1)
    s = 1
    while s < x.shape[1]:
        x = x + jnp.where(lane >= s, pltpu.roll(x, s, axis=1), 0.0)
        s *= 2
    return x


HEAD_PACK = 2


def _unit_lower_inverse(mats, ii, jj):
    eye = jnp.where(ii == jj, 1.0, 0.0).astype(F32)
    pair = ((ii // 2) == (jj // 2)) & (ii % 2 == 1) & (jj % 2 == 0)
    xs = [eye - jnp.where(pair, a, 0.0) for a in mats]
    s = 2
    while s < BLK:
        mask = ((ii // (2 * s)) == (jj // (2 * s))) & ((ii // s) % 2 == 1) & ((jj // s) % 2 == 0)
        ts = [_bdot(jnp.where(mask, a, 0.0), x) for a, x in zip(mats, xs)]
        xs = [x - _bdot(x, t) for x, t in zip(xs, ts)]
        s *= 2
    return xs


def _delta_scalars(ba, valid, alog_ref, dtb_ref, nh):
    bat = ba.T
    beta = jnp.where(valid, 1.0 / (1.0 + jnp.exp(-bat[0:nh])), 0.0)
    z = bat[nh:2 * nh] + dtb_ref[...]
    softplus = jnp.maximum(z, 0.0) + jnp.log(1.0 + jnp.exp(-jnp.abs(z)))
    g = jnp.where(valid, -jnp.exp(alog_ref[...]) * softplus, 0.0)
    gc = _cumsum_lanes(g)
    glast = jnp.broadcast_to(gc[:, BLK - 1:BLK], gc.shape)
    eg = jnp.exp(gc)
    rows = jnp.concatenate(
        [gc, eg, beta, beta * eg, jnp.exp(glast - gc), jnp.exp(glast),
         jnp.zeros((BLK - 6 * nh, BLK), F32)], axis=0)
    return gc, rows.T


def _l2n(x):
    return x * lax.rsqrt(jnp.sum(x * x, axis=-1, keepdims=True) + EPS)


def _delta_chunk_small(xq, xk, xv, ba, alog_ref, dtb_ref, s_ref, nh, dk):
    t = xq.shape[0]
    zrows = jnp.zeros((BLK - t, BLK), F32)
    lane = lax.broadcasted_iota(jnp.int32, (nh, BLK), 1)
    gc, cols = _delta_scalars(jnp.concatenate([ba, zrows], axis=0), lane < t, alog_ref, dtb_ref, nh)
    ii = lax.broadcasted_iota(jnp.int32, (t, BLK), 0)
    jj = lax.broadcasted_iota(jnp.int32, (t, BLK), 1)
    incl = ii >= jj
    strict = ii > jj
    each = lambda f: [f(h) for h in range(nh)]
    col = lambda h, qi: cols[0:t, qi * nh + h:qi * nh + h + 1]
    pad = lambda x: jnp.concatenate([x, zrows], axis=0)
    q = each(lambda h: _l2n(xq[:, h * dk:(h + 1) * dk]) * (dk ** -0.5))
    k = each(lambda h: _l2n(xk[:, h * dk:(h + 1) * dk]))
    s = each(lambda h: s_ref[h])
    kpad = each(lambda h: pad(k[h]))
    decay = each(lambda h: jnp.where(incl, jnp.exp(jnp.where(incl, col(h, 0) - gc[h:h + 1, :], 0.0)), 0.0))
    a_mat = each(lambda h: jnp.where(strict, _bdot_nt(k[h], kpad[h]) * decay[h] * col(h, 2), 0.0))
    qk = each(lambda h: _bdot_nt(q[h], kpad[h]) * decay[h])
    wu = each(lambda h: jnp.concatenate([col(h, 3) * k[h], col(h, 2) * xv[:, h * dk:(h + 1) * dk]], axis=-1))
    for j in range(t - 1):
        wu = each(lambda h: wu[h] - a_mat[h][:, j:j + 1] * wu[h][j:j + 1, :])
    v_new = each(lambda h: wu[h][:, dk:] - _bdot(wu[h][:, :dk], s[h]))
    o = each(lambda h: col(h, 1) * _bdot(q[h], s[h]))
    for j in range(t):
        o = each(lambda h: o[h] + qk[h][:, j:j + 1] * v_new[h][j:j + 1, :])
    kd = each(lambda h: pad(k[h] * col(h, 4)).T)
    s_new = each(lambda h: cols[:, 5 * nh + h:5 * nh + h + 1] * s[h] + _bdot(kd[h], pad(v_new[h])))
    for h in range(nh):
        s_ref[h] = s_new[h]
    return o


def _delta_chunk(xq, xk, xv, ba, valid, alog_ref, dtb_ref, s_ref, nh, dk):
    assert dk == BLK and nh % HEAD_PACK == 0
    gc, cols = _delta_scalars(ba, valid, alog_ref, dtb_ref, nh)
    n = HEAD_PACK * BLK
    ii = lax.broadcasted_iota(jnp.int32, (n, n), 0)
    jj = lax.broadcasted_iota(jnp.int32, (n, n), 1)
    same = (ii // BLK) == (jj // BLK)
    incl = same & (ii >= jj)
    strict = same & (ii > jj)
    packs = range(nh // HEAD_PACK)
    each = lambda f: [f(p) for p in packs]
    stack = lambda p, f: jnp.concatenate([f(h) for h in range(p * HEAD_PACK, (p + 1) * HEAD_PACK)], axis=0)
    col = lambda p, qi: stack(p, lambda h: cols[:, qi * nh + h:qi * nh + h + 1])
    diag = lambda m: jnp.where(same, jnp.concatenate([m] * HEAD_PACK, axis=1), 0.0)
    q = each(lambda p: stack(p, lambda h: _l2n(xq[:, h * dk:(h + 1) * dk]) * (dk ** -0.5)))
    k = each(lambda p: stack(p, lambda h: _l2n(xk[:, h * dk:(h + 1) * dk])))
    v = each(lambda p: stack(p, lambda h: xv[:, h * dk:(h + 1) * dk]))
    s = each(lambda p: stack(p, lambda h: s_ref[h]))
    diff = each(lambda p: col(p, 0) - jnp.concatenate(
        [gc[h:h + 1, :] for h in range(p * HEAD_PACK, (p + 1) * HEAD_PACK)], axis=1))
    decay = each(lambda p: jnp.where(incl, jnp.exp(jnp.where(incl, diff[p], 0.0)), 0.0))
    a_mat = each(lambda p: jnp.where(strict, _bdot_nt(k[p], k[p]) * decay[p] * col(p, 2), 0.0))
    qk = each(lambda p: _bdot_nt(q[p], k[p]) * decay[p])
    x = _unit_lower_inverse(a_mat, ii, jj)
    rhs = each(lambda p: jnp.concatenate([col(p, 3) * k[p], col(p, 2) * v[p]], axis=-1))
    wu = each(lambda p: _bdot(x[p], rhs[p]))
    v_new = each(lambda p: wu[p][:, dk:] - _bdot(diag(wu[p][:, :dk]), s[p]))
    qs = each(lambda p: _bdot(diag(q[p]), s[p]))
    o = each(lambda p: col(p, 1) * qs[p] + _bdot(qk[p], v_new[p]))
    kd = each(lambda p: jnp.where(same, jnp.concatenate([(k[p] * col(p, 4)).T] * HEAD_PACK, axis=0), 0.0))
    s_new = each(lambda p: col(p, 5) * s[p] + _bdot(kd[p], v_new[p]))
    outs = []
    for p in packs:
        for i in range(HEAD_PACK):
            s_ref[p * HEAD_PACK + i] = s_new[p][i * BLK:(i + 1) * BLK, :]
            outs.append(o[p][i * BLK:(i + 1) * BLK, :])
    return outs


def _delta_out(o, gate, ng_ref):
    return o * lax.rsqrt(jnp.mean(o * o, axis=-1, keepdims=True) + EPS) * ng_ref[...] * _silu(gate)


def _conv_silu(ext_ref, w_ref, t):
    acc = ext_ref[5:5 + t, :] * w_ref[0:1, :]
    for i in range(1, DN_CONV):
        acc = acc + ext_ref[5 + i:5 + i + t, :] * w_ref[i:i + 1, :]
    return _silu(acc)


def _delta_prompt_body(q_ref, k_ref, v_ref, gt_ref, ba_ref, wq_ref, wk_ref, wv_ref, alog_ref, dtb_ref, ng_ref,
                       y_ref, sout_ref, eq_ref, ek_ref, ev_ref, s_ref, *, nh, dk, nblk):
    c = pl.program_id(1)

    @pl.when(c == 0)
    def _():
        for e in (eq_ref, ek_ref, ev_ref):
            e[0:8, :] = jnp.zeros((8, e.shape[1]), F32)
        s_ref[...] = jnp.zeros(s_ref.shape, F32)

    xs = []
    for x_ref, e_ref, w_ref in ((q_ref, eq_ref, wq_ref), (k_ref, ek_ref, wk_ref), (v_ref, ev_ref, wv_ref)):
        e_ref[8:8 + BLK, :] = x_ref[...]
        xs.append(_conv_silu(e_ref, w_ref, BLK))
        e_ref[0:8, :] = x_ref[BLK - 8:, :]
    lane = lax.broadcasted_iota(jnp.int32, (nh, BLK), 1)
    valid = (c > 0) | (lane >= FRONT)
    outs = _delta_chunk(xs[0], xs[1], xs[2], ba_ref[...], valid, alog_ref, dtb_ref, s_ref, nh, dk)
    row = c * BLK + lax.broadcasted_iota(jnp.int32, (BLK, 1), 0)
    for h in range(nh):
        sl = slice(h * dk, (h + 1) * dk)
        y_ref[:, sl] = jnp.where(row >= FRONT, _delta_out(outs[h], gt_ref[:, sl], ng_ref), 0.0)

    @pl.when(c == nblk - 1)
    def _():
        sout_ref[0] = s_ref[...]


def _delta_prompt(p, ba, w_conv, a_log, dt_bias, norm_g, nseq, nblk, col_q, col_gate, nh, dk):
    hw = nh * dk
    cq, ck, cv, cg = col_q // hw, col_q // hw + 1, col_q // hw + 2, col_gate // hw
    rows = lambda cb: pl.BlockSpec((BLK, hw), lambda b, c: (b * nblk + c, cb))
    wcs = lambda cb: pl.BlockSpec((DN_CONV, hw), lambda b, c: (0, cb))
    small = lambda shape: pl.BlockSpec(shape, lambda b, c: (0, 0))
    return pl.pallas_call(
        functools.partial(_delta_prompt_body, nh=nh, dk=dk, nblk=nblk), grid=(nseq, nblk),
        in_specs=[rows(cq), rows(ck), rows(cv), rows(cg),
                  pl.BlockSpec((BLK, BLK), lambda b, c: (b * nblk + c, 0)),
                  wcs(0), wcs(1), wcs(2), small((nh, 1)), small((nh, 1)), small((1, dk))],
        out_specs=[pl.BlockSpec((BLK, hw), lambda b, c: (b * nblk + c, 0)),
                   pl.BlockSpec((1, nh, dk, dk), lambda b, c: (b, 0, 0, 0))],
        out_shape=[jax.ShapeDtypeStruct((p.shape[0], hw), F32),
                   jax.ShapeDtypeStruct((nseq, nh, dk, dk), F32)],
        scratch_shapes=[pltpu.VMEM((8 + BLK, hw), F32)] * 3 + [pltpu.VMEM((nh, dk, dk), F32)],
        compiler_params=_cparams("parallel", "arbitrary"), name="delta_prompt",
    )(p, p, p, p, ba, w_conv, w_conv, w_conv, a_log.reshape(nh, 1), dt_bias.reshape(nh, 1),
      norm_g.reshape(1, dk))


def _delta_sample_body(q_ref, k_ref, v_ref, gt_ref, ba_ref, hq_ref, hk_ref, hv_ref, s0_ref,
                       wq_ref, wk_ref, wv_ref, alog_ref, dtb_ref, ng_ref, ybuf_ref,
                       y_ref, sout_ref, eq_ref, ek_ref, ev_ref, s_ref, *, nh, dk):
    del ybuf_ref
    xs = []
    for x_ref, h_ref, e_ref, w_ref in ((q_ref, hq_ref, eq_ref, wq_ref), (k_ref, hk_ref, ek_ref, wk_ref),
                                       (v_ref, hv_ref, ev_ref, wv_ref)):
        e_ref[5:8, :] = h_ref[0]
        e_ref[8:8 + DEC_T, :] = x_ref[...]
        xs.append(_conv_silu(e_ref, w_ref, DEC_T))
    s_ref[...] = s0_ref[0]
    outs = _delta_chunk_small(xs[0], xs[1], xs[2], ba_ref[...], alog_ref, dtb_ref, s_ref, nh, dk)
    for h in range(nh):
        sl = slice(h * dk, (h + 1) * dk)
        y_ref[:, sl] = _delta_out(outs[h], gt_ref[:, sl], ng_ref)
    sout_ref[0] = s_ref[...]


def _delta_sample(p, ba, conv_hist, s0, w_conv, a_log, dt_bias, norm_g, ybuf, rows_per_seq,
                  col_q, col_gate, nh, dk):
    nb = s0.shape[0]
    hw = nh * dk
    cq, ck, cv, cg = col_q // hw, col_q // hw + 1, col_q // hw + 2, col_gate // hw
    rowmap = lambda cb: (lambda s: (_sample_row_block(s, rows_per_seq), cb))
    rows = lambda cb: pl.BlockSpec((DEC_T, hw), rowmap(cb))
    hist = lambda cb: pl.BlockSpec((1, DN_CONV - 1, hw), lambda s: (s, 0, cb))
    wcs = lambda cb: pl.BlockSpec((DN_CONV, hw), lambda s: (0, cb))
    small = lambda shape: pl.BlockSpec(shape, lambda s: (0, 0))
    return pl.pallas_call(
        functools.partial(_delta_sample_body, nh=nh, dk=dk), grid=(nb,),
        in_specs=[rows(cq), rows(ck), rows(cv), rows(cg), pl.BlockSpec((DEC_T, BLK), rowmap(0)),
                  hist(0), hist(1), hist(2),
                  pl.BlockSpec((1, nh, dk, dk), lambda s: (s, 0, 0, 0)),
                  wcs(0), wcs(1), wcs(2), small((nh, 1)), small((nh, 1)), small((1, dk)),
                  pl.BlockSpec(memory_space=pl.ANY)],
        out_specs=[pl.BlockSpec((DEC_T, hw), rowmap(0)),
                   pl.BlockSpec((1, nh, dk, dk), lambda s: (s, 0, 0, 0))],
        out_shape=[jax.ShapeDtypeStruct(ybuf.shape, F32), jax.ShapeDtypeStruct(s0.shape, F32)],
        scratch_shapes=[pltpu.VMEM((8 + DEC_T, hw), F32)] * 3 + [pltpu.VMEM((nh, dk, dk), F32)],
        input_output_aliases={15: 0},
        compiler_params=_cparams("arbitrary"), name="delta_sample",
    )(p, p, p, p, ba, conv_hist, conv_hist, conv_hist, s0, w_conv, w_conv, w_conv,
      a_log.reshape(nh, 1), dt_bias.reshape(nh, 1), norm_g.reshape(1, dk), ybuf)


def _attn_prompt_body(*refs, nkv, grp, hd, window):
    q_refs = refs[:nkv]
    kp_ref, kc_ref, vp_ref, vc_ref, slope_ref, sink_ref, y_ref = refs[nkv:]
    n = pl.program_id(1)
    i = lax.broadcasted_iota(jnp.int32, (BLK, 2 * BLK), 0)
    j = lax.broadcasted_iota(jnp.int32, (BLK, 2 * BLK), 1)
    dist = BLK + i - j
    krow = (n - 1) * BLK + j
    valid = (dist >= 0) & (dist < window) & (krow >= FRONT)
    distf = dist.astype(F32)
    kk = jnp.concatenate([kp_ref[...], kc_ref[...]], axis=0).astype(BF16)
    vv = jnp.concatenate([vp_ref[...], vc_ref[...]], axis=0).astype(BF16)
    heads = [(kv, g) for kv in range(nkv) for g in range(grp)]
    each = lambda f: [f(t, kv, g) for t, (kv, g) in enumerate(heads)]
    s = each(lambda t, kv, g: _bdot_nt(q_refs[kv][:, g * hd:(g + 1) * hd], kk[:, kv * hd:(kv + 1) * hd]))
    s = each(lambda t, kv, g: jnp.where(valid, s[t] * (hd ** -0.5) - slope_ref[t] * distf, NEG_INF))
    m = each(lambda t, kv, g: jnp.maximum(jnp.max(s[t], axis=-1, keepdims=True), sink_ref[t]))
    pr = each(lambda t, kv, g: jnp.exp(s[t] - m[t]))
    den = each(lambda t, kv, g: jnp.sum(pr[t], axis=-1, keepdims=True) + jnp.exp(sink_ref[t] - m[t]))
    pv = each(lambda t, kv, g: _bdot(pr[t], vv[:, kv * hd:(kv + 1) * hd]))
    real = n * BLK + lax.broadcasted_iota(jnp.int32, (BLK, 1), 0) >= FRONT
    for t in range(len(heads)):
        y_ref[:, t * hd:(t + 1) * hd] = jnp.where(real, pv[t] / den[t], 0.0)


def _attn_prompt(p, slopes, sinks, nseq, nblk, col_q, col_k, col_v, nkv, grp, hd, window):
    gw, kw = grp * hd, nkv * hd
    prev = lambda cb: (lambda b, n: (b * nblk + jnp.maximum(n - 1, 0), cb))
    cur = lambda cb: (lambda b, n: (b * nblk + n, cb))
    smem = pl.BlockSpec(memory_space=pltpu.SMEM)
    return pl.pallas_call(
        functools.partial(_attn_prompt_body, nkv=nkv, grp=grp, hd=hd, window=window), grid=(nseq, nblk),
        in_specs=[pl.BlockSpec((BLK, gw), cur(col_q // gw + kv)) for kv in range(nkv)]
        + [pl.BlockSpec((BLK, kw), prev(col_k // kw)), pl.BlockSpec((BLK, kw), cur(col_k // kw)),
           pl.BlockSpec((BLK, kw), prev(col_v // kw)), pl.BlockSpec((BLK, kw), cur(col_v // kw)),
           smem, smem],
        out_specs=pl.BlockSpec((BLK, nkv * gw), cur(0)),
        out_shape=jax.ShapeDtypeStruct((p.shape[0], nkv * gw), F32),
        compiler_params=_cparams("parallel", "arbitrary"), name="attn_prompt",
    )(*([p] * (nkv + 4)), slopes, sinks)


def _attn_sample_body(*refs, nkv, grp, hd, window):
    q_refs = refs[:nkv]
    k_ref, v_ref, ck_ref, cv_ref, slope_ref, sink_ref, ybuf_ref, y_ref = refs[nkv:]
    del ybuf_ref
    wc = ck_ref.shape[1]
    i = lax.broadcasted_iota(jnp.int32, (DEC_T, wc), 0)
    j = lax.broadcasted_iota(jnp.int32, (DEC_T, wc), 1)
    dist_c = wc + i - j
    valid_c = (dist_c >= 0) & (dist_c < window)
    i2 = lax.broadcasted_iota(jnp.int32, (DEC_T, DEC_T), 0)
    j2 = lax.broadcasted_iota(jnp.int32, (DEC_T, DEC_T), 1)
    dist_n = i2 - j2
    valid_n = (dist_n >= 0) & (dist_n < window)
    tile = lambda a: jnp.concatenate([a] * grp, axis=0)
    valid_c, valid_n = tile(valid_c), tile(valid_n)
    dist_c, dist_n = tile(dist_c).astype(F32), tile(dist_n).astype(F32)
    gi = lax.broadcasted_iota(jnp.int32, (grp * DEC_T, 1), 0) // DEC_T

    def per_row(ref, kv):
        out = jnp.full((grp * DEC_T, 1), ref[kv * grp], F32)
        for g in range(1, grp):
            out = jnp.where(gi == g, ref[kv * grp + g], out)
        return out

    each = lambda f: [f(kv) for kv in range(nkv)]
    cols = lambda ref, kv: ref[:, kv * hd:(kv + 1) * hd]
    q = each(lambda kv: jnp.concatenate([q_refs[kv][:, g * hd:(g + 1) * hd] for g in range(grp)], axis=0))
    slope = each(lambda kv: per_row(slope_ref, kv))
    sink = each(lambda kv: per_row(sink_ref, kv))
    sc = each(lambda kv: _bdot_nt(q[kv], ck_ref[0, :, kv * hd:(kv + 1) * hd]) * (hd ** -0.5))
    sn = each(lambda kv: _bdot_nt(q[kv], cols(k_ref, kv)) * (hd ** -0.5))
    sc = each(lambda kv: jnp.where(valid_c, sc[kv] - slope[kv] * dist_c, NEG_INF))
    sn = each(lambda kv: jnp.where(valid_n, sn[kv] - slope[kv] * dist_n, NEG_INF))
    m = each(lambda kv: jnp.maximum(jnp.maximum(jnp.max(sc[kv], axis=-1, keepdims=True),
                                                jnp.max(sn[kv], axis=-1, keepdims=True)), sink[kv]))
    pc = each(lambda kv: jnp.exp(sc[kv] - m[kv]))
    pn = each(lambda kv: jnp.exp(sn[kv] - m[kv]))
    den = each(lambda kv: jnp.sum(pc[kv], axis=-1, keepdims=True) + jnp.sum(pn[kv], axis=-1, keepdims=True)
               + jnp.exp(sink[kv] - m[kv]))
    o = each(lambda kv: (_bdot(pc[kv], cv_ref[0, :, kv * hd:(kv + 1) * hd]) + _bdot(pn[kv], cols(v_ref, kv)))
             / den[kv])
    for kv in range(nkv):
        for g in range(grp):
            head = kv * grp + g
            y_ref[:, head * hd:(head + 1) * hd] = o[kv][g * DEC_T:(g + 1) * DEC_T, :]


def _attn_sample(p, cache_k, cache_v, slopes, sinks, ybuf, rows_per_seq, col_q, col_k, col_v,
                 nkv, grp, hd, window):
    nb, wc = cache_k.shape[0], cache_k.shape[1]
    gw, kw = grp * hd, nkv * hd
    ck3 = cache_k.reshape(nb, wc, kw)
    cv3 = cache_v.reshape(nb, wc, kw)
    rowmap = lambda cb: (lambda s: (_sample_row_block(s, rows_per_seq), cb))
    smem = pl.BlockSpec(memory_space=pltpu.SMEM)
    return pl.pallas_call(
        functools.partial(_attn_sample_body, nkv=nkv, grp=grp, hd=hd, window=window), grid=(nb,),
        in_specs=[pl.BlockSpec((DEC_T, gw), rowmap(col_q // gw + kv)) for kv in range(nkv)]
        + [pl.BlockSpec((DEC_T, kw), rowmap(col_k // kw)),
           pl.BlockSpec((DEC_T, kw), rowmap(col_v // kw)),
           pl.BlockSpec((1, wc, kw), lambda s: (s, 0, 0)),
           pl.BlockSpec((1, wc, kw), lambda s: (s, 0, 0)),
           smem, smem, pl.BlockSpec(memory_space=pl.ANY)],
        out_specs=pl.BlockSpec((DEC_T, nkv * gw), rowmap(0)),
        out_shape=jax.ShapeDtypeStruct(ybuf.shape, F32),
        input_output_aliases={nkv + 6: 0},
        compiler_params=_cparams("arbitrary"), name="attn_sample",
    )(*([p] * (nkv + 2)), ck3, cv3, slopes, sinks, ybuf)


SUB = 8


def _bitonic_pairs(n, merge_only=False):
    out = []
    k = n if merge_only else 2
    while k <= n:
        j = k // 2
        while j >= 1:
            out += [(i, i ^ j, (i & k) == 0) for i in range(n) if (i ^ j) > i]
            j //= 2
        k *= 2
    return out


def _compare_exchange(v, pairs):
    v = list(v)
    for i, l, desc in pairs:
        hi, lo = jnp.maximum(v[i], v[l]), jnp.minimum(v[i], v[l])
        v[i], v[l] = (hi, lo) if desc else (lo, hi)
    return v


def _top16(rows):
    v = _compare_exchange(rows, _bitonic_pairs(TOPK))
    shift = SUB // 2
    while shift >= 1:
        w = [pltpu.roll(v[TOPK - 1 - r], shift, axis=0) for r in range(TOPK)]
        v = _compare_exchange([jnp.maximum(a, b) for a, b in zip(v, w)], _bitonic_pairs(TOPK, merge_only=True))
        shift //= 2
    return v


def _sublane_sum(x):
    shift = SUB // 2
    while shift >= 1:
        x = x + pltpu.roll(x, shift, axis=0)
        shift //= 2
    return x


def _on_sublanes(vs):
    sub = lax.broadcasted_iota(jnp.int32, vs[0].shape, 0)
    out = vs[SUB - 1]
    for j in range(SUB - 2, -1, -1):
        out = jnp.where(sub == j, vs[j], out)
    return out


def _peer_topk_body(q_ref, keys_ref, rk_ref, cut_ref, e1_ref, e2_ref, *, nh):
    nv = NKEYS // SUB
    for h in range(nh):
        sc = [_bdot_nt(keys_ref[2 * h + half], q_ref[:, (2 * h + half) * NKEYS:(2 * h + half + 1) * NKEYS])
              for half in (0, 1)]
        s1 = [sc[0][SUB * i:SUB * (i + 1), :] for i in range(nv)]
        s2 = [sc[1][SUB * i:SUB * (i + 1), :] for i in range(nv)]
        a = _top16(s1)
        b = _top16(s2)
        b_lo, b_hi, a_hi = _on_sublanes(b[:SUB]), _on_sublanes(b[SUB:]), _on_sublanes(a[SUB:])
        cand = [a[0] + b_lo, a[0] + b_hi, a_hi + b[0]] + [a[i] + b_lo for i in range(1, SUB)]
        cand += [jnp.full(cand[0].shape, LOWEST, F32)] * (TOPK - len(cand))
        top = _top16(cand)
        thr = top[TOPK - 1]
        zsum = jnp.exp(top[0] - top[0])
        for r in range(1, TOPK):
            zsum = zsum + jnp.exp(top[r] - top[0])
        rz = 1.0 / zsum
        height = [_sublane_sum(jnp.where(a[r] + b_lo >= thr, 1.0, 0.0) + jnp.where(a[r] + b_hi >= thr, 1.0, 0.0))
                  for r in range(TOPK)]
        cut, rank2 = [], []
        for i in range(nv):
            c = jnp.zeros(s1[i].shape, F32)
            for r in range(TOPK - 1, -1, -1):
                c = jnp.where(s1[i] == a[r], height[r], c)
            cut.append(c)
            k = jnp.where(b[0] > s2[i], 1.0, 0.0)
            for r in range(1, TOPK):
                k = k + jnp.where(b[r] > s2[i], 1.0, 0.0)
            rank2.append(k)
        rk_ref[h] = jnp.concatenate(rank2, axis=0).astype(BF16)
        cut_ref[h] = jnp.concatenate(cut, axis=0)
        e1_ref[h] = jnp.exp(sc[0] - a[0][0:1, :])
        e2_ref[h] = (jnp.exp(sc[1] - b[0][0:1, :]) * rz[0:1, :]).astype(BF16)


def _peer_topk(q, sub_keys, *, tm=256):
    n = q.shape[0]
    nh = sub_keys.shape[0]
    keys = sub_keys.reshape(2 * nh, NKEYS, sub_keys.shape[-1])
    sspec = pl.BlockSpec((nh, NKEYS, tm), lambda i: (0, 0, i))
    shape = lambda dt: jax.ShapeDtypeStruct((nh, NKEYS, n), dt)
    return pl.pallas_call(
        functools.partial(_peer_topk_body, nh=nh), grid=(n // tm,),
        in_specs=[pl.BlockSpec((tm, q.shape[1]), lambda i: (i, 0)),
                  pl.BlockSpec(keys.shape, lambda i: (0, 0, 0))],
        out_specs=[sspec] * 4,
        out_shape=[shape(BF16), shape(F32), shape(F32), shape(BF16)],
        compiler_params=_cparams("parallel"), name="peer_topk")(q, keys)


def _gelu(x):
    return 0.5 * x * (1.0 + lax.erf(x * (2.0 ** -0.5)))


PACK = 16


def _peer_expert_body(xn_ref, wd_ref, wu_ref, id_ref, iu_ref, rk_ref, cut_ref, e1_ref, e2_ref, y_ref,
                      xt_ref, ix_ref, *, nh, ei, sub):
    c = pl.program_id(1)
    tm = xt_ref.shape[1]
    tw = tm // sub
    slabs = [slice(t * tw, (t + 1) * tw) for t in range(sub)]
    each = lambda f: [f(t, tl) for t, tl in enumerate(slabs)]

    @pl.when(c == 0)
    def _():
        y_ref[...] = jnp.zeros(y_ref.shape, F32)
        x = xn_ref[...].astype(F32)
        sx = _amax_scale(x)
        xt_ref[...] = (x * sx).T.astype(FP8)
        ix_ref[...] = jnp.broadcast_to(1.0 / sx, ix_ref.shape)

    def gates(tl):
        def row16(ref, h, i1):
            return jnp.broadcast_to(ref[h, pl.ds(i1, 1), tl], (PACK, tw)).astype(BF16)
        pieces = []
        for ii in range(ei):
            i1 = c * ei + ii
            cut = [row16(cut_ref, h, i1) for h in range(nh)]
            e1 = [row16(e1_ref, h, i1) for h in range(nh)]
            for r in range(NKEYS // PACK):
                sl = slice(r * PACK, (r + 1) * PACK)
                gate = pieces[-1] * 0.0 if pieces else None
                for h in range(nh):
                    w = jnp.where(rk_ref[h, sl, tl] < cut[h], e1[h] * e2_ref[h, sl, tl], 0.0)
                    gate = w if gate is None else gate + w
                pieces.append(gate)
        return pieces

    inv_h = id_ref[0, 0:1, 0:1] * ix_ref[0:1, 0:1]
    inv_u = iu_ref[0, 0:1, 0:1]
    ht = each(lambda t, tl: jnp.dot(wd_ref[...], xt_ref[:, tl], preferred_element_type=F32) * inv_h)
    gate = each(lambda t, tl: gates(tl))
    act = each(lambda t, tl: _gelu(ht[t]))
    at = each(lambda t, tl: jnp.concatenate(
        [act[t][i * PACK:(i + 1) * PACK, :] * g.astype(F32) for i, g in enumerate(gate[t])], axis=0))
    sa = each(lambda t, tl: _amax_scale(at[t]))
    part = each(lambda t, tl: jnp.dot((at[t] * sa[t]).T.astype(FP8), wu_ref[...], preferred_element_type=F32))
    for t, tl in enumerate(slabs):
        y_ref[tl, :] += part[t] * (inv_u / sa[t])


PEER_TM, PEER_EI, PEER_SUB = 512, 8, 2


def _peer_expert(xn, w_down, w_up, inv_down, inv_up, rank2, cut, e1, e2, *, tm=PEER_TM, ei=PEER_EI, sub=PEER_SUB):
    n, d = xn.shape
    nh = rank2.shape[0]
    e = ei * NKEYS
    once = dict(pipeline_mode=pl.Buffered(1))
    sspec = pl.BlockSpec((nh, NKEYS, tm), lambda i, c: (0, 0, i))
    inv = pl.BlockSpec((1, 8, BLK), lambda i, c: (c, 0, 0))
    return pl.pallas_call(
        functools.partial(_peer_expert_body, nh=nh, ei=ei, sub=sub), grid=(n // tm, w_down.shape[0] // e),
        in_specs=[pl.BlockSpec((tm, d), lambda i, c: (i, 0)),
                  pl.BlockSpec((e, d), lambda i, c: (c, 0)),
                  pl.BlockSpec((e, d), lambda i, c: (c, 0)),
                  inv, inv, sspec, sspec, sspec, sspec],
        out_specs=pl.BlockSpec((tm, d), lambda i, c: (i, 0), **once),
        out_shape=jax.ShapeDtypeStruct((n, d), F32),
        scratch_shapes=[pltpu.VMEM((d, tm), FP8), pltpu.VMEM((8, BLK), F32)],
        compiler_params=_cparams("parallel", "arbitrary"), name="peer_expert",
    )(xn, w_down, w_up, inv_down, inv_up, rank2, cut, e1, e2)


def _final_norm(h, y, g, nseq, nblk, row_blk, first_blk, nblk_out):
    d = h.shape[1]
    row = pl.BlockSpec((row_blk, d), lambda b, i: (b * nblk + first_blk + i, 0))
    return pl.pallas_call(
        _final_norm_body, grid=(nseq, nblk_out),
        in_specs=[row, row, pl.BlockSpec((1, d), lambda b, i: (0, 0))],
        out_specs=pl.BlockSpec((1, row_blk, d), lambda b, i: (b, i, 0)),
        out_shape=jax.ShapeDtypeStruct((nseq, nblk_out * row_blk, d), F32),
        compiler_params=_cparams("parallel", "parallel"), name="final_norm",
    )(h, y, g.reshape(1, d).astype(F32))


def _final_norm_body(h_ref, y_ref, g_ref, o_ref):
    x = h_ref[...] + y_ref[...]
    ms = jnp.mean(x * x, axis=-1, keepdims=True)
    o_ref[0] = x * lax.rsqrt(ms + EPS) * g_ref[...]


def kernel(x_prompt, x_sample, state_pool, state_conv, state_delta, cache_k, cache_v, meta_tokens, norm1_g,
           w_in, w_pool, s_pool, w_conv, a_log, dt_bias, dn_norm_g, attn_sinks, w_out, norm2_g,
           peer_w_query, peer_sub_keys, peer_w_down, peer_w_up, final_norm_g):
    nseq, seq, d = x_prompt.shape
    nsamp, dec_t, _ = x_sample.shape
    depth = w_in.shape[0]
    pool_w = w_pool.shape[1] * w_pool.shape[2]
    dn_qkv = w_conv.shape[2]
    nh, dk = state_delta.shape[2], state_delta.shape[3]
    wc, nkv, hd = cache_k.shape[2], cache_k.shape[3], cache_k.shape[4]
    nq = attn_sinks.shape[1]
    grp = nq // nkv
    window = wc
    assert dec_t == DEC_T and nsamp == nseq * SLOTS and (N_META + seq) % BLK == N_META
    assert SLOTS * DEC_T + POOL_HIST <= FRONT and dn_qkv == 3 * nh * dk and wc == BLK
    rows_per_seq = FRONT + N_META + seq
    nblk = rows_per_seq // BLK
    col_qkv = pool_w
    col_gate = col_qkv + dn_qkv
    col_q = col_gate + nh * dk
    col_k = col_q + nq * hd
    col_v = col_k + nkv * hd
    src_ba = pool_w + dn_qkv

    slopes = jnp.exp2(-8.0 * (jnp.arange(nq, dtype=F32) + 1.0) / nq)
    new_p = [[] for _ in range(5)]
    new_s = [[] for _ in range(5)]
    y_peer = None
    w_in_t = jnp.swapaxes(w_in, 1, 2)
    for l in range(depth):
        w_main, w_ba = _cast_w_in(w_in_t, l, src_ba, 2 * nh)
        if l == 0:
            h, xn = _assemble_norm(x_prompt, x_sample, meta_tokens, norm1_g[l], nblk)
        else:
            h, xn = _addnorm(h, y_peer, norm1_g[l])
        p = _mm(xn, w_main, nt=True)
        ba = _mm(xn, w_ba, nt=True)

        y_pool = _pool_prompt(p, w_pool[l], s_pool[l].reshape(1, pool_w), nseq, nblk)
        y_pool = _pool_sample(p, state_pool[l], w_pool[l], s_pool[l].reshape(1, pool_w), y_pool,
                              rows_per_seq, PAST_LEN)
        y_dn, s_p = _delta_prompt(p, ba, w_conv[l], a_log[l], dt_bias[l], dn_norm_g[l], nseq, nblk,
                                  col_qkv, col_gate, nh, dk)
        y_dn, s_s = _delta_sample(p, ba, state_conv[l], state_delta[l], w_conv[l], a_log[l], dt_bias[l],
                                  dn_norm_g[l], y_dn, rows_per_seq, col_qkv, col_gate, nh, dk)
        y_att = _attn_prompt(p, slopes, attn_sinks[l], nseq, nblk, col_q, col_k, col_v, nkv, grp, hd, window)
        y_att = _attn_sample(p, cache_k[l], cache_v[l], slopes, attn_sinks[l], y_att, rows_per_seq,
                             col_q, col_k, col_v, nkv, grp, hd, window)
        h = _outproj(y_pool, y_dn, y_att, _cast(w_out, l), h)

        _, xn2 = _addnorm(h, None, norm2_g[l])
        q = _mm(xn2, _cast(peer_w_query, l))
        rank2, cut, e1, e2 = _peer_topk(q, peer_sub_keys[l])
        wd8, inv_d = _quant_fp8(peer_w_down, l, tr=PEER_EI * NKEYS)
        wu8, inv_u = _quant_fp8(peer_w_up, l, tr=PEER_EI * NKEYS)
        y_peer = _peer_expert(xn2, wd8, wu8, inv_d, inv_u, rank2, cut, e1, e2)

        p3 = p.reshape(nseq, rows_per_seq, p.shape[1])
        ps = p3[:, :SLOTS * DEC_T].reshape(nsamp, DEC_T, p.shape[1])
        new_p[0].append(p3[:, -POOL_HIST:, :pool_w])
        new_s[0].append(jnp.concatenate([state_pool[l], ps[:, :, :pool_w]], axis=1)[:, -POOL_HIST:])
        new_p[1].append(p3[:, -(DN_CONV - 1):, col_qkv:col_gate])
        new_s[1].append(jnp.concatenate([state_conv[l], ps[:, :, col_qkv:col_gate]], axis=1)[:, -(DN_CONV - 1):])
        new_p[2].append(s_p)
        new_s[2].append(s_s)
        new_p[3].append(p3[:, -window:, col_k:col_v].reshape(nseq, window, nkv, hd))
        new_s[3].append(jnp.concatenate([cache_k[l], ps[:, :, col_k:col_v].reshape(nsamp, DEC_T, nkv, hd)],
                                        axis=1)[:, -wc:])
        new_p[4].append(p3[:, -window:, col_v:].reshape(nseq, window, nkv, hd))
        new_s[4].append(jnp.concatenate([cache_v[l], ps[:, :, col_v:].reshape(nsamp, DEC_T, nkv, hd)],
                                        axis=1)[:, -wc:])

    y_prompt = _final_norm(h, y_peer, final_norm_g, nseq, nblk, BLK, 1, nblk - 1)
    y_sample = _final_norm(h, y_peer, final_norm_g, nseq, rows_per_seq // (SLOTS * DEC_T), SLOTS * DEC_T, 0, 1)
    y_sample = y_sample.reshape(nsamp, DEC_T, d)
    pool_p, conv_p, delta_p, k_p, v_p = (jnp.stack(a) for a in new_p)
    pool_s, conv_s, delta_s, k_s, v_s = (jnp.stack(a) for a in new_s)
    return (y_prompt, y_sample, pool_p, pool_s, conv_p, conv_s, delta_p, delta_s, k_p, k_s, v_p, v_s)
```
